```python
import jax, jax.numpy as jnp
from jax import lax
import numpy as np

D_MODEL = 1024
BATCH = 8
SEQ = 8192
DEPTH = 4

N_MIXERS = 2
N_HGRN_LAYERS = (DEPTH + 1) // 2
N_MLA_LAYERS = DEPTH // 2
RMS_EPS = 1e-6

HGRN_EXPAND = 128
HGRN_HEADS = D_MODEL // HGRN_EXPAND
HGRN_VDIM = D_MODEL // HGRN_HEADS
HGRN_CHUNK = 64

MLA_HEADS = D_MODEL // 128
MLA_NOPE = 128
MLA_ROPE = 64
MLA_QK_HEAD = MLA_NOPE + MLA_ROPE
MLA_V_HEAD = 128
MLA_Q_LORA = D_MODEL // 4
MLA_KV_LORA = D_MODEL // 4
ROPE_THETA = 10000.0
Q_BLOCK = 128

D_FF = 2816
FFN_CONV = 3

kernel_name = "hybrid_hgrn2_mla_convffn_trunk"


def _rmsnorm(x, gain):
    xf = x.astype(jnp.float32)
    y = xf * lax.rsqrt(jnp.mean(xf * xf, axis=-1, keepdims=True) + RMS_EPS)
    return (y * gain.astype(jnp.float32)).astype(x.dtype)


def _chunk_gated_recurrence(q, k, v, log_f):
    B, S, H, Dk = q.shape
    Dv = v.shape[-1]
    C = HGRN_CHUNK
    N = S // C

    def to_chunks(t):
        return t.astype(jnp.float32).reshape(B, N, C, H, t.shape[-1]).transpose(1, 0, 3, 2, 4)

    causal = jnp.tril(jnp.ones((C, C), dtype=bool))[:, :, None]

    def step(state, inp):
        qc, kc, vc, gc = inp
        G = jnp.cumsum(gc, axis=2)
        o_inter = jnp.einsum("bhtd,bhde->bhte", qc * jnp.exp(G), state)
        diff = G[:, :, :, None, :] - G[:, :, None, :, :]
        decay = jnp.where(causal, jnp.exp(jnp.where(causal, diff, 0.0)), 0.0)
        scores = jnp.einsum("bhtd,bhsd,bhtsd->bhts", qc, kc, decay)
        o_intra = jnp.einsum("bhts,bhse->bhte", scores, vc)
        G_last = G[:, :, -1:, :]
        new_state = state * jnp.exp(G_last[:, :, 0, :, None]) + jnp.einsum(
            "bhsd,bhse->bhde", kc * jnp.exp(G_last - G), vc)
        return new_state, o_inter + o_intra

    state0 = jnp.zeros((B, H, Dk, Dv), jnp.float32)
    _, o = lax.scan(step, state0, (to_chunks(q), to_chunks(k), to_chunks(v), to_chunks(log_f)))
    return o.transpose(1, 0, 3, 2, 4).reshape(B, S, H, Dv)


def _hgrn2_mixer(h, w_in, lower_bound, out_norm, w_out):
    B, S, _ = h.shape
    q, f, i, g = jnp.split(h @ w_in, 4, axis=-1)

    def heads(t):
        return t.reshape(B, S, HGRN_HEADS, -1)

    lb = lower_bound.astype(jnp.float32)
    forget = lb + (1.0 - lb) * jax.nn.sigmoid(f.astype(jnp.float32))
    key_in = 1.0 - forget
    log_f = jnp.log(forget)
    o = _chunk_gated_recurrence(heads(jax.nn.silu(q)), heads(key_in), heads(i), heads(log_f))
    o = _rmsnorm(o, out_norm) * jax.nn.silu(heads(g.astype(jnp.float32)))
    return o.reshape(B, S, D_MODEL).astype(h.dtype) @ w_out


def _rope_tail(x, cos, sin):
    x_nope, x1, x2 = jnp.split(x, [MLA_NOPE, MLA_NOPE + MLA_ROPE // 2], axis=-1)
    xf1 = x1.astype(jnp.float32)
    xf2 = x2.astype(jnp.float32)
    rot = jnp.concatenate([xf1 * cos - xf2 * sin, xf2 * cos + xf1 * sin], axis=-1).astype(x.dtype)
    return jnp.concatenate([x_nope, rot], axis=-1)


def _causal_block_attention(q, k, v):
    B, S, H, Dqk = q.shape
    nb = S // Q_BLOCK
    scale = Dqk ** -0.5
    q_blocks = q.reshape(B, nb, Q_BLOCK, H, Dqk).transpose(1, 0, 3, 2, 4)
    k_t = k.transpose(0, 2, 1, 3)
    v_t = v.transpose(0, 2, 1, 3)
    key_pos = jnp.arange(S)

    def attend(args):
        q_blk, blk = args
        s = jnp.einsum("bhqd,bhkd->bhqk", q_blk, k_t).astype(jnp.float32) * scale
        q_pos = blk * Q_BLOCK + jnp.arange(Q_BLOCK)
        s = jnp.where(key_pos[None, :] <= q_pos[:, None], s, -jnp.inf)
        p = jax.nn.softmax(s, axis=-1).astype(v_t.dtype)
        return jnp.einsum("bhqk,bhkv->bhqv", p, v_t)

    o = lax.map(attend, (q_blocks, jnp.arange(nb)))
    return o.transpose(1, 0, 3, 2, 4).reshape(B, S, H, -1)


def _mla_mixer(h, cos, sin, w_in, q_a_norm, w_q_up, kv_a_norm, w_kv_up, q_norm, k_norm, w_out):
    B, S, _ = h.shape
    c_q, c_kv, k_rope = jnp.split(h @ w_in, [MLA_Q_LORA, MLA_Q_LORA + MLA_KV_LORA], axis=-1)
    q = (_rmsnorm(c_q, q_a_norm) @ w_q_up).reshape(B, S, MLA_HEADS, MLA_QK_HEAD)
    kv = (_rmsnorm(c_kv, kv_a_norm) @ w_kv_up).reshape(B, S, MLA_HEADS, MLA_NOPE + MLA_V_HEAD)
    k_nope, v = jnp.split(kv, [MLA_NOPE], axis=-1)
    k = jnp.concatenate(
        [k_nope, jnp.broadcast_to(k_rope[:, :, None, :], (B, S, MLA_HEADS, MLA_ROPE))], axis=-1)
    q = _rope_tail(_rmsnorm(q, q_norm), cos, sin)
    k = _rope_tail(_rmsnorm(k, k_norm), cos, sin)
    o = _causal_block_attention(q, k, v)
    return o.reshape(B, S, MLA_HEADS * MLA_V_HEAD) @ w_out


def _conv_ffn(h, w_up, conv_w, conv_b, w_down):
    S = h.shape[1]
    u = h @ w_up
    u_pad = jnp.pad(u, ((0, 0), (FFN_CONV - 1, 0), (0, 0)))
    y = conv_b.astype(u.dtype)
    for j in range(FFN_CONV):
        y = y + u_pad[:, j:j + S, :] * conv_w[j]
    gate, up = jnp.split(y, 2, axis=-1)
    return (jax.nn.silu(gate) * up) @ w_down


def _fwd_setup_inputs(seed: int = 0) -> dict:
    key = jax.random.key(seed)
    ks = iter(jax.random.split(key, 32))

    def nrm(shape, scale):
        return jax.random.normal(next(ks), shape, jnp.float32) * scale

    def gain(shape):
        return 1.0 + 0.02 * jax.random.normal(next(ks), shape, jnp.float32)

    D = D_MODEL
    x = jax.random.normal(next(ks), (BATCH, SEQ, D), jnp.float32)
    offsets = jax.random.randint(next(ks), (BATCH, 1), 0, 4096, dtype=jnp.int32)
    positions = offsets + jnp.arange(SEQ, dtype=jnp.int32)[None, :]
    return {
        "x": x,
        "positions": positions,
        "norm_mix": gain((DEPTH, D)),
        "norm_ffn": gain((DEPTH, D)),
        "hgrn_w_in": nrm((N_HGRN_LAYERS, D, 4 * D), D ** -0.5),
        "hgrn_lower_bounds": nrm((N_HGRN_LAYERS, HGRN_HEADS * HGRN_EXPAND), 0.1),
        "hgrn_out_norm": gain((N_HGRN_LAYERS, HGRN_VDIM)),
        "hgrn_w_out": nrm((N_HGRN_LAYERS, D, D), D ** -0.5),
        "mla_w_in": nrm((N_MLA_LAYERS, D, MLA_Q_LORA + MLA_KV_LORA + MLA_ROPE), D ** -0.5),
        "mla_q_a_norm": gain((N_MLA_LAYERS, MLA_Q_LORA)),
        "mla_w_q_up": nrm((N_MLA_LAYERS, MLA_Q_LORA, MLA_HEADS * MLA_QK_HEAD), MLA_Q_LORA ** -0.5),
        "mla_kv_a_norm": gain((N_MLA_LAYERS, MLA_KV_LORA)),
        "mla_w_kv_up": nrm((N_MLA_LAYERS, MLA_KV_LORA, MLA_HEADS * (MLA_NOPE + MLA_V_HEAD)), MLA_KV_LORA ** -0.5),
        "mla_q_norm": gain((N_MLA_LAYERS, MLA_QK_HEAD)),
        "mla_k_norm": gain((N_MLA_LAYERS, MLA_QK_HEAD)),
        "mla_w_out": nrm((N_MLA_LAYERS, MLA_HEADS * MLA_V_HEAD, D), (MLA_HEADS * MLA_V_HEAD) ** -0.5),
        "ffn_w_up": nrm((DEPTH, D, 2 * D_FF), D ** -0.5),
        "ffn_conv_w": nrm((DEPTH, FFN_CONV, 2 * D_FF), FFN_CONV ** -0.5),
        "ffn_conv_b": nrm((DEPTH, 2 * D_FF), 0.01),
        "ffn_w_down": nrm((DEPTH, D_FF, D), D_FF ** -0.5),
    }


def _fwd_reference(x, positions, norm_mix, norm_ffn, hgrn_w_in, hgrn_lower_bounds, hgrn_out_norm, hgrn_w_out,
              mla_w_in, mla_q_a_norm, mla_w_q_up, mla_kv_a_norm, mla_w_kv_up, mla_q_norm, mla_k_norm,
              mla_w_out, ffn_w_up, ffn_conv_w, ffn_conv_b, ffn_w_down):
    lb_soft = jax.nn.softmax(hgrn_lower_bounds.astype(jnp.float32), axis=0)
    lower_bounds = jnp.cumsum(lb_soft, axis=0) - lb_soft[0:1]
    inv_freq = ROPE_THETA ** (-jnp.arange(0, MLA_ROPE, 2, dtype=jnp.float32) / MLA_ROPE)
    ang = positions.astype(jnp.float32)[..., None] * inv_freq
    cos = jnp.cos(ang)[:, :, None, :]
    sin = jnp.sin(ang)[:, :, None, :]

    for layer in range(DEPTH):
        h = _rmsnorm(x, norm_mix[layer])
        j = layer // N_MIXERS
        if layer % N_MIXERS == 0:
            y = _hgrn2_mixer(h, hgrn_w_in[j], lower_bounds[j], hgrn_out_norm[j], hgrn_w_out[j])
        else:
            y = _mla_mixer(h, cos, sin, mla_w_in[j], mla_q_a_norm[j], mla_w_q_up[j], mla_kv_a_norm[j],
                           mla_w_kv_up[j], mla_q_norm[j], mla_k_norm[j], mla_w_out[j])
        x = x + y
        h = _rmsnorm(x, norm_ffn[layer])
        x = x + _conv_ffn(h, ffn_w_up[layer], ffn_conv_w[layer], ffn_conv_b[layer], ffn_w_down[layer])
    return x


import jax as _jax
import jax.numpy as _jnp

TWIN_FORMAT = 'train_step'
FWD_PARAMS = ['x', 'positions', 'norm_mix', 'norm_ffn', 'hgrn_w_in', 'hgrn_lower_bounds', 'hgrn_out_norm', 'hgrn_w_out', 'mla_w_in', 'mla_q_a_norm', 'mla_w_q_up', 'mla_kv_a_norm', 'mla_w_kv_up', 'mla_q_norm', 'mla_k_norm', 'mla_w_out', 'ffn_w_up', 'ffn_conv_w', 'ffn_conv_b', 'ffn_w_down']
TWIN_WEIGHTS = ['norm_mix', 'norm_ffn', 'hgrn_w_in', 'hgrn_lower_bounds', 'hgrn_out_norm', 'hgrn_w_out', 'mla_w_in', 'mla_q_a_norm', 'mla_w_q_up', 'mla_kv_a_norm', 'mla_w_kv_up', 'mla_q_norm', 'mla_k_norm', 'mla_w_out', 'ffn_w_up', 'ffn_conv_w', 'ffn_conv_b', 'ffn_w_down']
TWIN_DIFF_INPUT = 'x'
TWIN_INPUTS = ['x', 'positions', 'norm_mix', 'norm_ffn', 'hgrn_w_in', 'hgrn_lower_bounds', 'hgrn_out_norm', 'hgrn_w_out', 'mla_w_in', 'mla_q_a_norm', 'mla_w_q_up', 'mla_kv_a_norm', 'mla_w_kv_up', 'mla_q_norm', 'mla_k_norm', 'mla_w_out', 'ffn_w_up', 'ffn_conv_w', 'ffn_conv_b', 'ffn_w_down', 'loss_target', 'm_norm_mix', 'm_norm_ffn', 'm_hgrn_w_in', 'm_hgrn_lower_bounds', 'm_hgrn_out_norm', 'm_hgrn_w_out', 'm_mla_w_in', 'm_mla_q_a_norm', 'm_mla_w_q_up', 'm_mla_kv_a_norm', 'm_mla_w_kv_up', 'm_mla_q_norm', 'm_mla_k_norm', 'm_mla_w_out', 'm_ffn_w_up', 'm_ffn_conv_w', 'm_ffn_conv_b', 'm_ffn_w_down', 'v_norm_mix', 'v_norm_ffn', 'v_hgrn_w_in', 'v_hgrn_lower_bounds', 'v_hgrn_out_norm', 'v_hgrn_w_out', 'v_mla_w_in', 'v_mla_q_a_norm', 'v_mla_w_q_up', 'v_mla_kv_a_norm', 'v_mla_w_kv_up', 'v_mla_q_norm', 'v_mla_k_norm', 'v_mla_w_out', 'v_ffn_w_up', 'v_ffn_conv_w', 'v_ffn_conv_b', 'v_ffn_w_down']
TWIN_OUTPUTS = ['loss', 'grad_x', 'grad_norm_mix', 'grad_norm_ffn', 'grad_hgrn_w_in', 'grad_hgrn_lower_bounds', 'grad_hgrn_out_norm', 'grad_hgrn_w_out', 'grad_mla_w_in', 'grad_mla_q_a_norm', 'grad_mla_w_q_up', 'grad_mla_kv_a_norm', 'grad_mla_w_kv_up', 'grad_mla_q_norm', 'grad_mla_k_norm', 'grad_mla_w_out', 'grad_ffn_w_up', 'grad_ffn_conv_w', 'grad_ffn_conv_b', 'grad_ffn_w_down', 'delta_norm_mix', 'delta_norm_ffn', 'delta_hgrn_w_in', 'delta_hgrn_lower_bounds', 'delta_hgrn_out_norm', 'delta_hgrn_w_out', 'delta_mla_w_in', 'delta_mla_q_a_norm', 'delta_mla_w_q_up', 'delta_mla_kv_a_norm', 'delta_mla_w_kv_up', 'delta_mla_q_norm', 'delta_mla_k_norm', 'delta_mla_w_out', 'delta_ffn_w_up', 'delta_ffn_conv_w', 'delta_ffn_conv_b', 'delta_ffn_w_down', 'new_m_norm_mix', 'new_m_norm_ffn', 'new_m_hgrn_w_in', 'new_m_hgrn_lower_bounds', 'new_m_hgrn_out_norm', 'new_m_hgrn_w_out', 'new_m_mla_w_in', 'new_m_mla_q_a_norm', 'new_m_mla_w_q_up', 'new_m_mla_kv_a_norm', 'new_m_mla_w_kv_up', 'new_m_mla_q_norm', 'new_m_mla_k_norm', 'new_m_mla_w_out', 'new_m_ffn_w_up', 'new_m_ffn_conv_w', 'new_m_ffn_conv_b', 'new_m_ffn_w_down', 'new_v_norm_mix', 'new_v_norm_ffn', 'new_v_hgrn_w_in', 'new_v_hgrn_lower_bounds', 'new_v_hgrn_out_norm', 'new_v_hgrn_w_out', 'new_v_mla_w_in', 'new_v_mla_q_a_norm', 'new_v_mla_w_q_up', 'new_v_mla_kv_a_norm', 'new_v_mla_w_kv_up', 'new_v_mla_q_norm', 'new_v_mla_k_norm', 'new_v_mla_w_out', 'new_v_ffn_w_up', 'new_v_ffn_conv_w', 'new_v_ffn_conv_b', 'new_v_ffn_w_down']
TWIN_LEAF_KINDS = {'loss': 'loss', 'grad_x': 'grad_x', 'grad_norm_mix': 'grad_w', 'grad_norm_ffn': 'grad_w', 'grad_hgrn_w_in': 'grad_w', 'grad_hgrn_lower_bounds': 'grad_w', 'grad_hgrn_out_norm': 'grad_w', 'grad_hgrn_w_out': 'grad_w', 'grad_mla_w_in': 'grad_w', 'grad_mla_q_a_norm': 'grad_w', 'grad_mla_w_q_up': 'grad_w', 'grad_mla_kv_a_norm': 'grad_w', 'grad_mla_w_kv_up': 'grad_w', 'grad_mla_q_norm': 'grad_w', 'grad_mla_k_norm': 'grad_w', 'grad_mla_w_out': 'grad_w', 'grad_ffn_w_up': 'grad_w', 'grad_ffn_conv_w': 'grad_w', 'grad_ffn_conv_b': 'grad_w', 'grad_ffn_w_down': 'grad_w', 'delta_norm_mix': 'delta_w', 'delta_norm_ffn': 'delta_w', 'delta_hgrn_w_in': 'delta_w', 'delta_hgrn_lower_bounds': 'delta_w', 'delta_hgrn_out_norm': 'delta_w', 'delta_hgrn_w_out': 'delta_w', 'delta_mla_w_in': 'delta_w', 'delta_mla_q_a_norm': 'delta_w', 'delta_mla_w_q_up': 'delta_w', 'delta_mla_kv_a_norm': 'delta_w', 'delta_mla_w_kv_up': 'delta_w', 'delta_mla_q_norm': 'delta_w', 'delta_mla_k_norm': 'delta_w', 'delta_mla_w_out': 'delta_w', 'delta_ffn_w_up': 'delta_w', 'delta_ffn_conv_w': 'delta_w', 'delta_ffn_conv_b': 'delta_w', 'delta_ffn_w_down': 'delta_w', 'new_m_norm_mix': 'new_m', 'new_m_norm_ffn': 'new_m', 'new_m_hgrn_w_in': 'new_m', 'new_m_hgrn_lower_bounds': 'new_m', 'new_m_hgrn_out_norm': 'new_m', 'new_m_hgrn_w_out': 'new_m', 'new_m_mla_w_in': 'new_m', 'new_m_mla_q_a_norm': 'new_m', 'new_m_mla_w_q_up': 'new_m', 'new_m_mla_kv_a_norm': 'new_m', 'new_m_mla_w_kv_up': 'new_m', 'new_m_mla_q_norm': 'new_m', 'new_m_mla_k_norm': 'new_m', 'new_m_mla_w_out': 'new_m', 'new_m_ffn_w_up': 'new_m', 'new_m_ffn_conv_w': 'new_m', 'new_m_ffn_conv_b': 'new_m', 'new_m_ffn_w_down': 'new_m', 'new_v_norm_mix': 'new_v', 'new_v_norm_ffn': 'new_v', 'new_v_hgrn_w_in': 'new_v', 'new_v_hgrn_lower_bounds': 'new_v', 'new_v_hgrn_out_norm': 'new_v', 'new_v_hgrn_w_out': 'new_v', 'new_v_mla_w_in': 'new_v', 'new_v_mla_q_a_norm': 'new_v', 'new_v_mla_w_q_up': 'new_v', 'new_v_mla_kv_a_norm': 'new_v', 'new_v_mla_w_kv_up': 'new_v', 'new_v_mla_q_norm': 'new_v', 'new_v_mla_k_norm': 'new_v', 'new_v_mla_w_out': 'new_v', 'new_v_ffn_w_up': 'new_v', 'new_v_ffn_conv_w': 'new_v', 'new_v_ffn_conv_b': 'new_v', 'new_v_ffn_w_down': 'new_v'}


def _forward(args):
    return _fwd_reference(*[args[k] for k in FWD_PARAMS])


def _output_shape():
    def fwd():
        inp = _fwd_setup_inputs(0)
        return _fwd_reference(*[inp[k] for k in FWD_PARAMS])
    out = _jax.eval_shape(fwd)
    return out.shape, out.dtype

N_MICROBATCH = 1
ADAM_LR = 0.001
ADAM_B1 = 0.9
ADAM_B2 = 0.999
ADAM_EPS = 1e-08
ADAM_WD = 0.01
ADAM_STEP = 10
PER_EXAMPLE_BATCH_AXIS = {'x': 0, 'positions': 0, 'loss_target': 0}
SHARED_INPUTS = []
_WEIGHT_DTYPES = {'norm_mix': _jnp.float32, 'norm_ffn': _jnp.float32, 'hgrn_w_in': _jnp.float32, 'hgrn_lower_bounds': _jnp.float32, 'hgrn_out_norm': _jnp.float32, 'hgrn_w_out': _jnp.float32, 'mla_w_in': _jnp.float32, 'mla_q_a_norm': _jnp.float32, 'mla_w_q_up': _jnp.float32, 'mla_kv_a_norm': _jnp.float32, 'mla_w_kv_up': _jnp.float32, 'mla_q_norm': _jnp.float32, 'mla_k_norm': _jnp.float32, 'mla_w_out': _jnp.float32, 'ffn_w_up': _jnp.float32, 'ffn_conv_w': _jnp.float32, 'ffn_conv_b': _jnp.float32, 'ffn_w_down': _jnp.float32}
MOMENT_SCALE = {'norm_mix': 1.857465e+01, 'norm_ffn': 5.097605e+01, 'hgrn_w_in': 6.057896e-01, 'hgrn_lower_bounds': 5.028317e-02, 'hgrn_out_norm': 1.825783e+02, 'hgrn_w_out': 8.774967e-01, 'mla_w_in': 6.330767e-01, 'mla_q_a_norm': 3.866155e-01, 'mla_w_q_up': 1.525135e-01, 'mla_kv_a_norm': 3.022089e+00, 'mla_w_kv_up': 2.879369e-01, 'mla_q_norm': 1.719975e+00, 'mla_k_norm': 1.719827e+00, 'mla_w_out': 3.546351e-01, 'ffn_w_up': 4.836707e-01, 'ffn_conv_w': 6.968575e+00, 'ffn_conv_b': 6.449862e+00, 'ffn_w_down': 7.383392e-01}


def _to_microbatches(a, axis):
    t = _jnp.moveaxis(a, axis, 0)
    t = t.reshape((N_MICROBATCH, t.shape[0] // N_MICROBATCH) + t.shape[1:])
    return _jnp.moveaxis(t, 1, axis + 1)


def setup_inputs(seed: int = 0) -> dict:
    inp = _fwd_setup_inputs(seed)
    key = _jax.random.fold_in(_jax.random.key(seed), 7919)
    shape, _ = _output_shape()
    out = dict(inp)
    out["loss_target"] = _jax.random.normal(_jax.random.fold_in(key, 0), shape, _jnp.float32)
    for i, name in enumerate(TWIN_WEIGHTS):
        w = inp[name].astype(_jnp.float32)
        if MOMENT_SCALE is None:
            s = _jnp.sqrt(_jnp.mean(_jnp.square(w)) + 1e-30)
        else:
            s = MOMENT_SCALE[name]
        km, kv = _jax.random.split(_jax.random.fold_in(key, i + 1))
        out[name] = w
        out["m_" + name] = s * _jax.random.normal(km, w.shape, _jnp.float32)
        out["v_" + name] = (s * s) * _jax.random.uniform(kv, w.shape, _jnp.float32, 0.5, 1.5)
    if N_MICROBATCH > 1:
        for name, axis in PER_EXAMPLE_BATCH_AXIS.items():
            out[name] = _to_microbatches(out[name], axis)
    return {'x': out['x'], 'positions': out['positions'], 'norm_mix': out['norm_mix'], 'norm_ffn': out['norm_ffn'], 'hgrn_w_in': out['hgrn_w_in'], 'hgrn_lower_bounds': out['hgrn_lower_bounds'], 'hgrn_out_norm': out['hgrn_out_norm'], 'hgrn_w_out': out['hgrn_w_out'], 'mla_w_in': out['mla_w_in'], 'mla_q_a_norm': out['mla_q_a_norm'], 'mla_w_q_up': out['mla_w_q_up'], 'mla_kv_a_norm': out['mla_kv_a_norm'], 'mla_w_kv_up': out['mla_w_kv_up'], 'mla_q_norm': out['mla_q_norm'], 'mla_k_norm': out['mla_k_norm'], 'mla_w_out': out['mla_w_out'], 'ffn_w_up': out['ffn_w_up'], 'ffn_conv_w': out['ffn_conv_w'], 'ffn_conv_b': out['ffn_conv_b'], 'ffn_w_down': out['ffn_w_down'], 'loss_target': out['loss_target'], 'm_norm_mix': out['m_norm_mix'], 'm_norm_ffn': out['m_norm_ffn'], 'm_hgrn_w_in': out['m_hgrn_w_in'], 'm_hgrn_lower_bounds': out['m_hgrn_lower_bounds'], 'm_hgrn_out_norm': out['m_hgrn_out_norm'], 'm_hgrn_w_out': out['m_hgrn_w_out'], 'm_mla_w_in': out['m_mla_w_in'], 'm_mla_q_a_norm': out['m_mla_q_a_norm'], 'm_mla_w_q_up': out['m_mla_w_q_up'], 'm_mla_kv_a_norm': out['m_mla_kv_a_norm'], 'm_mla_w_kv_up': out['m_mla_w_kv_up'], 'm_mla_q_norm': out['m_mla_q_norm'], 'm_mla_k_norm': out['m_mla_k_norm'], 'm_mla_w_out': out['m_mla_w_out'], 'm_ffn_w_up': out['m_ffn_w_up'], 'm_ffn_conv_w': out['m_ffn_conv_w'], 'm_ffn_conv_b': out['m_ffn_conv_b'], 'm_ffn_w_down': out['m_ffn_w_down'], 'v_norm_mix': out['v_norm_mix'], 'v_norm_ffn': out['v_norm_ffn'], 'v_hgrn_w_in': out['v_hgrn_w_in'], 'v_hgrn_lower_bounds': out['v_hgrn_lower_bounds'], 'v_hgrn_out_norm': out['v_hgrn_out_norm'], 'v_hgrn_w_out': out['v_hgrn_w_out'], 'v_mla_w_in': out['v_mla_w_in'], 'v_mla_q_a_norm': out['v_mla_q_a_norm'], 'v_mla_w_q_up': out['v_mla_w_q_up'], 'v_mla_kv_a_norm': out['v_mla_kv_a_norm'], 'v_mla_w_kv_up': out['v_mla_w_kv_up'], 'v_mla_q_norm': out['v_mla_q_norm'], 'v_mla_k_norm': out['v_mla_k_norm'], 'v_mla_w_out': out['v_mla_w_out'], 'v_ffn_w_up': out['v_ffn_w_up'], 'v_ffn_conv_w': out['v_ffn_conv_w'], 'v_ffn_conv_b': out['v_ffn_conv_b'], 'v_ffn_w_down': out['v_ffn_w_down']}


def _loss(weights, diff, rest, loss_target):
    with _jax.named_scope("forward"):
        args = {**rest, TWIN_DIFF_INPUT: diff, **{k: w.astype(_WEIGHT_DTYPES[k]) for k, w in weights.items()}}
        y = _forward(args)
    with _jax.named_scope("loss_head"):
        err = _jnp.square(y.astype(_jnp.float32) - loss_target)
        return 0.5 * _jnp.sum(_jnp.mean(err, axis=-1)) if err.ndim else 0.5 * err


def _adamw(w, g, m, v):
    m = ADAM_B1 * m + (1.0 - ADAM_B1) * g
    v = ADAM_B2 * v + (1.0 - ADAM_B2) * _jnp.square(g)
    m_hat = m / (1.0 - ADAM_B1 ** ADAM_STEP)
    v_hat = v / (1.0 - ADAM_B2 ** ADAM_STEP)
    delta = -ADAM_LR * (m_hat / (_jnp.sqrt(v_hat) + ADAM_EPS) + ADAM_WD * w)
    return delta, m, v


def reference(x, positions, norm_mix, norm_ffn, hgrn_w_in, hgrn_lower_bounds, hgrn_out_norm, hgrn_w_out, mla_w_in, mla_q_a_norm, mla_w_q_up, mla_kv_a_norm, mla_w_kv_up, mla_q_norm, mla_k_norm, mla_w_out, ffn_w_up, ffn_conv_w, ffn_conv_b, ffn_w_down, loss_target, m_norm_mix, m_norm_ffn, m_hgrn_w_in, m_hgrn_lower_bounds, m_hgrn_out_norm, m_hgrn_w_out, m_mla_w_in, m_mla_q_a_norm, m_mla_w_q_up, m_mla_kv_a_norm, m_mla_w_kv_up, m_mla_q_norm, m_mla_k_norm, m_mla_w_out, m_ffn_w_up, m_ffn_conv_w, m_ffn_conv_b, m_ffn_w_down, v_norm_mix, v_norm_ffn, v_hgrn_w_in, v_hgrn_lower_bounds, v_hgrn_out_norm, v_hgrn_w_out, v_mla_w_in, v_mla_q_a_norm, v_mla_w_q_up, v_mla_kv_a_norm, v_mla_w_kv_up, v_mla_q_norm, v_mla_k_norm, v_mla_w_out, v_ffn_w_up, v_ffn_conv_w, v_ffn_conv_b, v_ffn_w_down):
    given = dict(x=x, positions=positions, norm_mix=norm_mix, norm_ffn=norm_ffn, hgrn_w_in=hgrn_w_in, hgrn_lower_bounds=hgrn_lower_bounds, hgrn_out_norm=hgrn_out_norm, hgrn_w_out=hgrn_w_out, mla_w_in=mla_w_in, mla_q_a_norm=mla_q_a_norm, mla_w_q_up=mla_w_q_up, mla_kv_a_norm=mla_kv_a_norm, mla_w_kv_up=mla_w_kv_up, mla_q_norm=mla_q_norm, mla_k_norm=mla_k_norm, mla_w_out=mla_w_out, ffn_w_up=ffn_w_up, ffn_conv_w=ffn_conv_w, ffn_conv_b=ffn_conv_b, ffn_w_down=ffn_w_down, loss_target=loss_target, m_norm_mix=m_norm_mix, m_norm_ffn=m_norm_ffn, m_hgrn_w_in=m_hgrn_w_in, m_hgrn_lower_bounds=m_hgrn_lower_bounds, m_hgrn_out_norm=m_hgrn_out_norm, m_hgrn_w_out=m_hgrn_w_out, m_mla_w_in=m_mla_w_in, m_mla_q_a_norm=m_mla_q_a_norm, m_mla_w_q_up=m_mla_w_q_up, m_mla_kv_a_norm=m_mla_kv_a_norm, m_mla_w_kv_up=m_mla_w_kv_up, m_mla_q_norm=m_mla_q_norm, m_mla_k_norm=m_mla_k_norm, m_mla_w_out=m_mla_w_out, m_ffn_w_up=m_ffn_w_up, m_ffn_conv_w=m_ffn_conv_w, m_ffn_conv_b=m_ffn_conv_b, m_ffn_w_down=m_ffn_w_down, v_norm_mix=v_norm_mix, v_norm_ffn=v_norm_ffn, v_hgrn_w_in=v_hgrn_w_in, v_hgrn_lower_bounds=v_hgrn_lower_bounds, v_hgrn_out_norm=v_hgrn_out_norm, v_hgrn_w_out=v_hgrn_w_out, v_mla_w_in=v_mla_w_in, v_mla_q_a_norm=v_mla_q_a_norm, v_mla_w_q_up=v_mla_w_q_up, v_mla_kv_a_norm=v_mla_kv_a_norm, v_mla_w_kv_up=v_mla_w_kv_up, v_mla_q_norm=v_mla_q_norm, v_mla_k_norm=v_mla_k_norm, v_mla_w_out=v_mla_w_out, v_ffn_w_up=v_ffn_w_up, v_ffn_conv_w=v_ffn_conv_w, v_ffn_conv_b=v_ffn_conv_b, v_ffn_w_down=v_ffn_w_down)
    weights = {n: given[n] for n in TWIN_WEIGHTS}
    shared = {n: given[n] for n in SHARED_INPUTS}
    per_example = {n: given[n] for n in ['x', 'positions']}
    grad_fn = _jax.value_and_grad(_loss, argnums=(0, 1))

    def one_microbatch(ex, loss_target):
        ex = dict(ex)
        diff = ex.pop(TWIN_DIFF_INPUT)
        return grad_fn(weights, diff, {**shared, **ex}, loss_target)

    if N_MICROBATCH == 1:
        loss, (grad_w, grad_x) = one_microbatch(per_example, given["loss_target"])
    else:
        def body(carry, xs):
            loss_sum, grad_sum = carry
            l_k, (gw_k, gx_k) = one_microbatch(xs[0], xs[1])
            with _jax.named_scope("update"):
                return (loss_sum + l_k, _jax.tree.map(_jnp.add, grad_sum, gw_k)), gx_k

        init = (_jnp.zeros((), _jnp.float32), _jax.tree.map(_jnp.zeros_like, weights))
        (loss, grad_w), grad_x = _jax.lax.scan(body, init, (per_example, given["loss_target"]))
    with _jax.named_scope("update"):
        delta_w, new_m, new_v = {}, {}, {}
        for n in TWIN_WEIGHTS:
            delta_w[n], new_m[n], new_v[n] = _adamw(weights[n], grad_w[n], given["m_" + n], given["v_" + n])
    return (loss, grad_x, *[grad_w[n] for n in TWIN_WEIGHTS], *[delta_w[n] for n in TWIN_WEIGHTS],
            *[new_m[n] for n in TWIN_WEIGHTS], *[new_v[n] for n in TWIN_WEIGHTS])
```

```python
import functools

import jax
import jax.numpy as jnp
from jax import lax
from jax.experimental import pallas as pl
from jax.experimental.pallas import tpu as pltpu

F32 = jnp.float32
BF16 = jnp.bfloat16

RMS_EPS = 1e-6
ROPE_THETA = 10000.0
HEAD = 128
ROPE = 64
QK_HEAD = HEAD + ROPE
QK_PAD = 256
CHUNK = 64
SUB = 16
EXP_CLAMP = 60.0

ADAM_LR = 0.001
ADAM_B1 = 0.9
ADAM_B2 = 0.999
ADAM_EPS = 1e-08
ADAM_WD = 0.01
ADAM_STEP = 10

N_DEV = 8
LANES = 128
PACK_COLS = 1024
BIG_ROWS = 256
V7X_VMEM_LIMIT = 56 * 1024 * 1024

HI = lax.Precision.HIGHEST


def _params(sem):
    return pltpu.CompilerParams(dimension_semantics=sem, vmem_limit_bytes=V7X_VMEM_LIMIT)


def _blk(n, cap):
    if n <= cap:
        return n
    d = (cap // LANES) * LANES
    while d >= LANES:
        if n % d == 0:
            return d
        d -= LANES
    raise ValueError(f"no lane-aligned block for {n} under {cap}")


def _sigmoid(x):
    return jax.nn.sigmoid(x)


def _dot(a, b, dims, precision=None):
    return lax.dot_general(a, b, (dims, ((), ())), preferred_element_type=F32, precision=precision)


NN = ((1,), (0,))
NT = ((1,), (1,))
TN = ((0,), (0,))


def _mm(a, b, *, a_fmt="mk", b_fmt="kn", out_fmt="mn", res=None, out_dtype=F32, bm=512, bn=1024, bk=1024,
        name):
    if a_fmt == "mk":
        M, K = a.shape
    elif a_fmt == "km":
        K, M = a.shape
    elif a_fmt == "kmb":
        nb, K, B = a.shape
        M = nb * B
    else:
        nb, M, B = a.shape
        K = nb * B
    if b_fmt == "kn":
        Kb, N = b.shape
    else:
        N, Kb = b.shape
    assert K == Kb, (a.shape, b.shape, a_fmt, b_fmt)
    bm = B if a_fmt == "kmb" else _blk(M, bm)
    bn = _blk(N, bn)
    bk = _blk(B, bk) if a_fmt == "mkb" else _blk(K, bk)
    nm, nn, nk = M // bm, N // bn, K // bk

    if a_fmt == "mk":
        a_spec = pl.BlockSpec((bm, bk), lambda i, j, k: (i, k))
        a_dim = 1
    elif a_fmt == "km":
        a_spec = pl.BlockSpec((bk, bm), lambda i, j, k: (k, i))
        a_dim = 0
    elif a_fmt == "kmb":
        a_spec = pl.BlockSpec((None, bk, bm), lambda i, j, k: (i, k, 0))
        a_dim = 0
    else:
        per = B // bk
        a_spec = pl.BlockSpec((None, bm, bk), lambda i, j, k: (k // per, i, k % per))
        a_dim = 1
    if b_fmt == "kn":
        b_spec = pl.BlockSpec((bk, bn), lambda i, j, k: (k, j))
        b_dim = 0
    else:
        b_spec = pl.BlockSpec((bn, bk), lambda i, j, k: (j, k))
        b_dim = 1
    if out_fmt == "mn":
        o_spec = pl.BlockSpec((bm, bn), lambda i, j, k: (i, j))
        o_shape = jax.ShapeDtypeStruct((M, N), out_dtype)
    else:
        o_spec = pl.BlockSpec((None, bm, bn), lambda i, j, k: (j, i, 0))
        o_shape = jax.ShapeDtypeStruct((nn, M, bn), out_dtype)
    in_specs = [a_spec, b_spec]
    args = [a, b]
    if res is not None:
        assert out_fmt == "mn"
        in_specs.append(pl.BlockSpec((bm, bn), lambda i, j, k: (i, j)))
        args.append(res)
    dims = ((a_dim,), (b_dim,))
    has_res = res is not None

    def body(*refs):
        if has_res:
            a_ref, b_ref, r_ref, o_ref, acc_ref = refs
        else:
            a_ref, b_ref, o_ref, acc_ref = refs
        k = pl.program_id(2)
        part = _dot(a_ref[...].astype(BF16), b_ref[...].astype(BF16), dims)

        @pl.when(k == 0)
        def _():
            acc_ref[...] = part

        @pl.when(k > 0)
        def _():
            acc_ref[...] += part

        @pl.when(k == nk - 1)
        def _():
            out = acc_ref[...]
            if has_res:
                out = out + r_ref[...]
            o_ref[...] = out.astype(o_ref.dtype)

    return pl.pallas_call(
        body, name=name, grid=(nm, nn, nk), in_specs=in_specs, out_specs=o_spec, out_shape=o_shape,
        scratch_shapes=[pltpu.VMEM((bm, bn), F32)],
        compiler_params=_params(("parallel", "parallel", "arbitrary")),
    )(*args)


def _rmsnorm_fwd(x, gain, *, name, tm=512):
    T, D = x.shape
    tm = min(tm, T)

    def body(x_ref, g_ref, o_ref):
        xv = x_ref[...]
        r = lax.rsqrt(jnp.mean(xv * xv, axis=-1, keepdims=True) + RMS_EPS)
        o_ref[...] = ((xv * r) * g_ref[...]).astype(o_ref.dtype)

    return pl.pallas_call(
        body, name=name, grid=(T // tm,),
        in_specs=[pl.BlockSpec((tm, D), lambda i: (i, 0)), pl.BlockSpec((1, D), lambda i: (0, 0))],
        out_specs=pl.BlockSpec((tm, D), lambda i: (i, 0)),
        out_shape=jax.ShapeDtypeStruct((T, D), BF16),
        compiler_params=_params(("parallel",)),
    )(x, gain.reshape(1, D))


def _rmsnorm_bwd(x, gain, dh, dres, *, name, tm=512):
    T, D = x.shape
    tm = min(tm, T)

    def body(x_ref, g_ref, dh_ref, dr_ref, dx_ref, dg_ref):
        i = pl.program_id(0)
        xv = x_ref[...]
        r = lax.rsqrt(jnp.mean(xv * xv, axis=-1, keepdims=True) + RMS_EPS)
        n = xv * r
        dy = dh_ref[...].astype(F32)
        dn = dy * g_ref[...]
        dx_ref[...] = dr_ref[...] + r * (dn - n * jnp.mean(dn * n, axis=-1, keepdims=True))
        part = jnp.sum(dy * n, axis=0, keepdims=True)

        @pl.when(i == 0)
        def _():
            dg_ref[...] = part

        @pl.when(i > 0)
        def _():
            dg_ref[...] += part

    return pl.pallas_call(
        body, name=name, grid=(T // tm,),
        in_specs=[pl.BlockSpec((tm, D), lambda i: (i, 0)), pl.BlockSpec((1, D), lambda i: (0, 0)),
                  pl.BlockSpec((tm, D), lambda i: (i, 0)), pl.BlockSpec((tm, D), lambda i: (i, 0))],
        out_specs=[pl.BlockSpec((tm, D), lambda i: (i, 0)), pl.BlockSpec((1, D), lambda i: (0, 0))],
        out_shape=[jax.ShapeDtypeStruct((T, D), F32), jax.ShapeDtypeStruct((1, D), F32)],
        compiler_params=_params(("arbitrary",)),
    )(x, gain.reshape(1, D), dh, dres)


def _loss_head(y, target, *, name, tm=512):
    T, D = y.shape
    tm = min(tm, T)

    def body(y_ref, t_ref, dy_ref, l_ref):
        i = pl.program_id(0)
        e = y_ref[...] - t_ref[...]
        dy_ref[...] = e * (1.0 / D)
        s = 0.5 * jnp.sum(jnp.mean(e * e, axis=-1, keepdims=True), axis=0, keepdims=True)
        part = jnp.broadcast_to(s, (1, LANES))

        @pl.when(i == 0)
        def _():
            l_ref[...] = part

        @pl.when(i > 0)
        def _():
            l_ref[...] += part

    return pl.pallas_call(
        body, name=name, grid=(T // tm,),
        in_specs=[pl.BlockSpec((tm, D), lambda i: (i, 0)), pl.BlockSpec((tm, D), lambda i: (i, 0))],
        out_specs=[pl.BlockSpec((tm, D), lambda i: (i, 0)), pl.BlockSpec((1, LANES), lambda i: (0, 0))],
        out_shape=[jax.ShapeDtypeStruct((T, D), F32), jax.ShapeDtypeStruct((1, LANES), F32)],
        compiler_params=_params(("arbitrary",)),
    )(y, target)


def _hgrn_selectors():
    t = jnp.arange(CHUNK)[:, None]
    s = jnp.arange(CHUNK)[None, :]
    mats = [s <= t, s < (t // SUB) * SUB]
    for i in range(1, CHUNK // SUB):
        mats.append(jnp.broadcast_to(s < i * SUB, (CHUNK, CHUNK)))
    mats.append(jnp.ones((CHUNK, CHUNK), bool))
    sel = jnp.concatenate([m.astype(F32) for m in mats], axis=0)
    rev = (s >= t).astype(F32)
    return sel, rev


def _hgrn_gates(p, lb, D):
    qpre, fpre, iv, gpre = p[:, 0:D], p[:, D:2 * D], p[:, 2 * D:3 * D], p[:, 3 * D:4 * D]
    sig = _sigmoid(fpre)
    forget = lb + (1.0 - lb) * sig
    key = 1.0 - forget
    logf = jnp.log(forget)
    sq = _sigmoid(qpre)
    qs = qpre * sq
    return qpre, sq, qs, sig, forget, key, logf, iv, gpre


def _hgrn_scores(qt, kh, gh, rsel, row, col):
    nsub = CHUNK // SUB
    blocks, kts = [], []
    for i in range(nsub):
        ri = rsel[i]
        kt = kh * jnp.exp(jnp.minimum(ri - gh, EXP_CLAMP)) if ri is not None else \
            kh * jnp.exp(jnp.minimum(-gh, EXP_CLAMP))
        kts.append(kt)
        blocks.append(_dot(qt[i * SUB:(i + 1) * SUB].astype(BF16), kt.astype(BF16), NT))
    a = jnp.concatenate(blocks, axis=0)
    return jnp.where(col <= row, a, 0.0), kts


def _hgrn_fwd(p, lb, gn, *, name):
    T, D4 = p.shape
    D = D4 // 4
    H = D // HEAD
    nc = T // CHUNK
    sel, _ = _hgrn_selectors()
    nsel = sel.shape[0]
    nsub = CHUNK // SUB

    def body(p_ref, lb_ref, gn_ref, sel_ref, og_ref, o_ref, s0_ref, st_ref):
        c = pl.program_id(0)

        @pl.when(c == 0)
        def _():
            st_ref[...] = jnp.zeros_like(st_ref)

        pv = p_ref[...]
        _, _, qs, _, _, key, logf, iv, gpre = _hgrn_gates(pv, lb_ref[...], D)
        cums = _dot(sel_ref[...], logf, NN, precision=HI)
        g = cums[0:CHUNK]
        rrow = cums[CHUNK:2 * CHUNK]
        rsel_all = [None] + [cums[(1 + i) * CHUNK:(2 + i) * CHUNK] for i in range(1, nsub)]
        gl = cums[nsel - CHUNK:nsel]
        s0_ref[...] = st_ref[...]
        row = lax.broadcasted_iota(jnp.int32, (CHUNK, CHUNK), 0)
        col = lax.broadcasted_iota(jnp.int32, (CHUNK, CHUNK), 1)
        gnv = gn_ref[...]
        for h in range(H):
            sl = slice(h * HEAD, (h + 1) * HEAD)
            gh, qh, kh, vh = g[:, sl], qs[:, sl], key[:, sl], iv[:, sl]
            st = st_ref[sl, :]
            o = _dot((qh * jnp.exp(gh)).astype(BF16), st.astype(BF16), NT)
            qt = qh * jnp.exp(gh - rrow[:, sl])
            a, _ = _hgrn_scores(qt, kh, gh, [None if r is None else r[:, sl] for r in rsel_all], row, col)
            o = o + _dot(a.astype(BF16), vh.astype(BF16), NN)
            glh = gl[:, sl]
            kd = kh * jnp.exp(glh - gh)
            gl1 = jnp.max(glh, axis=0, keepdims=True)
            st_ref[sl, :] = st * jnp.exp(gl1) + _dot(vh.astype(BF16), kd.astype(BF16), TN)
            o_ref[:, sl] = o
            r = lax.rsqrt(jnp.mean(o * o, axis=-1, keepdims=True) + RMS_EPS)
            gp = gpre[:, sl]
            og_ref[:, sl] = (((o * r) * gnv) * (gp * _sigmoid(gp))).astype(og_ref.dtype)

    return pl.pallas_call(
        body, name=name, grid=(nc,),
        in_specs=[pl.BlockSpec((CHUNK, D4), lambda c: (c, 0)), pl.BlockSpec((1, D), lambda c: (0, 0)),
                  pl.BlockSpec((1, HEAD), lambda c: (0, 0)), pl.BlockSpec((nsel, CHUNK), lambda c: (0, 0))],
        out_specs=[pl.BlockSpec((CHUNK, D), lambda c: (c, 0)), pl.BlockSpec((CHUNK, D), lambda c: (c, 0)),
                   pl.BlockSpec((None, D, HEAD), lambda c: (c, 0, 0))],
        out_shape=[jax.ShapeDtypeStruct((T, D), BF16), jax.ShapeDtypeStruct((T, D), F32),
                   jax.ShapeDtypeStruct((nc, D, HEAD), F32)],
        scratch_shapes=[pltpu.VMEM((D, HEAD), F32)],
        compiler_params=_params(("arbitrary",)),
    )(p, lb.reshape(1, D), gn.reshape(1, HEAD), sel)


def _hgrn_bwd(p, lb, gn, s0, o_saved, dog, *, name):
    T, D4 = p.shape
    D = D4 // 4
    H = D // HEAD
    nc = T // CHUNK
    sel, rev = _hgrn_selectors()
    nsel = sel.shape[0]
    nsub = CHUNK // SUB

    def body(p_ref, lb_ref, gn_ref, sel_ref, rev_ref, s0_ref, s1_ref, o_ref, dog_ref,
             dp_ref, dlb_ref, dgn_ref, dst_ref):
        c = pl.program_id(0)

        @pl.when(c == 0)
        def _():
            dst_ref[...] = jnp.zeros_like(dst_ref)
            dlb_ref[...] = jnp.zeros_like(dlb_ref)
            dgn_ref[...] = jnp.zeros_like(dgn_ref)

        pv = p_ref[...]
        lbv = lb_ref[...]
        qpre, sq, qs, sig, forget, key, logf, iv, gpre = _hgrn_gates(pv, lbv, D)
        cums = _dot(sel_ref[...], logf, NN, precision=HI)
        g = cums[0:CHUNK]
        rrow = cums[CHUNK:2 * CHUNK]
        rsel_all = [None] + [cums[(1 + i) * CHUNK:(2 + i) * CHUNK] for i in range(1, nsub)]
        gl = cums[nsel - CHUNK:nsel]
        row = lax.broadcasted_iota(jnp.int32, (CHUNK, CHUNK), 0)
        col = lax.broadcasted_iota(jnp.int32, (CHUNK, CHUNK), 1)
        causal = col <= row
        gnv = gn_ref[...]
        dgn_acc = jnp.zeros((1, HEAD), F32)
        for h in range(H):
            sl = slice(h * HEAD, (h + 1) * HEAD)
            gh, qh, kh, vh = g[:, sl], qs[:, sl], key[:, sl], iv[:, sl]
            glh = gl[:, sl]
            gl1 = jnp.max(glh, axis=0, keepdims=True)
            st0 = s0_ref[sl, :]
            st1 = s1_ref[sl, :]
            dst = dst_ref[sl, :]
            o = o_ref[:, sl]
            r = lax.rsqrt(jnp.mean(o * o, axis=-1, keepdims=True) + RMS_EPS)
            n = o * r
            gp = gpre[:, sl]
            sg = _sigmoid(gp)
            dog_h = dog_ref[:, sl]
            d_on = dog_h * (gp * sg)
            dgpre = dog_h * (n * gnv) * (sg * (1.0 + gp * (1.0 - sg)))
            dgn_acc = dgn_acc + jnp.sum(d_on * n, axis=0, keepdims=True)
            dn = d_on * gnv
            do = r * (dn - n * jnp.mean(dn * n, axis=-1, keepdims=True))
            dob = do.astype(BF16)
            eg = jnp.exp(gh)
            qg = qh * eg
            eqr = jnp.exp(gh - rrow[:, sl])
            qt = qh * eqr
            rsel = [None if rr is None else rr[:, sl] for rr in rsel_all]
            a, kts = _hgrn_scores(qt, kh, gh, rsel, row, col)
            ekd = jnp.exp(glh - gh)
            kd = kh * ekd
            vb = vh.astype(BF16)
            dq = _dot(dob, st0.astype(BF16), NN) * eg
            da = jnp.where(causal, _dot(dob, vb, NT), 0.0)
            dab = da.astype(BF16)
            dqt_blocks = []
            dk = _dot(vb, dst.astype(BF16), NN) * ekd
            for i in range(nsub):
                rs = slice(i * SUB, (i + 1) * SUB)
                ktb = kts[i].astype(BF16)
                dqt_blocks.append(_dot(dab[rs], ktb, NN))
                dkt = _dot(dab[rs], qt[rs].astype(BF16), TN)
                ei = jnp.exp(jnp.minimum((rsel[i] if rsel[i] is not None else 0.0) - gh, EXP_CLAMP))
                dk = dk + dkt * ei
            dq = dq + jnp.concatenate(dqt_blocks, axis=0) * eqr
            dv = _dot(a.astype(BF16), dob, TN) + _dot(kd.astype(BF16), dst.astype(BF16), NT)
            dst_ref[sl, :] = dst * jnp.exp(gl1) + _dot(dob, qg.astype(BF16), TN)
            term = jnp.sum(dst * st1, axis=0, keepdims=True)
            dg = qh * dq - kh * dk
            dlogf = _dot(rev_ref[...], dg, NN, precision=HI) + term
            fg = forget[:, sl]
            sgf = sig[:, sl]
            lbh = lbv[:, sl]
            dforget = dlogf / fg - dk
            dfpre = dforget * (1.0 - lbh) * (sgf * (1.0 - sgf))
            dlb_ref[:, sl] += jnp.sum(dforget * (1.0 - sgf), axis=0, keepdims=True)
            sqh = sq[:, sl]
            dqpre = dq * (sqh * (1.0 + qpre[:, sl] * (1.0 - sqh)))
            dp_ref[:, h * HEAD:(h + 1) * HEAD] = dqpre.astype(dp_ref.dtype)
            dp_ref[:, D + h * HEAD:D + (h + 1) * HEAD] = dfpre.astype(dp_ref.dtype)
            dp_ref[:, 2 * D + h * HEAD:2 * D + (h + 1) * HEAD] = dv.astype(dp_ref.dtype)
            dp_ref[:, 3 * D + h * HEAD:3 * D + (h + 1) * HEAD] = dgpre.astype(dp_ref.dtype)
        dgn_ref[...] += dgn_acc

    rc = lambda c: nc - 1 - c
    return pl.pallas_call(
        body, name=name, grid=(nc,),
        in_specs=[pl.BlockSpec((CHUNK, D4), lambda c: (rc(c), 0)), pl.BlockSpec((1, D), lambda c: (0, 0)),
                  pl.BlockSpec((1, HEAD), lambda c: (0, 0)), pl.BlockSpec((nsel, CHUNK), lambda c: (0, 0)),
                  pl.BlockSpec((CHUNK, CHUNK), lambda c: (0, 0)),
                  pl.BlockSpec((None, D, HEAD), lambda c: (rc(c), 0, 0)),
                  pl.BlockSpec((None, D, HEAD), lambda c: (jnp.minimum(rc(c) + 1, nc - 1), 0, 0)),
                  pl.BlockSpec((CHUNK, D), lambda c: (rc(c), 0)), pl.BlockSpec((CHUNK, D), lambda c: (rc(c), 0))],
        out_specs=[pl.BlockSpec((CHUNK, D4), lambda c: (rc(c), 0)), pl.BlockSpec((1, D), lambda c: (0, 0)),
                   pl.BlockSpec((1, HEAD), lambda c: (0, 0))],
        out_shape=[jax.ShapeDtypeStruct((T, D4), BF16), jax.ShapeDtypeStruct((1, D), F32),
                   jax.ShapeDtypeStruct((1, HEAD), F32)],
        scratch_shapes=[pltpu.VMEM((D, HEAD), F32)],
        compiler_params=_params(("arbitrary",)),
    )(p, lb.reshape(1, D), gn.reshape(1, HEAD), sel, rev, s0, s0, o_saved, dog)


def _rope_tables(positions):
    inv_freq = ROPE_THETA ** (-jnp.arange(0, ROPE, 2, dtype=F32) / ROPE)
    ang = positions.astype(F32)[:, None] * inv_freq
    cos, sin = jnp.cos(ang), jnp.sin(ang)
    z = jnp.zeros_like(cos)
    ctab = jnp.concatenate([cos, cos, z, z], axis=-1)
    s1 = jnp.concatenate([-sin, z, z, z], axis=-1)
    s2 = jnp.concatenate([z, sin, z, z], axis=-1)
    return ctab, s1, s2


def _rope(z, ct, s1, s2):
    return z * ct + pltpu.roll(z, 96, 1) * s1 + pltpu.roll(z, 32, 1) * s2


def _rope_t(d, ct, s1, s2):
    return d * ct + pltpu.roll(d * s1, 32, 1) + pltpu.roll(d * s2, 96, 1)


def _mla_prep_fwd(c, wq, wkv, ga_q, ga_kv, gq, gk, ct, s1, s2, *, name, tm=256):
    T, CW = c.shape
    R = (CW - LANES) // 2
    H = wq.shape[1] // QK_PAD
    tm = min(tm, T)

    def body(c_ref, wq_ref, wkv_ref, gaq_ref, gakv_ref, gq_ref, gk_ref, ct_ref, s1_ref, s2_ref,
             q_ref, k_ref, v_ref):
        cv = c_ref[...]
        cq, ckv, kr = cv[:, 0:R], cv[:, R:2 * R], cv[:, 2 * R:2 * R + LANES]
        rq = lax.rsqrt(jnp.mean(cq * cq, axis=-1, keepdims=True) + RMS_EPS)
        cqn = ((cq * rq) * gaq_ref[...]).astype(BF16)
        rk = lax.rsqrt(jnp.mean(ckv * ckv, axis=-1, keepdims=True) + RMS_EPS)
        ckvn = ((ckv * rk) * gakv_ref[...]).astype(BF16)
        qp = _dot(cqn, wq_ref[...], NN)
        kvp = _dot(ckvn, wkv_ref[...], NN)
        ctv, s1v, s2v = ct_ref[...], s1_ref[...], s2_ref[...]
        gqv, gkv = gq_ref[...], gk_ref[...]
        krs = jnp.sum(kr * kr, axis=-1, keepdims=True)
        for h in range(H):
            b = h * QK_PAD
            qn, qr = qp[:, b:b + HEAD], qp[:, b + HEAD:b + QK_PAD]
            ss = jnp.sum(qn * qn, axis=-1, keepdims=True) + jnp.sum(qr * qr, axis=-1, keepdims=True)
            rr = lax.rsqrt(ss * (1.0 / QK_HEAD) + RMS_EPS)
            q_ref[:, b:b + HEAD] = ((qn * rr) * gqv[:, 0:HEAD]).astype(q_ref.dtype)
            q_ref[:, b + HEAD:b + QK_PAD] = _rope((qr * rr) * gqv[:, HEAD:QK_PAD], ctv, s1v, s2v).astype(q_ref.dtype)
            kn, vv = kvp[:, b:b + HEAD], kvp[:, b + HEAD:b + QK_PAD]
            ssk = jnp.sum(kn * kn, axis=-1, keepdims=True) + krs
            rrk = lax.rsqrt(ssk * (1.0 / QK_HEAD) + RMS_EPS)
            k_ref[:, b:b + HEAD] = ((kn * rrk) * gkv[:, 0:HEAD]).astype(k_ref.dtype)
            k_ref[:, b + HEAD:b + QK_PAD] = _rope((kr * rrk) * gkv[:, HEAD:QK_PAD], ctv, s1v, s2v).astype(k_ref.dtype)
            v_ref[:, h * HEAD:(h + 1) * HEAD] = vv.astype(v_ref.dtype)

    full = lambda shape: pl.BlockSpec(shape, lambda i: (0, 0))
    tok = lambda w: pl.BlockSpec((tm, w), lambda i: (i, 0))
    return pl.pallas_call(
        body, name=name, grid=(T // tm,),
        in_specs=[tok(CW), full(wq.shape), full(wkv.shape), full((1, R)), full((1, R)), full((1, QK_PAD)),
                  full((1, QK_PAD)), tok(LANES), tok(LANES), tok(LANES)],
        out_specs=[tok(H * QK_PAD), tok(H * QK_PAD), tok(H * HEAD)],
        out_shape=[jax.ShapeDtypeStruct((T, H * QK_PAD), BF16), jax.ShapeDtypeStruct((T, H * QK_PAD), BF16),
                   jax.ShapeDtypeStruct((T, H * HEAD), BF16)],
        compiler_params=_params(("parallel",)),
    )(c, wq, wkv, ga_q.reshape(1, R), ga_kv.reshape(1, R), gq, gk, ct, s1, s2)


def _mla_prep_bwd(c, wq, wkv, ga_q, ga_kv, gq, gk, ct, s1, s2, dq, dk, dv, *, name, tm=256):
    T, CW = c.shape
    R = (CW - LANES) // 2
    H = wq.shape[1] // QK_PAD
    tm = min(tm, T)

    def body(c_ref, wq_ref, wkv_ref, gaq_ref, gakv_ref, gq_ref, gk_ref, ct_ref, s1_ref, s2_ref,
             dq_ref, dk_ref, dv_ref,
             dc_ref, dwq_ref, dwkv_ref, dgaq_ref, dgakv_ref, dgq_ref, dgk_ref, dqp_ref, dkvp_ref):
        i = pl.program_id(0)

        @pl.when(i == 0)
        def _():
            for ref in (dwq_ref, dwkv_ref, dgaq_ref, dgakv_ref, dgq_ref, dgk_ref):
                ref[...] = jnp.zeros_like(ref)

        cv = c_ref[...]
        cq, ckv, kr = cv[:, 0:R], cv[:, R:2 * R], cv[:, 2 * R:2 * R + LANES]
        rq = lax.rsqrt(jnp.mean(cq * cq, axis=-1, keepdims=True) + RMS_EPS)
        nq = cq * rq
        cqn = (nq * gaq_ref[...]).astype(BF16)
        rk = lax.rsqrt(jnp.mean(ckv * ckv, axis=-1, keepdims=True) + RMS_EPS)
        nkv = ckv * rk
        ckvn = (nkv * gakv_ref[...]).astype(BF16)
        qp = _dot(cqn, wq_ref[...], NN)
        kvp = _dot(ckvn, wkv_ref[...], NN)
        ctv, s1v, s2v = ct_ref[...], s1_ref[...], s2_ref[...]
        gqv, gkv = gq_ref[...], gk_ref[...]
        krs = jnp.sum(kr * kr, axis=-1, keepdims=True)
        dkr = jnp.zeros((tm, LANES), F32)
        dgq_n = jnp.zeros((1, HEAD), F32)
        dgq_r = jnp.zeros((1, HEAD), F32)
        dgk_n = jnp.zeros((1, HEAD), F32)
        dgk_r = jnp.zeros((1, HEAD), F32)
        for h in range(H):
            b = h * QK_PAD
            qn, qr = qp[:, b:b + HEAD], qp[:, b + HEAD:b + QK_PAD]
            ss = jnp.sum(qn * qn, axis=-1, keepdims=True) + jnp.sum(qr * qr, axis=-1, keepdims=True)
            rr = lax.rsqrt(ss * (1.0 / QK_HEAD) + RMS_EPS)
            un, ur = qn * rr, qr * rr
            dzn = dq_ref[:, b:b + HEAD]
            dzr = _rope_t(dq_ref[:, b + HEAD:b + QK_PAD], ctv, s1v, s2v)
            dgq_n = dgq_n + jnp.sum(dzn * un, axis=0, keepdims=True)
            dgq_r = dgq_r + jnp.sum(dzr * ur, axis=0, keepdims=True)
            dun, dur = dzn * gqv[:, 0:HEAD], dzr * gqv[:, HEAD:QK_PAD]
            m = (jnp.sum(dun * un, axis=-1, keepdims=True) + jnp.sum(dur * ur, axis=-1, keepdims=True)) \
                * (1.0 / QK_HEAD)
            dqp_ref[:, b:b + HEAD] = (rr * (dun - un * m)).astype(BF16)
            dqp_ref[:, b + HEAD:b + QK_PAD] = (rr * (dur - ur * m)).astype(BF16)
            kn = kvp[:, b:b + HEAD]
            ssk = jnp.sum(kn * kn, axis=-1, keepdims=True) + krs
            rrk = lax.rsqrt(ssk * (1.0 / QK_HEAD) + RMS_EPS)
            vn, vr = kn * rrk, kr * rrk
            dyn = dk_ref[:, b:b + HEAD]
            dyr = _rope_t(dk_ref[:, b + HEAD:b + QK_PAD], ctv, s1v, s2v)
            dgk_n = dgk_n + jnp.sum(dyn * vn, axis=0, keepdims=True)
            dgk_r = dgk_r + jnp.sum(dyr * vr, axis=0, keepdims=True)
            dvn, dvr = dyn * gkv[:, 0:HEAD], dyr * gkv[:, HEAD:QK_PAD]
            mk = (jnp.sum(dvn * vn, axis=-1, keepdims=True) + jnp.sum(dvr * vr, axis=-1, keepdims=True)) \
                * (1.0 / QK_HEAD)
            dkvp_ref[:, b:b + HEAD] = (rrk * (dvn - vn * mk)).astype(BF16)
            dkr = dkr + rrk * (dvr - vr * mk)
            dkvp_ref[:, b + HEAD:b + QK_PAD] = dv_ref[:, h * HEAD:(h + 1) * HEAD].astype(BF16)
        dgq_ref[:, 0:HEAD] += dgq_n
        dgq_ref[:, HEAD:QK_PAD] += dgq_r
        dgk_ref[:, 0:HEAD] += dgk_n
        dgk_ref[:, HEAD:QK_PAD] += dgk_r
        dqp = dqp_ref[...]
        dkvp = dkvp_ref[...]
        dwq_ref[...] += _dot(cqn, dqp, TN)
        dwkv_ref[...] += _dot(ckvn, dkvp, TN)
        dcqn = _dot(dqp, wq_ref[...], NT)
        dckvn = _dot(dkvp, wkv_ref[...], NT)
        dgaq_ref[...] += jnp.sum(dcqn * nq, axis=0, keepdims=True)
        dgakv_ref[...] += jnp.sum(dckvn * nkv, axis=0, keepdims=True)
        dnq = dcqn * gaq_ref[...]
        dnkv = dckvn * gakv_ref[...]
        dc_ref[:, 0:R] = (rq * (dnq - nq * jnp.mean(dnq * nq, axis=-1, keepdims=True))).astype(dc_ref.dtype)
        dc_ref[:, R:2 * R] = (rk * (dnkv - nkv * jnp.mean(dnkv * nkv, axis=-1, keepdims=True))).astype(dc_ref.dtype)
        dc_ref[:, 2 * R:2 * R + LANES] = dkr.astype(dc_ref.dtype)

    full = lambda shape: pl.BlockSpec(shape, lambda i: (0, 0))
    tok = lambda w: pl.BlockSpec((tm, w), lambda i: (i, 0))
    return pl.pallas_call(
        body, name=name, grid=(T // tm,),
        in_specs=[tok(CW), full(wq.shape), full(wkv.shape), full((1, R)), full((1, R)), full((1, QK_PAD)),
                  full((1, QK_PAD)), tok(LANES), tok(LANES), tok(LANES),
                  tok(H * QK_PAD), tok(H * QK_PAD), tok(H * HEAD)],
        out_specs=[tok(CW), full(wq.shape), full(wkv.shape), full((1, R)), full((1, R)), full((1, QK_PAD)),
                   full((1, QK_PAD))],
        out_shape=[jax.ShapeDtypeStruct((T, CW), BF16), jax.ShapeDtypeStruct(wq.shape, F32),
                   jax.ShapeDtypeStruct(wkv.shape, F32), jax.ShapeDtypeStruct((1, R), F32),
                   jax.ShapeDtypeStruct((1, R), F32), jax.ShapeDtypeStruct((1, QK_PAD), F32),
                   jax.ShapeDtypeStruct((1, QK_PAD), F32)],
        scratch_shapes=[pltpu.VMEM((tm, H * QK_PAD), BF16), pltpu.VMEM((tm, H * QK_PAD), BF16)],
        compiler_params=_params(("arbitrary",)),
    )(c, wq, wkv, ga_q.reshape(1, R), ga_kv.reshape(1, R), gq, gk, ct, s1, s2, dq, dk, dv)


NEG = -1e30


def _attn_fwd(q, k, v, *, name, tb=512):
    T = q.shape[0]
    H = q.shape[1] // QK_PAD
    tb = min(tb, T)
    nq = T // tb
    scale = QK_HEAD ** -0.5

    def body(q_ref, k_ref, v_ref, ot_ref, lse_ref, m_ref, l_ref, acc_ref):
        i = pl.program_id(1)
        qb = q_ref[...]
        m_ref[...] = jnp.full_like(m_ref, NEG)
        l_ref[...] = jnp.zeros_like(l_ref)
        acc_ref[...] = jnp.zeros_like(acc_ref)

        def step(j, masked):
            off = pl.multiple_of(j * tb, tb)
            kb = k_ref[pl.ds(off, tb), :]
            vb = v_ref[pl.ds(off, tb), :]
            st = _dot(kb, qb, NT) * scale
            if masked:
                kpos = lax.broadcasted_iota(jnp.int32, (tb, tb), 0)
                qpos = lax.broadcasted_iota(jnp.int32, (tb, tb), 1)
                st = jnp.where(kpos <= qpos, st, NEG)
            m_old = m_ref[...]
            m_new = jnp.maximum(m_old, jnp.max(st, axis=0, keepdims=True))
            alpha = jnp.exp(m_old - m_new)
            pt = jnp.exp(st - m_new)
            l_ref[...] = l_ref[...] * alpha + jnp.sum(pt, axis=0, keepdims=True)
            acc_ref[...] = acc_ref[...] * alpha + _dot(vb, pt.astype(BF16), TN)
            m_ref[...] = m_new

        def loop_body(j, carry):
            step(j, False)
            return carry

        lax.fori_loop(0, i, loop_body, 0)
        step(i, True)
        l = l_ref[...]
        ot_ref[...] = (acc_ref[...] / l).astype(ot_ref.dtype)
        lse_ref[...] = m_ref[...] + jnp.log(l)

    return pl.pallas_call(
        body, name=name, grid=(H, nq),
        in_specs=[pl.BlockSpec((tb, QK_PAD), lambda h, i: (i, h)), pl.BlockSpec((T, QK_PAD), lambda h, i: (0, h)),
                  pl.BlockSpec((T, HEAD), lambda h, i: (0, h))],
        out_specs=[pl.BlockSpec((None, HEAD, tb), lambda h, i: (i, h, 0)),
                   pl.BlockSpec((None, None, 1, tb), lambda h, i: (h, i, 0, 0))],
        out_shape=[jax.ShapeDtypeStruct((nq, H * HEAD, tb), BF16), jax.ShapeDtypeStruct((H, nq, 1, tb), F32)],
        scratch_shapes=[pltpu.VMEM((1, tb), F32), pltpu.VMEM((1, tb), F32), pltpu.VMEM((HEAD, tb), F32)],
        compiler_params=_params(("parallel", "arbitrary")),
    )(q, k, v)


def _attn_bwd(q, k, v, ot, lse, dot_, *, name):
    T = q.shape[0]
    H = q.shape[1] // QK_PAD
    nq, _, tb = ot.shape
    scale = QK_HEAD ** -0.5

    def body(q_ref, k_ref, v_ref, ot_ref, lse_ref, dot_ref, dq_ref, dk_ref, dv_ref, dka_ref, dva_ref):
        j = pl.program_id(1)

        @pl.when(j == 0)
        def _():
            dq_ref[...] = jnp.zeros_like(dq_ref)

        kb = k_ref[...]
        vb = v_ref[...]
        dka_ref[...] = jnp.zeros_like(dka_ref)
        dva_ref[...] = jnp.zeros_like(dva_ref)

        def step(i, masked):
            off = pl.multiple_of(i * tb, tb)
            qb = q_ref[pl.ds(off, tb), :]
            dob = dot_ref[i]
            ob = ot_ref[i]
            st = _dot(kb, qb, NT) * scale
            if masked:
                kpos = lax.broadcasted_iota(jnp.int32, (tb, tb), 0)
                qpos = lax.broadcasted_iota(jnp.int32, (tb, tb), 1)
                st = jnp.where(kpos <= qpos, st, NEG)
            pt = jnp.exp(st - lse_ref[i])
            dpt = _dot(vb, dob, NN)
            delta = jnp.sum(dob.astype(F32) * ob.astype(F32), axis=0, keepdims=True)
            dst = (pt * (dpt - delta) * scale).astype(BF16)
            dva_ref[...] += _dot(pt.astype(BF16), dob, NT)
            dka_ref[...] += _dot(dst, qb, NN)
            dq_ref[pl.ds(off, tb), :] += _dot(dst, kb, TN)

        step(j, True)

        def loop_body(i, carry):
            step(i, False)
            return carry

        lax.fori_loop(j + 1, nq, loop_body, 0)
        dk_ref[...] = dka_ref[...]
        dv_ref[...] = dva_ref[...]

    return pl.pallas_call(
        body, name=name, grid=(H, nq),
        in_specs=[pl.BlockSpec((T, QK_PAD), lambda h, j: (0, h)), pl.BlockSpec((tb, QK_PAD), lambda h, j: (j, h)),
                  pl.BlockSpec((tb, HEAD), lambda h, j: (j, h)),
                  pl.BlockSpec((nq, HEAD, tb), lambda h, j: (0, h, 0)),
                  pl.BlockSpec((None, nq, 1, tb), lambda h, j: (h, 0, 0, 0)),
                  pl.BlockSpec((nq, HEAD, tb), lambda h, j: (0, h, 0))],
        out_specs=[pl.BlockSpec((T, QK_PAD), lambda h, j: (0, h)), pl.BlockSpec((tb, QK_PAD), lambda h, j: (j, h)),
                   pl.BlockSpec((tb, HEAD), lambda h, j: (j, h))],
        out_shape=[jax.ShapeDtypeStruct((T, H * QK_PAD), F32), jax.ShapeDtypeStruct((T, H * QK_PAD), F32),
                   jax.ShapeDtypeStruct((T, H * HEAD), F32)],
        scratch_shapes=[pltpu.VMEM((tb, QK_PAD), F32), pltpu.VMEM((tb, HEAD), F32)],
        compiler_params=_params(("parallel", "arbitrary")),
    )(q, k, v, ot, lse, dot_)


def _conv_taps(u, prev6, prev7, w_ref, rows):
    tm = u.shape[0]
    u1 = jnp.where(rows >= 1, pltpu.roll(u, 1, 0), prev7)
    u2 = jnp.where(rows >= 2, pltpu.roll(u, 2, 0), jnp.where(rows == 0, prev6, prev7))
    return u2, u1


def _ffn_up_fwd(h, w_up, conv_w, conv_b, *, name, tm=512, tn_cap=1408):
    T, D = h.shape
    F = w_up.shape[1] // 2
    tm = min(tm, T)
    tn = _blk(F, tn_cap)
    ncb = F // tn

    def body(h_ref, wg_ref, wu_ref, cwg_ref, cwu_ref, cbg_ref, cbu_ref, a_ref, u_ref, cg_ref, cu_ref):
        i = pl.program_id(1)

        @pl.when(i == 0)
        def _():
            cg_ref[...] = jnp.zeros_like(cg_ref)
            cu_ref[...] = jnp.zeros_like(cu_ref)

        hv = h_ref[...]
        rows = lax.broadcasted_iota(jnp.int32, (tm, 1), 0)
        ys = []
        for idx, (w_ref, cw_ref, cb_ref, carry) in enumerate(
                ((wg_ref, cwg_ref, cbg_ref, cg_ref), (wu_ref, cwu_ref, cbu_ref, cu_ref))):
            u = _dot(hv, w_ref[...], NN)
            u_ref[idx] = u.astype(u_ref.dtype)
            u2, u1 = _conv_taps(u, carry[6:7, :], carry[7:8, :], cw_ref, rows)
            y = cb_ref[...] + u2 * cw_ref[0:1, :]
            y = y + u1 * cw_ref[1:2, :]
            y = y + u * cw_ref[2:3, :]
            ys.append(y)
            carry[...] = u[tm - 8:tm, :]
        yg, yu = ys
        a_ref[...] = ((yg * _sigmoid(yg)) * yu).astype(a_ref.dtype)

    return pl.pallas_call(
        body, name=name, grid=(ncb, T // tm),
        in_specs=[pl.BlockSpec((tm, D), lambda j, i: (i, 0)),
                  pl.BlockSpec((D, tn), lambda j, i: (0, j)), pl.BlockSpec((D, tn), lambda j, i: (0, j + ncb)),
                  pl.BlockSpec((3, tn), lambda j, i: (0, j)), pl.BlockSpec((3, tn), lambda j, i: (0, j + ncb)),
                  pl.BlockSpec((1, tn), lambda j, i: (0, j)), pl.BlockSpec((1, tn), lambda j, i: (0, j + ncb))],
        out_specs=[pl.BlockSpec((tm, tn), lambda j, i: (i, j)), pl.BlockSpec((2, tm, tn), lambda j, i: (0, i, j))],
        out_shape=[jax.ShapeDtypeStruct((T, F), BF16), jax.ShapeDtypeStruct((2, T, F), BF16)],
        scratch_shapes=[pltpu.VMEM((8, tn), F32), pltpu.VMEM((8, tn), F32)],
        compiler_params=_params(("parallel", "arbitrary")),
    )(h, w_up, w_up, conv_w, conv_w, conv_b.reshape(1, 2 * F), conv_b.reshape(1, 2 * F))


def _ffn_act_bwd(dxo, w_down, u, conv_w, conv_b, *, name, tm=512, tn_cap=1408):
    T, D = dxo.shape
    F = w_down.shape[0]
    tm = min(tm, T)
    tn = _blk(F, tn_cap)
    ncb = F // tn
    nt = T // tm
    hb = tm // 8

    def body(dx_ref, wd_ref, u_ref, up_ref, cwg_ref, cwu_ref, cbg_ref, cbu_ref,
             du_ref, dcw_ref, dcb_ref, cg_ref, cu_ref):
        i = pl.program_id(1)
        tile = nt - 1 - i

        @pl.when(i == 0)
        def _():
            cg_ref[...] = jnp.zeros_like(cg_ref)
            cu_ref[...] = jnp.zeros_like(cu_ref)
            dcw_ref[...] = jnp.zeros_like(dcw_ref)
            dcb_ref[...] = jnp.zeros_like(dcb_ref)

        rows = lax.broadcasted_iota(jnp.int32, (tm, 1), 0)
        da = _dot(dx_ref[...].astype(BF16), wd_ref[...], NT)
        has_prev = (tile > 0).astype(F32)
        us, ys, shifted = [], [], []
        for idx, (cw_ref, cb_ref) in enumerate(((cwg_ref, cbg_ref), (cwu_ref, cbu_ref))):
            uv = u_ref[idx].astype(F32)
            p6 = up_ref[idx, 6:7, :].astype(F32) * has_prev
            p7 = up_ref[idx, 7:8, :].astype(F32) * has_prev
            u2, u1 = _conv_taps(uv, p6, p7, cw_ref, rows)
            y = cb_ref[...] + u2 * cw_ref[0:1, :]
            y = y + u1 * cw_ref[1:2, :]
            y = y + uv * cw_ref[2:3, :]
            us.append(uv)
            ys.append(y)
            shifted.append((u2, u1))
        yg, yu = ys
        sg = _sigmoid(yg)
        dys = (da * yu * (sg * (1.0 + yg * (1.0 - sg))), da * (yg * sg))
        for idx, (cw_ref, carry) in enumerate(((cwg_ref, cg_ref), (cwu_ref, cu_ref))):
            dy = dys[idx]
            u2, u1 = shifted[idx]
            dcb_ref[idx] += jnp.sum(dy, axis=0, keepdims=True)
            dcw_ref[idx, 0:1, :] += jnp.sum(dy * u2, axis=0, keepdims=True)
            dcw_ref[idx, 1:2, :] += jnp.sum(dy * u1, axis=0, keepdims=True)
            dcw_ref[idx, 2:3, :] += jnp.sum(dy * us[idx], axis=0, keepdims=True)
            n0, n1 = carry[0:1, :], carry[1:2, :]
            d1 = jnp.where(rows < tm - 1, pltpu.roll(dy, tm - 1, 0), n0)
            d2 = jnp.where(rows < tm - 2, pltpu.roll(dy, tm - 2, 0), jnp.where(rows == tm - 2, n0, n1))
            du = dy * cw_ref[2:3, :] + d1 * cw_ref[1:2, :] + d2 * cw_ref[0:1, :]
            du_ref[idx] = du.astype(du_ref.dtype)
            carry[...] = dy[0:8, :]

    rt = lambda i: nt - 1 - i
    return pl.pallas_call(
        body, name=name, grid=(ncb, nt),
        in_specs=[pl.BlockSpec((tm, D), lambda j, i: (rt(i), 0)), pl.BlockSpec((tn, D), lambda j, i: (j, 0)),
                  pl.BlockSpec((2, tm, tn), lambda j, i: (0, rt(i), j)),
                  pl.BlockSpec((2, 8, tn), lambda j, i: (0, jnp.maximum(rt(i) * hb - 1, 0), j)),
                  pl.BlockSpec((3, tn), lambda j, i: (0, j)), pl.BlockSpec((3, tn), lambda j, i: (0, j + ncb)),
                  pl.BlockSpec((1, tn), lambda j, i: (0, j)), pl.BlockSpec((1, tn), lambda j, i: (0, j + ncb))],
        out_specs=[pl.BlockSpec((2, tm, tn), lambda j, i: (0, rt(i), j)),
                   pl.BlockSpec((2, 3, tn), lambda j, i: (0, 0, j)), pl.BlockSpec((2, 1, tn), lambda j, i: (0, 0, j))],
        out_shape=[jax.ShapeDtypeStruct((2, T, F), BF16), jax.ShapeDtypeStruct((2, 3, F), F32),
                   jax.ShapeDtypeStruct((2, 1, F), F32)],
        scratch_shapes=[pltpu.VMEM((8, tn), F32), pltpu.VMEM((8, tn), F32)],
        compiler_params=_params(("parallel", "arbitrary")),
    )(dxo, w_down, u, u, conv_w, conv_w, conv_b.reshape(1, 2 * F), conv_b.reshape(1, 2 * F))


def _pad_cols(w, n):
    return jnp.pad(w, [(0, 0)] * (w.ndim - 1) + [(0, n - w.shape[-1])])


def _q_up_padded(w):
    R = w.shape[0]
    H = w.shape[1] // QK_HEAD
    return _pad_cols(w.reshape(R, H, QK_HEAD), QK_PAD).reshape(R, H * QK_PAD)


def _q_up_unpadded(w):
    R = w.shape[0]
    H = w.shape[1] // QK_PAD
    return w.reshape(R, H, QK_PAD)[:, :, :QK_HEAD].reshape(R, H * QK_HEAD)


def _local_step(x, positions, target, w):
    T, D = x.shape
    depth = w["norm_mix"].shape[0]
    ct, s1, s2 = _rope_tables(positions)
    lb_soft = jax.nn.softmax(w["hgrn_lower_bounds"].astype(F32), axis=0)
    lower_bounds = jnp.cumsum(lb_soft, axis=0) - lb_soft[0:1]
    R = w["mla_w_q_up"].shape[1]
    cw = 2 * R + LANES

    saved = []
    for layer in range(depth):
        j = layer // 2
        s = {"x_in": x}
        h = _rmsnorm_fwd(x, w["norm_mix"][layer], name=f"norm_mix_fwd_{layer}")
        s["h_mix"] = h
        if layer % 2 == 0:
            p = _mm(h, w["hgrn_w_in"][j], name=f"hgrn_in_{layer}")
            og, o, s0 = _hgrn_fwd(p, lower_bounds[j], w["hgrn_out_norm"][j], name=f"hgrn_fwd_{layer}")
            s.update(p=p, og=og, o=o, s0=s0)
            x = _mm(og, w["hgrn_w_out"][j], res=x, name=f"hgrn_out_{layer}")
        else:
            w_in = _pad_cols(w["mla_w_in"][j], cw)
            wq = _q_up_padded(w["mla_w_q_up"][j])
            gq = _pad_cols(w["mla_q_norm"][j].reshape(1, QK_HEAD), QK_PAD)
            gk = _pad_cols(w["mla_k_norm"][j].reshape(1, QK_HEAD), QK_PAD)
            c = _mm(h, w_in, name=f"mla_in_{layer}")
            q, k, v = _mla_prep_fwd(c, wq, w["mla_w_kv_up"][j], w["mla_q_a_norm"][j], w["mla_kv_a_norm"][j],
                                    gq, gk, ct, s1, s2, name=f"mla_prep_fwd_{layer}")
            ot, lse = _attn_fwd(q, k, v, name=f"attn_fwd_{layer}")
            s.update(c=c, q=q, k=k, v=v, ot=ot, lse=lse, w_in=w_in, wq=wq, gq=gq, gk=gk)
            x = _mm(ot, w["mla_w_out"][j], a_fmt="kmb", res=x, name=f"mla_out_{layer}")
        s["x_mid"] = x
        h = _rmsnorm_fwd(x, w["norm_ffn"][layer], name=f"norm_ffn_fwd_{layer}")
        s["h_ffn"] = h
        a, u = _ffn_up_fwd(h, w["ffn_w_up"][layer], w["ffn_conv_w"][layer], w["ffn_conv_b"][layer],
                           name=f"ffn_up_{layer}")
        s.update(a=a, u=u)
        x = _mm(a, w["ffn_w_down"][layer], res=x, name=f"ffn_down_{layer}")
        saved.append(s)

    dx, loss = _loss_head(x, target, name="loss_head")

    g = {n: [None] * w[n].shape[0] for n in w}
    for layer in reversed(range(depth)):
        j = layer // 2
        s = saved[layer]
        g["ffn_w_down"][layer] = _mm(s["a"], dx, a_fmt="km", name=f"ffn_down_dw_{layer}")
        du, dcw, dcb = _ffn_act_bwd(dx, w["ffn_w_down"][layer], s["u"], w["ffn_conv_w"][layer],
                                    w["ffn_conv_b"][layer], name=f"ffn_act_bwd_{layer}")
        F = du.shape[2]
        g["ffn_conv_w"][layer] = dcw.transpose(1, 0, 2).reshape(3, 2 * F)
        g["ffn_conv_b"][layer] = dcb.reshape(2 * F)
        dwu = [_mm(s["h_ffn"], du[half], a_fmt="km", name=f"ffn_up_dw{half}_{layer}") for half in range(2)]
        g["ffn_w_up"][layer] = jnp.concatenate(dwu, axis=1)
        dh = _mm(du, w["ffn_w_up"][layer], a_fmt="mkb", b_fmt="nk", bk=1408, name=f"ffn_up_dh_{layer}")
        dx, dgain = _rmsnorm_bwd(s["x_mid"], w["norm_ffn"][layer], dh, dx, name=f"norm_ffn_bwd_{layer}")
        g["norm_ffn"][layer] = dgain.reshape(D)
        if layer % 2 == 0:
            g["hgrn_w_out"][j] = _mm(s["og"], dx, a_fmt="km", name=f"hgrn_out_dw_{layer}")
            dog = _mm(dx, w["hgrn_w_out"][j], b_fmt="nk", name=f"hgrn_out_dx_{layer}")
            dp, dlb, dgn = _hgrn_bwd(s["p"], lower_bounds[j], w["hgrn_out_norm"][j], s["s0"], s["o"], dog,
                                     name=f"hgrn_bwd_{layer}")
            g["hgrn_lower_bounds"][j] = dlb.reshape(D)
            g["hgrn_out_norm"][j] = dgn.reshape(HEAD)
            g["hgrn_w_in"][j] = _mm(s["h_mix"], dp, a_fmt="km", name=f"hgrn_in_dw_{layer}")
            dh = _mm(dp, w["hgrn_w_in"][j], b_fmt="nk", name=f"hgrn_in_dx_{layer}")
        else:
            tb = s["ot"].shape[2]
            g["mla_w_out"][j] = _mm(s["ot"], dx, a_fmt="mkb", bk=tb, name=f"mla_out_dw_{layer}")
            dot_ = _mm(w["mla_w_out"][j], dx, b_fmt="nk", out_fmt="mnb", out_dtype=BF16, bm=D, bn=tb,
                       name=f"mla_out_dx_{layer}")
            dq, dk, dv = _attn_bwd(s["q"], s["k"], s["v"], s["ot"], s["lse"], dot_, name=f"attn_bwd_{layer}")
            dc, dwq, dwkv, dgaq, dgakv, dgq, dgk = _mla_prep_bwd(
                s["c"], s["wq"], w["mla_w_kv_up"][j], w["mla_q_a_norm"][j], w["mla_kv_a_norm"][j],
                s["gq"], s["gk"], ct, s1, s2, dq, dk, dv, name=f"mla_prep_bwd_{layer}")
            g["mla_w_q_up"][j] = _q_up_unpadded(dwq)
            g["mla_w_kv_up"][j] = dwkv
            g["mla_q_a_norm"][j] = dgaq.reshape(R)
            g["mla_kv_a_norm"][j] = dgakv.reshape(R)
            g["mla_q_norm"][j] = dgq[0, :QK_HEAD]
            g["mla_k_norm"][j] = dgk[0, :QK_HEAD]
            win_cols = w["mla_w_in"].shape[2]
            g["mla_w_in"][j] = _mm(s["h_mix"], dc, a_fmt="km", name=f"mla_in_dw_{layer}")[:, :win_cols]
            dh = _mm(dc, s["w_in"], b_fmt="nk", name=f"mla_in_dx_{layer}")
        dx, dgain = _rmsnorm_bwd(s["x_in"], w["norm_mix"][layer], dh, dx, name=f"norm_mix_bwd_{layer}")
        g["norm_mix"][layer] = dgain.reshape(D)

    grads = {n: jnp.stack(v) for n, v in g.items()}
    dlb_eff = grads["hgrn_lower_bounds"]
    dsoft = jnp.cumsum(dlb_eff[::-1], axis=0)[::-1]
    dsoft = dsoft.at[0].add(-jnp.sum(dlb_eff, axis=0))
    grads["hgrn_lower_bounds"] = lb_soft * (dsoft - jnp.sum(dsoft * lb_soft, axis=0, keepdims=True))
    return loss, dx, grads


def _exchange(srcs, kinds, *, name):
    n = len(srcs)
    out_shapes = []
    for s, kind in zip(srcs, kinds):
        shape = (N_DEV,) + s.shape if kind == "gather" else s.shape
        out_shapes.append(jax.ShapeDtypeStruct(shape, s.dtype))

    def body(*refs):
        src_refs, out_refs = refs[:n], refs[n:2 * n]
        send_sems, recv_sems, local_sems = refs[2 * n:]
        x, y, c = lax.axis_index("x"), lax.axis_index("y"), lax.axis_index("c")
        me = 4 * x + 2 * y + c
        copies = []
        for b in range(n):
            gather = kinds[b] == "gather"
            own = src_refs[b] if gather else src_refs[b].at[me]
            local = pltpu.make_async_copy(own, out_refs[b].at[me], local_sems.at[b])
            local.start()
            copies.append(local)
            for kk in range(1, N_DEV):
                px = 1 - x if kk & 4 else x
                py = 1 - y if kk & 2 else y
                pc = 1 - c if kk & 1 else c
                peer = 4 * px + 2 * py + pc
                src = src_refs[b] if gather else src_refs[b].at[peer]
                cp = pltpu.make_async_remote_copy(
                    src_ref=src, dst_ref=out_refs[b].at[me],
                    send_sem=send_sems.at[b * (N_DEV - 1) + kk - 1],
                    recv_sem=recv_sems.at[b * (N_DEV - 1) + kk - 1],
                    device_id=(px, py, pc), device_id_type=pl.DeviceIdType.MESH)
                cp.start()
                copies.append(cp)
        for cp in copies:
            cp.wait()

    hbm = pl.BlockSpec(memory_space=pl.ANY)
    return pl.pallas_call(
        body, name=name, in_specs=[hbm] * n, out_specs=[hbm] * n, out_shape=out_shapes,
        scratch_shapes=[pltpu.SemaphoreType.DMA((n * (N_DEV - 1),)), pltpu.SemaphoreType.DMA((n * (N_DEV - 1),)),
                        pltpu.SemaphoreType.DMA((n,))],
    )(*srcs)


def _sum_slots(parts, *, name):
    _, Rr, C = parts.shape

    def body(p_ref, o_ref):
        acc = p_ref[0].astype(F32)
        for d in range(1, N_DEV):
            acc = acc + p_ref[d].astype(F32)
        o_ref[...] = acc

    return pl.pallas_call(
        body, name=name, grid=(1,),
        in_specs=[pl.BlockSpec((N_DEV, Rr, C), lambda i: (0, 0, 0))],
        out_specs=pl.BlockSpec((Rr, C), lambda i: (0, 0)),
        out_shape=jax.ShapeDtypeStruct((Rr, C), F32),
        compiler_params=_params(("arbitrary",)),
    )(parts)


def _adamw(parts, w, m, v, *, name, tr=256):
    S, Rr, C = parts.shape
    tr = min(tr, Rr)
    assert Rr % tr == 0, (Rr, tr)
    c1 = 1.0 - ADAM_B1 ** ADAM_STEP
    c2 = 1.0 - ADAM_B2 ** ADAM_STEP

    def body(p_ref, w_ref, m_ref, v_ref, g_ref, d_ref, nm_ref, nv_ref):
        g = p_ref[0].astype(F32)
        for d in range(1, S):
            g = g + p_ref[d].astype(F32)
        mm = ADAM_B1 * m_ref[...] + (1.0 - ADAM_B1) * g
        vv = ADAM_B2 * v_ref[...] + (1.0 - ADAM_B2) * (g * g)
        m_hat = mm / c1
        v_hat = vv / c2
        g_ref[...] = g
        d_ref[...] = -ADAM_LR * (m_hat / (jnp.sqrt(v_hat) + ADAM_EPS) + ADAM_WD * w_ref[...])
        nm_ref[...] = mm
        nv_ref[...] = vv

    spec = pl.BlockSpec((tr, C), lambda i: (i, 0))
    shape = jax.ShapeDtypeStruct((Rr, C), F32)
    return pl.pallas_call(
        body, name=name, grid=(Rr // tr,),
        in_specs=[pl.BlockSpec((S, tr, C), lambda i: (0, i, 0)), spec, spec, spec],
        out_specs=[spec] * 4, out_shape=[shape] * 4,
        compiler_params=_params(("parallel",)),
    )(parts, w, m, v)


def _pack(arrs, dtype, row_mult):
    flat = jnp.concatenate([a.reshape(-1).astype(dtype) for a in arrs])
    per = row_mult * PACK_COLS
    total = -(-flat.shape[0] // per) * per
    return jnp.pad(flat, (0, total - flat.shape[0])).reshape(total // PACK_COLS, PACK_COLS)


def _unpack(packed, shapes, lead=()):
    flat = packed.reshape(lead + (-1,))
    out, off = [], 0
    for shp in shapes:
        n = 1
        for d in shp:
            n *= d
        out.append(flat[..., off:off + n].reshape(lead + tuple(shp)))
        off += n
    return out


BIG = {"hgrn_w_in": 2, "hgrn_w_out": 1, "mla_w_in": 1, "mla_w_q_up": 2, "mla_w_kv_up": 2, "mla_w_out": 1,
       "ffn_w_up": 2, "ffn_w_down": 1}
SMALL_SHARDED = {"ffn_conv_w": 2, "mla_q_a_norm": 1, "mla_kv_a_norm": 1}
REPLICATED = ["norm_mix", "norm_ffn", "hgrn_lower_bounds", "hgrn_out_norm", "mla_q_norm", "mla_k_norm",
              "ffn_conv_b"]
WEIGHTS = ["norm_mix", "norm_ffn", "hgrn_w_in", "hgrn_lower_bounds", "hgrn_out_norm", "hgrn_w_out", "mla_w_in",
           "mla_q_a_norm", "mla_w_q_up", "mla_kv_a_norm", "mla_w_kv_up", "mla_q_norm", "mla_k_norm", "mla_w_out",
           "ffn_w_up", "ffn_conv_w", "ffn_conv_b", "ffn_w_down"]


def _assemble(gathered, axis):
    g = jnp.moveaxis(gathered, 0, axis)
    shp = list(g.shape)
    return g.reshape(shp[:axis] + [shp[axis] * shp[axis + 1]] + shp[axis + 2:])


def _split(full, axis):
    shp = list(full.shape)
    g = full.reshape(shp[:axis] + [N_DEV, shp[axis] // N_DEV] + shp[axis + 1:])
    return jnp.moveaxis(g, axis, 0)


def kernel(x, positions, norm_mix, norm_ffn, hgrn_w_in, hgrn_lower_bounds, hgrn_out_norm, hgrn_w_out, mla_w_in, mla_q_a_norm, mla_w_q_up, mla_kv_a_norm, mla_w_kv_up, mla_q_norm, mla_k_norm, mla_w_out, ffn_w_up, ffn_conv_w, ffn_conv_b, ffn_w_down, loss_target, m_norm_mix, m_norm_ffn, m_hgrn_w_in, m_hgrn_lower_bounds, m_hgrn_out_norm, m_hgrn_w_out, m_mla_w_in, m_mla_q_a_norm, m_mla_w_q_up, m_mla_kv_a_norm, m_mla_w_kv_up, m_mla_q_norm, m_mla_k_norm, m_mla_w_out, m_ffn_w_up, m_ffn_conv_w, m_ffn_conv_b, m_ffn_w_down, v_norm_mix, v_norm_ffn, v_hgrn_w_in, v_hgrn_lower_bounds, v_hgrn_out_norm, v_hgrn_w_out, v_mla_w_in, v_mla_q_a_norm, v_mla_w_q_up, v_mla_kv_a_norm, v_mla_w_kv_up, v_mla_q_norm, v_mla_k_norm, v_mla_w_out, v_ffn_w_up, v_ffn_conv_w, v_ffn_conv_b, v_ffn_w_down):
    local = dict(norm_mix=norm_mix, norm_ffn=norm_ffn, hgrn_w_in=hgrn_w_in, hgrn_lower_bounds=hgrn_lower_bounds,
                 hgrn_out_norm=hgrn_out_norm, hgrn_w_out=hgrn_w_out, mla_w_in=mla_w_in, mla_q_a_norm=mla_q_a_norm,
                 mla_w_q_up=mla_w_q_up, mla_kv_a_norm=mla_kv_a_norm, mla_w_kv_up=mla_w_kv_up, mla_q_norm=mla_q_norm,
                 mla_k_norm=mla_k_norm, mla_w_out=mla_w_out, ffn_w_up=ffn_w_up, ffn_conv_w=ffn_conv_w,
                 ffn_conv_b=ffn_conv_b, ffn_w_down=ffn_w_down)
    mom_m = dict(norm_mix=m_norm_mix, norm_ffn=m_norm_ffn, hgrn_w_in=m_hgrn_w_in,
                 hgrn_lower_bounds=m_hgrn_lower_bounds, hgrn_out_norm=m_hgrn_out_norm, hgrn_w_out=m_hgrn_w_out,
                 mla_w_in=m_mla_w_in, mla_q_a_norm=m_mla_q_a_norm, mla_w_q_up=m_mla_w_q_up,
                 mla_kv_a_norm=m_mla_kv_a_norm, mla_w_kv_up=m_mla_w_kv_up, mla_q_norm=m_mla_q_norm,
                 mla_k_norm=m_mla_k_norm, mla_w_out=m_mla_w_out, ffn_w_up=m_ffn_w_up, ffn_conv_w=m_ffn_conv_w,
                 ffn_conv_b=m_ffn_conv_b, ffn_w_down=m_ffn_w_down)
    mom_v = dict(norm_mix=v_norm_mix, norm_ffn=v_norm_ffn, hgrn_w_in=v_hgrn_w_in,
                 hgrn_lower_bounds=v_hgrn_lower_bounds, hgrn_out_norm=v_hgrn_out_norm, hgrn_w_out=v_hgrn_w_out,
                 mla_w_in=v_mla_w_in, mla_q_a_norm=v_mla_q_a_norm, mla_w_q_up=v_mla_w_q_up,
                 mla_kv_a_norm=v_mla_kv_a_norm, mla_w_kv_up=v_mla_w_kv_up, mla_q_norm=v_mla_q_norm,
                 mla_k_norm=v_mla_k_norm, mla_w_out=v_mla_w_out, ffn_w_up=v_ffn_w_up, ffn_conv_w=v_ffn_conv_w,
                 ffn_conv_b=v_ffn_conv_b, ffn_w_down=v_ffn_w_down)
    me = 4 * lax.axis_index("x") + 2 * lax.axis_index("y") + lax.axis_index("c")

    big_names, small_names = list(BIG), list(SMALL_SHARDED)
    big_local = _pack([local[n] for n in big_names], BF16, BIG_ROWS)
    small_local = _pack([local[n] for n in small_names], F32, 8)
    big_all, small_all = _exchange([big_local, small_local], ["gather", "gather"], name="gather_weights")
    full = {n: local[n] for n in REPLICATED}
    for n, g in zip(big_names, _unpack(big_all, [local[n].shape for n in big_names], lead=(N_DEV,))):
        full[n] = _assemble(g, BIG[n])
    for n, g in zip(small_names, _unpack(small_all, [local[n].shape for n in small_names], lead=(N_DEV,))):
        full[n] = _assemble(g, SMALL_SHARDED[n])

    loss_part, grad_x, grads = _local_step(x[0], positions[0], loss_target[0], full)

    big_parts = jnp.stack([_pack([_split(grads[n], BIG[n])[d] for n in big_names], BF16, BIG_ROWS) for d in range(N_DEV)])
    small_grad_names = REPLICATED + small_names
    small_part = _pack([grads[n] for n in small_grad_names] + [loss_part], F32, 8)
    big_recv, small_recv = _exchange([big_parts, small_part], ["scatter", "gather"], name="exchange_grads")

    out = {}
    big_shapes = [local[n].shape for n in big_names]
    res = _adamw(big_recv, _pack([local[n] for n in big_names], F32, BIG_ROWS),
                 _pack([mom_m[n] for n in big_names], F32, BIG_ROWS), _pack([mom_v[n] for n in big_names], F32, BIG_ROWS),
                 name="adamw_big")
    for kind, packed in zip(("grad", "delta", "new_m", "new_v"), res):
        for n, a in zip(big_names, _unpack(packed, big_shapes)):
            out[kind, n] = a
    small_sum = _sum_slots(small_recv, name="sum_small")
    small_full = _unpack(small_sum, [grads[n].shape for n in small_grad_names] + [(1, LANES)])
    loss = small_full[-1][0, 0]
    g_small = {}
    for n, a in zip(small_grad_names, small_full[:-1]):
        if n in SMALL_SHARDED:
            ax = SMALL_SHARDED[n]
            size = local[n].shape[ax]
            a = lax.dynamic_slice_in_dim(a, me * size, size, axis=ax)
        g_small[n] = a
    small_shapes = [local[n].shape for n in small_grad_names]
    res = _adamw(_pack([g_small[n] for n in small_grad_names], F32, 8)[None],
                 _pack([local[n] for n in small_grad_names], F32, 8),
                 _pack([mom_m[n] for n in small_grad_names], F32, 8),
                 _pack([mom_v[n] for n in small_grad_names], F32, 8), name="adamw_small")
    for kind, packed in zip(("grad", "delta", "new_m", "new_v"), res):
        for n, a in zip(small_grad_names, _unpack(packed, small_shapes)):
            out[kind, n] = a

    outs = [loss, grad_x[None]]
    for kind in ("grad", "delta", "new_m", "new_v"):
        outs += [out[kind, n] for n in WEIGHTS]
    return tuple(outs)
```

```python
import functools

import jax
import jax.numpy as jnp
from jax import lax
from jax.experimental import pallas as pl
from jax.experimental.pallas import tpu as pltpu

F32 = jnp.float32
BF16 = jnp.bfloat16

RMS_EPS = 1e-6
ROPE_THETA = 10000.0
HEAD = 128
ROPE = 64
QK_HEAD = HEAD + ROPE
QK_PAD = 256
CHUNK = 64
SUB = 16
EXP_CLAMP = 60.0

ADAM_LR = 0.001
ADAM_B1 = 0.9
ADAM_B2 = 0.999
ADAM_EPS = 1e-08
ADAM_WD = 0.01
ADAM_STEP = 10

N_DEV = 8
LANES = 128
PACK_COLS = 1024
V7X_VMEM_LIMIT = 56 * 1024 * 1024

HI = lax.Precision.HIGHEST


def _params(sem):
    return pltpu.CompilerParams(dimension_semantics=sem, vmem_limit_bytes=V7X_VMEM_LIMIT)


def _blk(n, cap):
    if n <= cap:
        return n
    d = (cap // LANES) * LANES
    while d >= LANES:
        if n % d == 0:
            return d
        d -= LANES
    raise ValueError(f"no lane-aligned block for {n} under {cap}")


def _sigmoid(x):
    return jax.nn.sigmoid(x)


def _dot(a, b, dims, precision=None):
    return lax.dot_general(a, b, (dims, ((), ())), preferred_element_type=F32, precision=precision)


NN = ((1,), (0,))
NT = ((1,), (1,))
TN = ((0,), (0,))


def _mm(a, b, *, a_fmt="mk", b_fmt="kn", out_fmt="mn", res=None, out_dtype=F32, bm=512, bn=1024, bk=1024,
        name):
    if a_fmt == "mk":
        M, K = a.shape
    elif a_fmt == "km":
        K, M = a.shape
    elif a_fmt == "kmb":
        nb, K, B = a.shape
        M = nb * B
    else:
        nb, M, B = a.shape
        K = nb * B
    if b_fmt == "kn":
        Kb, N = b.shape
    elif b_fmt == "nk":
        N, Kb = b.shape
    elif b_fmt == "knb":
        nbb, Kb, Bb = b.shape
        N = nbb * Bb
    else:
        nbb, N, Bb = b.shape
        Kb = nbb * Bb
    assert K == Kb, (a.shape, b.shape, a_fmt, b_fmt)
    bm = B if a_fmt == "kmb" else _blk(M, bm)
    bn = Bb if b_fmt == "knb" else _blk(N, bn)
    if a_fmt == "mkb" and b_fmt == "nkb":
        assert B == Bb
    bk = _blk(B, bk) if a_fmt == "mkb" else (_blk(Bb, bk) if b_fmt == "nkb" else _blk(K, bk))
    nm, nn, nk = M // bm, N // bn, K // bk

    if a_fmt == "mk":
        a_spec = pl.BlockSpec((bm, bk), lambda i, j, k: (i, k))
        a_dim = 1
    elif a_fmt == "km":
        a_spec = pl.BlockSpec((bk, bm), lambda i, j, k: (k, i))
        a_dim = 0
    elif a_fmt == "kmb":
        a_spec = pl.BlockSpec((None, bk, bm), lambda i, j, k: (i, k, 0))
        a_dim = 0
    else:
        per = B // bk
        a_spec = pl.BlockSpec((None, bm, bk), lambda i, j, k: (k // per, i, k % per))
        a_dim = 1
    if b_fmt == "kn":
        b_spec = pl.BlockSpec((bk, bn), lambda i, j, k: (k, j))
        b_dim = 0
    elif b_fmt == "nk":
        b_spec = pl.BlockSpec((bn, bk), lambda i, j, k: (j, k))
        b_dim = 1
    elif b_fmt == "knb":
        b_spec = pl.BlockSpec((None, bk, bn), lambda i, j, k: (j, k, 0))
        b_dim = 0
    else:
        perb = Bb // bk
        b_spec = pl.BlockSpec((None, bn, bk), lambda i, j, k: (k // perb, j, k % perb))
        b_dim = 1
    if out_fmt == "mn":
        o_spec = pl.BlockSpec((bm, bn), lambda i, j, k: (i, j))
        o_shape = jax.ShapeDtypeStruct((M, N), out_dtype)
    else:
        o_spec = pl.BlockSpec((None, bm, bn), lambda i, j, k: (j, i, 0))
        o_shape = jax.ShapeDtypeStruct((nn, M, bn), out_dtype)
    in_specs = [a_spec, b_spec]
    args = [a, b]
    if res is not None:
        assert out_fmt == "mn"
        in_specs.append(pl.BlockSpec((bm, bn), lambda i, j, k: (i, j)))
        args.append(res)
    dims = ((a_dim,), (b_dim,))
    has_res = res is not None

    def body(*refs):
        if has_res:
            a_ref, b_ref, r_ref, o_ref, acc_ref = refs
        else:
            a_ref, b_ref, o_ref, acc_ref = refs
        k = pl.program_id(2)
        part = _dot(a_ref[...].astype(BF16), b_ref[...].astype(BF16), dims)

        @pl.when(k == 0)
        def _():
            acc_ref[...] = part

        @pl.when(k > 0)
        def _():
            acc_ref[...] += part

        @pl.when(k == nk - 1)
        def _():
            out = acc_ref[...]
            if has_res:
                out = out + r_ref[...]
            o_ref[...] = out.astype(o_ref.dtype)

    return pl.pallas_call(
        body, name=name, grid=(nm, nn, nk), in_specs=in_specs, out_specs=o_spec, out_shape=o_shape,
        scratch_shapes=[pltpu.VMEM((bm, bn), F32)],
        compiler_params=_params(("parallel", "parallel", "arbitrary")),
    )(*args)


def _rmsnorm_fwd(x, gain, *, name, tm=512):
    T, D = x.shape
    tm = min(tm, T)

    def body(x_ref, g_ref, o_ref):
        xv = x_ref[...]
        r = lax.rsqrt(jnp.mean(xv * xv, axis=-1, keepdims=True) + RMS_EPS)
        o_ref[...] = ((xv * r) * g_ref[...]).astype(o_ref.dtype)

    return pl.pallas_call(
        body, name=name, grid=(T // tm,),
        in_specs=[pl.BlockSpec((tm, D), lambda i: (i, 0)), pl.BlockSpec((1, D), lambda i: (0, 0))],
        out_specs=pl.BlockSpec((tm, D), lambda i: (i, 0)),
        out_shape=jax.ShapeDtypeStruct((T, D), BF16),
        compiler_params=_params(("parallel",)),
    )(x, gain.reshape(1, D))


def _rmsnorm_bwd(x, gain, dh, dres, *, name, tm=512):
    T, D = x.shape
    tm = min(tm, T)

    def body(x_ref, g_ref, dh_ref, dr_ref, dx_ref, dg_ref):
        i = pl.program_id(0)
        xv = x_ref[...]
        r = lax.rsqrt(jnp.mean(xv * xv, axis=-1, keepdims=True) + RMS_EPS)
        n = xv * r
        dy = dh_ref[...].astype(F32)
        dn = dy * g_ref[...]
        dx_ref[...] = dr_ref[...] + r * (dn - n * jnp.mean(dn * n, axis=-1, keepdims=True))
        part = jnp.sum(dy * n, axis=0, keepdims=True)

        @pl.when(i == 0)
        def _():
            dg_ref[...] = part

        @pl.when(i > 0)
        def _():
            dg_ref[...] += part

    return pl.pallas_call(
        body, name=name, grid=(T // tm,),
        in_specs=[pl.BlockSpec((tm, D), lambda i: (i, 0)), pl.BlockSpec((1, D), lambda i: (0, 0)),
                  pl.BlockSpec((tm, D), lambda i: (i, 0)), pl.BlockSpec((tm, D), lambda i: (i, 0))],
        out_specs=[pl.BlockSpec((tm, D), lambda i: (i, 0)), pl.BlockSpec((1, D), lambda i: (0, 0))],
        out_shape=[jax.ShapeDtypeStruct((T, D), F32), jax.ShapeDtypeStruct((1, D), F32)],
        compiler_params=_params(("arbitrary",)),
    )(x, gain.reshape(1, D), dh, dres)


def _loss_head(y, target, *, name, tm=512):
    T, D = y.shape
    tm = min(tm, T)

    def body(y_ref, t_ref, dy_ref, l_ref):
        i = pl.program_id(0)
        e = y_ref[...] - t_ref[...]
        dy_ref[...] = e * (1.0 / D)
        s = 0.5 * jnp.sum(jnp.mean(e * e, axis=-1, keepdims=True), axis=0, keepdims=True)
        part = jnp.broadcast_to(s, (1, LANES))

        @pl.when(i == 0)
        def _():
            l_ref[...] = part

        @pl.when(i > 0)
        def _():
            l_ref[...] += part

    return pl.pallas_call(
        body, name=name, grid=(T // tm,),
        in_specs=[pl.BlockSpec((tm, D), lambda i: (i, 0)), pl.BlockSpec((tm, D), lambda i: (i, 0))],
        out_specs=[pl.BlockSpec((tm, D), lambda i: (i, 0)), pl.BlockSpec((1, LANES), lambda i: (0, 0))],
        out_shape=[jax.ShapeDtypeStruct((T, D), F32), jax.ShapeDtypeStruct((1, LANES), F32)],
        compiler_params=_params(("arbitrary",)),
    )(y, target)


def _hgrn_selectors():
    t = jnp.arange(CHUNK)[:, None]
    s = jnp.arange(CHUNK)[None, :]
    mats = [s <= t, s < (t // SUB) * SUB]
    for i in range(1, CHUNK // SUB):
        mats.append(jnp.broadcast_to(s < i * SUB, (CHUNK, CHUNK)))
    mats.append(jnp.ones((CHUNK, CHUNK), bool))
    sel = jnp.concatenate([m.astype(F32) for m in mats], axis=0)
    rev = (s >= t).astype(F32)
    return sel, rev


def _hgrn_gates(p, lb, D):
    qpre, fpre, iv, gpre = p[:, 0:D], p[:, D:2 * D], p[:, 2 * D:3 * D], p[:, 3 * D:4 * D]
    sig = _sigmoid(fpre)
    forget = lb + (1.0 - lb) * sig
    key = 1.0 - forget
    logf = jnp.log(forget)
    sq = _sigmoid(qpre)
    qs = qpre * sq
    return qpre, sq, qs, sig, forget, key, logf, iv, gpre


def _hgrn_scores(qt, kh, gh, rsel, row, col):
    nsub = CHUNK // SUB
    blocks, kts = [], []
    for i in range(nsub):
        ri = rsel[i]
        kt = kh * jnp.exp(jnp.minimum(ri - gh, EXP_CLAMP)) if ri is not None else \
            kh * jnp.exp(jnp.minimum(-gh, EXP_CLAMP))
        kts.append(kt)
        blocks.append(_dot(qt[i * SUB:(i + 1) * SUB].astype(BF16), kt.astype(BF16), NT))
    a = jnp.concatenate(blocks, axis=0)
    return jnp.where(col <= row, a, 0.0), kts


def _hgrn_fwd(p, lb, gn, *, name, ride=None):
    T, D4 = p.shape
    D = D4 // 4
    H = D // HEAD
    nc = T // CHUNK
    sel, _ = _hgrn_selectors()
    nsel = sel.shape[0]
    nsub = CHUNK // SUB

    def body(p_ref, lb_ref, gn_ref, sel_ref, og_ref, o_ref, s0_ref, st_ref):
        c = pl.program_id(0)

        @pl.when(c == 0)
        def _():
            st_ref[...] = jnp.zeros_like(st_ref)

        pv = p_ref[...]
        _, _, qs, _, _, key, logf, iv, gpre = _hgrn_gates(pv, lb_ref[...], D)
        cums = _dot(sel_ref[...], logf, NN, precision=HI)
        g = cums[0:CHUNK]
        rrow = cums[CHUNK:2 * CHUNK]
        rsel_all = [None] + [cums[(1 + i) * CHUNK:(2 + i) * CHUNK] for i in range(1, nsub)]
        gl = cums[nsel - CHUNK:nsel]
        s0_ref[...] = st_ref[...]
        row = lax.broadcasted_iota(jnp.int32, (CHUNK, CHUNK), 0)
        col = lax.broadcasted_iota(jnp.int32, (CHUNK, CHUNK), 1)
        gnv = gn_ref[...]
        for h in range(H):
            sl = slice(h * HEAD, (h + 1) * HEAD)
            gh, qh, kh, vh = g[:, sl], qs[:, sl], key[:, sl], iv[:, sl]
            st = st_ref[sl, :]
            o = _dot((qh * jnp.exp(gh)).astype(BF16), st.astype(BF16), NT)
            qt = qh * jnp.exp(gh - rrow[:, sl])
            a, _ = _hgrn_scores(qt, kh, gh, [None if r is None else r[:, sl] for r in rsel_all], row, col)
            o = o + _dot(a.astype(BF16), vh.astype(BF16), NN)
            glh = gl[:, sl]
            kd = kh * jnp.exp(glh - gh)
            gl1 = jnp.max(glh, axis=0, keepdims=True)
            st_ref[sl, :] = st * jnp.exp(gl1) + _dot(vh.astype(BF16), kd.astype(BF16), TN)
            o_ref[:, sl] = o
            r = lax.rsqrt(jnp.mean(o * o, axis=-1, keepdims=True) + RMS_EPS)
            gp = gpre[:, sl]
            og_ref[:, sl] = (((o * r) * gnv) * (gp * _sigmoid(gp))).astype(og_ref.dtype)

    return _call(
        body, name=name, grid=(nc,),
        in_specs=[pl.BlockSpec((CHUNK, D4), lambda c: (c, 0)), pl.BlockSpec((1, D), lambda c: (0, 0)),
                  pl.BlockSpec((1, HEAD), lambda c: (0, 0)), pl.BlockSpec((nsel, CHUNK), lambda c: (0, 0))],
        out_specs=[pl.BlockSpec((CHUNK, D), lambda c: (c, 0)), pl.BlockSpec((CHUNK, D), lambda c: (c, 0)),
                   pl.BlockSpec((None, D, HEAD), lambda c: (c, 0, 0))],
        out_shape=[jax.ShapeDtypeStruct((T, D), BF16), jax.ShapeDtypeStruct((T, D), F32),
                   jax.ShapeDtypeStruct((nc, D, HEAD), F32)],
        scratch_shapes=[pltpu.VMEM((D, HEAD), F32)], sem=("arbitrary",), ride=ride,
        args=(p, lb.reshape(1, D), gn.reshape(1, HEAD), sel))


def _hgrn_bwd(p, lb, gn, s0, o_saved, dog, *, name, ride=None):
    T, D4 = p.shape
    D = D4 // 4
    H = D // HEAD
    nc = T // CHUNK
    sel, rev = _hgrn_selectors()
    nsel = sel.shape[0]
    nsub = CHUNK // SUB

    def body(p_ref, lb_ref, gn_ref, sel_ref, rev_ref, s0_ref, s1_ref, o_ref, dog_ref,
             dp_ref, dlb_ref, dgn_ref, dst_ref):
        c = pl.program_id(0)

        @pl.when(c == 0)
        def _():
            dst_ref[...] = jnp.zeros_like(dst_ref)
            dlb_ref[...] = jnp.zeros_like(dlb_ref)
            dgn_ref[...] = jnp.zeros_like(dgn_ref)

        pv = p_ref[...]
        lbv = lb_ref[...]
        qpre, sq, qs, sig, forget, key, logf, iv, gpre = _hgrn_gates(pv, lbv, D)
        cums = _dot(sel_ref[...], logf, NN, precision=HI)
        g = cums[0:CHUNK]
        rrow = cums[CHUNK:2 * CHUNK]
        rsel_all = [None] + [cums[(1 + i) * CHUNK:(2 + i) * CHUNK] for i in range(1, nsub)]
        gl = cums[nsel - CHUNK:nsel]
        row = lax.broadcasted_iota(jnp.int32, (CHUNK, CHUNK), 0)
        col = lax.broadcasted_iota(jnp.int32, (CHUNK, CHUNK), 1)
        causal = col <= row
        gnv = gn_ref[...]
        dgn_acc = jnp.zeros((1, HEAD), F32)
        for h in range(H):
            sl = slice(h * HEAD, (h + 1) * HEAD)
            gh, qh, kh, vh = g[:, sl], qs[:, sl], key[:, sl], iv[:, sl]
            glh = gl[:, sl]
            gl1 = jnp.max(glh, axis=0, keepdims=True)
            st0 = s0_ref[sl, :]
            st1 = s1_ref[sl, :]
            dst = dst_ref[sl, :]
            o = o_ref[:, sl]
            r = lax.rsqrt(jnp.mean(o * o, axis=-1, keepdims=True) + RMS_EPS)
            n = o * r
            gp = gpre[:, sl]
            sg = _sigmoid(gp)
            dog_h = dog_ref[:, sl]
            d_on = dog_h * (gp * sg)
            dgpre = dog_h * (n * gnv) * (sg * (1.0 + gp * (1.0 - sg)))
            dgn_acc = dgn_acc + jnp.sum(d_on * n, axis=0, keepdims=True)
            dn = d_on * gnv
            do = r * (dn - n * jnp.mean(dn * n, axis=-1, keepdims=True))
            dob = do.astype(BF16)
            eg = jnp.exp(gh)
            qg = qh * eg
            eqr = jnp.exp(gh - rrow[:, sl])
            qt = qh * eqr
            rsel = [None if rr is None else rr[:, sl] for rr in rsel_all]
            a, kts = _hgrn_scores(qt, kh, gh, rsel, row, col)
            ekd = jnp.exp(glh - gh)
            kd = kh * ekd
            vb = vh.astype(BF16)
            dq = _dot(dob, st0.astype(BF16), NN) * eg
            da = jnp.where(causal, _dot(dob, vb, NT), 0.0)
            dab = da.astype(BF16)
            dqt_blocks = []
            dk = _dot(vb, dst.astype(BF16), NN) * ekd
            for i in range(nsub):
                rs = slice(i * SUB, (i + 1) * SUB)
                ktb = kts[i].astype(BF16)
                dqt_blocks.append(_dot(dab[rs], ktb, NN))
                dkt = _dot(dab[rs], qt[rs].astype(BF16), TN)
                ei = jnp.exp(jnp.minimum((rsel[i] if rsel[i] is not None else 0.0) - gh, EXP_CLAMP))
                dk = dk + dkt * ei
            dq = dq + jnp.concatenate(dqt_blocks, axis=0) * eqr
            dv = _dot(a.astype(BF16), dob, TN) + _dot(kd.astype(BF16), dst.astype(BF16), NT)
            dst_ref[sl, :] = dst * jnp.exp(gl1) + _dot(dob, qg.astype(BF16), TN)
            term = jnp.sum(dst * st1, axis=0, keepdims=True)
            dg = qh * dq - kh * dk
            dlogf = _dot(rev_ref[...], dg, NN, precision=HI) + term
            fg = forget[:, sl]
            sgf = sig[:, sl]
            lbh = lbv[:, sl]
            dforget = dlogf / fg - dk
            dfpre = dforget * (1.0 - lbh) * (sgf * (1.0 - sgf))
            dlb_ref[:, sl] += jnp.sum(dforget * (1.0 - sgf), axis=0, keepdims=True)
            sqh = sq[:, sl]
            dqpre = dq * (sqh * (1.0 + qpre[:, sl] * (1.0 - sqh)))
            dp_ref[:, h * HEAD:(h + 1) * HEAD] = dqpre.astype(dp_ref.dtype)
            dp_ref[:, D + h * HEAD:D + (h + 1) * HEAD] = dfpre.astype(dp_ref.dtype)
            dp_ref[:, 2 * D + h * HEAD:2 * D + (h + 1) * HEAD] = dv.astype(dp_ref.dtype)
            dp_ref[:, 3 * D + h * HEAD:3 * D + (h + 1) * HEAD] = dgpre.astype(dp_ref.dtype)
        dgn_ref[...] += dgn_acc

    rc = lambda c: nc - 1 - c
    return _call(
        body, name=name, grid=(nc,),
        in_specs=[pl.BlockSpec((CHUNK, D4), lambda c: (rc(c), 0)), pl.BlockSpec((1, D), lambda c: (0, 0)),
                  pl.BlockSpec((1, HEAD), lambda c: (0, 0)), pl.BlockSpec((nsel, CHUNK), lambda c: (0, 0)),
                  pl.BlockSpec((CHUNK, CHUNK), lambda c: (0, 0)),
                  pl.BlockSpec((None, D, HEAD), lambda c: (rc(c), 0, 0)),
                  pl.BlockSpec((None, D, HEAD), lambda c: (jnp.minimum(rc(c) + 1, nc - 1), 0, 0)),
                  pl.BlockSpec((CHUNK, D), lambda c: (rc(c), 0)), pl.BlockSpec((CHUNK, D), lambda c: (rc(c), 0))],
        out_specs=[pl.BlockSpec((CHUNK, D4), lambda c: (rc(c), 0)), pl.BlockSpec((1, D), lambda c: (0, 0)),
                   pl.BlockSpec((1, HEAD), lambda c: (0, 0))],
        out_shape=[jax.ShapeDtypeStruct((T, D4), BF16), jax.ShapeDtypeStruct((1, D), F32),
                   jax.ShapeDtypeStruct((1, HEAD), F32)],
        scratch_shapes=[pltpu.VMEM((D, HEAD), F32)], sem=("arbitrary",), ride=ride,
        args=(p, lb.reshape(1, D), gn.reshape(1, HEAD), sel, rev, s0, s0, o_saved, dog))


def _rope_tables(positions):
    inv_freq = ROPE_THETA ** (-jnp.arange(0, ROPE, 2, dtype=F32) / ROPE)
    ang = positions.astype(F32)[:, None] * inv_freq
    cos, sin = jnp.cos(ang), jnp.sin(ang)
    z = jnp.zeros_like(cos)
    ctab = jnp.concatenate([cos, cos, z, z], axis=-1)
    s1 = jnp.concatenate([-sin, z, z, z], axis=-1)
    s2 = jnp.concatenate([z, sin, z, z], axis=-1)
    return ctab, s1, s2


def _rope(z, ct, s1, s2):
    return z * ct + pltpu.roll(z, 96, 1) * s1 + pltpu.roll(z, 32, 1) * s2


def _rope_t(d, ct, s1, s2):
    return d * ct + pltpu.roll(d * s1, 32, 1) + pltpu.roll(d * s2, 96, 1)


def _mla_prep_fwd(c, wq, wkv, ga_q, ga_kv, gq, gk, ct, s1, s2, *, name, tm=256):
    T, CW = c.shape
    R = (CW - LANES) // 2
    H = wq.shape[1] // QK_PAD
    tm = min(tm, T)

    def body(c_ref, wq_ref, wkv_ref, gaq_ref, gakv_ref, gq_ref, gk_ref, ct_ref, s1_ref, s2_ref,
             q_ref, k_ref, v_ref):
        cv = c_ref[...]
        cq, ckv, kr = cv[:, 0:R], cv[:, R:2 * R], cv[:, 2 * R:2 * R + LANES]
        rq = lax.rsqrt(jnp.mean(cq * cq, axis=-1, keepdims=True) + RMS_EPS)
        cqn = ((cq * rq) * gaq_ref[...]).astype(BF16)
        rk = lax.rsqrt(jnp.mean(ckv * ckv, axis=-1, keepdims=True) + RMS_EPS)
        ckvn = ((ckv * rk) * gakv_ref[...]).astype(BF16)
        qp = _dot(cqn, wq_ref[...], NN)
        kvp = _dot(ckvn, wkv_ref[...], NN)
        ctv, s1v, s2v = ct_ref[...], s1_ref[...], s2_ref[...]
        gqv, gkv = gq_ref[...], gk_ref[...]
        krs = jnp.sum(kr * kr, axis=-1, keepdims=True)
        for h in range(H):
            b = h * QK_PAD
            qn, qr = qp[:, b:b + HEAD], qp[:, b + HEAD:b + QK_PAD]
            ss = jnp.sum(qn * qn, axis=-1, keepdims=True) + jnp.sum(qr * qr, axis=-1, keepdims=True)
            rr = lax.rsqrt(ss * (1.0 / QK_HEAD) + RMS_EPS)
            q_ref[:, b:b + HEAD] = ((qn * rr) * gqv[:, 0:HEAD]).astype(q_ref.dtype)
            q_ref[:, b + HEAD:b + QK_PAD] = _rope((qr * rr) * gqv[:, HEAD:QK_PAD], ctv, s1v, s2v).astype(q_ref.dtype)
            kn, vv = kvp[:, b:b + HEAD], kvp[:, b + HEAD:b + QK_PAD]
            ssk = jnp.sum(kn * kn, axis=-1, keepdims=True) + krs
            rrk = lax.rsqrt(ssk * (1.0 / QK_HEAD) + RMS_EPS)
            k_ref[:, b:b + HEAD] = ((kn * rrk) * gkv[:, 0:HEAD]).astype(k_ref.dtype)
            k_ref[:, b + HEAD:b + QK_PAD] = _rope((kr * rrk) * gkv[:, HEAD:QK_PAD], ctv, s1v, s2v).astype(k_ref.dtype)
            v_ref[:, h * HEAD:(h + 1) * HEAD] = vv.astype(v_ref.dtype)

    full = lambda shape: pl.BlockSpec(shape, lambda i: (0, 0))
    tok = lambda w: pl.BlockSpec((tm, w), lambda i: (i, 0))
    return pl.pallas_call(
        body, name=name, grid=(T // tm,),
        in_specs=[tok(CW), full(wq.shape), full(wkv.shape), full((1, R)), full((1, R)), full((1, QK_PAD)),
                  full((1, QK_PAD)), tok(LANES), tok(LANES), tok(LANES)],
        out_specs=[tok(H * QK_PAD), tok(H * QK_PAD), tok(H * HEAD)],
        out_shape=[jax.ShapeDtypeStruct((T, H * QK_PAD), BF16), jax.ShapeDtypeStruct((T, H * QK_PAD), BF16),
                   jax.ShapeDtypeStruct((T, H * HEAD), BF16)],
        compiler_params=_params(("parallel",)),
    )(c, wq, wkv, ga_q.reshape(1, R), ga_kv.reshape(1, R), gq, gk, ct, s1, s2)


def _mla_prep_bwd(c, wq, wkv, ga_q, ga_kv, gq, gk, ct, s1, s2, dq, dk, dv, *, name, tm=256):
    T, CW = c.shape
    R = (CW - LANES) // 2
    H = wq.shape[1] // QK_PAD
    tm = min(tm, T)

    def body(c_ref, wq_ref, wkv_ref, gaq_ref, gakv_ref, gq_ref, gk_ref, ct_ref, s1_ref, s2_ref,
             dq_ref, dk_ref, dv_ref,
             dc_ref, dwq_ref, dwkv_ref, dgaq_ref, dgakv_ref, dgq_ref, dgk_ref, dqp_ref, dkvp_ref):
        i = pl.program_id(0)

        @pl.when(i == 0)
        def _():
            for ref in (dwq_ref, dwkv_ref, dgaq_ref, dgakv_ref, dgq_ref, dgk_ref):
                ref[...] = jnp.zeros_like(ref)

        cv = c_ref[...]
        cq, ckv, kr = cv[:, 0:R], cv[:, R:2 * R], cv[:, 2 * R:2 * R + LANES]
        rq = lax.rsqrt(jnp.mean(cq * cq, axis=-1, keepdims=True) + RMS_EPS)
        nq = cq * rq
        cqn = (nq * gaq_ref[...]).astype(BF16)
        rk = lax.rsqrt(jnp.mean(ckv * ckv, axis=-1, keepdims=True) + RMS_EPS)
        nkv = ckv * rk
        ckvn = (nkv * gakv_ref[...]).astype(BF16)
        qp = _dot(cqn, wq_ref[...], NN)
        kvp = _dot(ckvn, wkv_ref[...], NN)
        ctv, s1v, s2v = ct_ref[...], s1_ref[...], s2_ref[...]
        gqv, gkv = gq_ref[...], gk_ref[...]
        krs = jnp.sum(kr * kr, axis=-1, keepdims=True)
        dkr = jnp.zeros((tm, LANES), F32)
        dgq_n = jnp.zeros((1, HEAD), F32)
        dgq_r = jnp.zeros((1, HEAD), F32)
        dgk_n = jnp.zeros((1, HEAD), F32)
        dgk_r = jnp.zeros((1, HEAD), F32)
        for h in range(H):
            b = h * QK_PAD
            qn, qr = qp[:, b:b + HEAD], qp[:, b + HEAD:b + QK_PAD]
            ss = jnp.sum(qn * qn, axis=-1, keepdims=True) + jnp.sum(qr * qr, axis=-1, keepdims=True)
            rr = lax.rsqrt(ss * (1.0 / QK_HEAD) + RMS_EPS)
            un, ur = qn * rr, qr * rr
            dzn = dq_ref[:, b:b + HEAD]
            dzr = _rope_t(dq_ref[:, b + HEAD:b + QK_PAD], ctv, s1v, s2v)
            dgq_n = dgq_n + jnp.sum(dzn * un, axis=0, keepdims=True)
            dgq_r = dgq_r + jnp.sum(dzr * ur, axis=0, keepdims=True)
            dun, dur = dzn * gqv[:, 0:HEAD], dzr * gqv[:, HEAD:QK_PAD]
            m = (jnp.sum(dun * un, axis=-1, keepdims=True) + jnp.sum(dur * ur, axis=-1, keepdims=True)) \
                * (1.0 / QK_HEAD)
            dqp_ref[:, b:b + HEAD] = (rr * (dun - un * m)).astype(BF16)
            dqp_ref[:, b + HEAD:b + QK_PAD] = (rr * (dur - ur * m)).astype(BF16)
            kn = kvp[:, b:b + HEAD]
            ssk = jnp.sum(kn * kn, axis=-1, keepdims=True) + krs
            rrk = lax.rsqrt(ssk * (1.0 / QK_HEAD) + RMS_EPS)
            vn, vr = kn * rrk, kr * rrk
            dyn = dk_ref[:, b:b + HEAD]
            dyr = _rope_t(dk_ref[:, b + HEAD:b + QK_PAD], ctv, s1v, s2v)
            dgk_n = dgk_n + jnp.sum(dyn * vn, axis=0, keepdims=True)
            dgk_r = dgk_r + jnp.sum(dyr * vr, axis=0, keepdims=True)
            dvn, dvr = dyn * gkv[:, 0:HEAD], dyr * gkv[:, HEAD:QK_PAD]
            mk = (jnp.sum(dvn * vn, axis=-1, keepdims=True) + jnp.sum(dvr * vr, axis=-1, keepdims=True)) \
                * (1.0 / QK_HEAD)
            dkvp_ref[:, b:b + HEAD] = (rrk * (dvn - vn * mk)).astype(BF16)
            dkr = dkr + rrk * (dvr - vr * mk)
            dkvp_ref[:, b + HEAD:b + QK_PAD] = dv_ref[:, h * HEAD:(h + 1) * HEAD].astype(BF16)
        dgq_ref[:, 0:HEAD] += dgq_n
        dgq_ref[:, HEAD:QK_PAD] += dgq_r
        dgk_ref[:, 0:HEAD] += dgk_n
        dgk_ref[:, HEAD:QK_PAD] += dgk_r
        dqp = dqp_ref[...]
        dkvp = dkvp_ref[...]
        dwq_ref[...] += _dot(cqn, dqp, TN)
        dwkv_ref[...] += _dot(ckvn, dkvp, TN)
        dcqn = _dot(dqp, wq_ref[...], NT)
        dckvn = _dot(dkvp, wkv_ref[...], NT)
        dgaq_ref[...] += jnp.sum(dcqn * nq, axis=0, keepdims=True)
        dgakv_ref[...] += jnp.sum(dckvn * nkv, axis=0, keepdims=True)
        dnq = dcqn * gaq_ref[...]
        dnkv = dckvn * gakv_ref[...]
        dc_ref[:, 0:R] = (rq * (dnq - nq * jnp.mean(dnq * nq, axis=-1, keepdims=True))).astype(dc_ref.dtype)
        dc_ref[:, R:2 * R] = (rk * (dnkv - nkv * jnp.mean(dnkv * nkv, axis=-1, keepdims=True))).astype(dc_ref.dtype)
        dc_ref[:, 2 * R:2 * R + LANES] = dkr.astype(dc_ref.dtype)

    full = lambda shape: pl.BlockSpec(shape, lambda i: (0, 0))
    tok = lambda w: pl.BlockSpec((tm, w), lambda i: (i, 0))
    return pl.pallas_call(
        body, name=name, grid=(T // tm,),
        in_specs=[tok(CW), full(wq.shape), full(wkv.shape), full((1, R)), full((1, R)), full((1, QK_PAD)),
                  full((1, QK_PAD)), tok(LANES), tok(LANES), tok(LANES),
                  tok(H * QK_PAD), tok(H * QK_PAD), tok(H * HEAD)],
        out_specs=[tok(CW), full(wq.shape), full(wkv.shape), full((1, R)), full((1, R)), full((1, QK_PAD)),
                   full((1, QK_PAD))],
        out_shape=[jax.ShapeDtypeStruct((T, CW), BF16), jax.ShapeDtypeStruct(wq.shape, F32),
                   jax.ShapeDtypeStruct(wkv.shape, F32), jax.ShapeDtypeStruct((1, R), F32),
                   jax.ShapeDtypeStruct((1, R), F32), jax.ShapeDtypeStruct((1, QK_PAD), F32),
                   jax.ShapeDtypeStruct((1, QK_PAD), F32)],
        scratch_shapes=[pltpu.VMEM((tm, H * QK_PAD), BF16), pltpu.VMEM((tm, H * QK_PAD), BF16)],
        compiler_params=_params(("arbitrary",)),
    )(c, wq, wkv, ga_q.reshape(1, R), ga_kv.reshape(1, R), gq, gk, ct, s1, s2, dq, dk, dv)


NEG = -1e30


def _attn_fwd(q, k, v, *, name, tb=512, ride=None):
    T = q.shape[0]
    H = q.shape[1] // QK_PAD
    tb = min(tb, T)
    nq = T // tb
    scale = QK_HEAD ** -0.5

    def body(q_ref, k_ref, v_ref, ot_ref, lse_ref, m_ref, l_ref, acc_ref):
        i = pl.program_id(1)
        qb = q_ref[...]
        m_ref[...] = jnp.full_like(m_ref, NEG)
        l_ref[...] = jnp.zeros_like(l_ref)
        acc_ref[...] = jnp.zeros_like(acc_ref)

        def step(j, masked):
            off = pl.multiple_of(j * tb, tb)
            kb = k_ref[pl.ds(off, tb), :]
            vb = v_ref[pl.ds(off, tb), :]
            st = _dot(kb, qb, NT) * scale
            if masked:
                kpos = lax.broadcasted_iota(jnp.int32, (tb, tb), 0)
                qpos = lax.broadcasted_iota(jnp.int32, (tb, tb), 1)
                st = jnp.where(kpos <= qpos, st, NEG)
            m_old = m_ref[...]
            m_new = jnp.maximum(m_old, jnp.max(st, axis=0, keepdims=True))
            alpha = jnp.exp(m_old - m_new)
            pt = jnp.exp(st - m_new)
            l_ref[...] = l_ref[...] * alpha + jnp.sum(pt, axis=0, keepdims=True)
            acc_ref[...] = acc_ref[...] * alpha + _dot(vb, pt.astype(BF16), TN)
            m_ref[...] = m_new

        def loop_body(j, carry):
            step(j, False)
            return carry

        lax.fori_loop(0, i, loop_body, 0)
        step(i, True)
        l = l_ref[...]
        ot_ref[...] = (acc_ref[...] / l).astype(ot_ref.dtype)
        lse_ref[...] = m_ref[...] + jnp.log(l)

    return _call(
        body, name=name, grid=(H, nq),
        in_specs=[pl.BlockSpec((tb, QK_PAD), lambda h, i: (i, h)), pl.BlockSpec((T, QK_PAD), lambda h, i: (0, h)),
                  pl.BlockSpec((T, HEAD), lambda h, i: (0, h))],
        out_specs=[pl.BlockSpec((None, HEAD, tb), lambda h, i: (i, h, 0)),
                   pl.BlockSpec((None, None, 1, tb), lambda h, i: (h, i, 0, 0))],
        out_shape=[jax.ShapeDtypeStruct((nq, H * HEAD, tb), BF16), jax.ShapeDtypeStruct((H, nq, 1, tb), F32)],
        scratch_shapes=[pltpu.VMEM((1, tb), F32), pltpu.VMEM((1, tb), F32), pltpu.VMEM((HEAD, tb), F32)],
        sem=("parallel", "arbitrary"), ride=ride, args=(q, k, v))


def _attn_bwd(q, k, v, ot, lse, dot_, *, name, ride=None):
    T = q.shape[0]
    H = q.shape[1] // QK_PAD
    nq, _, tb = ot.shape
    scale = QK_HEAD ** -0.5

    def body(q_ref, k_ref, v_ref, ot_ref, lse_ref, dot_ref, dq_ref, dk_ref, dv_ref, dka_ref, dva_ref):
        j = pl.program_id(1)

        @pl.when(j == 0)
        def _():
            dq_ref[...] = jnp.zeros_like(dq_ref)

        kb = k_ref[...]
        vb = v_ref[...]
        dka_ref[...] = jnp.zeros_like(dka_ref)
        dva_ref[...] = jnp.zeros_like(dva_ref)

        def step(i, masked):
            off = pl.multiple_of(i * tb, tb)
            qb = q_ref[pl.ds(off, tb), :]
            dob = dot_ref[i]
            ob = ot_ref[i]
            st = _dot(kb, qb, NT) * scale
            if masked:
                kpos = lax.broadcasted_iota(jnp.int32, (tb, tb), 0)
                qpos = lax.broadcasted_iota(jnp.int32, (tb, tb), 1)
                st = jnp.where(kpos <= qpos, st, NEG)
            pt = jnp.exp(st - lse_ref[i])
            dpt = _dot(vb, dob, NN)
            delta = jnp.sum(dob.astype(F32) * ob.astype(F32), axis=0, keepdims=True)
            dst = (pt * (dpt - delta) * scale).astype(BF16)
            dva_ref[...] += _dot(pt.astype(BF16), dob, NT)
            dka_ref[...] += _dot(dst, qb, NN)
            dq_ref[pl.ds(off, tb), :] += _dot(dst, kb, TN)

        step(j, True)

        def loop_body(i, carry):
            step(i, False)
            return carry

        lax.fori_loop(j + 1, nq, loop_body, 0)
        dk_ref[...] = dka_ref[...]
        dv_ref[...] = dva_ref[...]

    return _call(
        body, name=name, grid=(H, nq),
        in_specs=[pl.BlockSpec((T, QK_PAD), lambda h, j: (0, h)), pl.BlockSpec((tb, QK_PAD), lambda h, j: (j, h)),
                  pl.BlockSpec((tb, HEAD), lambda h, j: (j, h)),
                  pl.BlockSpec((nq, HEAD, tb), lambda h, j: (0, h, 0)),
                  pl.BlockSpec((None, nq, 1, tb), lambda h, j: (h, 0, 0, 0)),
                  pl.BlockSpec((nq, HEAD, tb), lambda h, j: (0, h, 0))],
        out_specs=[pl.BlockSpec((T, QK_PAD), lambda h, j: (0, h)), pl.BlockSpec((tb, QK_PAD), lambda h, j: (j, h)),
                   pl.BlockSpec((tb, HEAD), lambda h, j: (j, h))],
        out_shape=[jax.ShapeDtypeStruct((T, H * QK_PAD), F32), jax.ShapeDtypeStruct((T, H * QK_PAD), F32),
                   jax.ShapeDtypeStruct((T, H * HEAD), F32)],
        scratch_shapes=[pltpu.VMEM((tb, QK_PAD), F32), pltpu.VMEM((tb, HEAD), F32)],
        sem=("parallel", "arbitrary"), ride=ride, args=(q, k, v, ot, lse, dot_))


def _conv_taps(u, prev6, prev7, rows):
    u1 = jnp.where(rows >= 1, pltpu.roll(u, 1, 0), prev7)
    u2 = jnp.where(rows >= 2, pltpu.roll(u, 2, 0), jnp.where(rows == 0, prev6, prev7))
    return u2, u1


def _ffn_up_fwd(h, w_up, conv_w, conv_b, *, name, tm=512):
    T, D = h.shape
    ns, _, fs = w_up.shape
    nh = ns // 2
    tm = min(tm, T)

    def body(h_ref, wg_ref, wu_ref, cwg_ref, cwu_ref, cbg_ref, cbu_ref, a_ref, u_ref, cg_ref, cu_ref):
        i = pl.program_id(1)

        @pl.when(i == 0)
        def _():
            cg_ref[...] = jnp.zeros_like(cg_ref)
            cu_ref[...] = jnp.zeros_like(cu_ref)

        hv = h_ref[...]
        rows = lax.broadcasted_iota(jnp.int32, (tm, 1), 0)
        ys = []
        for idx, (w_ref, cw_ref, cb_ref, carry) in enumerate(
                ((wg_ref, cwg_ref, cbg_ref, cg_ref), (wu_ref, cwu_ref, cbu_ref, cu_ref))):
            u = _dot(hv, w_ref[...], NN)
            u_ref[idx] = u.astype(u_ref.dtype)
            u2, u1 = _conv_taps(u, carry[6:7, :], carry[7:8, :], rows)
            y = cb_ref[...] + u2 * cw_ref[0:1, :]
            y = y + u1 * cw_ref[1:2, :]
            y = y + u * cw_ref[2:3, :]
            ys.append(y)
            carry[...] = u[tm - 8:tm, :]
        yg, yu = ys
        a_ref[...] = ((yg * _sigmoid(yg)) * yu).astype(a_ref.dtype)

    shard = lambda r, off: pl.BlockSpec((None, r, fs), lambda j, i: (j + off, 0, 0))
    return pl.pallas_call(
        body, name=name, grid=(nh, T // tm),
        in_specs=[pl.BlockSpec((tm, D), lambda j, i: (i, 0)), shard(D, 0), shard(D, nh),
                  shard(3, 0), shard(3, nh), shard(1, 0), shard(1, nh)],
        out_specs=[pl.BlockSpec((None, tm, fs), lambda j, i: (j, i, 0)),
                   pl.BlockSpec((2, None, tm, fs), lambda j, i: (0, j, i, 0))],
        out_shape=[jax.ShapeDtypeStruct((nh, T, fs), BF16), jax.ShapeDtypeStruct((2, nh, T, fs), BF16)],
        scratch_shapes=[pltpu.VMEM((8, fs), F32), pltpu.VMEM((8, fs), F32)],
        compiler_params=_params(("parallel", "arbitrary")),
    )(h, w_up, w_up, conv_w, conv_w, conv_b, conv_b)


def _ffn_act_bwd(dxo, w_down, u, conv_w, conv_b, *, name, tm=512):
    T, D = dxo.shape
    _, nh, _, fs = u.shape
    tm = min(tm, T)
    nt = T // tm
    hb = tm // 8

    def body(dx_ref, wd_ref, u_ref, up_ref, cwg_ref, cwu_ref, cbg_ref, cbu_ref,
             du_ref, dcw_ref, dcb_ref, cg_ref, cu_ref):
        i = pl.program_id(1)
        tile = nt - 1 - i

        @pl.when(i == 0)
        def _():
            cg_ref[...] = jnp.zeros_like(cg_ref)
            cu_ref[...] = jnp.zeros_like(cu_ref)
            dcw_ref[...] = jnp.zeros_like(dcw_ref)
            dcb_ref[...] = jnp.zeros_like(dcb_ref)

        rows = lax.broadcasted_iota(jnp.int32, (tm, 1), 0)
        da = _dot(dx_ref[...].astype(BF16), wd_ref[...], NT)
        has_prev = (tile > 0).astype(F32)
        us, ys, shifted = [], [], []
        for idx, (cw_ref, cb_ref) in enumerate(((cwg_ref, cbg_ref), (cwu_ref, cbu_ref))):
            uv = u_ref[idx].astype(F32)
            p6 = up_ref[idx, 6:7, :].astype(F32) * has_prev
            p7 = up_ref[idx, 7:8, :].astype(F32) * has_prev
            u2, u1 = _conv_taps(uv, p6, p7, rows)
            y = cb_ref[...] + u2 * cw_ref[0:1, :]
            y = y + u1 * cw_ref[1:2, :]
            y = y + uv * cw_ref[2:3, :]
            us.append(uv)
            ys.append(y)
            shifted.append((u2, u1))
        yg, yu = ys
        sg = _sigmoid(yg)
        dys = (da * yu * (sg * (1.0 + yg * (1.0 - sg))), da * (yg * sg))
        for idx, (cw_ref, carry) in enumerate(((cwg_ref, cg_ref), (cwu_ref, cu_ref))):
            dy = dys[idx]
            u2, u1 = shifted[idx]
            dcb_ref[idx] += jnp.sum(dy, axis=0, keepdims=True)
            dcw_ref[idx, 0:1, :] += jnp.sum(dy * u2, axis=0, keepdims=True)
            dcw_ref[idx, 1:2, :] += jnp.sum(dy * u1, axis=0, keepdims=True)
            dcw_ref[idx, 2:3, :] += jnp.sum(dy * us[idx], axis=0, keepdims=True)
            n0, n1 = carry[0:1, :], carry[1:2, :]
            d1 = jnp.where(rows < tm - 1, pltpu.roll(dy, tm - 1, 0), n0)
            d2 = jnp.where(rows < tm - 2, pltpu.roll(dy, tm - 2, 0), jnp.where(rows == tm - 2, n0, n1))
            du = dy * cw_ref[2:3, :] + d1 * cw_ref[1:2, :] + d2 * cw_ref[0:1, :]
            du_ref[idx] = du.astype(du_ref.dtype)
            carry[...] = dy[0:8, :]

    rt = lambda i: nt - 1 - i
    shard = lambda r, off: pl.BlockSpec((None, r, fs), lambda j, i: (j + off, 0, 0))
    return pl.pallas_call(
        body, name=name, grid=(nh, nt),
        in_specs=[pl.BlockSpec((tm, D), lambda j, i: (rt(i), 0)), pl.BlockSpec((fs, D), lambda j, i: (j, 0)),
                  pl.BlockSpec((2, None, tm, fs), lambda j, i: (0, j, rt(i), 0)),
                  pl.BlockSpec((2, None, 8, fs), lambda j, i: (0, j, jnp.maximum(rt(i) * hb - 1, 0), 0)),
                  shard(3, 0), shard(3, nh), shard(1, 0), shard(1, nh)],
        out_specs=[pl.BlockSpec((2, None, tm, fs), lambda j, i: (0, j, rt(i), 0)),
                   pl.BlockSpec((2, None, 3, fs), lambda j, i: (0, j, 0, 0)),
                   pl.BlockSpec((2, None, 1, fs), lambda j, i: (0, j, 0, 0))],
        out_shape=[jax.ShapeDtypeStruct((2, nh, T, fs), BF16), jax.ShapeDtypeStruct((2, nh, 3, fs), F32),
                   jax.ShapeDtypeStruct((2, nh, 1, fs), F32)],
        scratch_shapes=[pltpu.VMEM((8, fs), F32), pltpu.VMEM((8, fs), F32)],
        compiler_params=_params(("parallel", "arbitrary")),
    )(dxo, w_down, u, u, conv_w, conv_w, conv_b, conv_b)


def _pad_cols(w, n):
    return jnp.pad(w, [(0, 0)] * (w.ndim - 1) + [(0, n - w.shape[-1])])


def _q_up_padded(w):
    R = w.shape[0]
    H = w.shape[1] // QK_HEAD
    return _pad_cols(w.reshape(R, H, QK_HEAD), QK_PAD).reshape(R, H * QK_PAD)


def _q_up_unpadded(w):
    R = w.shape[0]
    H = w.shape[1] // QK_PAD
    return w.reshape(R, H, QK_PAD)[:, :, :QK_HEAD].reshape(R, H * QK_HEAD)


def _xchg_copies(src_refs, out_refs, kinds, send_sems, recv_sems, local_sems):
    x, y, c = lax.axis_index("x"), lax.axis_index("y"), lax.axis_index("c")
    me = 4 * x + 2 * y + c
    copies = []
    for b, kind in enumerate(kinds):
        gather = kind == "gather"
        own = src_refs[b] if gather else src_refs[b].at[me]
        copies.append(pltpu.make_async_copy(own, out_refs[b].at[me], local_sems.at[b]))
        for kk in range(1, N_DEV):
            px = 1 - x if kk & 4 else x
            py = 1 - y if kk & 2 else y
            pc = 1 - c if kk & 1 else c
            peer = 4 * px + 2 * py + pc
            src = src_refs[b] if gather else src_refs[b].at[peer]
            copies.append(pltpu.make_async_remote_copy(
                src_ref=src, dst_ref=out_refs[b].at[me],
                send_sem=send_sems.at[b * (N_DEV - 1) + kk - 1],
                recv_sem=recv_sems.at[b * (N_DEV - 1) + kk - 1],
                device_id=(px, py, pc), device_id_type=pl.DeviceIdType.MESH))
    return copies


def _xchg_out_shapes(srcs, kinds):
    return [jax.ShapeDtypeStruct((N_DEV,) + s.shape if kind == "gather" else s.shape, s.dtype)
            for s, kind in zip(srcs, kinds)]


def _xchg_scratch(n):
    return [pltpu.SemaphoreType.DMA((n * (N_DEV - 1),)), pltpu.SemaphoreType.DMA((n * (N_DEV - 1),)),
            pltpu.SemaphoreType.DMA((n,))]


def _exchange(srcs, kinds, *, name):
    n = len(srcs)

    def body(*refs):
        copies = _xchg_copies(refs[:n], refs[n:2 * n], kinds, *refs[2 * n:])
        for cp in copies:
            cp.start()
        for cp in copies:
            cp.wait()

    hbm = pl.BlockSpec(memory_space=pl.ANY)
    return pl.pallas_call(
        body, name=name, in_specs=[hbm] * n, out_specs=[hbm] * n, out_shape=_xchg_out_shapes(srcs, kinds),
        scratch_shapes=_xchg_scratch(n),
    )(*srcs)


def _call(body, *, name, grid, in_specs, out_specs, out_shape, scratch_shapes, args, sem, ride=None):
    if ride is None:
        outs = pl.pallas_call(body, name=name, grid=grid, in_specs=in_specs, out_specs=out_specs,
                              out_shape=out_shape, scratch_shapes=scratch_shapes,
                              compiler_params=_params(sem))(*args)
        return list(outs), []
    srcs, kinds = ride
    n_in, n_out, n_sc, nx = len(in_specs), len(out_specs), len(scratch_shapes), len(srcs)

    def wrapped(*refs):
        ins, xs = refs[:n_in], refs[n_in:n_in + nx]
        o0 = n_in + nx
        outs, xo = refs[o0:o0 + n_out], refs[o0 + n_out:o0 + n_out + nx]
        s0 = o0 + n_out + nx
        sc, sems = refs[s0:s0 + n_sc], refs[s0 + n_sc:]
        first = functools.reduce(jnp.logical_and, [pl.program_id(d) == 0 for d in range(len(grid))])
        last = functools.reduce(jnp.logical_and, [pl.program_id(d) == grid[d] - 1 for d in range(len(grid))])

        @pl.when(first)
        def _():
            for cp in _xchg_copies(xs, xo, kinds, *sems):
                cp.start()

        body(*ins, *outs, *sc)

        @pl.when(last)
        def _():
            for cp in _xchg_copies(xs, xo, kinds, *sems):
                cp.wait()

    hbm = pl.BlockSpec(memory_space=pl.ANY)
    outs = pl.pallas_call(
        wrapped, name=name, grid=grid, in_specs=list(in_specs) + [hbm] * nx,
        out_specs=list(out_specs) + [hbm] * nx, out_shape=list(out_shape) + _xchg_out_shapes(srcs, kinds),
        scratch_shapes=list(scratch_shapes) + _xchg_scratch(nx),
        compiler_params=_params(("arbitrary",) * len(grid)),
    )(*args, *srcs)
    return list(outs[:n_out]), list(outs[n_out:])


def _sum_slots(parts, *, name):
    _, Rr, C = parts.shape

    def body(p_ref, o_ref):
        acc = p_ref[0].astype(F32)
        for d in range(1, N_DEV):
            acc = acc + p_ref[d].astype(F32)
        o_ref[...] = acc

    return pl.pallas_call(
        body, name=name, grid=(1,),
        in_specs=[pl.BlockSpec((N_DEV, Rr, C), lambda i: (0, 0, 0))],
        out_specs=pl.BlockSpec((Rr, C), lambda i: (0, 0)),
        out_shape=jax.ShapeDtypeStruct((Rr, C), F32),
        compiler_params=_params(("arbitrary",)),
    )(parts)


def _row_tile(rows, cap=512):
    if rows <= cap:
        return rows
    d = (cap // 8) * 8
    while d >= 8:
        if rows % d == 0:
            return d
        d -= 8
    raise ValueError(f"no row tile for {rows}")


def _adamw(parts, w, m, v, *, name):
    S, Rr, C = parts.shape
    tr = _row_tile(Rr)
    c1 = 1.0 - ADAM_B1 ** ADAM_STEP
    c2 = 1.0 - ADAM_B2 ** ADAM_STEP

    def body(p_ref, w_ref, m_ref, v_ref, g_ref, d_ref, nm_ref, nv_ref):
        g = p_ref[0].astype(F32)
        for d in range(1, S):
            g = g + p_ref[d].astype(F32)
        mm = ADAM_B1 * m_ref[...] + (1.0 - ADAM_B1) * g
        vv = ADAM_B2 * v_ref[...] + (1.0 - ADAM_B2) * (g * g)
        m_hat = mm / c1
        v_hat = vv / c2
        g_ref[...] = g
        d_ref[...] = -ADAM_LR * (m_hat / (jnp.sqrt(v_hat) + ADAM_EPS) + ADAM_WD * w_ref[...])
        nm_ref[...] = mm
        nv_ref[...] = vv

    spec = pl.BlockSpec((tr, C), lambda i: (i, 0))
    shape = jax.ShapeDtypeStruct((Rr, C), F32)
    return pl.pallas_call(
        body, name=name, grid=(Rr // tr,),
        in_specs=[pl.BlockSpec((S, tr, C), lambda i: (0, i, 0)), spec, spec, spec],
        out_specs=[spec] * 4, out_shape=[shape] * 4,
        compiler_params=_params(("parallel",)),
    )(parts, w, m, v)


def _pack(arrs, dtype, row_mult):
    flat = jnp.concatenate([a.reshape(-1).astype(dtype) for a in arrs])
    per = row_mult * PACK_COLS
    total = -(-flat.shape[0] // per) * per
    return jnp.pad(flat, (0, total - flat.shape[0])).reshape(total // PACK_COLS, PACK_COLS)


def _unpack(packed, shapes, lead=()):
    flat = packed.reshape(lead + (-1,))
    out, off = [], 0
    for shp in shapes:
        n = 1
        for d in shp:
            n *= d
        out.append(flat[..., off:off + n].reshape(lead + tuple(shp)))
        off += n
    return out


HGRN_W = ("hgrn_w_in", "hgrn_w_out")
MLA_W = ("mla_w_in", "mla_w_q_up", "mla_w_kv_up", "mla_w_out")
FFN_W = ("ffn_w_up", "ffn_w_down")
BIG = HGRN_W + MLA_W + FFN_W
SMALL_SHARDED = {"ffn_conv_w": 2, "mla_q_a_norm": 1, "mla_kv_a_norm": 1}
REPLICATED = ["norm_mix", "norm_ffn", "hgrn_lower_bounds", "hgrn_out_norm", "mla_q_norm", "mla_k_norm",
              "ffn_conv_b"]
WEIGHTS = ["norm_mix", "norm_ffn", "hgrn_w_in", "hgrn_lower_bounds", "hgrn_out_norm", "hgrn_w_out", "mla_w_in",
           "mla_q_a_norm", "mla_w_q_up", "mla_kv_a_norm", "mla_w_kv_up", "mla_q_norm", "mla_k_norm", "mla_w_out",
           "ffn_w_up", "ffn_conv_w", "ffn_conv_b", "ffn_w_down"]


def _shards_to_cols(g):
    return g.transpose(1, 0, 2).reshape(g.shape[1], N_DEV * g.shape[2])


def _cols_to_shards(w):
    R = w.shape[0]
    return w.reshape(R, N_DEV, w.shape[1] // N_DEV).transpose(1, 0, 2)


def kernel(x, positions, norm_mix, norm_ffn, hgrn_w_in, hgrn_lower_bounds, hgrn_out_norm, hgrn_w_out, mla_w_in, mla_q_a_norm, mla_w_q_up, mla_kv_a_norm, mla_w_kv_up, mla_q_norm, mla_k_norm, mla_w_out, ffn_w_up, ffn_conv_w, ffn_conv_b, ffn_w_down, loss_target, m_norm_mix, m_norm_ffn, m_hgrn_w_in, m_hgrn_lower_bounds, m_hgrn_out_norm, m_hgrn_w_out, m_mla_w_in, m_mla_q_a_norm, m_mla_w_q_up, m_mla_kv_a_norm, m_mla_w_kv_up, m_mla_q_norm, m_mla_k_norm, m_mla_w_out, m_ffn_w_up, m_ffn_conv_w, m_ffn_conv_b, m_ffn_w_down, v_norm_mix, v_norm_ffn, v_hgrn_w_in, v_hgrn_lower_bounds, v_hgrn_out_norm, v_hgrn_w_out, v_mla_w_in, v_mla_q_a_norm, v_mla_w_q_up, v_mla_kv_a_norm, v_mla_w_kv_up, v_mla_q_norm, v_mla_k_norm, v_mla_w_out, v_ffn_w_up, v_ffn_conv_w, v_ffn_conv_b, v_ffn_w_down):
    local = dict(norm_mix=norm_mix, norm_ffn=norm_ffn, hgrn_w_in=hgrn_w_in, hgrn_lower_bounds=hgrn_lower_bounds,
                 hgrn_out_norm=hgrn_out_norm, hgrn_w_out=hgrn_w_out, mla_w_in=mla_w_in, mla_q_a_norm=mla_q_a_norm,
                 mla_w_q_up=mla_w_q_up, mla_kv_a_norm=mla_kv_a_norm, mla_w_kv_up=mla_w_kv_up, mla_q_norm=mla_q_norm,
                 mla_k_norm=mla_k_norm, mla_w_out=mla_w_out, ffn_w_up=ffn_w_up, ffn_conv_w=ffn_conv_w,
                 ffn_conv_b=ffn_conv_b, ffn_w_down=ffn_w_down)
    mom_m = dict(norm_mix=m_norm_mix, norm_ffn=m_norm_ffn, hgrn_w_in=m_hgrn_w_in,
                 hgrn_lower_bounds=m_hgrn_lower_bounds, hgrn_out_norm=m_hgrn_out_norm, hgrn_w_out=m_hgrn_w_out,
                 mla_w_in=m_mla_w_in, mla_q_a_norm=m_mla_q_a_norm, mla_w_q_up=m_mla_w_q_up,
                 mla_kv_a_norm=m_mla_kv_a_norm, mla_w_kv_up=m_mla_w_kv_up, mla_q_norm=m_mla_q_norm,
                 mla_k_norm=m_mla_k_norm, mla_w_out=m_mla_w_out, ffn_w_up=m_ffn_w_up, ffn_conv_w=m_ffn_conv_w,
                 ffn_conv_b=m_ffn_conv_b, ffn_w_down=m_ffn_w_down)
    mom_v = dict(norm_mix=v_norm_mix, norm_ffn=v_norm_ffn, hgrn_w_in=v_hgrn_w_in,
                 hgrn_lower_bounds=v_hgrn_lower_bounds, hgrn_out_norm=v_hgrn_out_norm, hgrn_w_out=v_hgrn_w_out,
                 mla_w_in=v_mla_w_in, mla_q_a_norm=v_mla_q_a_norm, mla_w_q_up=v_mla_w_q_up,
                 mla_kv_a_norm=v_mla_kv_a_norm, mla_w_kv_up=v_mla_w_kv_up, mla_q_norm=v_mla_q_norm,
                 mla_k_norm=v_mla_k_norm, mla_w_out=v_mla_w_out, ffn_w_up=v_ffn_w_up, ffn_conv_w=v_ffn_conv_w,
                 ffn_conv_b=v_ffn_conv_b, ffn_w_down=v_ffn_w_down)
    me = 4 * lax.axis_index("x") + 2 * lax.axis_index("y") + lax.axis_index("c")
    x, positions, target = x[0], positions[0], loss_target[0]
    T, D = x.shape
    depth = norm_mix.shape[0]
    R = mla_w_q_up.shape[1]
    cw = 2 * R + LANES
    small_names = list(SMALL_SHARDED)

    def block_of(kind, l):
        names = {"hgrn": HGRN_W, "mla": MLA_W, "ffn": FFN_W}[kind]
        idx = l if kind == "ffn" else l // 2
        return [(n, idx) for n in names]

    def mixer_kind(l):
        return "hgrn" if l % 2 == 0 else "mla"

    def riders(l):
        keys = block_of("ffn", l)
        if l + 1 < depth:
            keys += block_of(mixer_kind(l + 1), l + 1)
        return keys

    gathered = {}

    def gather_ride(l):
        keys = riders(l)
        return keys, ([local[n][i].astype(BF16) for n, i in keys], ["gather"] * len(keys))

    def take_gathered(keys, arrs):
        for key, a in zip(keys, arrs):
            gathered[key] = a

    keys0 = block_of("hgrn", 0)
    small_local = _pack([local[n] for n in small_names], F32, 8)
    got = _exchange([local[n][i].astype(BF16) for n, i in keys0] + [small_local],
                    ["gather"] * (len(keys0) + 1), name="gather_first")
    take_gathered(keys0, got[:-1])
    small_all = _unpack(got[-1], [local[n].shape for n in small_names], lead=(N_DEV,))
    conv_w_all = small_all[0].transpose(1, 0, 2, 3)
    qa_all = small_all[1].transpose(1, 0, 2).reshape(-1, R)
    kva_all = small_all[2].transpose(1, 0, 2).reshape(-1, R)
    fs = conv_w_all.shape[-1]
    conv_b_s = ffn_conv_b.reshape(depth, N_DEV, 1, fs)

    ct, s1, s2 = _rope_tables(positions)
    lb_soft = jax.nn.softmax(hgrn_lower_bounds.astype(F32), axis=0)
    lower_bounds = jnp.cumsum(lb_soft, axis=0) - lb_soft[0:1]

    def mla_views(j):
        w_in = _pad_cols(gathered["mla_w_in", j].reshape(D, -1), cw)
        wq = _q_up_padded(_shards_to_cols(gathered["mla_w_q_up", j]))
        wkv = _shards_to_cols(gathered["mla_w_kv_up", j])
        gq = _pad_cols(mla_q_norm[j].reshape(1, QK_HEAD), QK_PAD)
        gk = _pad_cols(mla_k_norm[j].reshape(1, QK_HEAD), QK_PAD)
        return w_in, wq, wkv, gq, gk

    saved = []
    for layer in range(depth):
        j = layer // 2
        s = {"x_in": x}
        h = _rmsnorm_fwd(x, norm_mix[layer], name=f"norm_mix_fwd_{layer}")
        s["h_mix"] = h
        keys, ride = gather_ride(layer)
        if layer % 2 == 0:
            p = _mm(h, gathered["hgrn_w_in", j], b_fmt="knb", bm=1024, name=f"hgrn_in_{layer}")
            (og, o, s0), got = _hgrn_fwd(p, lower_bounds[j], hgrn_out_norm[j], name=f"hgrn_fwd_{layer}", ride=ride)
            s.update(p=p, og=og, o=o, s0=s0)
            take_gathered(keys, got)
            x = _mm(og, gathered["hgrn_w_out", j].reshape(D, D), res=x, name=f"hgrn_out_{layer}")
        else:
            w_in, wq, wkv, gq, gk = mla_views(j)
            c = _mm(h, w_in, bm=1024, name=f"mla_in_{layer}")
            q, k, v = _mla_prep_fwd(c, wq, wkv, qa_all[j], kva_all[j], gq, gk, ct, s1, s2,
                                    name=f"mla_prep_fwd_{layer}")
            (ot, lse), got = _attn_fwd(q, k, v, name=f"attn_fwd_{layer}", ride=ride)
            s.update(c=c, q=q, k=k, v=v, ot=ot, lse=lse)
            take_gathered(keys, got)
            x = _mm(ot, gathered["mla_w_out", j].reshape(D, D), a_fmt="kmb", res=x, name=f"mla_out_{layer}")
        s["x_mid"] = x
        h = _rmsnorm_fwd(x, norm_ffn[layer], name=f"norm_ffn_fwd_{layer}")
        s["h_ffn"] = h
        a, u = _ffn_up_fwd(h, gathered["ffn_w_up", layer], conv_w_all[layer], conv_b_s[layer], name=f"ffn_up_{layer}")
        s.update(a=a, u=u)
        x = _mm(a, gathered["ffn_w_down", layer].reshape(-1, D), a_fmt="mkb", res=x, bk=fs, name=f"ffn_down_{layer}")
        saved.append(s)

    dx, loss_part = _loss_head(x, target, name="loss_head")

    parts = {}
    received = {}
    g_small = {n: [None] * local[n].shape[0] for n in REPLICATED + small_names}

    def scatter_ride(l):
        keys = riders(l)
        return keys, ([parts[key] for key in keys], ["scatter"] * len(keys))

    def take_received(keys, arrs):
        for key, a in zip(keys, arrs):
            received[key] = a

    for layer in reversed(range(depth)):
        j = layer // 2
        s = saved[layer]
        parts["ffn_w_down", layer] = _mm(s["a"], dx, a_fmt="kmb", out_dtype=BF16, name=f"ffn_down_dw_{layer}"
                                         ).reshape(N_DEV, -1, D)
        du, dcw, dcb = _ffn_act_bwd(dx, gathered["ffn_w_down", layer].reshape(-1, D), s["u"], conv_w_all[layer],
                                    conv_b_s[layer], name=f"ffn_act_bwd_{layer}")
        g_small["ffn_conv_w"][layer] = dcw.reshape(N_DEV, 3, fs)
        g_small["ffn_conv_b"][layer] = dcb.reshape(N_DEV * fs)
        du8 = du.reshape(N_DEV, T, fs)
        parts["ffn_w_up", layer] = _mm(s["h_ffn"], du8, a_fmt="km", b_fmt="knb", out_fmt="mnb", out_dtype=BF16,
                                       bm=1024, name=f"ffn_up_dw_{layer}")
        dh = _mm(du8, gathered["ffn_w_up", layer], a_fmt="mkb", b_fmt="nkb", bk=fs, name=f"ffn_up_dh_{layer}")
        dx, dgain = _rmsnorm_bwd(s["x_mid"], norm_ffn[layer], dh, dx, name=f"norm_ffn_bwd_{layer}")
        g_small["norm_ffn"][layer] = dgain.reshape(D)
        keys, ride = scatter_ride(layer)
        if layer % 2 == 0:
            w_out = gathered["hgrn_w_out", j].reshape(D, D)
            parts["hgrn_w_out", j] = _mm(s["og"], dx, a_fmt="km", out_dtype=BF16, bm=1024,
                                         name=f"hgrn_out_dw_{layer}").reshape(N_DEV, -1, D)
            dog = _mm(dx, w_out, b_fmt="nk", name=f"hgrn_out_dx_{layer}")
            (dp, dlb, dgn), got = _hgrn_bwd(s["p"], lower_bounds[j], hgrn_out_norm[j], s["s0"], s["o"], dog,
                                            name=f"hgrn_bwd_{layer}", ride=ride)
            take_received(keys, got)
            g_small["hgrn_lower_bounds"][j] = dlb.reshape(D)
            g_small["hgrn_out_norm"][j] = dgn.reshape(HEAD)
            w_in_s = gathered["hgrn_w_in", j]
            parts["hgrn_w_in", j] = _mm(s["h_mix"], dp, a_fmt="km", out_fmt="mnb", out_dtype=BF16, bm=1024,
                                        bn=w_in_s.shape[2], name=f"hgrn_in_dw_{layer}")
            dh = _mm(dp, w_in_s, b_fmt="nkb", name=f"hgrn_in_dx_{layer}")
        else:
            w_in, wq, wkv, gq, gk = mla_views(j)
            w_out = gathered["mla_w_out", j].reshape(D, D)
            tb = s["ot"].shape[2]
            parts["mla_w_out", j] = _mm(s["ot"], dx, a_fmt="mkb", bk=tb, out_dtype=BF16, bm=1024,
                                        name=f"mla_out_dw_{layer}").reshape(N_DEV, -1, D)
            dot_ = _mm(w_out, dx, b_fmt="nk", out_fmt="mnb", out_dtype=BF16, bm=D, bn=tb,
                       name=f"mla_out_dx_{layer}")
            (dq, dk, dv), got = _attn_bwd(s["q"], s["k"], s["v"], s["ot"], s["lse"], dot_,
                                          name=f"attn_bwd_{layer}", ride=ride)
            take_received(keys, got)
            dc, dwq, dwkv, dgaq, dgakv, dgq, dgk = _mla_prep_bwd(
                s["c"], wq, wkv, qa_all[j], kva_all[j], gq, gk, ct, s1, s2, dq, dk, dv,
                name=f"mla_prep_bwd_{layer}")
            parts["mla_w_q_up", j] = _cols_to_shards(_q_up_unpadded(dwq)).astype(BF16)
            parts["mla_w_kv_up", j] = _cols_to_shards(dwkv).astype(BF16)
            g_small["mla_q_a_norm"][j] = dgaq.reshape(R)
            g_small["mla_kv_a_norm"][j] = dgakv.reshape(R)
            g_small["mla_q_norm"][j] = dgq[0, :QK_HEAD]
            g_small["mla_k_norm"][j] = dgk[0, :QK_HEAD]
            win_cols = mla_w_in.shape[2]
            dw_in = _mm(s["h_mix"], dc, a_fmt="km", bm=1024, name=f"mla_in_dw_{layer}")
            parts["mla_w_in", j] = dw_in[:, :win_cols].astype(BF16).reshape(N_DEV, -1, win_cols)
            dh = _mm(dc, w_in, b_fmt="nk", name=f"mla_in_dx_{layer}")
        dx, dgain = _rmsnorm_bwd(s["x_in"], norm_mix[layer], dh, dx, name=f"norm_mix_bwd_{layer}")
        g_small["norm_mix"][layer] = dgain.reshape(D)
    grad_x = dx

    dlb_eff = jnp.stack(g_small["hgrn_lower_bounds"])
    dsoft = jnp.cumsum(dlb_eff[::-1], axis=0)[::-1]
    dsoft = dsoft.at[0].add(-jnp.sum(dlb_eff, axis=0))
    g_lb = lb_soft * (dsoft - jnp.sum(dsoft * lb_soft, axis=0, keepdims=True))
    small_grads = {n: (g_lb if n == "hgrn_lower_bounds" else jnp.stack(g_small[n])) for n in g_small}

    small_grad_names = REPLICATED + small_names
    small_part = _pack([small_grads[n] for n in small_grad_names] + [loss_part], F32, 8)
    got = _exchange([parts[key] for key in keys0] + [small_part], ["scatter"] * len(keys0) + ["gather"],
                    name="exchange_last")
    take_received(keys0, got[:-1])
    small_recv = got[-1]

    out = {}
    for n in BIG:
        layers = local[n].shape[0]
        shard = local[n].shape[1:]
        p2 = jnp.concatenate([received[n, i].reshape(N_DEV, -1, shard[-1]) for i in range(layers)], axis=1)
        flat = lambda a: a.reshape(-1, shard[-1])
        res = _adamw(p2, flat(local[n]), flat(mom_m[n]), flat(mom_v[n]), name=f"adamw_{n}")
        for kind, a in zip(("grad", "delta", "new_m", "new_v"), res):
            out[kind, n] = a.reshape(local[n].shape)
    small_sum = _sum_slots(small_recv, name="sum_small")
    small_full = _unpack(small_sum, [small_grads[n].shape for n in small_grad_names] + [(1, LANES)])
    loss = small_full[-1][0, 0]
    g_mine = {}
    for n, a in zip(small_grad_names, small_full[:-1]):
        if n == "ffn_conv_w":
            a = lax.dynamic_index_in_dim(a, me, axis=1, keepdims=False)
        elif n in SMALL_SHARDED:
            size = local[n].shape[1]
            a = lax.dynamic_slice_in_dim(a, me * size, size, axis=1)
        g_mine[n] = a
    small_shapes = [local[n].shape for n in small_grad_names]
    res = _adamw(_pack([g_mine[n] for n in small_grad_names], F32, 8)[None],
                 _pack([local[n] for n in small_grad_names], F32, 8),
                 _pack([mom_m[n] for n in small_grad_names], F32, 8),
                 _pack([mom_v[n] for n in small_grad_names], F32, 8), name="adamw_small")
    for kind, packed in zip(("grad", "delta", "new_m", "new_v"), res):
        for n, a in zip(small_grad_names, _unpack(packed, small_shapes)):
            out[kind, n] = a

    outs = [loss, grad_x[None]]
    for kind in ("grad", "delta", "new_m", "new_v"):
        outs += [out[kind, n] for n in WEIGHTS]
    return tuple(outs)
```

```python
import functools

import jax
import jax.numpy as jnp
from jax import lax
from jax.experimental import pallas as pl
from jax.experimental.pallas import tpu as pltpu

F32 = jnp.float32
BF16 = jnp.bfloat16

RMS_EPS = 1e-6
ROPE_THETA = 10000.0
HEAD = 128
ROPE = 64
QK_HEAD = HEAD + ROPE
QK_PAD = 256
CHUNK = 64
SUB = 16
EXP_CLAMP = 60.0

ADAM_LR = 0.001
ADAM_B1 = 0.9
ADAM_B2 = 0.999
ADAM_EPS = 1e-08
ADAM_WD = 0.01
ADAM_STEP = 10

N_DEV = 8
LANES = 128
PACK_COLS = 1024
V7X_VMEM_LIMIT = 56 * 1024 * 1024

HI = lax.Precision.HIGHEST


def _params(sem):
    return pltpu.CompilerParams(dimension_semantics=sem, vmem_limit_bytes=V7X_VMEM_LIMIT)


def _blk(n, cap):
    if n <= cap:
        return n
    d = (cap // LANES) * LANES
    while d >= LANES:
        if n % d == 0:
            return d
        d -= LANES
    raise ValueError(f"no lane-aligned block for {n} under {cap}")


def _sigmoid(x):
    return jax.nn.sigmoid(x)


def _dot(a, b, dims, precision=None):
    return lax.dot_general(a, b, (dims, ((), ())), preferred_element_type=F32, precision=precision)


NN = ((1,), (0,))
NT = ((1,), (1,))
TN = ((0,), (0,))


def _mm(a, b, *, a_fmt="mk", b_fmt="kn", out_fmt="mn", res=None, out_dtype=F32, bm=512, bn=1024, bk=1024,
        name):
    if a_fmt == "mk":
        M, K = a.shape
    elif a_fmt == "km":
        K, M = a.shape
    elif a_fmt == "kmb":
        nb, K, B = a.shape
        M = nb * B
    else:
        nb, M, B = a.shape
        K = nb * B
    if b_fmt == "kn":
        Kb, N = b.shape
    elif b_fmt == "nk":
        N, Kb = b.shape
    elif b_fmt == "knb":
        nbb, Kb, Bb = b.shape
        N = nbb * Bb
    else:
        nbb, N, Bb = b.shape
        Kb = nbb * Bb
    assert K == Kb, (a.shape, b.shape, a_fmt, b_fmt)
    bm = B if a_fmt == "kmb" else _blk(M, bm)
    bn = Bb if b_fmt == "knb" else _blk(N, bn)
    if a_fmt == "mkb" and b_fmt == "nkb":
        assert B == Bb
    bk = _blk(B, bk) if a_fmt == "mkb" else (_blk(Bb, bk) if b_fmt == "nkb" else _blk(K, bk))
    nm, nn, nk = M // bm, N // bn, K // bk

    if a_fmt == "mk":
        a_spec = pl.BlockSpec((bm, bk), lambda i, j, k: (i, k))
        a_dim = 1
    elif a_fmt == "km":
        a_spec = pl.BlockSpec((bk, bm), lambda i, j, k: (k, i))
        a_dim = 0
    elif a_fmt == "kmb":
        a_spec = pl.BlockSpec((None, bk, bm), lambda i, j, k: (i, k, 0))
        a_dim = 0
    else:
        per = B // bk
        a_spec = pl.BlockSpec((None, bm, bk), lambda i, j, k: (k // per, i, k % per))
        a_dim = 1
    if b_fmt == "kn":
        b_spec = pl.BlockSpec((bk, bn), lambda i, j, k: (k, j))
        b_dim = 0
    elif b_fmt == "nk":
        b_spec = pl.BlockSpec((bn, bk), lambda i, j, k: (j, k))
        b_dim = 1
    elif b_fmt == "knb":
        b_spec = pl.BlockSpec((None, bk, bn), lambda i, j, k: (j, k, 0))
        b_dim = 0
    else:
        perb = Bb // bk
        b_spec = pl.BlockSpec((None, bn, bk), lambda i, j, k: (k // perb, j, k % perb))
        b_dim = 1
    if out_fmt == "mn":
        o_spec = pl.BlockSpec((bm, bn), lambda i, j, k: (i, j))
        o_shape = jax.ShapeDtypeStruct((M, N), out_dtype)
    else:
        o_spec = pl.BlockSpec((None, bm, bn), lambda i, j, k: (j, i, 0))
        o_shape = jax.ShapeDtypeStruct((nn, M, bn), out_dtype)
    in_specs = [a_spec, b_spec]
    args = [a, b]
    if res is not None:
        assert out_fmt == "mn"
        in_specs.append(pl.BlockSpec((bm, bn), lambda i, j, k: (i, j)))
        args.append(res)
    dims = ((a_dim,), (b_dim,))
    has_res = res is not None

    def body(*refs):
        a_ref, b_ref = refs[0], refs[1]
        r_ref = refs[2] if has_res else None
        o_ref = refs[3] if has_res else refs[2]
        part = _dot(a_ref[...].astype(BF16), b_ref[...].astype(BF16), dims)

        def finish(out):
            if has_res:
                out = out + r_ref[...]
            o_ref[...] = out.astype(o_ref.dtype)

        if nk == 1:
            finish(part)
            return
        acc_ref = refs[-1]
        k = pl.program_id(2)

        @pl.when(k == 0)
        def _():
            acc_ref[...] = part

        @pl.when(jnp.logical_and(k > 0, k < nk - 1))
        def _():
            acc_ref[...] += part

        @pl.when(k == nk - 1)
        def _():
            finish(acc_ref[...] + part)

    return pl.pallas_call(
        body, name=name, grid=(nm, nn, nk), in_specs=in_specs, out_specs=o_spec, out_shape=o_shape,
        scratch_shapes=[] if nk == 1 else [pltpu.VMEM((bm, bn), F32)],
        compiler_params=_params(("parallel", "parallel", "arbitrary")),
    )(*args)


def _rmsnorm_fwd(x, gain, *, name, tm=512):
    T, D = x.shape
    tm = min(tm, T)

    def body(x_ref, g_ref, o_ref):
        xv = x_ref[...]
        r = lax.rsqrt(jnp.mean(xv * xv, axis=-1, keepdims=True) + RMS_EPS)
        o_ref[...] = ((xv * r) * g_ref[...]).astype(o_ref.dtype)

    return pl.pallas_call(
        body, name=name, grid=(T // tm,),
        in_specs=[pl.BlockSpec((tm, D), lambda i: (i, 0)), pl.BlockSpec((1, D), lambda i: (0, 0))],
        out_specs=pl.BlockSpec((tm, D), lambda i: (i, 0)),
        out_shape=jax.ShapeDtypeStruct((T, D), BF16),
        compiler_params=_params(("parallel",)),
    )(x, gain.reshape(1, D))


def _rmsnorm_bwd(x, gain, dh, dres, *, name, tm=512):
    T, D = x.shape
    tm = min(tm, T)

    def body(x_ref, g_ref, dh_ref, dr_ref, dx_ref, dg_ref):
        i = pl.program_id(0)
        xv = x_ref[...]
        r = lax.rsqrt(jnp.mean(xv * xv, axis=-1, keepdims=True) + RMS_EPS)
        n = xv * r
        dy = dh_ref[...].astype(F32)
        dn = dy * g_ref[...]
        dx_ref[...] = dr_ref[...] + r * (dn - n * jnp.mean(dn * n, axis=-1, keepdims=True))
        part = jnp.sum(dy * n, axis=0, keepdims=True)

        @pl.when(i == 0)
        def _():
            dg_ref[...] = part

        @pl.when(i > 0)
        def _():
            dg_ref[...] += part

    return pl.pallas_call(
        body, name=name, grid=(T // tm,),
        in_specs=[pl.BlockSpec((tm, D), lambda i: (i, 0)), pl.BlockSpec((1, D), lambda i: (0, 0)),
                  pl.BlockSpec((tm, D), lambda i: (i, 0)), pl.BlockSpec((tm, D), lambda i: (i, 0))],
        out_specs=[pl.BlockSpec((tm, D), lambda i: (i, 0)), pl.BlockSpec((1, D), lambda i: (0, 0))],
        out_shape=[jax.ShapeDtypeStruct((T, D), F32), jax.ShapeDtypeStruct((1, D), F32)],
        compiler_params=_params(("arbitrary",)),
    )(x, gain.reshape(1, D), dh, dres)


def _loss_head(y, target, *, name, tm=512):
    T, D = y.shape
    tm = min(tm, T)

    def body(y_ref, t_ref, dy_ref, l_ref):
        i = pl.program_id(0)
        e = y_ref[...] - t_ref[...]
        dy_ref[...] = e * (1.0 / D)
        s = 0.5 * jnp.sum(jnp.mean(e * e, axis=-1, keepdims=True), axis=0, keepdims=True)
        part = jnp.broadcast_to(s, (1, LANES))

        @pl.when(i == 0)
        def _():
            l_ref[...] = part

        @pl.when(i > 0)
        def _():
            l_ref[...] += part

    return pl.pallas_call(
        body, name=name, grid=(T // tm,),
        in_specs=[pl.BlockSpec((tm, D), lambda i: (i, 0)), pl.BlockSpec((tm, D), lambda i: (i, 0))],
        out_specs=[pl.BlockSpec((tm, D), lambda i: (i, 0)), pl.BlockSpec((1, LANES), lambda i: (0, 0))],
        out_shape=[jax.ShapeDtypeStruct((T, D), F32), jax.ShapeDtypeStruct((1, LANES), F32)],
        compiler_params=_params(("arbitrary",)),
    )(y, target)


def _hgrn_selectors():
    t = jnp.arange(CHUNK)[:, None]
    s = jnp.arange(CHUNK)[None, :]
    mats = [s <= t, s < (t // SUB) * SUB]
    for i in range(1, CHUNK // SUB):
        mats.append(jnp.broadcast_to(s < i * SUB, (8, CHUNK)))
    mats.append(jnp.ones((8, CHUNK), bool))
    sel = jnp.concatenate([m.astype(BF16) for m in mats], axis=0)
    rev = (s >= t).astype(BF16)
    return sel, rev


def _select_sums(sel, x):
    hi = x.astype(BF16)
    r1 = x - hi.astype(F32)
    mid = r1.astype(BF16)
    lo = (r1 - mid.astype(F32)).astype(BF16)
    return _dot(sel, hi, NN) + (_dot(sel, mid, NN) + _dot(sel, lo, NN))


def _hgrn_cums(sel, logf):
    nsub = CHUNK // SUB
    cums = _select_sums(sel, logf)
    g = cums[0:CHUNK]
    rrow = cums[CHUNK:2 * CHUNK]
    base = 2 * CHUNK
    rsel = [None] + [jnp.max(cums[base + 8 * (i - 1):base + 8 * i], axis=0, keepdims=True) for i in range(1, nsub)]
    gl = jnp.max(cums[base + 8 * (nsub - 1):base + 8 * nsub], axis=0, keepdims=True)
    return g, rrow, rsel, gl


def _hgrn_gates(p, lb, D):
    qpre, fpre, iv, gpre = p[:, 0:D], p[:, D:2 * D], p[:, 2 * D:3 * D], p[:, 3 * D:4 * D]
    sig = _sigmoid(fpre)
    forget = lb + (1.0 - lb) * sig
    key = 1.0 - forget
    logf = jnp.log(forget)
    sq = _sigmoid(qpre)
    qs = qpre * sq
    return qpre, sq, qs, sig, forget, key, logf, iv, gpre


def _hgrn_scores(qt, kh, gh, rsel, row, col):
    nsub = CHUNK // SUB
    blocks, kts = [], []
    for i in range(nsub):
        ri = rsel[i]
        kt = kh * jnp.exp(jnp.minimum(ri - gh, EXP_CLAMP)) if ri is not None else \
            kh * jnp.exp(jnp.minimum(-gh, EXP_CLAMP))
        kts.append(kt)
        blocks.append(_dot(qt[i * SUB:(i + 1) * SUB].astype(BF16), kt.astype(BF16), NT))
    a = jnp.concatenate(blocks, axis=0)
    return jnp.where(col <= row, a, 0.0), kts


def _hgrn_fwd(p, lb, gn, *, name, ride=None):
    T, D4 = p.shape
    D = D4 // 4
    H = D // HEAD
    nc = T // CHUNK
    sel, _ = _hgrn_selectors()
    nsel = sel.shape[0]
    nsub = CHUNK // SUB

    def body(p_ref, lb_ref, gn_ref, sel_ref, og_ref, o_ref, s0_ref, st_ref):
        c = pl.program_id(0)

        @pl.when(c == 0)
        def _():
            st_ref[...] = jnp.zeros_like(st_ref)

        pv = p_ref[...]
        _, _, qs, _, _, key, logf, iv, gpre = _hgrn_gates(pv, lb_ref[...], D)
        g, rrow, rsel_all, gl = _hgrn_cums(sel_ref[...], logf)
        s0_ref[...] = st_ref[...]
        row = lax.broadcasted_iota(jnp.int32, (CHUNK, CHUNK), 0)
        col = lax.broadcasted_iota(jnp.int32, (CHUNK, CHUNK), 1)
        gnv = gn_ref[...]
        for h in range(H):
            sl = slice(h * HEAD, (h + 1) * HEAD)
            gh, qh, kh, vh = g[:, sl], qs[:, sl], key[:, sl], iv[:, sl]
            st = st_ref[sl, :]
            o = _dot((qh * jnp.exp(gh)).astype(BF16), st.astype(BF16), NT)
            qt = qh * jnp.exp(gh - rrow[:, sl])
            a, _ = _hgrn_scores(qt, kh, gh, [None if r is None else r[:, sl] for r in rsel_all], row, col)
            o = o + _dot(a.astype(BF16), vh.astype(BF16), NN)
            glh = gl[:, sl]
            kd = kh * jnp.exp(glh - gh)
            st_ref[sl, :] = st * jnp.exp(glh) + _dot(vh.astype(BF16), kd.astype(BF16), TN)
            o_ref[:, sl] = o
            r = lax.rsqrt(jnp.mean(o * o, axis=-1, keepdims=True) + RMS_EPS)
            gp = gpre[:, sl]
            og_ref[:, sl] = (((o * r) * gnv) * (gp * _sigmoid(gp))).astype(og_ref.dtype)

    return _call(
        body, name=name, grid=(nc,),
        in_specs=[pl.BlockSpec((CHUNK, D4), lambda c: (c, 0)), pl.BlockSpec((1, D), lambda c: (0, 0)),
                  pl.BlockSpec((1, HEAD), lambda c: (0, 0)), pl.BlockSpec((nsel, CHUNK), lambda c: (0, 0))],
        out_specs=[pl.BlockSpec((CHUNK, D), lambda c: (c, 0)), pl.BlockSpec((CHUNK, D), lambda c: (c, 0)),
                   pl.BlockSpec((None, D, HEAD), lambda c: (c, 0, 0))],
        out_shape=[jax.ShapeDtypeStruct((T, D), BF16), jax.ShapeDtypeStruct((T, D), F32),
                   jax.ShapeDtypeStruct((nc, D, HEAD), F32)],
        scratch_shapes=[pltpu.VMEM((D, HEAD), F32)], sem=("arbitrary",), ride=ride,
        args=(p, lb.reshape(1, D), gn.reshape(1, HEAD), sel))


def _hgrn_bwd(p, lb, gn, s0, o_saved, dog, *, name, ride=None):
    T, D4 = p.shape
    D = D4 // 4
    H = D // HEAD
    nc = T // CHUNK
    sel, rev = _hgrn_selectors()
    nsel = sel.shape[0]
    nsub = CHUNK // SUB

    def body(p_ref, lb_ref, gn_ref, sel_ref, rev_ref, s0_ref, s1_ref, o_ref, dog_ref,
             dp_ref, dlb_ref, dgn_ref, dst_ref):
        c = pl.program_id(0)

        @pl.when(c == 0)
        def _():
            dst_ref[...] = jnp.zeros_like(dst_ref)
            dlb_ref[...] = jnp.zeros_like(dlb_ref)
            dgn_ref[...] = jnp.zeros_like(dgn_ref)

        pv = p_ref[...]
        lbv = lb_ref[...]
        qpre, sq, qs, sig, forget, key, logf, iv, gpre = _hgrn_gates(pv, lbv, D)
        g, rrow, rsel_all, gl = _hgrn_cums(sel_ref[...], logf)
        row = lax.broadcasted_iota(jnp.int32, (CHUNK, CHUNK), 0)
        col = lax.broadcasted_iota(jnp.int32, (CHUNK, CHUNK), 1)
        causal = col <= row
        gnv = gn_ref[...]
        dgn_acc = jnp.zeros((1, HEAD), F32)
        for h in range(H):
            sl = slice(h * HEAD, (h + 1) * HEAD)
            gh, qh, kh, vh = g[:, sl], qs[:, sl], key[:, sl], iv[:, sl]
            glh = gl[:, sl]
            st0 = s0_ref[sl, :]
            st1 = s1_ref[sl, :]
            dst = dst_ref[sl, :]
            o = o_ref[:, sl]
            r = lax.rsqrt(jnp.mean(o * o, axis=-1, keepdims=True) + RMS_EPS)
            n = o * r
            gp = gpre[:, sl]
            sg = _sigmoid(gp)
            dog_h = dog_ref[:, sl]
            d_on = dog_h * (gp * sg)
            dgpre = dog_h * (n * gnv) * (sg * (1.0 + gp * (1.0 - sg)))
            dgn_acc = dgn_acc + jnp.sum(d_on * n, axis=0, keepdims=True)
            dn = d_on * gnv
            do = r * (dn - n * jnp.mean(dn * n, axis=-1, keepdims=True))
            dob = do.astype(BF16)
            eg = jnp.exp(gh)
            qg = qh * eg
            eqr = jnp.exp(gh - rrow[:, sl])
            qt = qh * eqr
            rsel = [None if rr is None else rr[:, sl] for rr in rsel_all]
            a, kts = _hgrn_scores(qt, kh, gh, rsel, row, col)
            ekd = jnp.exp(glh - gh)
            kd = kh * ekd
            vb = vh.astype(BF16)
            dq = _dot(dob, st0.astype(BF16), NN) * eg
            da = jnp.where(causal, _dot(dob, vb, NT), 0.0)
            dab = da.astype(BF16)
            dqt_blocks = []
            dk = _dot(vb, dst.astype(BF16), NN) * ekd
            for i in range(nsub):
                rs = slice(i * SUB, (i + 1) * SUB)
                ktb = kts[i].astype(BF16)
                dqt_blocks.append(_dot(dab[rs], ktb, NN))
                dkt = _dot(dab[rs], qt[rs].astype(BF16), TN)
                ei = jnp.exp(jnp.minimum((rsel[i] if rsel[i] is not None else 0.0) - gh, EXP_CLAMP))
                dk = dk + dkt * ei
            dq = dq + jnp.concatenate(dqt_blocks, axis=0) * eqr
            dv = _dot(a.astype(BF16), dob, TN) + _dot(kd.astype(BF16), dst.astype(BF16), NT)
            dst_ref[sl, :] = dst * jnp.exp(glh) + _dot(dob, qg.astype(BF16), TN)
            term = jnp.sum(dst * st1, axis=0, keepdims=True)
            dg = qh * dq - kh * dk
            dlogf = _select_sums(rev_ref[...], dg) + term
            fg = forget[:, sl]
            sgf = sig[:, sl]
            lbh = lbv[:, sl]
            dforget = dlogf / fg - dk
            dfpre = dforget * (1.0 - lbh) * (sgf * (1.0 - sgf))
            dlb_ref[:, sl] += jnp.sum(dforget * (1.0 - sgf), axis=0, keepdims=True)
            sqh = sq[:, sl]
            dqpre = dq * (sqh * (1.0 + qpre[:, sl] * (1.0 - sqh)))
            dp_ref[:, h * HEAD:(h + 1) * HEAD] = dqpre.astype(dp_ref.dtype)
            dp_ref[:, D + h * HEAD:D + (h + 1) * HEAD] = dfpre.astype(dp_ref.dtype)
            dp_ref[:, 2 * D + h * HEAD:2 * D + (h + 1) * HEAD] = dv.astype(dp_ref.dtype)
            dp_ref[:, 3 * D + h * HEAD:3 * D + (h + 1) * HEAD] = dgpre.astype(dp_ref.dtype)
        dgn_ref[...] += dgn_acc

    rc = lambda c: nc - 1 - c
    return _call(
        body, name=name, grid=(nc,),
        in_specs=[pl.BlockSpec((CHUNK, D4), lambda c: (rc(c), 0)), pl.BlockSpec((1, D), lambda c: (0, 0)),
                  pl.BlockSpec((1, HEAD), lambda c: (0, 0)), pl.BlockSpec((nsel, CHUNK), lambda c: (0, 0)),
                  pl.BlockSpec((CHUNK, CHUNK), lambda c: (0, 0)),
                  pl.BlockSpec((None, D, HEAD), lambda c: (rc(c), 0, 0)),
                  pl.BlockSpec((None, D, HEAD), lambda c: (jnp.minimum(rc(c) + 1, nc - 1), 0, 0)),
                  pl.BlockSpec((CHUNK, D), lambda c: (rc(c), 0)), pl.BlockSpec((CHUNK, D), lambda c: (rc(c), 0))],
        out_specs=[pl.BlockSpec((CHUNK, D4), lambda c: (rc(c), 0)), pl.BlockSpec((1, D), lambda c: (0, 0)),
                   pl.BlockSpec((1, HEAD), lambda c: (0, 0))],
        out_shape=[jax.ShapeDtypeStruct((T, D4), BF16), jax.ShapeDtypeStruct((1, D), F32),
                   jax.ShapeDtypeStruct((1, HEAD), F32)],
        scratch_shapes=[pltpu.VMEM((D, HEAD), F32)], sem=("arbitrary",), ride=ride,
        args=(p, lb.reshape(1, D), gn.reshape(1, HEAD), sel, rev, s0, s0, o_saved, dog))


def _rope_tables(positions):
    inv_freq = ROPE_THETA ** (-jnp.arange(0, ROPE, 2, dtype=F32) / ROPE)
    ang = positions.astype(F32)[:, None] * inv_freq
    cos, sin = jnp.cos(ang), jnp.sin(ang)
    z = jnp.zeros_like(cos)
    ctab = jnp.concatenate([cos, cos, z, z], axis=-1)
    s1 = jnp.concatenate([-sin, z, z, z], axis=-1)
    s2 = jnp.concatenate([z, sin, z, z], axis=-1)
    return ctab, s1, s2


def _rope(z, ct, s1, s2):
    return z * ct + pltpu.roll(z, 96, 1) * s1 + pltpu.roll(z, 32, 1) * s2


def _rope_t(d, ct, s1, s2):
    return d * ct + pltpu.roll(d * s1, 32, 1) + pltpu.roll(d * s2, 96, 1)


def _mla_prep_fwd(c, wq, wkv, ga_q, ga_kv, gq, gk, ct, s1, s2, *, name, tm=256):
    T, CW = c.shape
    R = (CW - LANES) // 2
    H = wq.shape[1] // QK_PAD
    tm = min(tm, T)

    def body(c_ref, wq_ref, wkv_ref, gaq_ref, gakv_ref, gq_ref, gk_ref, ct_ref, s1_ref, s2_ref,
             q_ref, k_ref, v_ref):
        cv = c_ref[...]
        cq, ckv, kr = cv[:, 0:R], cv[:, R:2 * R], cv[:, 2 * R:2 * R + LANES]
        rq = lax.rsqrt(jnp.mean(cq * cq, axis=-1, keepdims=True) + RMS_EPS)
        cqn = ((cq * rq) * gaq_ref[...]).astype(BF16)
        rk = lax.rsqrt(jnp.mean(ckv * ckv, axis=-1, keepdims=True) + RMS_EPS)
        ckvn = ((ckv * rk) * gakv_ref[...]).astype(BF16)
        qp = _dot(cqn, wq_ref[...], NN)
        kvp = _dot(ckvn, wkv_ref[...], NN)
        ctv, s1v, s2v = ct_ref[...], s1_ref[...], s2_ref[...]
        gqv, gkv = gq_ref[...], gk_ref[...]
        krs = jnp.sum(kr * kr, axis=-1, keepdims=True)
        for h in range(H):
            b = h * QK_PAD
            qn, qr = qp[:, b:b + HEAD], qp[:, b + HEAD:b + QK_PAD]
            ss = jnp.sum(qn * qn, axis=-1, keepdims=True) + jnp.sum(qr * qr, axis=-1, keepdims=True)
            rr = lax.rsqrt(ss * (1.0 / QK_HEAD) + RMS_EPS)
            q_ref[:, b:b + HEAD] = ((qn * rr) * gqv[:, 0:HEAD]).astype(q_ref.dtype)
            q_ref[:, b + HEAD:b + QK_PAD] = _rope((qr * rr) * gqv[:, HEAD:QK_PAD], ctv, s1v, s2v).astype(q_ref.dtype)
            kn, vv = kvp[:, b:b + HEAD], kvp[:, b + HEAD:b + QK_PAD]
            ssk = jnp.sum(kn * kn, axis=-1, keepdims=True) + krs
            rrk = lax.rsqrt(ssk * (1.0 / QK_HEAD) + RMS_EPS)
            k_ref[:, b:b + HEAD] = ((kn * rrk) * gkv[:, 0:HEAD]).astype(k_ref.dtype)
            k_ref[:, b + HEAD:b + QK_PAD] = _rope((kr * rrk) * gkv[:, HEAD:QK_PAD], ctv, s1v, s2v).astype(k_ref.dtype)
            v_ref[:, h * HEAD:(h + 1) * HEAD] = vv.astype(v_ref.dtype)

    full = lambda shape: pl.BlockSpec(shape, lambda i: (0, 0))
    tok = lambda w: pl.BlockSpec((tm, w), lambda i: (i, 0))
    return pl.pallas_call(
        body, name=name, grid=(T // tm,),
        in_specs=[tok(CW), full(wq.shape), full(wkv.shape), full((1, R)), full((1, R)), full((1, QK_PAD)),
                  full((1, QK_PAD)), tok(LANES), tok(LANES), tok(LANES)],
        out_specs=[tok(H * QK_PAD), tok(H * QK_PAD), tok(H * HEAD)],
        out_shape=[jax.ShapeDtypeStruct((T, H * QK_PAD), BF16), jax.ShapeDtypeStruct((T, H * QK_PAD), BF16),
                   jax.ShapeDtypeStruct((T, H * HEAD), BF16)],
        compiler_params=_params(("parallel",)),
    )(c, wq, wkv, ga_q.reshape(1, R), ga_kv.reshape(1, R), gq, gk, ct, s1, s2)


def _mla_prep_bwd(c, wq, wkv, ga_q, ga_kv, gq, gk, ct, s1, s2, dq, dk, dv, *, name, tm=256):
    T, CW = c.shape
    R = (CW - LANES) // 2
    H = wq.shape[1] // QK_PAD
    tm = min(tm, T)

    def body(c_ref, wq_ref, wkv_ref, gaq_ref, gakv_ref, gq_ref, gk_ref, ct_ref, s1_ref, s2_ref,
             dq_ref, dk_ref, dv_ref,
             dc_ref, dwq_ref, dwkv_ref, dgaq_ref, dgakv_ref, dgq_ref, dgk_ref, dqp_ref, dkvp_ref):
        i = pl.program_id(0)

        @pl.when(i == 0)
        def _():
            for ref in (dwq_ref, dwkv_ref, dgaq_ref, dgakv_ref, dgq_ref, dgk_ref):
                ref[...] = jnp.zeros_like(ref)

        cv = c_ref[...]
        cq, ckv, kr = cv[:, 0:R], cv[:, R:2 * R], cv[:, 2 * R:2 * R + LANES]
        rq = lax.rsqrt(jnp.mean(cq * cq, axis=-1, keepdims=True) + RMS_EPS)
        nq = cq * rq
        cqn = (nq * gaq_ref[...]).astype(BF16)
        rk = lax.rsqrt(jnp.mean(ckv * ckv, axis=-1, keepdims=True) + RMS_EPS)
        nkv = ckv * rk
        ckvn = (nkv * gakv_ref[...]).astype(BF16)
        qp = _dot(cqn, wq_ref[...], NN)
        kvp = _dot(ckvn, wkv_ref[...], NN)
        ctv, s1v, s2v = ct_ref[...], s1_ref[...], s2_ref[...]
        gqv, gkv = gq_ref[...], gk_ref[...]
        krs = jnp.sum(kr * kr, axis=-1, keepdims=True)
        dkr = jnp.zeros((tm, LANES), F32)
        dgq_n = jnp.zeros((1, HEAD), F32)
        dgq_r = jnp.zeros((1, HEAD), F32)
        dgk_n = jnp.zeros((1, HEAD), F32)
        dgk_r = jnp.zeros((1, HEAD), F32)
        for h in range(H):
            b = h * QK_PAD
            qn, qr = qp[:, b:b + HEAD], qp[:, b + HEAD:b + QK_PAD]
            ss = jnp.sum(qn * qn, axis=-1, keepdims=True) + jnp.sum(qr * qr, axis=-1, keepdims=True)
            rr = lax.rsqrt(ss * (1.0 / QK_HEAD) + RMS_EPS)
            un, ur = qn * rr, qr * rr
            dzn = dq_ref[:, b:b + HEAD]
            dzr = _rope_t(dq_ref[:, b + HEAD:b + QK_PAD], ctv, s1v, s2v)
            dgq_n = dgq_n + jnp.sum(dzn * un, axis=0, keepdims=True)
            dgq_r = dgq_r + jnp.sum(dzr * ur, axis=0, keepdims=True)
            dun, dur = dzn * gqv[:, 0:HEAD], dzr * gqv[:, HEAD:QK_PAD]
            m = (jnp.sum(dun * un, axis=-1, keepdims=True) + jnp.sum(dur * ur, axis=-1, keepdims=True)) \
                * (1.0 / QK_HEAD)
            dqp_ref[:, b:b + HEAD] = (rr * (dun - un * m)).astype(BF16)
            dqp_ref[:, b + HEAD:b + QK_PAD] = (rr * (dur - ur * m)).astype(BF16)
            kn = kvp[:, b:b + HEAD]
            ssk = jnp.sum(kn * kn, axis=-1, keepdims=True) + krs
            rrk = lax.rsqrt(ssk * (1.0 / QK_HEAD) + RMS_EPS)
            vn, vr = kn * rrk, kr * rrk
            dyn = dk_ref[:, b:b + HEAD]
            dyr = _rope_t(dk_ref[:, b + HEAD:b + QK_PAD], ctv, s1v, s2v)
            dgk_n = dgk_n + jnp.sum(dyn * vn, axis=0, keepdims=True)
            dgk_r = dgk_r + jnp.sum(dyr * vr, axis=0, keepdims=True)
            dvn, dvr = dyn * gkv[:, 0:HEAD], dyr * gkv[:, HEAD:QK_PAD]
            mk = (jnp.sum(dvn * vn, axis=-1, keepdims=True) + jnp.sum(dvr * vr, axis=-1, keepdims=True)) \
                * (1.0 / QK_HEAD)
            dkvp_ref[:, b:b + HEAD] = (rrk * (dvn - vn * mk)).astype(BF16)
            dkr = dkr + rrk * (dvr - vr * mk)
            dkvp_ref[:, b + HEAD:b + QK_PAD] = dv_ref[:, h * HEAD:(h + 1) * HEAD].astype(BF16)
        dgq_ref[:, 0:HEAD] += dgq_n
        dgq_ref[:, HEAD:QK_PAD] += dgq_r
        dgk_ref[:, 0:HEAD] += dgk_n
        dgk_ref[:, HEAD:QK_PAD] += dgk_r
        dqp = dqp_ref[...]
        dkvp = dkvp_ref[...]
        dwq_ref[...] += _dot(cqn, dqp, TN)
        dwkv_ref[...] += _dot(ckvn, dkvp, TN)
        dcqn = _dot(dqp, wq_ref[...], NT)
        dckvn = _dot(dkvp, wkv_ref[...], NT)
        dgaq_ref[...] += jnp.sum(dcqn * nq, axis=0, keepdims=True)
        dgakv_ref[...] += jnp.sum(dckvn * nkv, axis=0, keepdims=True)
        dnq = dcqn * gaq_ref[...]
        dnkv = dckvn * gakv_ref[...]
        dc_ref[:, 0:R] = (rq * (dnq - nq * jnp.mean(dnq * nq, axis=-1, keepdims=True))).astype(dc_ref.dtype)
        dc_ref[:, R:2 * R] = (rk * (dnkv - nkv * jnp.mean(dnkv * nkv, axis=-1, keepdims=True))).astype(dc_ref.dtype)
        dc_ref[:, 2 * R:2 * R + LANES] = dkr.astype(dc_ref.dtype)

    full = lambda shape: pl.BlockSpec(shape, lambda i: (0, 0))
    tok = lambda w: pl.BlockSpec((tm, w), lambda i: (i, 0))
    return pl.pallas_call(
        body, name=name, grid=(T // tm,),
        in_specs=[tok(CW), full(wq.shape), full(wkv.shape), full((1, R)), full((1, R)), full((1, QK_PAD)),
                  full((1, QK_PAD)), tok(LANES), tok(LANES), tok(LANES),
                  tok(H * QK_PAD), tok(H * QK_PAD), tok(H * HEAD)],
        out_specs=[tok(CW), full(wq.shape), full(wkv.shape), full((1, R)), full((1, R)), full((1, QK_PAD)),
                   full((1, QK_PAD))],
        out_shape=[jax.ShapeDtypeStruct((T, CW), BF16), jax.ShapeDtypeStruct(wq.shape, F32),
                   jax.ShapeDtypeStruct(wkv.shape, F32), jax.ShapeDtypeStruct((1, R), F32),
                   jax.ShapeDtypeStruct((1, R), F32), jax.ShapeDtypeStruct((1, QK_PAD), F32),
                   jax.ShapeDtypeStruct((1, QK_PAD), F32)],
        scratch_shapes=[pltpu.VMEM((tm, H * QK_PAD), BF16), pltpu.VMEM((tm, H * QK_PAD), BF16)],
        compiler_params=_params(("arbitrary",)),
    )(c, wq, wkv, ga_q.reshape(1, R), ga_kv.reshape(1, R), gq, gk, ct, s1, s2, dq, dk, dv)


NEG = -1e30


def _attn_fwd(q, k, v, *, name, tb=512, hp=2, ride=None):
    T = q.shape[0]
    H = q.shape[1] // QK_PAD
    tb = min(tb, T)
    nq = T // tb
    scale = QK_HEAD ** -0.5
    assert H % hp == 0

    def body(q_ref, k_ref, v_ref, ot_ref, lse_ref, m_ref, l_ref, acc_ref):
        i = pl.program_id(1)
        m_ref[...] = jnp.full_like(m_ref, NEG)
        l_ref[...] = jnp.zeros_like(l_ref)
        acc_ref[...] = jnp.zeros_like(acc_ref)

        def step(j, masked):
            off = pl.multiple_of(j * tb, tb)
            for hh in range(hp):
                qs, vs = slice(hh * QK_PAD, (hh + 1) * QK_PAD), slice(hh * HEAD, (hh + 1) * HEAD)
                kb = k_ref[pl.ds(off, tb), qs]
                vb = v_ref[pl.ds(off, tb), vs]
                st = _dot(kb, q_ref[:, qs], NT) * scale
                if masked:
                    kpos = lax.broadcasted_iota(jnp.int32, (tb, tb), 0)
                    qpos = lax.broadcasted_iota(jnp.int32, (tb, tb), 1)
                    st = jnp.where(kpos <= qpos, st, NEG)
                m_old = m_ref[hh]
                m_new = jnp.maximum(m_old, jnp.max(st, axis=0, keepdims=True))
                alpha = jnp.exp(m_old - m_new)
                pt = jnp.exp(st - m_new)
                l_ref[hh] = l_ref[hh] * alpha + jnp.sum(pt, axis=0, keepdims=True)
                acc_ref[vs, :] = acc_ref[vs, :] * alpha + _dot(vb, pt.astype(BF16), TN)
                m_ref[hh] = m_new

        def loop_body(j, carry):
            step(j, False)
            return carry

        lax.fori_loop(0, i, loop_body, 0)
        step(i, True)
        for hh in range(hp):
            vs = slice(hh * HEAD, (hh + 1) * HEAD)
            l = l_ref[hh]
            ot_ref[vs, :] = (acc_ref[vs, :] / l).astype(ot_ref.dtype)
            lse_ref[hh] = m_ref[hh] + jnp.log(l)

    return _call(
        body, name=name, grid=(H // hp, nq),
        in_specs=[pl.BlockSpec((tb, hp * QK_PAD), lambda g, i: (i, g)),
                  pl.BlockSpec((T, hp * QK_PAD), lambda g, i: (0, g)),
                  pl.BlockSpec((T, hp * HEAD), lambda g, i: (0, g))],
        out_specs=[pl.BlockSpec((None, hp * HEAD, tb), lambda g, i: (i, g, 0)),
                   pl.BlockSpec((hp, None, 1, tb), lambda g, i: (g, i, 0, 0))],
        out_shape=[jax.ShapeDtypeStruct((nq, H * HEAD, tb), BF16), jax.ShapeDtypeStruct((H, nq, 1, tb), F32)],
        scratch_shapes=[pltpu.VMEM((hp, 1, tb), F32), pltpu.VMEM((hp, 1, tb), F32),
                        pltpu.VMEM((hp * HEAD, tb), F32)],
        sem=("parallel", "arbitrary"), ride=ride, args=(q, k, v))


def _attn_bwd(q, k, v, ot, lse, dot_, *, name, ride=None):
    T = q.shape[0]
    H = q.shape[1] // QK_PAD
    nq, _, tb = ot.shape
    scale = QK_HEAD ** -0.5

    def body(q_ref, k_ref, v_ref, ot_ref, lse_ref, dot_ref, dq_ref, dk_ref, dv_ref, dka_ref, dva_ref):
        j = pl.program_id(1)

        @pl.when(j == 0)
        def _():
            dq_ref[...] = jnp.zeros_like(dq_ref)

        kb = k_ref[...]
        vb = v_ref[...]

        def step(i, masked):
            off = i * tb if isinstance(i, int) else pl.multiple_of(i * tb, tb)
            qb = q_ref[pl.ds(off, tb), :]
            dob = dot_ref[i]
            ob = ot_ref[i]
            st = _dot(kb, qb, NT) * scale
            if masked:
                kpos = lax.broadcasted_iota(jnp.int32, (tb, tb), 0)
                qpos = lax.broadcasted_iota(jnp.int32, (tb, tb), 1)
                st = jnp.where(kpos <= qpos, st, NEG)
            pt = jnp.exp(st - lse_ref[i])
            dpt = _dot(vb, dob, NN)
            delta = jnp.sum(dob.astype(F32) * ob.astype(F32), axis=0, keepdims=True)
            dst = (pt * (dpt - delta) * scale).astype(BF16)
            dq_ref[pl.ds(off, tb), :] += _dot(dst, kb, TN)
            return _dot(dst, qb, NN), _dot(pt.astype(BF16), dob, NT)

        dk0, dv0 = step(j, True)
        dka_ref[...] = dk0
        dva_ref[...] = dv0

        rest = nq - 1 - j

        def pair_body(t, carry):
            i0 = j + 1 + 2 * t
            dk1, dv1 = step(i0, False)
            dk2, dv2 = step(i0 + 1, False)
            dka_ref[...] += dk1 + dk2
            dva_ref[...] += dv1 + dv2
            return carry

        lax.fori_loop(0, rest // 2, pair_body, 0)

        @pl.when(rest % 2 == 1)
        def _():
            dk1, dv1 = step(nq - 1, False)
            dka_ref[...] += dk1
            dva_ref[...] += dv1

        dk_ref[...] = dka_ref[...]
        dv_ref[...] = dva_ref[...]

    return _call(
        body, name=name, grid=(H, nq),
        in_specs=[pl.BlockSpec((T, QK_PAD), lambda h, j: (0, h)), pl.BlockSpec((tb, QK_PAD), lambda h, j: (j, h)),
                  pl.BlockSpec((tb, HEAD), lambda h, j: (j, h)),
                  pl.BlockSpec((nq, HEAD, tb), lambda h, j: (0, h, 0)),
                  pl.BlockSpec((None, nq, 1, tb), lambda h, j: (h, 0, 0, 0)),
                  pl.BlockSpec((nq, HEAD, tb), lambda h, j: (0, h, 0))],
        out_specs=[pl.BlockSpec((T, QK_PAD), lambda h, j: (0, h)), pl.BlockSpec((tb, QK_PAD), lambda h, j: (j, h)),
                   pl.BlockSpec((tb, HEAD), lambda h, j: (j, h))],
        out_shape=[jax.ShapeDtypeStruct((T, H * QK_PAD), F32), jax.ShapeDtypeStruct((T, H * QK_PAD), F32),
                   jax.ShapeDtypeStruct((T, H * HEAD), F32)],
        scratch_shapes=[pltpu.VMEM((tb, QK_PAD), F32), pltpu.VMEM((tb, HEAD), F32)],
        sem=("parallel", "arbitrary"), ride=ride, args=(q, k, v, ot, lse, dot_))


def _conv_taps(u, prev6, prev7, rows):
    u1 = jnp.where(rows >= 1, pltpu.roll(u, 1, 0), prev7)
    u2 = jnp.where(rows >= 2, pltpu.roll(u, 2, 0), jnp.where(rows == 0, prev6, prev7))
    return u2, u1


def _ffn_up_fwd(h, w_up, conv_w, conv_b, *, name, tm=512):
    T, D = h.shape
    ns, _, fs = w_up.shape
    nh = ns // 2
    tm = min(tm, T)

    def body(h_ref, wg_ref, wu_ref, cwg_ref, cwu_ref, cbg_ref, cbu_ref, a_ref, u_ref, cg_ref, cu_ref):
        i = pl.program_id(1)

        @pl.when(i == 0)
        def _():
            cg_ref[...] = jnp.zeros_like(cg_ref)
            cu_ref[...] = jnp.zeros_like(cu_ref)

        hv = h_ref[...]
        rows = lax.broadcasted_iota(jnp.int32, (tm, 1), 0)
        ys = []
        for idx, (w_ref, cw_ref, cb_ref, carry) in enumerate(
                ((wg_ref, cwg_ref, cbg_ref, cg_ref), (wu_ref, cwu_ref, cbu_ref, cu_ref))):
            u = _dot(hv, w_ref[...], NN)
            u_ref[idx] = u.astype(u_ref.dtype)
            u2, u1 = _conv_taps(u, carry[6:7, :], carry[7:8, :], rows)
            y = cb_ref[...] + u2 * cw_ref[0:1, :]
            y = y + u1 * cw_ref[1:2, :]
            y = y + u * cw_ref[2:3, :]
            ys.append(y)
            carry[...] = u[tm - 8:tm, :]
        yg, yu = ys
        a_ref[...] = ((yg * _sigmoid(yg)) * yu).astype(a_ref.dtype)

    shard = lambda r, off: pl.BlockSpec((None, r, fs), lambda j, i: (j + off, 0, 0))
    return pl.pallas_call(
        body, name=name, grid=(nh, T // tm),
        in_specs=[pl.BlockSpec((tm, D), lambda j, i: (i, 0)), shard(D, 0), shard(D, nh),
                  shard(3, 0), shard(3, nh), shard(1, 0), shard(1, nh)],
        out_specs=[pl.BlockSpec((None, tm, fs), lambda j, i: (j, i, 0)),
                   pl.BlockSpec((2, None, tm, fs), lambda j, i: (0, j, i, 0))],
        out_shape=[jax.ShapeDtypeStruct((nh, T, fs), BF16), jax.ShapeDtypeStruct((2, nh, T, fs), BF16)],
        scratch_shapes=[pltpu.VMEM((8, fs), F32), pltpu.VMEM((8, fs), F32)],
        compiler_params=_params(("parallel", "arbitrary")),
    )(h, w_up, w_up, conv_w, conv_w, conv_b, conv_b)


def _ffn_act_bwd(dxo, w_down, u, conv_w, conv_b, *, name, tm=512):
    T, D = dxo.shape
    _, nh, _, fs = u.shape
    tm = min(tm, T)
    nt = T // tm
    hb = tm // 8

    def body(dx_ref, wd_ref, u_ref, up_ref, cwg_ref, cwu_ref, cbg_ref, cbu_ref,
             du_ref, dcw_ref, dcb_ref, cg_ref, cu_ref):
        i = pl.program_id(1)
        tile = nt - 1 - i

        @pl.when(i == 0)
        def _():
            cg_ref[...] = jnp.zeros_like(cg_ref)
            cu_ref[...] = jnp.zeros_like(cu_ref)
            dcw_ref[...] = jnp.zeros_like(dcw_ref)
            dcb_ref[...] = jnp.zeros_like(dcb_ref)

        rows = lax.broadcasted_iota(jnp.int32, (tm, 1), 0)
        da = _dot(dx_ref[...].astype(BF16), wd_ref[...], NT)
        has_prev = (tile > 0).astype(F32)
        us, ys, shifted = [], [], []
        for idx, (cw_ref, cb_ref) in enumerate(((cwg_ref, cbg_ref), (cwu_ref, cbu_ref))):
            uv = u_ref[idx].astype(F32)
            p6 = up_ref[idx, 6:7, :].astype(F32) * has_prev
            p7 = up_ref[idx, 7:8, :].astype(F32) * has_prev
            u2, u1 = _conv_taps(uv, p6, p7, rows)
            y = cb_ref[...] + u2 * cw_ref[0:1, :]
            y = y + u1 * cw_ref[1:2, :]
            y = y + uv * cw_ref[2:3, :]
            us.append(uv)
            ys.append(y)
            shifted.append((u2, u1))
        yg, yu = ys
        sg = _sigmoid(yg)
        dys = (da * yu * (sg * (1.0 + yg * (1.0 - sg))), da * (yg * sg))
        for idx, (cw_ref, carry) in enumerate(((cwg_ref, cg_ref), (cwu_ref, cu_ref))):
            dy = dys[idx]
            u2, u1 = shifted[idx]
            dcb_ref[idx] += jnp.sum(dy, axis=0, keepdims=True)
            dcw_ref[idx, 0:1, :] += jnp.sum(dy * u2, axis=0, keepdims=True)
            dcw_ref[idx, 1:2, :] += jnp.sum(dy * u1, axis=0, keepdims=True)
            dcw_ref[idx, 2:3, :] += jnp.sum(dy * us[idx], axis=0, keepdims=True)
            n0, n1 = carry[0:1, :], carry[1:2, :]
            d1 = jnp.where(rows < tm - 1, pltpu.roll(dy, tm - 1, 0), n0)
            d2 = jnp.where(rows < tm - 2, pltpu.roll(dy, tm - 2, 0), jnp.where(rows == tm - 2, n0, n1))
            du = dy * cw_ref[2:3, :] + d1 * cw_ref[1:2, :] + d2 * cw_ref[0:1, :]
            du_ref[idx] = du.astype(du_ref.dtype)
            carry[...] = dy[0:8, :]

    rt = lambda i: nt - 1 - i
    shard = lambda r, off: pl.BlockSpec((None, r, fs), lambda j, i: (j + off, 0, 0))
    return pl.pallas_call(
        body, name=name, grid=(nh, nt),
        in_specs=[pl.BlockSpec((tm, D), lambda j, i: (rt(i), 0)), pl.BlockSpec((fs, D), lambda j, i: (j, 0)),
                  pl.BlockSpec((2, None, tm, fs), lambda j, i: (0, j, rt(i), 0)),
                  pl.BlockSpec((2, None, 8, fs), lambda j, i: (0, j, jnp.maximum(rt(i) * hb - 1, 0), 0)),
                  shard(3, 0), shard(3, nh), shard(1, 0), shard(1, nh)],
        out_specs=[pl.BlockSpec((2, None, tm, fs), lambda j, i: (0, j, rt(i), 0)),
                   pl.BlockSpec((2, None, 3, fs), lambda j, i: (0, j, 0, 0)),
                   pl.BlockSpec((2, None, 1, fs), lambda j, i: (0, j, 0, 0))],
        out_shape=[jax.ShapeDtypeStruct((2, nh, T, fs), BF16), jax.ShapeDtypeStruct((2, nh, 3, fs), F32),
                   jax.ShapeDtypeStruct((2, nh, 1, fs), F32)],
        scratch_shapes=[pltpu.VMEM((8, fs), F32), pltpu.VMEM((8, fs), F32)],
        compiler_params=_params(("parallel", "arbitrary")),
    )(dxo, w_down, u, u, conv_w, conv_w, conv_b, conv_b)


def _pad_cols(w, n):
    return jnp.pad(w, [(0, 0)] * (w.ndim - 1) + [(0, n - w.shape[-1])])


def _q_up_padded(w):
    R = w.shape[0]
    H = w.shape[1] // QK_HEAD
    return _pad_cols(w.reshape(R, H, QK_HEAD), QK_PAD).reshape(R, H * QK_PAD)


def _q_up_unpadded(w):
    R = w.shape[0]
    H = w.shape[1] // QK_PAD
    return w.reshape(R, H, QK_PAD)[:, :, :QK_HEAD].reshape(R, H * QK_HEAD)


def _xchg_copies(src_refs, out_refs, kinds, send_sems, recv_sems, local_sems):
    x, y, c = lax.axis_index("x"), lax.axis_index("y"), lax.axis_index("c")
    me = 4 * x + 2 * y + c
    copies = []
    for b, kind in enumerate(kinds):
        gather = kind == "gather"
        own = src_refs[b] if gather else src_refs[b].at[me]
        copies.append(pltpu.make_async_copy(own, out_refs[b].at[me], local_sems.at[b]))
        for kk in range(1, N_DEV):
            px = 1 - x if kk & 4 else x
            py = 1 - y if kk & 2 else y
            pc = 1 - c if kk & 1 else c
            peer = 4 * px + 2 * py + pc
            src = src_refs[b] if gather else src_refs[b].at[peer]
            copies.append(pltpu.make_async_remote_copy(
                src_ref=src, dst_ref=out_refs[b].at[me],
                send_sem=send_sems.at[b * (N_DEV - 1) + kk - 1],
                recv_sem=recv_sems.at[b * (N_DEV - 1) + kk - 1],
                device_id=(px, py, pc), device_id_type=pl.DeviceIdType.MESH))
    return copies


def _xchg_out_shapes(srcs, kinds):
    return [jax.ShapeDtypeStruct((N_DEV,) + s.shape if kind == "gather" else s.shape, s.dtype)
            for s, kind in zip(srcs, kinds)]


def _xchg_scratch(n):
    return [pltpu.SemaphoreType.DMA((n * (N_DEV - 1),)), pltpu.SemaphoreType.DMA((n * (N_DEV - 1),)),
            pltpu.SemaphoreType.DMA((n,))]


def _exchange(srcs, kinds, *, name):
    n = len(srcs)

    def body(*refs):
        copies = _xchg_copies(refs[:n], refs[n:2 * n], kinds, *refs[2 * n:])
        for cp in copies:
            cp.start()
        for cp in copies:
            cp.wait()

    hbm = pl.BlockSpec(memory_space=pl.ANY)
    return pl.pallas_call(
        body, name=name, in_specs=[hbm] * n, out_specs=[hbm] * n, out_shape=_xchg_out_shapes(srcs, kinds),
        scratch_shapes=_xchg_scratch(n),
    )(*srcs)


def _call(body, *, name, grid, in_specs, out_specs, out_shape, scratch_shapes, args, sem, ride=None):
    if ride is None:
        outs = pl.pallas_call(body, name=name, grid=grid, in_specs=in_specs, out_specs=out_specs,
                              out_shape=out_shape, scratch_shapes=scratch_shapes,
                              compiler_params=_params(sem))(*args)
        return list(outs), []
    srcs, kinds = ride
    n_in, n_out, n_sc, nx = len(in_specs), len(out_specs), len(scratch_shapes), len(srcs)

    def wrapped(*refs):
        ins, xs = refs[:n_in], refs[n_in:n_in + nx]
        o0 = n_in + nx
        outs, xo = refs[o0:o0 + n_out], refs[o0 + n_out:o0 + n_out + nx]
        s0 = o0 + n_out + nx
        sc, sems = refs[s0:s0 + n_sc], refs[s0 + n_sc:]
        first = functools.reduce(jnp.logical_and, [pl.program_id(d) == 0 for d in range(len(grid))])
        last = functools.reduce(jnp.logical_and, [pl.program_id(d) == grid[d] - 1 for d in range(len(grid))])

        @pl.when(first)
        def _():
            for cp in _xchg_copies(xs, xo, kinds, *sems):
                cp.start()

        body(*ins, *outs, *sc)

        @pl.when(last)
        def _():
            for cp in _xchg_copies(xs, xo, kinds, *sems):
                cp.wait()

    hbm = pl.BlockSpec(memory_space=pl.ANY)
    outs = pl.pallas_call(
        wrapped, name=name, grid=grid, in_specs=list(in_specs) + [hbm] * nx,
        out_specs=list(out_specs) + [hbm] * nx, out_shape=list(out_shape) + _xchg_out_shapes(srcs, kinds),
        scratch_shapes=list(scratch_shapes) + _xchg_scratch(nx),
        compiler_params=_params(("arbitrary",) * len(grid)),
    )(*args, *srcs)
    return list(outs[:n_out]), list(outs[n_out:])


def _sum_slots(parts, *, name):
    _, Rr, C = parts.shape

    def body(p_ref, o_ref):
        acc = p_ref[0].astype(F32)
        for d in range(1, N_DEV):
            acc = acc + p_ref[d].astype(F32)
        o_ref[...] = acc

    return pl.pallas_call(
        body, name=name, grid=(1,),
        in_specs=[pl.BlockSpec((N_DEV, Rr, C), lambda i: (0, 0, 0))],
        out_specs=pl.BlockSpec((Rr, C), lambda i: (0, 0)),
        out_shape=jax.ShapeDtypeStruct((Rr, C), F32),
        compiler_params=_params(("arbitrary",)),
    )(parts)


def _row_tile(rows, cap=512):
    if rows <= cap:
        return rows
    d = (cap // 8) * 8
    while d >= 8:
        if rows % d == 0:
            return d
        d -= 8
    raise ValueError(f"no row tile for {rows}")


def _adamw(parts, w, m, v, *, name):
    S, Rr, C = parts.shape
    tr = _row_tile(Rr)
    c1 = 1.0 - ADAM_B1 ** ADAM_STEP
    c2 = 1.0 - ADAM_B2 ** ADAM_STEP

    def body(p_ref, w_ref, m_ref, v_ref, g_ref, d_ref, nm_ref, nv_ref):
        g = p_ref[0].astype(F32)
        for d in range(1, S):
            g = g + p_ref[d].astype(F32)
        mm = ADAM_B1 * m_ref[...] + (1.0 - ADAM_B1) * g
        vv = ADAM_B2 * v_ref[...] + (1.0 - ADAM_B2) * (g * g)
        m_hat = mm / c1
        v_hat = vv / c2
        g_ref[...] = g
        d_ref[...] = -ADAM_LR * (m_hat / (jnp.sqrt(v_hat) + ADAM_EPS) + ADAM_WD * w_ref[...])
        nm_ref[...] = mm
        nv_ref[...] = vv

    spec = pl.BlockSpec((tr, C), lambda i: (i, 0))
    shape = jax.ShapeDtypeStruct((Rr, C), F32)
    return pl.pallas_call(
        body, name=name, grid=(Rr // tr,),
        in_specs=[pl.BlockSpec((S, tr, C), lambda i: (0, i, 0)), spec, spec, spec],
        out_specs=[spec] * 4, out_shape=[shape] * 4,
        compiler_params=_params(("parallel",)),
    )(parts, w, m, v)


def _pack(arrs, dtype, row_mult):
    flat = jnp.concatenate([a.reshape(-1).astype(dtype) for a in arrs])
    per = row_mult * PACK_COLS
    total = -(-flat.shape[0] // per) * per
    return jnp.pad(flat, (0, total - flat.shape[0])).reshape(total // PACK_COLS, PACK_COLS)


def _unpack(packed, shapes, lead=()):
    flat = packed.reshape(lead + (-1,))
    out, off = [], 0
    for shp in shapes:
        n = 1
        for d in shp:
            n *= d
        out.append(flat[..., off:off + n].reshape(lead + tuple(shp)))
        off += n
    return out


HGRN_W = ("hgrn_w_in", "hgrn_w_out")
MLA_W = ("mla_w_in", "mla_w_q_up", "mla_w_kv_up", "mla_w_out")
FFN_W = ("ffn_w_up", "ffn_w_down")
BIG = HGRN_W + MLA_W + FFN_W
SMALL_SHARDED = {"ffn_conv_w": 2, "mla_q_a_norm": 1, "mla_kv_a_norm": 1}
REPLICATED = ["norm_mix", "norm_ffn", "hgrn_lower_bounds", "hgrn_out_norm", "mla_q_norm", "mla_k_norm",
              "ffn_conv_b"]
WEIGHTS = ["norm_mix", "norm_ffn", "hgrn_w_in", "hgrn_lower_bounds", "hgrn_out_norm", "hgrn_w_out", "mla_w_in",
           "mla_q_a_norm", "mla_w_q_up", "mla_kv_a_norm", "mla_w_kv_up", "mla_q_norm", "mla_k_norm", "mla_w_out",
           "ffn_w_up", "ffn_conv_w", "ffn_conv_b", "ffn_w_down"]


def _shards_to_cols(g):
    return g.transpose(1, 0, 2).reshape(g.shape[1], N_DEV * g.shape[2])


def _cols_to_shards(w):
    R = w.shape[0]
    return w.reshape(R, N_DEV, w.shape[1] // N_DEV).transpose(1, 0, 2)


def kernel(x, positions, norm_mix, norm_ffn, hgrn_w_in, hgrn_lower_bounds, hgrn_out_norm, hgrn_w_out, mla_w_in, mla_q_a_norm, mla_w_q_up, mla_kv_a_norm, mla_w_kv_up, mla_q_norm, mla_k_norm, mla_w_out, ffn_w_up, ffn_conv_w, ffn_conv_b, ffn_w_down, loss_target, m_norm_mix, m_norm_ffn, m_hgrn_w_in, m_hgrn_lower_bounds, m_hgrn_out_norm, m_hgrn_w_out, m_mla_w_in, m_mla_q_a_norm, m_mla_w_q_up, m_mla_kv_a_norm, m_mla_w_kv_up, m_mla_q_norm, m_mla_k_norm, m_mla_w_out, m_ffn_w_up, m_ffn_conv_w, m_ffn_conv_b, m_ffn_w_down, v_norm_mix, v_norm_ffn, v_hgrn_w_in, v_hgrn_lower_bounds, v_hgrn_out_norm, v_hgrn_w_out, v_mla_w_in, v_mla_q_a_norm, v_mla_w_q_up, v_mla_kv_a_norm, v_mla_w_kv_up, v_mla_q_norm, v_mla_k_norm, v_mla_w_out, v_ffn_w_up, v_ffn_conv_w, v_ffn_conv_b, v_ffn_w_down):
    local = dict(norm_mix=norm_mix, norm_ffn=norm_ffn, hgrn_w_in=hgrn_w_in, hgrn_lower_bounds=hgrn_lower_bounds,
                 hgrn_out_norm=hgrn_out_norm, hgrn_w_out=hgrn_w_out, mla_w_in=mla_w_in, mla_q_a_norm=mla_q_a_norm,
                 mla_w_q_up=mla_w_q_up, mla_kv_a_norm=mla_kv_a_norm, mla_w_kv_up=mla_w_kv_up, mla_q_norm=mla_q_norm,
                 mla_k_norm=mla_k_norm, mla_w_out=mla_w_out, ffn_w_up=ffn_w_up, ffn_conv_w=ffn_conv_w,
                 ffn_conv_b=ffn_conv_b, ffn_w_down=ffn_w_down)
    mom_m = dict(norm_mix=m_norm_mix, norm_ffn=m_norm_ffn, hgrn_w_in=m_hgrn_w_in,
                 hgrn_lower_bounds=m_hgrn_lower_bounds, hgrn_out_norm=m_hgrn_out_norm, hgrn_w_out=m_hgrn_w_out,
                 mla_w_in=m_mla_w_in, mla_q_a_norm=m_mla_q_a_norm, mla_w_q_up=m_mla_w_q_up,
                 mla_kv_a_norm=m_mla_kv_a_norm, mla_w_kv_up=m_mla_w_kv_up, mla_q_norm=m_mla_q_norm,
                 mla_k_norm=m_mla_k_norm, mla_w_out=m_mla_w_out, ffn_w_up=m_ffn_w_up, ffn_conv_w=m_ffn_conv_w,
                 ffn_conv_b=m_ffn_conv_b, ffn_w_down=m_ffn_w_down)
    mom_v = dict(norm_mix=v_norm_mix, norm_ffn=v_norm_ffn, hgrn_w_in=v_hgrn_w_in,
                 hgrn_lower_bounds=v_hgrn_lower_bounds, hgrn_out_norm=v_hgrn_out_norm, hgrn_w_out=v_hgrn_w_out,
                 mla_w_in=v_mla_w_in, mla_q_a_norm=v_mla_q_a_norm, mla_w_q_up=v_mla_w_q_up,
                 mla_kv_a_norm=v_mla_kv_a_norm, mla_w_kv_up=v_mla_w_kv_up, mla_q_norm=v_mla_q_norm,
                 mla_k_norm=v_mla_k_norm, mla_w_out=v_mla_w_out, ffn_w_up=v_ffn_w_up, ffn_conv_w=v_ffn_conv_w,
                 ffn_conv_b=v_ffn_conv_b, ffn_w_down=v_ffn_w_down)
    me = 4 * lax.axis_index("x") + 2 * lax.axis_index("y") + lax.axis_index("c")
    x, positions, target = x[0], positions[0], loss_target[0]
    T, D = x.shape
    depth = norm_mix.shape[0]
    R = mla_w_q_up.shape[1]
    cw = 2 * R + LANES
    small_names = list(SMALL_SHARDED)

    def block_of(kind, l):
        names = {"hgrn": HGRN_W, "mla": MLA_W, "ffn": FFN_W}[kind]
        idx = l if kind == "ffn" else l // 2
        return [(n, idx) for n in names]

    def mixer_kind(l):
        return "hgrn" if l % 2 == 0 else "mla"

    def riders(l):
        keys = block_of("ffn", l)
        if l + 1 < depth:
            keys += block_of(mixer_kind(l + 1), l + 1)
        return keys

    gathered = {}

    def gather_ride(l):
        keys = riders(l)
        return keys, ([local[n][i].astype(BF16) for n, i in keys], ["gather"] * len(keys))

    def take_gathered(keys, arrs):
        for key, a in zip(keys, arrs):
            gathered[key] = a

    keys0 = block_of("hgrn", 0)
    small_local = _pack([local[n] for n in small_names], F32, 8)
    got = _exchange([local[n][i].astype(BF16) for n, i in keys0] + [small_local],
                    ["gather"] * (len(keys0) + 1), name="gather_first")
    take_gathered(keys0, got[:-1])
    small_all = _unpack(got[-1], [local[n].shape for n in small_names], lead=(N_DEV,))
    conv_w_all = small_all[0].transpose(1, 0, 2, 3)
    qa_all = small_all[1].transpose(1, 0, 2).reshape(-1, R)
    kva_all = small_all[2].transpose(1, 0, 2).reshape(-1, R)
    fs = conv_w_all.shape[-1]
    conv_b_s = ffn_conv_b.reshape(depth, N_DEV, 1, fs)

    ct, s1, s2 = _rope_tables(positions)
    lb_soft = jax.nn.softmax(hgrn_lower_bounds.astype(F32), axis=0)
    lower_bounds = jnp.cumsum(lb_soft, axis=0) - lb_soft[0:1]

    def mla_views(j):
        w_in = _pad_cols(gathered["mla_w_in", j].reshape(D, -1), cw)
        wq = _q_up_padded(_shards_to_cols(gathered["mla_w_q_up", j]))
        wkv = _shards_to_cols(gathered["mla_w_kv_up", j])
        gq = _pad_cols(mla_q_norm[j].reshape(1, QK_HEAD), QK_PAD)
        gk = _pad_cols(mla_k_norm[j].reshape(1, QK_HEAD), QK_PAD)
        return w_in, wq, wkv, gq, gk

    saved = []
    for layer in range(depth):
        j = layer // 2
        s = {"x_in": x}
        h = _rmsnorm_fwd(x, norm_mix[layer], name=f"norm_mix_fwd_{layer}")
        s["h_mix"] = h
        keys, ride = gather_ride(layer)
        if layer % 2 == 0:
            p = _mm(h, gathered["hgrn_w_in", j], b_fmt="knb", bm=1024, name=f"hgrn_in_{layer}")
            (og, o, s0), got = _hgrn_fwd(p, lower_bounds[j], hgrn_out_norm[j], name=f"hgrn_fwd_{layer}", ride=ride)
            s.update(p=p, og=og, o=o, s0=s0)
            take_gathered(keys, got)
            x = _mm(og, gathered["hgrn_w_out", j].reshape(D, D), res=x, name=f"hgrn_out_{layer}")
        else:
            w_in, wq, wkv, gq, gk = mla_views(j)
            c = _mm(h, w_in, bm=1024, name=f"mla_in_{layer}")
            q, k, v = _mla_prep_fwd(c, wq, wkv, qa_all[j], kva_all[j], gq, gk, ct, s1, s2,
                                    name=f"mla_prep_fwd_{layer}")
            (ot, lse), got = _attn_fwd(q, k, v, name=f"attn_fwd_{layer}", ride=ride)
            s.update(c=c, q=q, k=k, v=v, ot=ot, lse=lse)
            take_gathered(keys, got)
            x = _mm(ot, gathered["mla_w_out", j].reshape(D, D), a_fmt="kmb", res=x, name=f"mla_out_{layer}")
        s["x_mid"] = x
        h = _rmsnorm_fwd(x, norm_ffn[layer], name=f"norm_ffn_fwd_{layer}")
        s["h_ffn"] = h
        a, u = _ffn_up_fwd(h, gathered["ffn_w_up", layer], conv_w_all[layer], conv_b_s[layer], name=f"ffn_up_{layer}")
        s.update(a=a, u=u)
        x = _mm(a, gathered["ffn_w_down", layer].reshape(-1, D), a_fmt="mkb", res=x, bk=fs, name=f"ffn_down_{layer}")
        saved.append(s)

    dx, loss_part = _loss_head(x, target, name="loss_head")

    parts = {}
    received = {}
    g_small = {n: [None] * local[n].shape[0] for n in REPLICATED + small_names}

    def scatter_ride(l):
        keys = riders(l)
        return keys, ([parts[key] for key in keys], ["scatter"] * len(keys))

    def take_received(keys, arrs):
        for key, a in zip(keys, arrs):
            received[key] = a

    for layer in reversed(range(depth)):
        j = layer // 2
        s = saved[layer]
        parts["ffn_w_down", layer] = _mm(s["a"], dx, a_fmt="kmb", out_dtype=BF16, name=f"ffn_down_dw_{layer}"
                                         ).reshape(N_DEV, -1, D)
        du, dcw, dcb = _ffn_act_bwd(dx, gathered["ffn_w_down", layer].reshape(-1, D), s["u"], conv_w_all[layer],
                                    conv_b_s[layer], name=f"ffn_act_bwd_{layer}")
        g_small["ffn_conv_w"][layer] = dcw.reshape(N_DEV, 3, fs)
        g_small["ffn_conv_b"][layer] = dcb.reshape(N_DEV * fs)
        du8 = du.reshape(N_DEV, T, fs)
        parts["ffn_w_up", layer] = _mm(s["h_ffn"], du8, a_fmt="km", b_fmt="knb", out_fmt="mnb", out_dtype=BF16,
                                       bm=1024, name=f"ffn_up_dw_{layer}")
        dh = _mm(du8, gathered["ffn_w_up", layer], a_fmt="mkb", b_fmt="nkb", bm=1024, bk=fs,
                 name=f"ffn_up_dh_{layer}")
        dx, dgain = _rmsnorm_bwd(s["x_mid"], norm_ffn[layer], dh, dx, name=f"norm_ffn_bwd_{layer}")
        g_small["norm_ffn"][layer] = dgain.reshape(D)
        keys, ride = scatter_ride(layer)
        if layer % 2 == 0:
            w_out = gathered["hgrn_w_out", j].reshape(D, D)
            parts["hgrn_w_out", j] = _mm(s["og"], dx, a_fmt="km", out_dtype=BF16, bm=1024,
                                         name=f"hgrn_out_dw_{layer}").reshape(N_DEV, -1, D)
            dog = _mm(dx, w_out, b_fmt="nk", name=f"hgrn_out_dx_{layer}")
            (dp, dlb, dgn), got = _hgrn_bwd(s["p"], lower_bounds[j], hgrn_out_norm[j], s["s0"], s["o"], dog,
                                            name=f"hgrn_bwd_{layer}", ride=ride)
            take_received(keys, got)
            g_small["hgrn_lower_bounds"][j] = dlb.reshape(D)
            g_small["hgrn_out_norm"][j] = dgn.reshape(HEAD)
            w_in_s = gathered["hgrn_w_in", j]
            parts["hgrn_w_in", j] = _mm(s["h_mix"], dp, a_fmt="km", out_fmt="mnb", out_dtype=BF16, bm=1024,
                                        bn=w_in_s.shape[2], name=f"hgrn_in_dw_{layer}")
            dh = _mm(dp, w_in_s, b_fmt="nkb", bm=1024, name=f"hgrn_in_dx_{layer}")
        else:
            w_in, wq, wkv, gq, gk = mla_views(j)
            w_out = gathered["mla_w_out", j].reshape(D, D)
            tb = s["ot"].shape[2]
            parts["mla_w_out", j] = _mm(s["ot"], dx, a_fmt="mkb", bk=tb, out_dtype=BF16, bm=1024,
                                        name=f"mla_out_dw_{layer}").reshape(N_DEV, -1, D)
            dot_ = _mm(w_out, dx, b_fmt="nk", out_fmt="mnb", out_dtype=BF16, bm=D, bn=tb,
                       name=f"mla_out_dx_{layer}")
            (dq, dk, dv), got = _attn_bwd(s["q"], s["k"], s["v"], s["ot"], s["lse"], dot_,
                                          name=f"attn_bwd_{layer}", ride=ride)
            take_received(keys, got)
            dc, dwq, dwkv, dgaq, dgakv, dgq, dgk = _mla_prep_bwd(
                s["c"], wq, wkv, qa_all[j], kva_all[j], gq, gk, ct, s1, s2, dq, dk, dv,
                name=f"mla_prep_bwd_{layer}")
            parts["mla_w_q_up", j] = _cols_to_shards(_q_up_unpadded(dwq)).astype(BF16)
            parts["mla_w_kv_up", j] = _cols_to_shards(dwkv).astype(BF16)
            g_small["mla_q_a_norm"][j] = dgaq.reshape(R)
            g_small["mla_kv_a_norm"][j] = dgakv.reshape(R)
            g_small["mla_q_norm"][j] = dgq[0, :QK_HEAD]
            g_small["mla_k_norm"][j] = dgk[0, :QK_HEAD]
            win_cols = mla_w_in.shape[2]
            dw_in = _mm(s["h_mix"], dc, a_fmt="km", bm=1024, name=f"mla_in_dw_{layer}")
            parts["mla_w_in", j] = dw_in[:, :win_cols].astype(BF16).reshape(N_DEV, -1, win_cols)
            dh = _mm(dc, w_in, b_fmt="nk", name=f"mla_in_dx_{layer}")
        dx, dgain = _rmsnorm_bwd(s["x_in"], norm_mix[layer], dh, dx, name=f"norm_mix_bwd_{layer}")
        g_small["norm_mix"][layer] = dgain.reshape(D)
    grad_x = dx

    dlb_eff = jnp.stack(g_small["hgrn_lower_bounds"])
    dsoft = jnp.cumsum(dlb_eff[::-1], axis=0)[::-1]
    dsoft = dsoft.at[0].add(-jnp.sum(dlb_eff, axis=0))
    g_lb = lb_soft * (dsoft - jnp.sum(dsoft * lb_soft, axis=0, keepdims=True))
    small_grads = {n: (g_lb if n == "hgrn_lower_bounds" else jnp.stack(g_small[n])) for n in g_small}

    small_grad_names = REPLICATED + small_names
    small_part = _pack([small_grads[n] for n in small_grad_names] + [loss_part], F32, 8)
    got = _exchange([parts[key] for key in keys0] + [small_part], ["scatter"] * len(keys0) + ["gather"],
                    name="exchange_last")
    take_received(keys0, got[:-1])
    small_recv = got[-1]

    out = {}
    for n in BIG:
        layers = local[n].shape[0]
        shard = local[n].shape[1:]
        p2 = jnp.concatenate([received[n, i].reshape(N_DEV, -1, shard[-1]) for i in range(layers)], axis=1)
        flat = lambda a: a.reshape(-1, shard[-1])
        res = _adamw(p2, flat(local[n]), flat(mom_m[n]), flat(mom_v[n]), name=f"adamw_{n}")
        for kind, a in zip(("grad", "delta", "new_m", "new_v"), res):
            out[kind, n] = a.reshape(local[n].shape)
    small_sum = _sum_slots(small_recv, name="sum_small")
    small_full = _unpack(small_sum, [small_grads[n].shape for n in small_grad_names] + [(1, LANES)])
    loss = small_full[-1][0, 0]
    g_mine = {}
    for n, a in zip(small_grad_names, small_full[:-1]):
        if n == "ffn_conv_w":
            a = lax.dynamic_index_in_dim(a, me, axis=1, keepdims=False)
        elif n in SMALL_SHARDED:
            size = local[n].shape[1]
            a = lax.dynamic_slice_in_dim(a, me * size, size, axis=1)
        g_mine[n] = a
    small_shapes = [local[n].shape for n in small_grad_names]
    res = _adamw(_pack([g_mine[n] for n in small_grad_names], F32, 8)[None],
                 _pack([local[n] for n in small_grad_names], F32, 8),
                 _pack([mom_m[n] for n in small_grad_names], F32, 8),
                 _pack([mom_v[n] for n in small_grad_names], F32, 8), name="adamw_small")
    for kind, packed in zip(("grad", "delta", "new_m", "new_v"), res):
        for n, a in zip(small_grad_names, _unpack(packed, small_shapes)):
            out[kind, n] = a

    outs = [loss, grad_x[None]]
    for kind in ("grad", "delta", "new_m", "new_v"):
        outs += [out[kind, n] for n in WEIGHTS]
    return tuple(outs)
```

```python
import functools

import jax
import jax.numpy as jnp
from jax import lax
from jax.experimental import pallas as pl
from jax.experimental.pallas import tpu as pltpu

F32 = jnp.float32
BF16 = jnp.bfloat16

RMS_EPS = 1e-6
ROPE_THETA = 10000.0
HEAD = 128
ROPE = 64
QK_HEAD = HEAD + ROPE
QK_PAD = 256
CHUNK = 64
SUB = 16
EXP_CLAMP = 60.0

ADAM_LR = 0.001
ADAM_B1 = 0.9
ADAM_B2 = 0.999
ADAM_EPS = 1e-08
ADAM_WD = 0.01
ADAM_STEP = 10

N_DEV = 8
LANES = 128
PACK_COLS = 1024
V7X_VMEM_LIMIT = 56 * 1024 * 1024

HI = lax.Precision.HIGHEST


def _params(sem):
    return pltpu.CompilerParams(dimension_semantics=sem, vmem_limit_bytes=V7X_VMEM_LIMIT)


def _blk(n, cap):
    if n <= cap:
        return n
    d = (cap // LANES) * LANES
    while d >= LANES:
        if n % d == 0:
            return d
        d -= LANES
    raise ValueError(f"no lane-aligned block for {n} under {cap}")


def _sigmoid(x):
    return jax.nn.sigmoid(x)


def _dot(a, b, dims, precision=None):
    return lax.dot_general(a, b, (dims, ((), ())), preferred_element_type=F32, precision=precision)


NN = ((1,), (0,))
NT = ((1,), (1,))
TN = ((0,), (0,))


def _mm(a, b, *, a_fmt="mk", b_fmt="kn", out_fmt="mn", res=None, norm_out=None, norm_bwd=None, out_dtype=F32,
        bm=512, bn=1024, bk=1024, name):
    if a_fmt == "mk":
        M, K = a.shape
    elif a_fmt == "km":
        K, M = a.shape
    elif a_fmt == "kmb":
        nb, K, B = a.shape
        M = nb * B
    else:
        nb, M, B = a.shape
        K = nb * B
    if b_fmt == "kn":
        Kb, N = b.shape
    elif b_fmt == "nk":
        N, Kb = b.shape
    elif b_fmt == "knb":
        nbb, Kb, Bb = b.shape
        N = nbb * Bb
    else:
        nbb, N, Bb = b.shape
        Kb = nbb * Bb
    assert K == Kb, (a.shape, b.shape, a_fmt, b_fmt)
    bm = B if a_fmt == "kmb" else _blk(M, bm)
    bn = Bb if b_fmt == "knb" else _blk(N, bn)
    if a_fmt == "mkb" and b_fmt == "nkb":
        assert B == Bb
    bk = _blk(B, bk) if a_fmt == "mkb" else (_blk(Bb, bk) if b_fmt == "nkb" else _blk(K, bk))
    nm, nn, nk = M // bm, N // bn, K // bk

    if a_fmt == "mk":
        a_spec = pl.BlockSpec((bm, bk), lambda i, j, k: (i, k))
        a_dim = 1
    elif a_fmt == "km":
        a_spec = pl.BlockSpec((bk, bm), lambda i, j, k: (k, i))
        a_dim = 0
    elif a_fmt == "kmb":
        a_spec = pl.BlockSpec((None, bk, bm), lambda i, j, k: (i, k, 0))
        a_dim = 0
    else:
        per = B // bk
        a_spec = pl.BlockSpec((None, bm, bk), lambda i, j, k: (k // per, i, k % per))
        a_dim = 1
    if b_fmt == "kn":
        b_spec = pl.BlockSpec((bk, bn), lambda i, j, k: (k, j))
        b_dim = 0
    elif b_fmt == "nk":
        b_spec = pl.BlockSpec((bn, bk), lambda i, j, k: (j, k))
        b_dim = 1
    elif b_fmt == "knb":
        b_spec = pl.BlockSpec((None, bk, bn), lambda i, j, k: (j, k, 0))
        b_dim = 0
    else:
        perb = Bb // bk
        b_spec = pl.BlockSpec((None, bn, bk), lambda i, j, k: (k // perb, j, k % perb))
        b_dim = 1
    if out_fmt == "mn":
        o_spec = pl.BlockSpec((bm, bn), lambda i, j, k: (i, j))
        o_shape = jax.ShapeDtypeStruct((M, N), out_dtype)
    else:
        o_spec = pl.BlockSpec((None, bm, bn), lambda i, j, k: (j, i, 0))
        o_shape = jax.ShapeDtypeStruct((nn, M, bn), out_dtype)
    in_specs = [a_spec, b_spec]
    args = [a, b]
    row_tile = pl.BlockSpec((bm, bn), lambda i, j, k: (i, j))
    row_vec = pl.BlockSpec((1, bn), lambda i, j, k: (0, j))
    if res is not None:
        assert out_fmt == "mn"
        in_specs.append(row_tile)
        args.append(res)
    out_specs, out_shapes = [o_spec], [o_shape]
    if norm_out is not None:
        assert nn == 1 and out_fmt == "mn"
        in_specs.append(row_vec)
        args.append(norm_out.reshape(1, N))
        out_specs.append(row_tile)
        out_shapes.append(jax.ShapeDtypeStruct((M, N), BF16))
    if norm_bwd is not None:
        assert nn == 1 and out_fmt == "mn" and res is None and norm_out is None
        xin, gain, dres = norm_bwd
        in_specs += [row_tile, row_vec, row_tile]
        args += [xin, gain.reshape(1, N), dres]
        out_specs.append(row_vec)
        out_shapes.append(jax.ShapeDtypeStruct((1, N), F32))
    dims = ((a_dim,), (b_dim,))
    has_res = res is not None
    n_in = len(in_specs)

    def body(*refs):
        a_ref, b_ref = refs[0], refs[1]
        extra_in = list(refs[2:n_in])
        o_ref = refs[n_in]
        part = _dot(a_ref[...].astype(BF16), b_ref[...].astype(BF16), dims)

        def finish(out):
            if has_res:
                out = out + extra_in[0][...]
            if norm_out is not None:
                g_ref, h_ref = extra_in[-1], refs[n_in + 1]
                r = lax.rsqrt(jnp.mean(out * out, axis=-1, keepdims=True) + RMS_EPS)
                h_ref[...] = ((out * r) * g_ref[...]).astype(h_ref.dtype)
            if norm_bwd is not None:
                x_ref, g_ref, dr_ref = extra_in
                dg_ref = refs[n_in + 1]
                xv = x_ref[...]
                r = lax.rsqrt(jnp.mean(xv * xv, axis=-1, keepdims=True) + RMS_EPS)
                n = xv * r
                dn = out * g_ref[...]
                gpart = jnp.sum(out * n, axis=0, keepdims=True)
                i = pl.program_id(0)

                @pl.when(i == 0)
                def _():
                    dg_ref[...] = gpart

                @pl.when(i > 0)
                def _():
                    dg_ref[...] += gpart

                out = dr_ref[...] + r * (dn - n * jnp.mean(dn * n, axis=-1, keepdims=True))
            o_ref[...] = out.astype(o_ref.dtype)

        if nk == 1:
            finish(part)
            return
        acc_ref = refs[-1]
        k = pl.program_id(2)

        @pl.when(k == 0)
        def _():
            acc_ref[...] = part

        @pl.when(jnp.logical_and(k > 0, k < nk - 1))
        def _():
            acc_ref[...] += part

        @pl.when(k == nk - 1)
        def _():
            finish(acc_ref[...] + part)

    multi = len(out_specs) > 1
    return pl.pallas_call(
        body, name=name, grid=(nm, nn, nk), in_specs=in_specs,
        out_specs=out_specs if multi else o_spec, out_shape=out_shapes if multi else o_shape,
        scratch_shapes=[] if nk == 1 else [pltpu.VMEM((bm, bn), F32)],
        compiler_params=_params(("arbitrary",) * 3 if norm_bwd is not None else ("parallel", "parallel", "arbitrary")),
    )(*args)


def _rmsnorm_fwd(x, gain, *, name, tm=512):
    T, D = x.shape
    tm = min(tm, T)

    def body(x_ref, g_ref, o_ref):
        xv = x_ref[...]
        r = lax.rsqrt(jnp.mean(xv * xv, axis=-1, keepdims=True) + RMS_EPS)
        o_ref[...] = ((xv * r) * g_ref[...]).astype(o_ref.dtype)

    return pl.pallas_call(
        body, name=name, grid=(T // tm,),
        in_specs=[pl.BlockSpec((tm, D), lambda i: (i, 0)), pl.BlockSpec((1, D), lambda i: (0, 0))],
        out_specs=pl.BlockSpec((tm, D), lambda i: (i, 0)),
        out_shape=jax.ShapeDtypeStruct((T, D), BF16),
        compiler_params=_params(("parallel",)),
    )(x, gain.reshape(1, D))


def _loss_head(y, target, *, name, tm=512):
    T, D = y.shape
    tm = min(tm, T)

    def body(y_ref, t_ref, dy_ref, l_ref):
        i = pl.program_id(0)
        e = y_ref[...] - t_ref[...]
        dy_ref[...] = e * (1.0 / D)
        s = 0.5 * jnp.sum(jnp.mean(e * e, axis=-1, keepdims=True), axis=0, keepdims=True)
        part = jnp.broadcast_to(s, (1, LANES))

        @pl.when(i == 0)
        def _():
            l_ref[...] = part

        @pl.when(i > 0)
        def _():
            l_ref[...] += part

    return pl.pallas_call(
        body, name=name, grid=(T // tm,),
        in_specs=[pl.BlockSpec((tm, D), lambda i: (i, 0)), pl.BlockSpec((tm, D), lambda i: (i, 0))],
        out_specs=[pl.BlockSpec((tm, D), lambda i: (i, 0)), pl.BlockSpec((1, LANES), lambda i: (0, 0))],
        out_shape=[jax.ShapeDtypeStruct((T, D), F32), jax.ShapeDtypeStruct((1, LANES), F32)],
        compiler_params=_params(("arbitrary",)),
    )(y, target)


def _hgrn_selectors():
    t = jnp.arange(CHUNK)[:, None]
    s = jnp.arange(CHUNK)[None, :]
    mats = [s <= t, s < (t // SUB) * SUB]
    for i in range(1, CHUNK // SUB):
        mats.append(jnp.broadcast_to(s < i * SUB, (8, CHUNK)))
    mats.append(jnp.ones((8, CHUNK), bool))
    sel = jnp.concatenate([m.astype(BF16) for m in mats], axis=0)
    rev = (s >= t).astype(BF16)
    return sel, rev


def _select_sums(sel, x):
    hi = x.astype(BF16)
    r1 = x - hi.astype(F32)
    mid = r1.astype(BF16)
    lo = (r1 - mid.astype(F32)).astype(BF16)
    return _dot(sel, hi, NN) + (_dot(sel, mid, NN) + _dot(sel, lo, NN))


def _hgrn_cums(sel, logf):
    nsub = CHUNK // SUB
    cums = _select_sums(sel, logf)
    g = cums[0:CHUNK]
    rrow = cums[CHUNK:2 * CHUNK]
    base = 2 * CHUNK
    rsel = [None] + [jnp.max(cums[base + 8 * (i - 1):base + 8 * i], axis=0, keepdims=True) for i in range(1, nsub)]
    gl = jnp.max(cums[base + 8 * (nsub - 1):base + 8 * nsub], axis=0, keepdims=True)
    return g, rrow, rsel, gl


def _hgrn_gates(p, lb, D):
    qpre, fpre, iv, gpre = p[:, 0:D], p[:, D:2 * D], p[:, 2 * D:3 * D], p[:, 3 * D:4 * D]
    sig = _sigmoid(fpre)
    forget = lb + (1.0 - lb) * sig
    key = 1.0 - forget
    logf = jnp.log(forget)
    sq = _sigmoid(qpre)
    qs = qpre * sq
    return qpre, sq, qs, sig, forget, key, logf, iv, gpre


def _hgrn_scores(qt, kh, gh, rsel, row, col):
    nsub = CHUNK // SUB
    blocks, kts = [], []
    for i in range(nsub):
        ri = rsel[i]
        kt = kh * jnp.exp(jnp.minimum(ri - gh, EXP_CLAMP)) if ri is not None else \
            kh * jnp.exp(jnp.minimum(-gh, EXP_CLAMP))
        kts.append(kt)
        blocks.append(_dot(qt[i * SUB:(i + 1) * SUB].astype(BF16), kt.astype(BF16), NT))
    a = jnp.concatenate(blocks, axis=0)
    return jnp.where(col <= row, a, 0.0), kts


def _hgrn_fwd(p, lb, gn, *, name, ride=None):
    T, D4 = p.shape
    D = D4 // 4
    H = D // HEAD
    nc = T // CHUNK
    sel, _ = _hgrn_selectors()
    nsel = sel.shape[0]
    nsub = CHUNK // SUB

    def body(p_ref, lb_ref, gn_ref, sel_ref, og_ref, o_ref, s0_ref, st_ref):
        c = pl.program_id(0)

        @pl.when(c == 0)
        def _():
            st_ref[...] = jnp.zeros_like(st_ref)

        pv = p_ref[...]
        _, _, qs, _, _, key, logf, iv, gpre = _hgrn_gates(pv, lb_ref[...], D)
        g, rrow, rsel_all, gl = _hgrn_cums(sel_ref[...], logf)
        s0_ref[...] = st_ref[...]
        row = lax.broadcasted_iota(jnp.int32, (CHUNK, CHUNK), 0)
        col = lax.broadcasted_iota(jnp.int32, (CHUNK, CHUNK), 1)
        gnv = gn_ref[...]
        for h in range(H):
            sl = slice(h * HEAD, (h + 1) * HEAD)
            gh, qh, kh, vh = g[:, sl], qs[:, sl], key[:, sl], iv[:, sl]
            st = st_ref[sl, :]
            o = _dot((qh * jnp.exp(gh)).astype(BF16), st.astype(BF16), NT)
            qt = qh * jnp.exp(gh - rrow[:, sl])
            a, _ = _hgrn_scores(qt, kh, gh, [None if r is None else r[:, sl] for r in rsel_all], row, col)
            o = o + _dot(a.astype(BF16), vh.astype(BF16), NN)
            glh = gl[:, sl]
            kd = kh * jnp.exp(glh - gh)
            st_ref[sl, :] = st * jnp.exp(glh) + _dot(vh.astype(BF16), kd.astype(BF16), TN)
            o_ref[:, sl] = o
            r = lax.rsqrt(jnp.mean(o * o, axis=-1, keepdims=True) + RMS_EPS)
            gp = gpre[:, sl]
            og_ref[:, sl] = (((o * r) * gnv) * (gp * _sigmoid(gp))).astype(og_ref.dtype)

    return _call(
        body, name=name, grid=(nc,),
        in_specs=[pl.BlockSpec((CHUNK, D4), lambda c: (c, 0)), pl.BlockSpec((1, D), lambda c: (0, 0)),
                  pl.BlockSpec((1, HEAD), lambda c: (0, 0)), pl.BlockSpec((nsel, CHUNK), lambda c: (0, 0))],
        out_specs=[pl.BlockSpec((CHUNK, D), lambda c: (c, 0)), pl.BlockSpec((CHUNK, D), lambda c: (c, 0)),
                   pl.BlockSpec((None, D, HEAD), lambda c: (c, 0, 0))],
        out_shape=[jax.ShapeDtypeStruct((T, D), BF16), jax.ShapeDtypeStruct((T, D), F32),
                   jax.ShapeDtypeStruct((nc, D, HEAD), F32)],
        scratch_shapes=[pltpu.VMEM((D, HEAD), F32)], sem=("arbitrary",), ride=ride,
        args=(p, lb.reshape(1, D), gn.reshape(1, HEAD), sel))


def _hgrn_bwd(p, lb, gn, s0, o_saved, dog, *, name, ride=None):
    T, D4 = p.shape
    D = D4 // 4
    H = D // HEAD
    nc = T // CHUNK
    sel, rev = _hgrn_selectors()
    nsel = sel.shape[0]
    nsub = CHUNK // SUB

    def body(p_ref, lb_ref, gn_ref, sel_ref, rev_ref, s0_ref, s1_ref, o_ref, dog_ref,
             dp_ref, dlb_ref, dgn_ref, dst_ref):
        c = pl.program_id(0)

        @pl.when(c == 0)
        def _():
            dst_ref[...] = jnp.zeros_like(dst_ref)
            dlb_ref[...] = jnp.zeros_like(dlb_ref)
            dgn_ref[...] = jnp.zeros_like(dgn_ref)

        pv = p_ref[...]
        lbv = lb_ref[...]
        qpre, sq, qs, sig, forget, key, logf, iv, gpre = _hgrn_gates(pv, lbv, D)
        g, rrow, rsel_all, gl = _hgrn_cums(sel_ref[...], logf)
        row = lax.broadcasted_iota(jnp.int32, (CHUNK, CHUNK), 0)
        col = lax.broadcasted_iota(jnp.int32, (CHUNK, CHUNK), 1)
        causal = col <= row
        gnv = gn_ref[...]
        dgn_acc = jnp.zeros((1, HEAD), F32)
        for h in range(H):
            sl = slice(h * HEAD, (h + 1) * HEAD)
            gh, qh, kh, vh = g[:, sl], qs[:, sl], key[:, sl], iv[:, sl]
            glh = gl[:, sl]
            st0 = s0_ref[sl, :]
            st1 = s1_ref[sl, :]
            dst = dst_ref[sl, :]
            o = o_ref[:, sl]
            r = lax.rsqrt(jnp.mean(o * o, axis=-1, keepdims=True) + RMS_EPS)
            n = o * r
            gp = gpre[:, sl]
            sg = _sigmoid(gp)
            dog_h = dog_ref[:, sl]
            d_on = dog_h * (gp * sg)
            dgpre = dog_h * (n * gnv) * (sg * (1.0 + gp * (1.0 - sg)))
            dgn_acc = dgn_acc + jnp.sum(d_on * n, axis=0, keepdims=True)
            dn = d_on * gnv
            do = r * (dn - n * jnp.mean(dn * n, axis=-1, keepdims=True))
            dob = do.astype(BF16)
            eg = jnp.exp(gh)
            qg = qh * eg
            eqr = jnp.exp(gh - rrow[:, sl])
            qt = qh * eqr
            rsel = [None if rr is None else rr[:, sl] for rr in rsel_all]
            a, kts = _hgrn_scores(qt, kh, gh, rsel, row, col)
            ekd = jnp.exp(glh - gh)
            kd = kh * ekd
            vb = vh.astype(BF16)
            dq = _dot(dob, st0.astype(BF16), NN) * eg
            da = jnp.where(causal, _dot(dob, vb, NT), 0.0)
            dab = da.astype(BF16)
            dqt_blocks = []
            dk = _dot(vb, dst.astype(BF16), NN) * ekd
            for i in range(nsub):
                rs = slice(i * SUB, (i + 1) * SUB)
                ktb = kts[i].astype(BF16)
                dqt_blocks.append(_dot(dab[rs], ktb, NN))
                dkt = _dot(dab[rs], qt[rs].astype(BF16), TN)
                ei = jnp.exp(jnp.minimum((rsel[i] if rsel[i] is not None else 0.0) - gh, EXP_CLAMP))
                dk = dk + dkt * ei
            dq = dq + jnp.concatenate(dqt_blocks, axis=0) * eqr
            dv = _dot(a.astype(BF16), dob, TN) + _dot(kd.astype(BF16), dst.astype(BF16), NT)
            dst_ref[sl, :] = dst * jnp.exp(glh) + _dot(dob, qg.astype(BF16), TN)
            term = jnp.sum(dst * st1, axis=0, keepdims=True)
            dg = qh * dq - kh * dk
            dlogf = _select_sums(rev_ref[...], dg) + term
            fg = forget[:, sl]
            sgf = sig[:, sl]
            lbh = lbv[:, sl]
            dforget = dlogf / fg - dk
            dfpre = dforget * (1.0 - lbh) * (sgf * (1.0 - sgf))
            dlb_ref[:, sl] += jnp.sum(dforget * (1.0 - sgf), axis=0, keepdims=True)
            sqh = sq[:, sl]
            dqpre = dq * (sqh * (1.0 + qpre[:, sl] * (1.0 - sqh)))
            dp_ref[:, h * HEAD:(h + 1) * HEAD] = dqpre.astype(dp_ref.dtype)
            dp_ref[:, D + h * HEAD:D + (h + 1) * HEAD] = dfpre.astype(dp_ref.dtype)
            dp_ref[:, 2 * D + h * HEAD:2 * D + (h + 1) * HEAD] = dv.astype(dp_ref.dtype)
            dp_ref[:, 3 * D + h * HEAD:3 * D + (h + 1) * HEAD] = dgpre.astype(dp_ref.dtype)
        dgn_ref[...] += dgn_acc

    rc = lambda c: nc - 1 - c
    return _call(
        body, name=name, grid=(nc,),
        in_specs=[pl.BlockSpec((CHUNK, D4), lambda c: (rc(c), 0)), pl.BlockSpec((1, D), lambda c: (0, 0)),
                  pl.BlockSpec((1, HEAD), lambda c: (0, 0)), pl.BlockSpec((nsel, CHUNK), lambda c: (0, 0)),
                  pl.BlockSpec((CHUNK, CHUNK), lambda c: (0, 0)),
                  pl.BlockSpec((None, D, HEAD), lambda c: (rc(c), 0, 0)),
                  pl.BlockSpec((None, D, HEAD), lambda c: (jnp.minimum(rc(c) + 1, nc - 1), 0, 0)),
                  pl.BlockSpec((CHUNK, D), lambda c: (rc(c), 0)), pl.BlockSpec((CHUNK, D), lambda c: (rc(c), 0))],
        out_specs=[pl.BlockSpec((CHUNK, D4), lambda c: (rc(c), 0)), pl.BlockSpec((1, D), lambda c: (0, 0)),
                   pl.BlockSpec((1, HEAD), lambda c: (0, 0))],
        out_shape=[jax.ShapeDtypeStruct((T, D4), BF16), jax.ShapeDtypeStruct((1, D), F32),
                   jax.ShapeDtypeStruct((1, HEAD), F32)],
        scratch_shapes=[pltpu.VMEM((D, HEAD), F32)], sem=("arbitrary",), ride=ride,
        args=(p, lb.reshape(1, D), gn.reshape(1, HEAD), sel, rev, s0, s0, o_saved, dog))


def _rope_tables(positions):
    inv_freq = ROPE_THETA ** (-jnp.arange(0, ROPE, 2, dtype=F32) / ROPE)
    ang = positions.astype(F32)[:, None] * inv_freq
    cos, sin = jnp.cos(ang), jnp.sin(ang)
    z = jnp.zeros_like(cos)
    ctab = jnp.concatenate([cos, cos, z, z], axis=-1)
    s1 = jnp.concatenate([-sin, z, z, z], axis=-1)
    s2 = jnp.concatenate([z, sin, z, z], axis=-1)
    return ctab, s1, s2


def _rope(z, ct, s1, s2):
    return z * ct + pltpu.roll(z, 96, 1) * s1 + pltpu.roll(z, 32, 1) * s2


def _rope_t(d, ct, s1, s2):
    return d * ct + pltpu.roll(d * s1, 32, 1) + pltpu.roll(d * s2, 96, 1)


def _mla_prep_fwd(c, wq, wkv, ga_q, ga_kv, gq, gk, ct, s1, s2, *, name, tm=256):
    T, CW = c.shape
    R = (CW - LANES) // 2
    H = wq.shape[1] // QK_PAD
    tm = min(tm, T)

    def body(c_ref, wq_ref, wkv_ref, gaq_ref, gakv_ref, gq_ref, gk_ref, ct_ref, s1_ref, s2_ref,
             q_ref, k_ref, v_ref):
        cv = c_ref[...]
        cq, ckv, kr = cv[:, 0:R], cv[:, R:2 * R], cv[:, 2 * R:2 * R + LANES]
        rq = lax.rsqrt(jnp.mean(cq * cq, axis=-1, keepdims=True) + RMS_EPS)
        cqn = ((cq * rq) * gaq_ref[...]).astype(BF16)
        rk = lax.rsqrt(jnp.mean(ckv * ckv, axis=-1, keepdims=True) + RMS_EPS)
        ckvn = ((ckv * rk) * gakv_ref[...]).astype(BF16)
        qp = _dot(cqn, wq_ref[...], NN)
        kvp = _dot(ckvn, wkv_ref[...], NN)
        ctv, s1v, s2v = ct_ref[...], s1_ref[...], s2_ref[...]
        gqv, gkv = gq_ref[...], gk_ref[...]
        krs = jnp.sum(kr * kr, axis=-1, keepdims=True)
        for h in range(H):
            b = h * QK_PAD
            qn, qr = qp[:, b:b + HEAD], qp[:, b + HEAD:b + QK_PAD]
            ss = jnp.sum(qn * qn, axis=-1, keepdims=True) + jnp.sum(qr * qr, axis=-1, keepdims=True)
            rr = lax.rsqrt(ss * (1.0 / QK_HEAD) + RMS_EPS)
            q_ref[:, b:b + HEAD] = ((qn * rr) * gqv[:, 0:HEAD]).astype(q_ref.dtype)
            q_ref[:, b + HEAD:b + QK_PAD] = _rope((qr * rr) * gqv[:, HEAD:QK_PAD], ctv, s1v, s2v).astype(q_ref.dtype)
            kn, vv = kvp[:, b:b + HEAD], kvp[:, b + HEAD:b + QK_PAD]
            ssk = jnp.sum(kn * kn, axis=-1, keepdims=True) + krs
            rrk = lax.rsqrt(ssk * (1.0 / QK_HEAD) + RMS_EPS)
            k_ref[:, b:b + HEAD] = ((kn * rrk) * gkv[:, 0:HEAD]).astype(k_ref.dtype)
            k_ref[:, b + HEAD:b + QK_PAD] = _rope((kr * rrk) * gkv[:, HEAD:QK_PAD], ctv, s1v, s2v).astype(k_ref.dtype)
            v_ref[:, h * HEAD:(h + 1) * HEAD] = vv.astype(v_ref.dtype)

    full = lambda shape: pl.BlockSpec(shape, lambda i: (0, 0))
    tok = lambda w: pl.BlockSpec((tm, w), lambda i: (i, 0))
    return pl.pallas_call(
        body, name=name, grid=(T // tm,),
        in_specs=[tok(CW), full(wq.shape), full(wkv.shape), full((1, R)), full((1, R)), full((1, QK_PAD)),
                  full((1, QK_PAD)), tok(LANES), tok(LANES), tok(LANES)],
        out_specs=[tok(H * QK_PAD), tok(H * QK_PAD), tok(H * HEAD)],
        out_shape=[jax.ShapeDtypeStruct((T, H * QK_PAD), BF16), jax.ShapeDtypeStruct((T, H * QK_PAD), BF16),
                   jax.ShapeDtypeStruct((T, H * HEAD), BF16)],
        compiler_params=_params(("parallel",)),
    )(c, wq, wkv, ga_q.reshape(1, R), ga_kv.reshape(1, R), gq, gk, ct, s1, s2)


def _mla_prep_bwd(c, wq, wkv, ga_q, ga_kv, gq, gk, ct, s1, s2, dq, dk, dv, *, name, tm=256):
    T, CW = c.shape
    R = (CW - LANES) // 2
    H = wq.shape[1] // QK_PAD
    tm = min(tm, T)

    def body(c_ref, wq_ref, wkv_ref, gaq_ref, gakv_ref, gq_ref, gk_ref, ct_ref, s1_ref, s2_ref,
             dq_ref, dk_ref, dv_ref,
             dc_ref, dwq_ref, dwkv_ref, dgaq_ref, dgakv_ref, dgq_ref, dgk_ref, dqp_ref, dkvp_ref):
        i = pl.program_id(0)

        @pl.when(i == 0)
        def _():
            for ref in (dwq_ref, dwkv_ref, dgaq_ref, dgakv_ref, dgq_ref, dgk_ref):
                ref[...] = jnp.zeros_like(ref)

        cv = c_ref[...]
        cq, ckv, kr = cv[:, 0:R], cv[:, R:2 * R], cv[:, 2 * R:2 * R + LANES]
        rq = lax.rsqrt(jnp.mean(cq * cq, axis=-1, keepdims=True) + RMS_EPS)
        nq = cq * rq
        cqn = (nq * gaq_ref[...]).astype(BF16)
        rk = lax.rsqrt(jnp.mean(ckv * ckv, axis=-1, keepdims=True) + RMS_EPS)
        nkv = ckv * rk
        ckvn = (nkv * gakv_ref[...]).astype(BF16)
        qp = _dot(cqn, wq_ref[...], NN)
        kvp = _dot(ckvn, wkv_ref[...], NN)
        ctv, s1v, s2v = ct_ref[...], s1_ref[...], s2_ref[...]
        gqv, gkv = gq_ref[...], gk_ref[...]
        krs = jnp.sum(kr * kr, axis=-1, keepdims=True)
        dkr = jnp.zeros((tm, LANES), F32)
        dgq_n = jnp.zeros((1, HEAD), F32)
        dgq_r = jnp.zeros((1, HEAD), F32)
        dgk_n = jnp.zeros((1, HEAD), F32)
        dgk_r = jnp.zeros((1, HEAD), F32)
        for h in range(H):
            b = h * QK_PAD
            qn, qr = qp[:, b:b + HEAD], qp[:, b + HEAD:b + QK_PAD]
            ss = jnp.sum(qn * qn, axis=-1, keepdims=True) + jnp.sum(qr * qr, axis=-1, keepdims=True)
            rr = lax.rsqrt(ss * (1.0 / QK_HEAD) + RMS_EPS)
            un, ur = qn * rr, qr * rr
            dzn = dq_ref[:, b:b + HEAD]
            dzr = _rope_t(dq_ref[:, b + HEAD:b + QK_PAD], ctv, s1v, s2v)
            dgq_n = dgq_n + jnp.sum(dzn * un, axis=0, keepdims=True)
            dgq_r = dgq_r + jnp.sum(dzr * ur, axis=0, keepdims=True)
            dun, dur = dzn * gqv[:, 0:HEAD], dzr * gqv[:, HEAD:QK_PAD]
            m = (jnp.sum(dun * un, axis=-1, keepdims=True) + jnp.sum(dur * ur, axis=-1, keepdims=True)) \
                * (1.0 / QK_HEAD)
            dqp_ref[:, b:b + HEAD] = (rr * (dun - un * m)).astype(BF16)
            dqp_ref[:, b + HEAD:b + QK_PAD] = (rr * (dur - ur * m)).astype(BF16)
            kn = kvp[:, b:b + HEAD]
            ssk = jnp.sum(kn * kn, axis=-1, keepdims=True) + krs
            rrk = lax.rsqrt(ssk * (1.0 / QK_HEAD) + RMS_EPS)
            vn, vr = kn * rrk, kr * rrk
            dyn = dk_ref[:, b:b + HEAD]
            dyr = _rope_t(dk_ref[:, b + HEAD:b + QK_PAD], ctv, s1v, s2v)
            dgk_n = dgk_n + jnp.sum(dyn * vn, axis=0, keepdims=True)
            dgk_r = dgk_r + jnp.sum(dyr * vr, axis=0, keepdims=True)
            dvn, dvr = dyn * gkv[:, 0:HEAD], dyr * gkv[:, HEAD:QK_PAD]
            mk = (jnp.sum(dvn * vn, axis=-1, keepdims=True) + jnp.sum(dvr * vr, axis=-1, keepdims=True)) \
                * (1.0 / QK_HEAD)
            dkvp_ref[:, b:b + HEAD] = (rrk * (dvn - vn * mk)).astype(BF16)
            dkr = dkr + rrk * (dvr - vr * mk)
            dkvp_ref[:, b + HEAD:b + QK_PAD] = dv_ref[:, h * HEAD:(h + 1) * HEAD].astype(BF16)
        dgq_ref[:, 0:HEAD] += dgq_n
        dgq_ref[:, HEAD:QK_PAD] += dgq_r
        dgk_ref[:, 0:HEAD] += dgk_n
        dgk_ref[:, HEAD:QK_PAD] += dgk_r
        dqp = dqp_ref[...]
        dkvp = dkvp_ref[...]
        dwq_ref[...] += _dot(cqn, dqp, TN)
        dwkv_ref[...] += _dot(ckvn, dkvp, TN)
        dcqn = _dot(dqp, wq_ref[...], NT)
        dckvn = _dot(dkvp, wkv_ref[...], NT)
        dgaq_ref[...] += jnp.sum(dcqn * nq, axis=0, keepdims=True)
        dgakv_ref[...] += jnp.sum(dckvn * nkv, axis=0, keepdims=True)
        dnq = dcqn * gaq_ref[...]
        dnkv = dckvn * gakv_ref[...]
        dc_ref[:, 0:R] = (rq * (dnq - nq * jnp.mean(dnq * nq, axis=-1, keepdims=True))).astype(dc_ref.dtype)
        dc_ref[:, R:2 * R] = (rk * (dnkv - nkv * jnp.mean(dnkv * nkv, axis=-1, keepdims=True))).astype(dc_ref.dtype)
        dc_ref[:, 2 * R:2 * R + LANES] = dkr.astype(dc_ref.dtype)

    full = lambda shape: pl.BlockSpec(shape, lambda i: (0, 0))
    tok = lambda w: pl.BlockSpec((tm, w), lambda i: (i, 0))
    return pl.pallas_call(
        body, name=name, grid=(T // tm,),
        in_specs=[tok(CW), full(wq.shape), full(wkv.shape), full((1, R)), full((1, R)), full((1, QK_PAD)),
                  full((1, QK_PAD)), tok(LANES), tok(LANES), tok(LANES),
                  tok(H * QK_PAD), tok(H * QK_PAD), tok(H * HEAD)],
        out_specs=[tok(CW), full(wq.shape), full(wkv.shape), full((1, R)), full((1, R)), full((1, QK_PAD)),
                   full((1, QK_PAD))],
        out_shape=[jax.ShapeDtypeStruct((T, CW), BF16), jax.ShapeDtypeStruct(wq.shape, F32),
                   jax.ShapeDtypeStruct(wkv.shape, F32), jax.ShapeDtypeStruct((1, R), F32),
                   jax.ShapeDtypeStruct((1, R), F32), jax.ShapeDtypeStruct((1, QK_PAD), F32),
                   jax.ShapeDtypeStruct((1, QK_PAD), F32)],
        scratch_shapes=[pltpu.VMEM((tm, H * QK_PAD), BF16), pltpu.VMEM((tm, H * QK_PAD), BF16)],
        compiler_params=_params(("arbitrary",)),
    )(c, wq, wkv, ga_q.reshape(1, R), ga_kv.reshape(1, R), gq, gk, ct, s1, s2, dq, dk, dv)


NEG = -1e30
LOG2E = 1.4426950408889634


def _attn_fwd(q, k, v, *, name, tb=512, hp=2, ride=None):
    T = q.shape[0]
    H = q.shape[1] // QK_PAD
    tb = min(tb, T)
    nq = T // tb
    scale = QK_HEAD ** -0.5
    c2 = scale * LOG2E
    assert H % hp == 0

    def body(q_ref, k_ref, v_ref, ot_ref, lse_ref, m_ref, l_ref, acc_ref):
        i = pl.program_id(1)
        m_ref[...] = jnp.full_like(m_ref, NEG)
        l_ref[...] = jnp.zeros_like(l_ref)
        acc_ref[...] = jnp.zeros_like(acc_ref)

        def step(j, masked):
            off = pl.multiple_of(j * tb, tb)
            for hh in range(hp):
                qs, vs = slice(hh * QK_PAD, (hh + 1) * QK_PAD), slice(hh * HEAD, (hh + 1) * HEAD)
                kb = k_ref[pl.ds(off, tb), qs]
                vb = v_ref[pl.ds(off, tb), vs]
                st = _dot(kb, q_ref[:, qs], NT)
                if masked:
                    kpos = lax.broadcasted_iota(jnp.int32, (tb, tb), 0)
                    qpos = lax.broadcasted_iota(jnp.int32, (tb, tb), 1)
                    st = jnp.where(kpos <= qpos, st, NEG)
                m_old = m_ref[hh]
                m_new = jnp.maximum(m_old, jnp.max(st, axis=0, keepdims=True))
                alpha = jnp.exp2((m_old - m_new) * c2)
                pt = jnp.exp2((st - m_new) * c2)
                l_ref[hh] = l_ref[hh] * alpha + jnp.sum(pt, axis=0, keepdims=True)
                acc_ref[vs, :] = acc_ref[vs, :] * alpha + _dot(vb, pt.astype(BF16), TN)
                m_ref[hh] = m_new

        def loop_body(j, carry):
            step(j, False)
            return carry

        lax.fori_loop(0, i, loop_body, 0)
        step(i, True)
        for hh in range(hp):
            vs = slice(hh * HEAD, (hh + 1) * HEAD)
            l = l_ref[hh]
            ot_ref[vs, :] = (acc_ref[vs, :] / l).astype(ot_ref.dtype)
            lse_ref[hh] = m_ref[hh] * scale + jnp.log(l)

    return _call(
        body, name=name, grid=(H // hp, nq),
        in_specs=[pl.BlockSpec((tb, hp * QK_PAD), lambda g, i: (i, g)),
                  pl.BlockSpec((T, hp * QK_PAD), lambda g, i: (0, g)),
                  pl.BlockSpec((T, hp * HEAD), lambda g, i: (0, g))],
        out_specs=[pl.BlockSpec((None, hp * HEAD, tb), lambda g, i: (i, g, 0)),
                   pl.BlockSpec((hp, None, 1, tb), lambda g, i: (g, i, 0, 0))],
        out_shape=[jax.ShapeDtypeStruct((nq, H * HEAD, tb), BF16), jax.ShapeDtypeStruct((H, nq, 1, tb), F32)],
        scratch_shapes=[pltpu.VMEM((hp, 1, tb), F32), pltpu.VMEM((hp, 1, tb), F32),
                        pltpu.VMEM((hp * HEAD, tb), F32)],
        sem=("parallel", "arbitrary"), ride=ride, args=(q, k, v))


def _attn_bwd(q, k, v, ot, lse, dot_, *, name, ride=None):
    T = q.shape[0]
    H = q.shape[1] // QK_PAD
    nq, _, tb = ot.shape
    scale = QK_HEAD ** -0.5
    c2 = scale * LOG2E

    def body(q_ref, k_ref, v_ref, ot_ref, lse_ref, dot_ref, dq_ref, dk_ref, dv_ref, dka_ref, dva_ref):
        j = pl.program_id(1)

        @pl.when(j == 0)
        def _():
            dq_ref[...] = jnp.zeros_like(dq_ref)

        kb = k_ref[...]
        vb = v_ref[...]

        def step(i, masked):
            off = i * tb if isinstance(i, int) else pl.multiple_of(i * tb, tb)
            qb = q_ref[pl.ds(off, tb), :]
            dob = dot_ref[i]
            ob = ot_ref[i]
            st = _dot(kb, qb, NT)
            if masked:
                kpos = lax.broadcasted_iota(jnp.int32, (tb, tb), 0)
                qpos = lax.broadcasted_iota(jnp.int32, (tb, tb), 1)
                st = jnp.where(kpos <= qpos, st, NEG)
            pt = jnp.exp2(st * c2 - lse_ref[i] * LOG2E)
            dpt = _dot(vb, dob, NN)
            delta = jnp.sum(dob.astype(F32) * ob.astype(F32), axis=0, keepdims=True)
            dst = (pt * (dpt - delta)).astype(BF16)
            dq_ref[pl.ds(off, tb), :] += _dot(dst, kb, TN)
            return _dot(dst, qb, NN), _dot(pt.astype(BF16), dob, NT)

        dk0, dv0 = step(j, True)
        dka_ref[...] = dk0
        dva_ref[...] = dv0

        rest = nq - 1 - j

        def pair_body(t, carry):
            i0 = j + 1 + 2 * t
            dk1, dv1 = step(i0, False)
            dk2, dv2 = step(i0 + 1, False)
            dka_ref[...] += dk1 + dk2
            dva_ref[...] += dv1 + dv2
            return carry

        lax.fori_loop(0, rest // 2, pair_body, 0)

        @pl.when(rest % 2 == 1)
        def _():
            dk1, dv1 = step(nq - 1, False)
            dka_ref[...] += dk1
            dva_ref[...] += dv1

        dk_ref[...] = dka_ref[...] * scale
        dv_ref[...] = dva_ref[...]

        @pl.when(j == nq - 1)
        def _():
            dq_ref[...] = dq_ref[...] * scale

    return _call(
        body, name=name, grid=(H, nq),
        in_specs=[pl.BlockSpec((T, QK_PAD), lambda h, j: (0, h)), pl.BlockSpec((tb, QK_PAD), lambda h, j: (j, h)),
                  pl.BlockSpec((tb, HEAD), lambda h, j: (j, h)),
                  pl.BlockSpec((nq, HEAD, tb), lambda h, j: (0, h, 0)),
                  pl.BlockSpec((None, nq, 1, tb), lambda h, j: (h, 0, 0, 0)),
                  pl.BlockSpec((nq, HEAD, tb), lambda h, j: (0, h, 0))],
        out_specs=[pl.BlockSpec((T, QK_PAD), lambda h, j: (0, h)), pl.BlockSpec((tb, QK_PAD), lambda h, j: (j, h)),
                   pl.BlockSpec((tb, HEAD), lambda h, j: (j, h))],
        out_shape=[jax.ShapeDtypeStruct((T, H * QK_PAD), F32), jax.ShapeDtypeStruct((T, H * QK_PAD), F32),
                   jax.ShapeDtypeStruct((T, H * HEAD), F32)],
        scratch_shapes=[pltpu.VMEM((tb, QK_PAD), F32), pltpu.VMEM((tb, HEAD), F32)],
        sem=("parallel", "arbitrary"), ride=ride, args=(q, k, v, ot, lse, dot_))


def _conv_taps(u, prev6, prev7, rows):
    u1 = jnp.where(rows >= 1, pltpu.roll(u, 1, 0), prev7)
    u2 = jnp.where(rows >= 2, pltpu.roll(u, 2, 0), jnp.where(rows == 0, prev6, prev7))
    return u2, u1


def _ffn_up_fwd(h, w_up, conv_w, conv_b, *, name, tm=512):
    T, D = h.shape
    ns, _, fs = w_up.shape
    nh = ns // 2
    tm = min(tm, T)

    def body(h_ref, wg_ref, wu_ref, cwg_ref, cwu_ref, cbg_ref, cbu_ref, a_ref, u_ref, cg_ref, cu_ref):
        i = pl.program_id(1)

        @pl.when(i == 0)
        def _():
            cg_ref[...] = jnp.zeros_like(cg_ref)
            cu_ref[...] = jnp.zeros_like(cu_ref)

        hv = h_ref[...]
        rows = lax.broadcasted_iota(jnp.int32, (tm, 1), 0)
        ys = []
        for idx, (w_ref, cw_ref, cb_ref, carry) in enumerate(
                ((wg_ref, cwg_ref, cbg_ref, cg_ref), (wu_ref, cwu_ref, cbu_ref, cu_ref))):
            u = _dot(hv, w_ref[...], NN)
            u_ref[idx] = u.astype(u_ref.dtype)
            u2, u1 = _conv_taps(u, carry[6:7, :], carry[7:8, :], rows)
            y = cb_ref[...] + u2 * cw_ref[0:1, :]
            y = y + u1 * cw_ref[1:2, :]
            y = y + u * cw_ref[2:3, :]
            ys.append(y)
            carry[...] = u[tm - 8:tm, :]
        yg, yu = ys
        a_ref[...] = ((yg * _sigmoid(yg)) * yu).astype(a_ref.dtype)

    shard = lambda r, off: pl.BlockSpec((None, r, fs), lambda j, i: (j + off, 0, 0))
    return pl.pallas_call(
        body, name=name, grid=(nh, T // tm),
        in_specs=[pl.BlockSpec((tm, D), lambda j, i: (i, 0)), shard(D, 0), shard(D, nh),
                  shard(3, 0), shard(3, nh), shard(1, 0), shard(1, nh)],
        out_specs=[pl.BlockSpec((None, tm, fs), lambda j, i: (j, i, 0)),
                   pl.BlockSpec((2, None, tm, fs), lambda j, i: (0, j, i, 0))],
        out_shape=[jax.ShapeDtypeStruct((nh, T, fs), BF16), jax.ShapeDtypeStruct((2, nh, T, fs), BF16)],
        scratch_shapes=[pltpu.VMEM((8, fs), F32), pltpu.VMEM((8, fs), F32)],
        compiler_params=_params(("parallel", "arbitrary")),
    )(h, w_up, w_up, conv_w, conv_w, conv_b, conv_b)


def _ffn_act_bwd(dxo, w_down, u, conv_w, conv_b, *, name, tm=512):
    T, D = dxo.shape
    _, nh, _, fs = u.shape
    tm = min(tm, T)
    nt = T // tm
    hb = tm // 8

    def body(dx_ref, wd_ref, u_ref, up_ref, cwg_ref, cwu_ref, cbg_ref, cbu_ref,
             du_ref, dcw_ref, dcb_ref, cg_ref, cu_ref):
        i = pl.program_id(1)
        tile = nt - 1 - i

        @pl.when(i == 0)
        def _():
            cg_ref[...] = jnp.zeros_like(cg_ref)
            cu_ref[...] = jnp.zeros_like(cu_ref)
            dcw_ref[...] = jnp.zeros_like(dcw_ref)
            dcb_ref[...] = jnp.zeros_like(dcb_ref)

        rows = lax.broadcasted_iota(jnp.int32, (tm, 1), 0)
        da = _dot(dx_ref[...].astype(BF16), wd_ref[...], NT)
        has_prev = (tile > 0).astype(F32)
        us, ys, shifted = [], [], []
        for idx, (cw_ref, cb_ref) in enumerate(((cwg_ref, cbg_ref), (cwu_ref, cbu_ref))):
            uv = u_ref[idx].astype(F32)
            p6 = up_ref[idx, 6:7, :].astype(F32) * has_prev
            p7 = up_ref[idx, 7:8, :].astype(F32) * has_prev
            u2, u1 = _conv_taps(uv, p6, p7, rows)
            y = cb_ref[...] + u2 * cw_ref[0:1, :]
            y = y + u1 * cw_ref[1:2, :]
            y = y + uv * cw_ref[2:3, :]
            us.append(uv)
            ys.append(y)
            shifted.append((u2, u1))
        yg, yu = ys
        sg = _sigmoid(yg)
        dys = (da * yu * (sg * (1.0 + yg * (1.0 - sg))), da * (yg * sg))
        for idx, (cw_ref, carry) in enumerate(((cwg_ref, cg_ref), (cwu_ref, cu_ref))):
            dy = dys[idx]
            u2, u1 = shifted[idx]
            dcb_ref[idx] += jnp.sum(dy, axis=0, keepdims=True)
            dcw_ref[idx, 0:1, :] += jnp.sum(dy * u2, axis=0, keepdims=True)
            dcw_ref[idx, 1:2, :] += jnp.sum(dy * u1, axis=0, keepdims=True)
            dcw_ref[idx, 2:3, :] += jnp.sum(dy * us[idx], axis=0, keepdims=True)
            n0, n1 = carry[0:1, :], carry[1:2, :]
            d1 = jnp.where(rows < tm - 1, pltpu.roll(dy, tm - 1, 0), n0)
            d2 = jnp.where(rows < tm - 2, pltpu.roll(dy, tm - 2, 0), jnp.where(rows == tm - 2, n0, n1))
            du = dy * cw_ref[2:3, :] + d1 * cw_ref[1:2, :] + d2 * cw_ref[0:1, :]
            du_ref[idx] = du.astype(du_ref.dtype)
            carry[...] = dy[0:8, :]

    rt = lambda i: nt - 1 - i
    shard = lambda r, off: pl.BlockSpec((None, r, fs), lambda j, i: (j + off, 0, 0))
    return pl.pallas_call(
        body, name=name, grid=(nh, nt),
        in_specs=[pl.BlockSpec((tm, D), lambda j, i: (rt(i), 0)), pl.BlockSpec((fs, D), lambda j, i: (j, 0)),
                  pl.BlockSpec((2, None, tm, fs), lambda j, i: (0, j, rt(i), 0)),
                  pl.BlockSpec((2, None, 8, fs), lambda j, i: (0, j, jnp.maximum(rt(i) * hb - 1, 0), 0)),
                  shard(3, 0), shard(3, nh), shard(1, 0), shard(1, nh)],
        out_specs=[pl.BlockSpec((2, None, tm, fs), lambda j, i: (0, j, rt(i), 0)),
                   pl.BlockSpec((2, None, 3, fs), lambda j, i: (0, j, 0, 0)),
                   pl.BlockSpec((2, None, 1, fs), lambda j, i: (0, j, 0, 0))],
        out_shape=[jax.ShapeDtypeStruct((2, nh, T, fs), BF16), jax.ShapeDtypeStruct((2, nh, 3, fs), F32),
                   jax.ShapeDtypeStruct((2, nh, 1, fs), F32)],
        scratch_shapes=[pltpu.VMEM((8, fs), F32), pltpu.VMEM((8, fs), F32)],
        compiler_params=_params(("parallel", "arbitrary")),
    )(dxo, w_down, u, u, conv_w, conv_w, conv_b, conv_b)


def _pad_cols(w, n):
    return jnp.pad(w, [(0, 0)] * (w.ndim - 1) + [(0, n - w.shape[-1])])


def _q_up_padded(w):
    R = w.shape[0]
    H = w.shape[1] // QK_HEAD
    return _pad_cols(w.reshape(R, H, QK_HEAD), QK_PAD).reshape(R, H * QK_PAD)


def _q_up_unpadded(w):
    R = w.shape[0]
    H = w.shape[1] // QK_PAD
    return w.reshape(R, H, QK_PAD)[:, :, :QK_HEAD].reshape(R, H * QK_HEAD)


def _xchg_copies(src_refs, out_refs, kinds, send_sems, recv_sems, local_sems):
    x, y, c = lax.axis_index("x"), lax.axis_index("y"), lax.axis_index("c")
    me = 4 * x + 2 * y + c
    copies = []
    for b, kind in enumerate(kinds):
        gather = kind == "gather"
        own = src_refs[b] if gather else src_refs[b].at[me]
        copies.append(pltpu.make_async_copy(own, out_refs[b].at[me], local_sems.at[b]))
        for kk in range(1, N_DEV):
            px = 1 - x if kk & 4 else x
            py = 1 - y if kk & 2 else y
            pc = 1 - c if kk & 1 else c
            peer = 4 * px + 2 * py + pc
            src = src_refs[b] if gather else src_refs[b].at[peer]
            copies.append(pltpu.make_async_remote_copy(
                src_ref=src, dst_ref=out_refs[b].at[me],
                send_sem=send_sems.at[b * (N_DEV - 1) + kk - 1],
                recv_sem=recv_sems.at[b * (N_DEV - 1) + kk - 1],
                device_id=(px, py, pc), device_id_type=pl.DeviceIdType.MESH))
    return copies


def _xchg_out_shapes(srcs, kinds):
    return [jax.ShapeDtypeStruct((N_DEV,) + s.shape if kind == "gather" else s.shape, s.dtype)
            for s, kind in zip(srcs, kinds)]


def _xchg_scratch(n):
    return [pltpu.SemaphoreType.DMA((n * (N_DEV - 1),)), pltpu.SemaphoreType.DMA((n * (N_DEV - 1),)),
            pltpu.SemaphoreType.DMA((n,))]


def _exchange(srcs, kinds, *, name):
    n = len(srcs)

    def body(*refs):
        copies = _xchg_copies(refs[:n], refs[n:2 * n], kinds, *refs[2 * n:])
        for cp in copies:
            cp.start()
        for cp in copies:
            cp.wait()

    hbm = pl.BlockSpec(memory_space=pl.ANY)
    return pl.pallas_call(
        body, name=name, in_specs=[hbm] * n, out_specs=[hbm] * n, out_shape=_xchg_out_shapes(srcs, kinds),
        scratch_shapes=_xchg_scratch(n),
    )(*srcs)


def _call(body, *, name, grid, in_specs, out_specs, out_shape, scratch_shapes, args, sem, ride=None):
    if ride is None:
        outs = pl.pallas_call(body, name=name, grid=grid, in_specs=in_specs, out_specs=out_specs,
                              out_shape=out_shape, scratch_shapes=scratch_shapes,
                              compiler_params=_params(sem))(*args)
        return list(outs), []
    srcs, kinds = ride
    n_in, n_out, n_sc, nx = len(in_specs), len(out_specs), len(scratch_shapes), len(srcs)

    def wrapped(*refs):
        ins, xs = refs[:n_in], refs[n_in:n_in + nx]
        o0 = n_in + nx
        outs, xo = refs[o0:o0 + n_out], refs[o0 + n_out:o0 + n_out + nx]
        s0 = o0 + n_out + nx
        sc, sems = refs[s0:s0 + n_sc], refs[s0 + n_sc:]
        first = functools.reduce(jnp.logical_and, [pl.program_id(d) == 0 for d in range(len(grid))])
        last = functools.reduce(jnp.logical_and, [pl.program_id(d) == grid[d] - 1 for d in range(len(grid))])

        @pl.when(first)
        def _():
            for cp in _xchg_copies(xs, xo, kinds, *sems):
                cp.start()

        body(*ins, *outs, *sc)

        @pl.when(last)
        def _():
            for cp in _xchg_copies(xs, xo, kinds, *sems):
                cp.wait()

    hbm = pl.BlockSpec(memory_space=pl.ANY)
    outs = pl.pallas_call(
        wrapped, name=name, grid=grid, in_specs=list(in_specs) + [hbm] * nx,
        out_specs=list(out_specs) + [hbm] * nx, out_shape=list(out_shape) + _xchg_out_shapes(srcs, kinds),
        scratch_shapes=list(scratch_shapes) + _xchg_scratch(nx),
        compiler_params=_params(("arbitrary",) * len(grid)),
    )(*args, *srcs)
    return list(outs[:n_out]), list(outs[n_out:])


def _sum_slots(parts, *, name):
    _, Rr, C = parts.shape

    def body(p_ref, o_ref):
        acc = p_ref[0].astype(F32)
        for d in range(1, N_DEV):
            acc = acc + p_ref[d].astype(F32)
        o_ref[...] = acc

    return pl.pallas_call(
        body, name=name, grid=(1,),
        in_specs=[pl.BlockSpec((N_DEV, Rr, C), lambda i: (0, 0, 0))],
        out_specs=pl.BlockSpec((Rr, C), lambda i: (0, 0)),
        out_shape=jax.ShapeDtypeStruct((Rr, C), F32),
        compiler_params=_params(("arbitrary",)),
    )(parts)


def _row_tile(rows, cap=512):
    if rows <= cap:
        return rows
    d = (cap // 8) * 8
    while d >= 8:
        if rows % d == 0:
            return d
        d -= 8
    raise ValueError(f"no row tile for {rows}")


def _adamw(parts, w, m, v, *, name):
    S, Rr, C = parts.shape
    tr = _row_tile(Rr)
    c1 = 1.0 - ADAM_B1 ** ADAM_STEP
    c2 = 1.0 - ADAM_B2 ** ADAM_STEP

    def body(p_ref, w_ref, m_ref, v_ref, g_ref, d_ref, nm_ref, nv_ref):
        g = p_ref[0].astype(F32)
        for d in range(1, S):
            g = g + p_ref[d].astype(F32)
        mm = ADAM_B1 * m_ref[...] + (1.0 - ADAM_B1) * g
        vv = ADAM_B2 * v_ref[...] + (1.0 - ADAM_B2) * (g * g)
        m_hat = mm / c1
        v_hat = vv / c2
        g_ref[...] = g
        d_ref[...] = -ADAM_LR * (m_hat / (jnp.sqrt(v_hat) + ADAM_EPS) + ADAM_WD * w_ref[...])
        nm_ref[...] = mm
        nv_ref[...] = vv

    spec = pl.BlockSpec((tr, C), lambda i: (i, 0))
    shape = jax.ShapeDtypeStruct((Rr, C), F32)
    return pl.pallas_call(
        body, name=name, grid=(Rr // tr,),
        in_specs=[pl.BlockSpec((S, tr, C), lambda i: (0, i, 0)), spec, spec, spec],
        out_specs=[spec] * 4, out_shape=[shape] * 4,
        compiler_params=_params(("parallel",)),
    )(parts, w, m, v)


def _pack(arrs, dtype, row_mult):
    flat = jnp.concatenate([a.reshape(-1).astype(dtype) for a in arrs])
    per = row_mult * PACK_COLS
    total = -(-flat.shape[0] // per) * per
    return jnp.pad(flat, (0, total - flat.shape[0])).reshape(total // PACK_COLS, PACK_COLS)


def _unpack(packed, shapes, lead=()):
    flat = packed.reshape(lead + (-1,))
    out, off = [], 0
    for shp in shapes:
        n = 1
        for d in shp:
            n *= d
        out.append(flat[..., off:off + n].reshape(lead + tuple(shp)))
        off += n
    return out


HGRN_W = ("hgrn_w_in", "hgrn_w_out")
MLA_W = ("mla_w_in", "mla_w_q_up", "mla_w_kv_up", "mla_w_out")
FFN_W = ("ffn_w_up", "ffn_w_down")
BIG = HGRN_W + MLA_W + FFN_W
SMALL_SHARDED = {"ffn_conv_w": 2, "mla_q_a_norm": 1, "mla_kv_a_norm": 1}
REPLICATED = ["norm_mix", "norm_ffn", "hgrn_lower_bounds", "hgrn_out_norm", "mla_q_norm", "mla_k_norm",
              "ffn_conv_b"]
WEIGHTS = ["norm_mix", "norm_ffn", "hgrn_w_in", "hgrn_lower_bounds", "hgrn_out_norm", "hgrn_w_out", "mla_w_in",
           "mla_q_a_norm", "mla_w_q_up", "mla_kv_a_norm", "mla_w_kv_up", "mla_q_norm", "mla_k_norm", "mla_w_out",
           "ffn_w_up", "ffn_conv_w", "ffn_conv_b", "ffn_w_down"]


def _shards_to_cols(g):
    return g.transpose(1, 0, 2).reshape(g.shape[1], N_DEV * g.shape[2])


def _cols_to_shards(w):
    R = w.shape[0]
    return w.reshape(R, N_DEV, w.shape[1] // N_DEV).transpose(1, 0, 2)


def kernel(x, positions, norm_mix, norm_ffn, hgrn_w_in, hgrn_lower_bounds, hgrn_out_norm, hgrn_w_out, mla_w_in, mla_q_a_norm, mla_w_q_up, mla_kv_a_norm, mla_w_kv_up, mla_q_norm, mla_k_norm, mla_w_out, ffn_w_up, ffn_conv_w, ffn_conv_b, ffn_w_down, loss_target, m_norm_mix, m_norm_ffn, m_hgrn_w_in, m_hgrn_lower_bounds, m_hgrn_out_norm, m_hgrn_w_out, m_mla_w_in, m_mla_q_a_norm, m_mla_w_q_up, m_mla_kv_a_norm, m_mla_w_kv_up, m_mla_q_norm, m_mla_k_norm, m_mla_w_out, m_ffn_w_up, m_ffn_conv_w, m_ffn_conv_b, m_ffn_w_down, v_norm_mix, v_norm_ffn, v_hgrn_w_in, v_hgrn_lower_bounds, v_hgrn_out_norm, v_hgrn_w_out, v_mla_w_in, v_mla_q_a_norm, v_mla_w_q_up, v_mla_kv_a_norm, v_mla_w_kv_up, v_mla_q_norm, v_mla_k_norm, v_mla_w_out, v_ffn_w_up, v_ffn_conv_w, v_ffn_conv_b, v_ffn_w_down):
    local = dict(norm_mix=norm_mix, norm_ffn=norm_ffn, hgrn_w_in=hgrn_w_in, hgrn_lower_bounds=hgrn_lower_bounds,
                 hgrn_out_norm=hgrn_out_norm, hgrn_w_out=hgrn_w_out, mla_w_in=mla_w_in, mla_q_a_norm=mla_q_a_norm,
                 mla_w_q_up=mla_w_q_up, mla_kv_a_norm=mla_kv_a_norm, mla_w_kv_up=mla_w_kv_up, mla_q_norm=mla_q_norm,
                 mla_k_norm=mla_k_norm, mla_w_out=mla_w_out, ffn_w_up=ffn_w_up, ffn_conv_w=ffn_conv_w,
                 ffn_conv_b=ffn_conv_b, ffn_w_down=ffn_w_down)
    mom_m = dict(norm_mix=m_norm_mix, norm_ffn=m_norm_ffn, hgrn_w_in=m_hgrn_w_in,
                 hgrn_lower_bounds=m_hgrn_lower_bounds, hgrn_out_norm=m_hgrn_out_norm, hgrn_w_out=m_hgrn_w_out,
                 mla_w_in=m_mla_w_in, mla_q_a_norm=m_mla_q_a_norm, mla_w_q_up=m_mla_w_q_up,
                 mla_kv_a_norm=m_mla_kv_a_norm, mla_w_kv_up=m_mla_w_kv_up, mla_q_norm=m_mla_q_norm,
                 mla_k_norm=m_mla_k_norm, mla_w_out=m_mla_w_out, ffn_w_up=m_ffn_w_up, ffn_conv_w=m_ffn_conv_w,
                 ffn_conv_b=m_ffn_conv_b, ffn_w_down=m_ffn_w_down)
    mom_v = dict(norm_mix=v_norm_mix, norm_ffn=v_norm_ffn, hgrn_w_in=v_hgrn_w_in,
                 hgrn_lower_bounds=v_hgrn_lower_bounds, hgrn_out_norm=v_hgrn_out_norm, hgrn_w_out=v_hgrn_w_out,
                 mla_w_in=v_mla_w_in, mla_q_a_norm=v_mla_q_a_norm, mla_w_q_up=v_mla_w_q_up,
                 mla_kv_a_norm=v_mla_kv_a_norm, mla_w_kv_up=v_mla_w_kv_up, mla_q_norm=v_mla_q_norm,
                 mla_k_norm=v_mla_k_norm, mla_w_out=v_mla_w_out, ffn_w_up=v_ffn_w_up, ffn_conv_w=v_ffn_conv_w,
                 ffn_conv_b=v_ffn_conv_b, ffn_w_down=v_ffn_w_down)
    me = 4 * lax.axis_index("x") + 2 * lax.axis_index("y") + lax.axis_index("c")
    x, positions, target = x[0], positions[0], loss_target[0]
    T, D = x.shape
    depth = norm_mix.shape[0]
    R = mla_w_q_up.shape[1]
    cw = 2 * R + LANES
    small_names = list(SMALL_SHARDED)

    def block_of(kind, l):
        names = {"hgrn": HGRN_W, "mla": MLA_W, "ffn": FFN_W}[kind]
        idx = l if kind == "ffn" else l // 2
        return [(n, idx) for n in names]

    def mixer_kind(l):
        return "hgrn" if l % 2 == 0 else "mla"

    def riders(l):
        keys = block_of("ffn", l)
        if l + 1 < depth:
            keys += block_of(mixer_kind(l + 1), l + 1)
        return keys

    gathered = {}

    def gather_ride(l):
        keys = riders(l)
        return keys, ([local[n][i].astype(BF16) for n, i in keys], ["gather"] * len(keys))

    def take_gathered(keys, arrs):
        for key, a in zip(keys, arrs):
            gathered[key] = a

    keys0 = block_of("hgrn", 0)
    small_local = _pack([local[n] for n in small_names], F32, 8)
    got = _exchange([local[n][i].astype(BF16) for n, i in keys0] + [small_local],
                    ["gather"] * (len(keys0) + 1), name="gather_first")
    take_gathered(keys0, got[:-1])
    small_all = _unpack(got[-1], [local[n].shape for n in small_names], lead=(N_DEV,))
    conv_w_all = small_all[0].transpose(1, 0, 2, 3)
    qa_all = small_all[1].transpose(1, 0, 2).reshape(-1, R)
    kva_all = small_all[2].transpose(1, 0, 2).reshape(-1, R)
    fs = conv_w_all.shape[-1]
    conv_b_s = ffn_conv_b.reshape(depth, N_DEV, 1, fs)

    ct, s1, s2 = _rope_tables(positions)
    lb_soft = jax.nn.softmax(hgrn_lower_bounds.astype(F32), axis=0)
    lower_bounds = jnp.cumsum(lb_soft, axis=0) - lb_soft[0:1]

    def mla_views(j):
        w_in = _pad_cols(gathered["mla_w_in", j].reshape(D, -1), cw)
        wq = _q_up_padded(_shards_to_cols(gathered["mla_w_q_up", j]))
        wkv = _shards_to_cols(gathered["mla_w_kv_up", j])
        gq = _pad_cols(mla_q_norm[j].reshape(1, QK_HEAD), QK_PAD)
        gk = _pad_cols(mla_k_norm[j].reshape(1, QK_HEAD), QK_PAD)
        return w_in, wq, wkv, gq, gk

    saved = []
    h = _rmsnorm_fwd(x, norm_mix[0], name="norm_mix_fwd_0")
    for layer in range(depth):
        j = layer // 2
        s = {"x_in": x}
        s["h_mix"] = h
        keys, ride = gather_ride(layer)
        if layer % 2 == 0:
            p = _mm(h, gathered["hgrn_w_in", j], b_fmt="knb", bm=1024, name=f"hgrn_in_{layer}")
            (og, o, s0), got = _hgrn_fwd(p, lower_bounds[j], hgrn_out_norm[j], name=f"hgrn_fwd_{layer}", ride=ride)
            s.update(p=p, og=og, o=o, s0=s0)
            take_gathered(keys, got)
            x, h = _mm(og, gathered["hgrn_w_out", j].reshape(D, D), res=x, norm_out=norm_ffn[layer],
                       name=f"hgrn_out_{layer}")
        else:
            w_in, wq, wkv, gq, gk = mla_views(j)
            c = _mm(h, w_in, bm=1024, name=f"mla_in_{layer}")
            q, k, v = _mla_prep_fwd(c, wq, wkv, qa_all[j], kva_all[j], gq, gk, ct, s1, s2,
                                    name=f"mla_prep_fwd_{layer}")
            (ot, lse), got = _attn_fwd(q, k, v, name=f"attn_fwd_{layer}", ride=ride)
            s.update(c=c, q=q, k=k, v=v, ot=ot, lse=lse)
            take_gathered(keys, got)
            x, h = _mm(ot, gathered["mla_w_out", j].reshape(D, D), a_fmt="kmb", res=x, norm_out=norm_ffn[layer],
                       name=f"mla_out_{layer}")
        s["x_mid"] = x
        s["h_ffn"] = h
        a, u = _ffn_up_fwd(h, gathered["ffn_w_up", layer], conv_w_all[layer], conv_b_s[layer], name=f"ffn_up_{layer}")
        s.update(a=a, u=u)
        w_down = gathered["ffn_w_down", layer].reshape(-1, D)
        if layer + 1 < depth:
            x, h = _mm(a, w_down, a_fmt="mkb", res=x, norm_out=norm_mix[layer + 1], bk=fs, name=f"ffn_down_{layer}")
        else:
            x = _mm(a, w_down, a_fmt="mkb", res=x, bk=fs, name=f"ffn_down_{layer}")
        saved.append(s)

    dx, loss_part = _loss_head(x, target, name="loss_head")

    parts = {}
    received = {}
    g_small = {n: [None] * local[n].shape[0] for n in REPLICATED + small_names}

    def scatter_ride(l):
        keys = riders(l)
        return keys, ([parts[key] for key in keys], ["scatter"] * len(keys))

    def take_received(keys, arrs):
        for key, a in zip(keys, arrs):
            received[key] = a

    for layer in reversed(range(depth)):
        j = layer // 2
        s = saved[layer]
        parts["ffn_w_down", layer] = _mm(s["a"], dx, a_fmt="kmb", out_dtype=BF16, name=f"ffn_down_dw_{layer}"
                                         ).reshape(N_DEV, -1, D)
        du, dcw, dcb = _ffn_act_bwd(dx, gathered["ffn_w_down", layer].reshape(-1, D), s["u"], conv_w_all[layer],
                                    conv_b_s[layer], name=f"ffn_act_bwd_{layer}")
        g_small["ffn_conv_w"][layer] = dcw.reshape(N_DEV, 3, fs)
        g_small["ffn_conv_b"][layer] = dcb.reshape(N_DEV * fs)
        du8 = du.reshape(N_DEV, T, fs)
        parts["ffn_w_up", layer] = _mm(s["h_ffn"], du8, a_fmt="km", b_fmt="knb", out_fmt="mnb", out_dtype=BF16,
                                       bm=1024, name=f"ffn_up_dw_{layer}")
        dx, dgain = _mm(du8, gathered["ffn_w_up", layer], a_fmt="mkb", b_fmt="nkb", bm=1024, bk=fs,
                        norm_bwd=(s["x_mid"], norm_ffn[layer], dx), name=f"ffn_up_dh_{layer}")
        g_small["norm_ffn"][layer] = dgain.reshape(D)
        keys, ride = scatter_ride(layer)
        if layer % 2 == 0:
            w_out = gathered["hgrn_w_out", j].reshape(D, D)
            parts["hgrn_w_out", j] = _mm(s["og"], dx, a_fmt="km", out_dtype=BF16, bm=1024,
                                         name=f"hgrn_out_dw_{layer}").reshape(N_DEV, -1, D)
            dog = _mm(dx, w_out, b_fmt="nk", name=f"hgrn_out_dx_{layer}")
            (dp, dlb, dgn), got = _hgrn_bwd(s["p"], lower_bounds[j], hgrn_out_norm[j], s["s0"], s["o"], dog,
                                            name=f"hgrn_bwd_{layer}", ride=ride)
            take_received(keys, got)
            g_small["hgrn_lower_bounds"][j] = dlb.reshape(D)
            g_small["hgrn_out_norm"][j] = dgn.reshape(HEAD)
            w_in_s = gathered["hgrn_w_in", j]
            parts["hgrn_w_in", j] = _mm(s["h_mix"], dp, a_fmt="km", out_fmt="mnb", out_dtype=BF16, bm=1024,
                                        bn=w_in_s.shape[2], name=f"hgrn_in_dw_{layer}")
            dx, dgain = _mm(dp, w_in_s, b_fmt="nkb", bm=1024, norm_bwd=(s["x_in"], norm_mix[layer], dx),
                            name=f"hgrn_in_dx_{layer}")
        else:
            w_in, wq, wkv, gq, gk = mla_views(j)
            w_out = gathered["mla_w_out", j].reshape(D, D)
            tb = s["ot"].shape[2]
            parts["mla_w_out", j] = _mm(s["ot"], dx, a_fmt="mkb", bk=tb, out_dtype=BF16, bm=1024,
                                        name=f"mla_out_dw_{layer}").reshape(N_DEV, -1, D)
            dot_ = _mm(w_out, dx, b_fmt="nk", out_fmt="mnb", out_dtype=BF16, bm=D, bn=tb,
                       name=f"mla_out_dx_{layer}")
            (dq, dk, dv), got = _attn_bwd(s["q"], s["k"], s["v"], s["ot"], s["lse"], dot_,
                                          name=f"attn_bwd_{layer}", ride=ride)
            take_received(keys, got)
            dc, dwq, dwkv, dgaq, dgakv, dgq, dgk = _mla_prep_bwd(
                s["c"], wq, wkv, qa_all[j], kva_all[j], gq, gk, ct, s1, s2, dq, dk, dv,
                name=f"mla_prep_bwd_{layer}")
            parts["mla_w_q_up", j] = _cols_to_shards(_q_up_unpadded(dwq)).astype(BF16)
            parts["mla_w_kv_up", j] = _cols_to_shards(dwkv).astype(BF16)
            g_small["mla_q_a_norm"][j] = dgaq.reshape(R)
            g_small["mla_kv_a_norm"][j] = dgakv.reshape(R)
            g_small["mla_q_norm"][j] = dgq[0, :QK_HEAD]
            g_small["mla_k_norm"][j] = dgk[0, :QK_HEAD]
            win_cols = mla_w_in.shape[2]
            dw_in = _mm(s["h_mix"], dc, a_fmt="km", bm=1024, name=f"mla_in_dw_{layer}")
            parts["mla_w_in", j] = dw_in[:, :win_cols].astype(BF16).reshape(N_DEV, -1, win_cols)
            dx, dgain = _mm(dc, w_in, b_fmt="nk", norm_bwd=(s["x_in"], norm_mix[layer], dx),
                            name=f"mla_in_dx_{layer}")
        g_small["norm_mix"][layer] = dgain.reshape(D)
    grad_x = dx

    dlb_eff = jnp.stack(g_small["hgrn_lower_bounds"])
    dsoft = jnp.cumsum(dlb_eff[::-1], axis=0)[::-1]
    dsoft = dsoft.at[0].add(-jnp.sum(dlb_eff, axis=0))
    g_lb = lb_soft * (dsoft - jnp.sum(dsoft * lb_soft, axis=0, keepdims=True))
    small_grads = {n: (g_lb if n == "hgrn_lower_bounds" else jnp.stack(g_small[n])) for n in g_small}

    small_grad_names = REPLICATED + small_names
    small_part = _pack([small_grads[n] for n in small_grad_names] + [loss_part], F32, 8)
    got = _exchange([parts[key] for key in keys0] + [small_part], ["scatter"] * len(keys0) + ["gather"],
                    name="exchange_last")
    take_received(keys0, got[:-1])
    small_recv = got[-1]

    out = {}
    for n in BIG:
        layers = local[n].shape[0]
        shard = local[n].shape[1:]
        p2 = jnp.concatenate([received[n, i].reshape(N_DEV, -1, shard[-1]) for i in range(layers)], axis=1)
        flat = lambda a: a.reshape(-1, shard[-1])
        res = _adamw(p2, flat(local[n]), flat(mom_m[n]), flat(mom_v[n]), name=f"adamw_{n}")
        for kind, a in zip(("grad", "delta", "new_m", "new_v"), res):
            out[kind, n] = a.reshape(local[n].shape)
    small_sum = _sum_slots(small_recv, name="sum_small")
    small_full = _unpack(small_sum, [small_grads[n].shape for n in small_grad_names] + [(1, LANES)])
    loss = small_full[-1][0, 0]
    g_mine = {}
    for n, a in zip(small_grad_names, small_full[:-1]):
        if n == "ffn_conv_w":
            a = lax.dynamic_index_in_dim(a, me, axis=1, keepdims=False)
        elif n in SMALL_SHARDED:
            size = local[n].shape[1]
            a = lax.dynamic_slice_in_dim(a, me * size, size, axis=1)
        g_mine[n] = a
    small_shapes = [local[n].shape for n in small_grad_names]
    res = _adamw(_pack([g_mine[n] for n in small_grad_names], F32, 8)[None],
                 _pack([local[n] for n in small_grad_names], F32, 8),
                 _pack([mom_m[n] for n in small_grad_names], F32, 8),
                 _pack([mom_v[n] for n in small_grad_names], F32, 8), name="adamw_small")
    for kind, packed in zip(("grad", "delta", "new_m", "new_v"), res):
        for n, a in zip(small_grad_names, _unpack(packed, small_shapes)):
            out[kind, n] = a

    outs = [loss, grad_x[None]]
    for kind in ("grad", "delta", "new_m", "new_v"):
        outs += [out[kind, n] for n in WEIGHTS]
    return tuple(outs)
```

```python
import functools

import jax
import jax.numpy as jnp
from jax import lax
from jax.experimental import pallas as pl
from jax.experimental.pallas import tpu as pltpu

F32 = jnp.float32
BF16 = jnp.bfloat16

RMS_EPS = 1e-6
ROPE_THETA = 10000.0
HEAD = 128
ROPE = 64
QK_HEAD = HEAD + ROPE
QK_PAD = 256
CHUNK = 64
SUB = 16
EXP_CLAMP = 60.0
HGRN_CHUNKS_PER_STEP = 4
HGRN_BWD_CHUNKS_PER_STEP = 2

ADAM_LR = 0.001
ADAM_B1 = 0.9
ADAM_B2 = 0.999
ADAM_EPS = 1e-08
ADAM_WD = 0.01
ADAM_STEP = 10

N_DEV = 8
LANES = 128
PACK_COLS = 1024
V7X_VMEM_LIMIT = 56 * 1024 * 1024

HI = lax.Precision.HIGHEST


def _params(sem):
    return pltpu.CompilerParams(dimension_semantics=sem, vmem_limit_bytes=V7X_VMEM_LIMIT)


def _blk(n, cap):
    if n <= cap:
        return n
    d = (cap // LANES) * LANES
    while d >= LANES:
        if n % d == 0:
            return d
        d -= LANES
    raise ValueError(f"no lane-aligned block for {n} under {cap}")


def _sigmoid(x):
    return jax.nn.sigmoid(x)


def _dot(a, b, dims, precision=None):
    return lax.dot_general(a, b, (dims, ((), ())), preferred_element_type=F32, precision=precision)


NN = ((1,), (0,))
NT = ((1,), (1,))
TN = ((0,), (0,))


def _mm(a, b, *, a_fmt="mk", b_fmt="kn", out_fmt="mn", res=None, norm_out=None, norm_bwd=None, out_dtype=F32,
        bm=512, bn=1024, bk=1024, name):
    if a_fmt == "mk":
        M, K = a.shape
    elif a_fmt == "km":
        K, M = a.shape
    elif a_fmt == "kmb":
        nb, K, B = a.shape
        M = nb * B
    else:
        nb, M, B = a.shape
        K = nb * B
    if b_fmt == "kn":
        Kb, N = b.shape
    elif b_fmt == "nk":
        N, Kb = b.shape
    elif b_fmt == "knb":
        nbb, Kb, Bb = b.shape
        N = nbb * Bb
    else:
        nbb, N, Bb = b.shape
        Kb = nbb * Bb
    assert K == Kb, (a.shape, b.shape, a_fmt, b_fmt)
    bm = B if a_fmt == "kmb" else _blk(M, bm)
    bn = Bb if b_fmt == "knb" else _blk(N, bn)
    if a_fmt == "mkb" and b_fmt == "nkb":
        assert B == Bb
    bk = _blk(B, bk) if a_fmt == "mkb" else (_blk(Bb, bk) if b_fmt == "nkb" else _blk(K, bk))
    nm, nn, nk = M // bm, N // bn, K // bk

    if a_fmt == "mk":
        a_spec = pl.BlockSpec((bm, bk), lambda i, j, k: (i, k))
        a_dim = 1
    elif a_fmt == "km":
        a_spec = pl.BlockSpec((bk, bm), lambda i, j, k: (k, i))
        a_dim = 0
    elif a_fmt == "kmb":
        a_spec = pl.BlockSpec((None, bk, bm), lambda i, j, k: (i, k, 0))
        a_dim = 0
    else:
        per = B // bk
        a_spec = pl.BlockSpec((None, bm, bk), lambda i, j, k: (k // per, i, k % per))
        a_dim = 1
    if b_fmt == "kn":
        b_spec = pl.BlockSpec((bk, bn), lambda i, j, k: (k, j))
        b_dim = 0
    elif b_fmt == "nk":
        b_spec = pl.BlockSpec((bn, bk), lambda i, j, k: (j, k))
        b_dim = 1
    elif b_fmt == "knb":
        b_spec = pl.BlockSpec((None, bk, bn), lambda i, j, k: (j, k, 0))
        b_dim = 0
    else:
        perb = Bb // bk
        b_spec = pl.BlockSpec((None, bn, bk), lambda i, j, k: (k // perb, j, k % perb))
        b_dim = 1
    if out_fmt == "mn":
        o_spec = pl.BlockSpec((bm, bn), lambda i, j, k: (i, j))
        o_shape = jax.ShapeDtypeStruct((M, N), out_dtype)
    else:
        o_spec = pl.BlockSpec((None, bm, bn), lambda i, j, k: (j, i, 0))
        o_shape = jax.ShapeDtypeStruct((nn, M, bn), out_dtype)
    in_specs = [a_spec, b_spec]
    args = [a, b]
    row_tile = pl.BlockSpec((bm, bn), lambda i, j, k: (i, j))
    row_vec = pl.BlockSpec((1, bn), lambda i, j, k: (0, j))
    if res is not None:
        assert out_fmt == "mn"
        in_specs.append(row_tile)
        args.append(res)
    out_specs, out_shapes = [o_spec], [o_shape]
    if norm_out is not None:
        assert nn == 1 and out_fmt == "mn"
        in_specs.append(row_vec)
        args.append(norm_out.reshape(1, N))
        out_specs.append(row_tile)
        out_shapes.append(jax.ShapeDtypeStruct((M, N), BF16))
    if norm_bwd is not None:
        assert nn == 1 and out_fmt == "mn" and res is None and norm_out is None
        xin, gain, dres = norm_bwd
        in_specs += [row_tile, row_vec, row_tile]
        args += [xin, gain.reshape(1, N), dres]
        out_specs.append(row_vec)
        out_shapes.append(jax.ShapeDtypeStruct((1, N), F32))
    dims = ((a_dim,), (b_dim,))
    has_res = res is not None
    n_in = len(in_specs)

    def body(*refs):
        a_ref, b_ref = refs[0], refs[1]
        extra_in = list(refs[2:n_in])
        o_ref = refs[n_in]
        part = _dot(a_ref[...].astype(BF16), b_ref[...].astype(BF16), dims)

        def finish(out):
            if has_res:
                out = out + extra_in[0][...]
            if norm_out is not None:
                g_ref, h_ref = extra_in[-1], refs[n_in + 1]
                r = lax.rsqrt(jnp.mean(out * out, axis=-1, keepdims=True) + RMS_EPS)
                h_ref[...] = ((out * r) * g_ref[...]).astype(h_ref.dtype)
            if norm_bwd is not None:
                x_ref, g_ref, dr_ref = extra_in
                dg_ref = refs[n_in + 1]
                xv = x_ref[...]
                r = lax.rsqrt(jnp.mean(xv * xv, axis=-1, keepdims=True) + RMS_EPS)
                n = xv * r
                dn = out * g_ref[...]
                gpart = jnp.sum(out * n, axis=0, keepdims=True)
                i = pl.program_id(0)

                @pl.when(i == 0)
                def _():
                    dg_ref[...] = gpart

                @pl.when(i > 0)
                def _():
                    dg_ref[...] += gpart

                out = dr_ref[...] + r * (dn - n * jnp.mean(dn * n, axis=-1, keepdims=True))
            o_ref[...] = out.astype(o_ref.dtype)

        if nk == 1:
            finish(part)
            return
        acc_ref = refs[-1]
        k = pl.program_id(2)

        @pl.when(k == 0)
        def _():
            acc_ref[...] = part

        @pl.when(jnp.logical_and(k > 0, k < nk - 1))
        def _():
            acc_ref[...] += part

        @pl.when(k == nk - 1)
        def _():
            finish(acc_ref[...] + part)

    multi = len(out_specs) > 1
    return pl.pallas_call(
        body, name=name, grid=(nm, nn, nk), in_specs=in_specs,
        out_specs=out_specs if multi else o_spec, out_shape=out_shapes if multi else o_shape,
        scratch_shapes=[] if nk == 1 else [pltpu.VMEM((bm, bn), F32)],
        compiler_params=_params(("arbitrary",) * 3 if norm_bwd is not None else ("parallel", "parallel", "arbitrary")),
    )(*args)


def _rmsnorm_fwd(x, gain, *, name, tm=512):
    T, D = x.shape
    tm = min(tm, T)

    def body(x_ref, g_ref, o_ref):
        xv = x_ref[...]
        r = lax.rsqrt(jnp.mean(xv * xv, axis=-1, keepdims=True) + RMS_EPS)
        o_ref[...] = ((xv * r) * g_ref[...]).astype(o_ref.dtype)

    return pl.pallas_call(
        body, name=name, grid=(T // tm,),
        in_specs=[pl.BlockSpec((tm, D), lambda i: (i, 0)), pl.BlockSpec((1, D), lambda i: (0, 0))],
        out_specs=pl.BlockSpec((tm, D), lambda i: (i, 0)),
        out_shape=jax.ShapeDtypeStruct((T, D), BF16),
        compiler_params=_params(("parallel",)),
    )(x, gain.reshape(1, D))


def _loss_head(y, target, *, name, tm=512):
    T, D = y.shape
    tm = min(tm, T)

    def body(y_ref, t_ref, dy_ref, l_ref):
        i = pl.program_id(0)
        e = y_ref[...] - t_ref[...]
        dy_ref[...] = e * (1.0 / D)
        s = 0.5 * jnp.sum(jnp.mean(e * e, axis=-1, keepdims=True), axis=0, keepdims=True)
        part = jnp.broadcast_to(s, (1, LANES))

        @pl.when(i == 0)
        def _():
            l_ref[...] = part

        @pl.when(i > 0)
        def _():
            l_ref[...] += part

    return pl.pallas_call(
        body, name=name, grid=(T // tm,),
        in_specs=[pl.BlockSpec((tm, D), lambda i: (i, 0)), pl.BlockSpec((tm, D), lambda i: (i, 0))],
        out_specs=[pl.BlockSpec((tm, D), lambda i: (i, 0)), pl.BlockSpec((1, LANES), lambda i: (0, 0))],
        out_shape=[jax.ShapeDtypeStruct((T, D), F32), jax.ShapeDtypeStruct((1, LANES), F32)],
        compiler_params=_params(("arbitrary",)),
    )(y, target)


def _hgrn_selectors():
    t = jnp.arange(CHUNK)[:, None]
    s = jnp.arange(CHUNK)[None, :]
    mats = [s <= t, s < (t // SUB) * SUB]
    for i in range(1, CHUNK // SUB):
        mats.append(jnp.broadcast_to(s < i * SUB, (8, CHUNK)))
    mats.append(jnp.ones((8, CHUNK), bool))
    sel = jnp.concatenate([m.astype(BF16) for m in mats], axis=0)
    rev = (s >= t).astype(BF16)
    return sel, rev


def _select_sums(sel, x):
    hi = x.astype(BF16)
    r1 = x - hi.astype(F32)
    mid = r1.astype(BF16)
    lo = (r1 - mid.astype(F32)).astype(BF16)
    return _dot(sel, hi, NN) + (_dot(sel, mid, NN) + _dot(sel, lo, NN))


def _hgrn_cums(sel, logf):
    nsub = CHUNK // SUB
    cums = _select_sums(sel, logf)
    g = cums[0:CHUNK]
    rrow = cums[CHUNK:2 * CHUNK]
    base = 2 * CHUNK
    rsel = [None] + [jnp.max(cums[base + 8 * (i - 1):base + 8 * i], axis=0, keepdims=True) for i in range(1, nsub)]
    gl = jnp.max(cums[base + 8 * (nsub - 1):base + 8 * nsub], axis=0, keepdims=True)
    return g, rrow, rsel, gl


def _hgrn_gates(p, lb, D):
    qpre, fpre, iv, gpre = p[:, 0:D], p[:, D:2 * D], p[:, 2 * D:3 * D], p[:, 3 * D:4 * D]
    sig = _sigmoid(fpre)
    forget = lb + (1.0 - lb) * sig
    key = 1.0 - forget
    logf = jnp.log(forget)
    sq = _sigmoid(qpre)
    qs = qpre * sq
    return qpre, sq, qs, sig, forget, key, logf, iv, gpre


def _hgrn_scores(qt, kh, gh, rsel, row, col):
    nsub = CHUNK // SUB
    blocks, kts = [], []
    for i in range(nsub):
        ri = rsel[i]
        kt = kh * jnp.exp(jnp.minimum(ri - gh, EXP_CLAMP)) if ri is not None else \
            kh * jnp.exp(jnp.minimum(-gh, EXP_CLAMP))
        kts.append(kt)
        blocks.append(_dot(qt[i * SUB:(i + 1) * SUB].astype(BF16), kt.astype(BF16), NT))
    a = jnp.concatenate(blocks, axis=0)
    return jnp.where(col <= row, a, 0.0), kts


def _hgrn_fwd(p, lb, gn, *, name, ride=None):
    T, D4 = p.shape
    D = D4 // 4
    H = D // HEAD
    nc = T // CHUNK
    nb = min(HGRN_CHUNKS_PER_STEP, nc)
    assert nc % nb == 0
    sel, _ = _hgrn_selectors()
    nsel = sel.shape[0]
    nsub = CHUNK // SUB
    heads = [slice(h * HEAD, (h + 1) * HEAD) for h in range(H)]

    def body(p_ref, lb_ref, gn_ref, sel_ref, og_ref, o_ref, s0_ref, st_ref):
        @pl.when(pl.program_id(0) == 0)
        def _():
            st_ref[...] = jnp.zeros_like(st_ref)

        row = lax.broadcasted_iota(jnp.int32, (CHUNK, CHUNK), 0)
        col = lax.broadcasted_iota(jnp.int32, (CHUNK, CHUNK), 1)
        gnv = gn_ref[...]
        lbv = lb_ref[...]
        selv = sel_ref[...]
        pre = []
        for cc in range(nb):
            pv = p_ref[cc * CHUNK:(cc + 1) * CHUNK, :]
            _, _, qs, _, _, key, logf, iv, gpre = _hgrn_gates(pv, lbv, D)
            g, rrow, rsel_all, gl = _hgrn_cums(selv, logf)
            qgb = (qs * jnp.exp(g)).astype(BF16)
            qtb = (qs * jnp.exp(g - rrow)).astype(BF16)
            ktb = [(key * jnp.exp(jnp.minimum((0.0 if r is None else r) - g, EXP_CLAMP))).astype(BF16)
                   for r in rsel_all]
            kdb = (key * jnp.exp(gl - g)).astype(BF16)
            vb = iv.astype(BF16)
            blocks = [[_dot(qtb[i * SUB:(i + 1) * SUB, sl], ktb[i][:, sl], NT) for i in range(nsub)]
                      for sl in heads]
            amats = [jnp.where(col <= row, jnp.concatenate(bl, axis=0), 0.0).astype(BF16) for bl in blocks]
            pre.append(dict(qgb=qgb, egl=jnp.exp(gl), gate=gpre * _sigmoid(gpre),
                            intra=[_dot(a, vb[:, sl], NN) for a, sl in zip(amats, heads)],
                            upd=[_dot(vb[:, sl], kdb[:, sl], TN) for sl in heads]))
        sts = [st_ref[sl, :] for sl in heads]
        for cc, d in enumerate(pre):
            rows = slice(cc * CHUNK, (cc + 1) * CHUNK)
            inter = [_dot(d["qgb"][:, sl], st.astype(BF16), NT) for sl, st in zip(heads, sts)]
            for h, sl in enumerate(heads):
                s0_ref[cc, sl, :] = sts[h]
                o = inter[h] + d["intra"][h]
                o_ref[rows, sl] = o
                r = lax.rsqrt(jnp.mean(o * o, axis=-1, keepdims=True) + RMS_EPS)
                og_ref[rows, sl] = (((o * r) * gnv) * d["gate"][:, sl]).astype(og_ref.dtype)
            sts = [st * d["egl"][:, sl] + u for st, sl, u in zip(sts, heads, d["upd"])]
        for sl, st in zip(heads, sts):
            st_ref[sl, :] = st

    rb = nb * CHUNK
    return _call(
        body, name=name, grid=(nc // nb,),
        in_specs=[pl.BlockSpec((rb, D4), lambda c: (c, 0)), pl.BlockSpec((1, D), lambda c: (0, 0)),
                  pl.BlockSpec((1, HEAD), lambda c: (0, 0)), pl.BlockSpec((nsel, CHUNK), lambda c: (0, 0))],
        out_specs=[pl.BlockSpec((rb, D), lambda c: (c, 0)), pl.BlockSpec((rb, D), lambda c: (c, 0)),
                   pl.BlockSpec((nb, D, HEAD), lambda c: (c, 0, 0))],
        out_shape=[jax.ShapeDtypeStruct((T, D), BF16), jax.ShapeDtypeStruct((T, D), F32),
                   jax.ShapeDtypeStruct((nc, D, HEAD), F32)],
        scratch_shapes=[pltpu.VMEM((D, HEAD), F32)], sem=("arbitrary",), ride=ride,
        args=(p, lb.reshape(1, D), gn.reshape(1, HEAD), sel))


def _hgrn_bwd(p, lb, gn, s0, o_saved, dog, *, name, ride=None):
    T, D4 = p.shape
    D = D4 // 4
    H = D // HEAD
    nc = T // CHUNK
    nb = min(HGRN_BWD_CHUNKS_PER_STEP, nc)
    assert nc % nb == 0
    nsteps = nc // nb
    sel, rev = _hgrn_selectors()
    nsel = sel.shape[0]
    nsub = CHUNK // SUB
    heads = [slice(h * HEAD, (h + 1) * HEAD) for h in range(H)]
    cat = lambda xs: jnp.concatenate(xs, axis=1)

    def body(p_ref, lb_ref, gn_ref, sel_ref, rev_ref, s0_ref, s1_ref, o_ref, dog_ref,
             dp_ref, dlb_ref, dgn_ref, dst_ref):
        @pl.when(pl.program_id(0) == 0)
        def _():
            dst_ref[...] = jnp.zeros_like(dst_ref)
            dlb_ref[...] = jnp.zeros_like(dlb_ref)
            dgn_ref[...] = jnp.zeros_like(dgn_ref)

        row = lax.broadcasted_iota(jnp.int32, (CHUNK, CHUNK), 0)
        col = lax.broadcasted_iota(jnp.int32, (CHUNK, CHUNK), 1)
        causal = col <= row
        gnv, lbv, selv, revv = gn_ref[...], lb_ref[...], sel_ref[...], rev_ref[...]
        dgn_acc = jnp.zeros((1, HEAD), F32)
        pre = []
        for cc in range(nb):
            rows = slice(cc * CHUNK, (cc + 1) * CHUNK)
            qpre, sq, qs, sig, forget, key, logf, iv, gpre = _hgrn_gates(p_ref[rows, :], lbv, D)
            g, rrow, rsel_all, gl = _hgrn_cums(selv, logf)
            eg = jnp.exp(g)
            eqr = jnp.exp(g - rrow)
            eis = [jnp.exp(jnp.minimum((0.0 if r is None else r) - g, EXP_CLAMP)) for r in rsel_all]
            ekd = jnp.exp(gl - g)
            qgb = (qs * eg).astype(BF16)
            qtb = (qs * eqr).astype(BF16)
            ktb = [(key * e).astype(BF16) for e in eis]
            kdb = (key * ekd).astype(BF16)
            vb = iv.astype(BF16)
            sg = _sigmoid(gpre)
            gate = gpre * sg
            dgate = sg * (1.0 + gpre * (1.0 - sg))
            dobs, dgpres = [], []
            for sl in heads:
                o = o_ref[rows, sl]
                r = lax.rsqrt(jnp.mean(o * o, axis=-1, keepdims=True) + RMS_EPS)
                n = o * r
                dog_h = dog_ref[rows, sl]
                d_on = dog_h * gate[:, sl]
                dgpres.append(dog_h * (n * gnv) * dgate[:, sl])
                dgn_acc = dgn_acc + jnp.sum(d_on * n, axis=0, keepdims=True)
                dn = d_on * gnv
                dobs.append((r * (dn - n * jnp.mean(dn * n, axis=-1, keepdims=True))).astype(BF16))
            blocks = [[_dot(qtb[i * SUB:(i + 1) * SUB, sl], ktb[i][:, sl], NT) for i in range(nsub)]
                      for sl in heads]
            amats = [jnp.where(causal, jnp.concatenate(bl, axis=0), 0.0).astype(BF16) for bl in blocks]
            dabs = [jnp.where(causal, _dot(dob, vb[:, sl], NT), 0.0).astype(BF16) for dob, sl in zip(dobs, heads)]
            dq_inter = [_dot(dob, s0_ref[cc, sl, :].astype(BF16), NN) for dob, sl in zip(dobs, heads)]
            dqt = [jnp.concatenate([_dot(dab[i * SUB:(i + 1) * SUB], ktb[i][:, sl], NN) for i in range(nsub)],
                                   axis=0) for dab, sl in zip(dabs, heads)]
            dkt = [[_dot(dab[i * SUB:(i + 1) * SUB], qtb[i * SUB:(i + 1) * SUB, sl], TN) for i in range(nsub)]
                   for dab, sl in zip(dabs, heads)]
            dv_intra = [_dot(a, dob, TN) for a, dob in zip(amats, dobs)]
            upd = [_dot(dob, qgb[:, sl], TN) for dob, sl in zip(dobs, heads)]
            dq = cat(dq_inter) * eg + cat(dqt) * eqr
            dk_intra = cat([dkt[h][0] for h in range(H)]) * eis[0]
            for i in range(1, nsub):
                dk_intra = dk_intra + cat([dkt[h][i] for h in range(H)]) * eis[i]
            s1 = [s0_ref[cc + 1, sl, :] if cc + 1 < nb else s1_ref[sl, :] for sl in heads]
            pre.append(dict(qpre=qpre, sq=sq, qs=qs, sig=sig, forget=forget, key=key, ekd=ekd, egl=jnp.exp(gl),
                            kdb=kdb, vb=vb, dq=dq, dk_intra=dk_intra, dv_intra=dv_intra, upd=upd, s1=s1,
                            dgpre=cat(dgpres)))
        dsts = [dst_ref[sl, :] for sl in heads]
        dlb_acc = jnp.zeros((1, D), F32)
        for cc in reversed(range(nb)):
            d = pre[cc]
            rows = slice(cc * CHUNK, (cc + 1) * CHUNK)
            dstb = [x.astype(BF16) for x in dsts]
            dk_state = cat([_dot(d["vb"][:, sl], x, NN) for sl, x in zip(heads, dstb)])
            dv = cat([dvi + _dot(d["kdb"][:, sl], x, NT) for dvi, sl, x in zip(d["dv_intra"], heads, dstb)])
            term = cat([jnp.sum(x * s, axis=0, keepdims=True) for x, s in zip(dsts, d["s1"])])
            dsts = [x * d["egl"][:, sl] + u for x, sl, u in zip(dsts, heads, d["upd"])]
            dk = d["dk_intra"] + dk_state * d["ekd"]
            dq = d["dq"]
            dg = d["qs"] * dq - d["key"] * dk
            dlogf = _select_sums(revv, dg) + term
            sgf = d["sig"]
            dforget = dlogf / d["forget"] - dk
            dlb_acc = dlb_acc + jnp.sum(dforget * (1.0 - sgf), axis=0, keepdims=True)
            sqv = d["sq"]
            dp_ref[rows, 0:D] = (dq * (sqv * (1.0 + d["qpre"] * (1.0 - sqv)))).astype(dp_ref.dtype)
            dp_ref[rows, D:2 * D] = (dforget * (1.0 - lbv) * (sgf * (1.0 - sgf))).astype(dp_ref.dtype)
            dp_ref[rows, 2 * D:3 * D] = dv.astype(dp_ref.dtype)
            dp_ref[rows, 3 * D:4 * D] = d["dgpre"].astype(dp_ref.dtype)
        for sl, x in zip(heads, dsts):
            dst_ref[sl, :] = x
        dlb_ref[...] += dlb_acc
        dgn_ref[...] += dgn_acc

    rb = nb * CHUNK
    rc = lambda c: nsteps - 1 - c
    return _call(
        body, name=name, grid=(nsteps,),
        in_specs=[pl.BlockSpec((rb, D4), lambda c: (rc(c), 0)), pl.BlockSpec((1, D), lambda c: (0, 0)),
                  pl.BlockSpec((1, HEAD), lambda c: (0, 0)), pl.BlockSpec((nsel, CHUNK), lambda c: (0, 0)),
                  pl.BlockSpec((CHUNK, CHUNK), lambda c: (0, 0)),
                  pl.BlockSpec((nb, D, HEAD), lambda c: (rc(c), 0, 0)),
                  pl.BlockSpec((None, D, HEAD), lambda c: (jnp.minimum((rc(c) + 1) * nb, nc - 1), 0, 0)),
                  pl.BlockSpec((rb, D), lambda c: (rc(c), 0)), pl.BlockSpec((rb, D), lambda c: (rc(c), 0))],
        out_specs=[pl.BlockSpec((rb, D4), lambda c: (rc(c), 0)), pl.BlockSpec((1, D), lambda c: (0, 0)),
                   pl.BlockSpec((1, HEAD), lambda c: (0, 0))],
        out_shape=[jax.ShapeDtypeStruct((T, D4), BF16), jax.ShapeDtypeStruct((1, D), F32),
                   jax.ShapeDtypeStruct((1, HEAD), F32)],
        scratch_shapes=[pltpu.VMEM((D, HEAD), F32)], sem=("arbitrary",), ride=ride,
        args=(p, lb.reshape(1, D), gn.reshape(1, HEAD), sel, rev, s0, s0, o_saved, dog))


def _rope_tables(positions):
    inv_freq = ROPE_THETA ** (-jnp.arange(0, ROPE, 2, dtype=F32) / ROPE)
    ang = positions.astype(F32)[:, None] * inv_freq
    cos, sin = jnp.cos(ang), jnp.sin(ang)
    z = jnp.zeros_like(cos)
    ctab = jnp.concatenate([cos, cos, z, z], axis=-1)
    s1 = jnp.concatenate([-sin, z, z, z], axis=-1)
    s2 = jnp.concatenate([z, sin, z, z], axis=-1)
    return ctab, s1, s2


def _rope(z, ct, s1, s2):
    return z * ct + pltpu.roll(z, 96, 1) * s1 + pltpu.roll(z, 32, 1) * s2


def _rope_t(d, ct, s1, s2):
    return d * ct + pltpu.roll(d * s1, 32, 1) + pltpu.roll(d * s2, 96, 1)


def _mla_prep_fwd(c, wq, wkv, ga_q, ga_kv, gq, gk, ct, s1, s2, *, name, tm=256):
    T, CW = c.shape
    R = (CW - LANES) // 2
    H = wq.shape[1] // QK_PAD
    tm = min(tm, T)

    def body(c_ref, wq_ref, wkv_ref, gaq_ref, gakv_ref, gq_ref, gk_ref, ct_ref, s1_ref, s2_ref,
             q_ref, k_ref, v_ref):
        cv = c_ref[...]
        cq, ckv, kr = cv[:, 0:R], cv[:, R:2 * R], cv[:, 2 * R:2 * R + LANES]
        rq = lax.rsqrt(jnp.mean(cq * cq, axis=-1, keepdims=True) + RMS_EPS)
        cqn = ((cq * rq) * gaq_ref[...]).astype(BF16)
        rk = lax.rsqrt(jnp.mean(ckv * ckv, axis=-1, keepdims=True) + RMS_EPS)
        ckvn = ((ckv * rk) * gakv_ref[...]).astype(BF16)
        qp = _dot(cqn, wq_ref[...], NN)
        kvp = _dot(ckvn, wkv_ref[...], NN)
        ctv, s1v, s2v = ct_ref[...], s1_ref[...], s2_ref[...]
        gqv, gkv = gq_ref[...], gk_ref[...]
        krs = jnp.sum(kr * kr, axis=-1, keepdims=True)
        for h in range(H):
            b = h * QK_PAD
            qn, qr = qp[:, b:b + HEAD], qp[:, b + HEAD:b + QK_PAD]
            ss = jnp.sum(qn * qn, axis=-1, keepdims=True) + jnp.sum(qr * qr, axis=-1, keepdims=True)
            rr = lax.rsqrt(ss * (1.0 / QK_HEAD) + RMS_EPS)
            q_ref[:, b:b + HEAD] = ((qn * rr) * gqv[:, 0:HEAD]).astype(q_ref.dtype)
            q_ref[:, b + HEAD:b + QK_PAD] = _rope((qr * rr) * gqv[:, HEAD:QK_PAD], ctv, s1v, s2v).astype(q_ref.dtype)
            kn, vv = kvp[:, b:b + HEAD], kvp[:, b + HEAD:b + QK_PAD]
            ssk = jnp.sum(kn * kn, axis=-1, keepdims=True) + krs
            rrk = lax.rsqrt(ssk * (1.0 / QK_HEAD) + RMS_EPS)
            k_ref[:, b:b + HEAD] = ((kn * rrk) * gkv[:, 0:HEAD]).astype(k_ref.dtype)
            k_ref[:, b + HEAD:b + QK_PAD] = _rope((kr * rrk) * gkv[:, HEAD:QK_PAD], ctv, s1v, s2v).astype(k_ref.dtype)
            v_ref[:, h * HEAD:(h + 1) * HEAD] = vv.astype(v_ref.dtype)

    full = lambda shape: pl.BlockSpec(shape, lambda i: (0, 0))
    tok = lambda w: pl.BlockSpec((tm, w), lambda i: (i, 0))
    return pl.pallas_call(
        body, name=name, grid=(T // tm,),
        in_specs=[tok(CW), full(wq.shape), full(wkv.shape), full((1, R)), full((1, R)), full((1, QK_PAD)),
                  full((1, QK_PAD)), tok(LANES), tok(LANES), tok(LANES)],
        out_specs=[tok(H * QK_PAD), tok(H * QK_PAD), tok(H * HEAD)],
        out_shape=[jax.ShapeDtypeStruct((T, H * QK_PAD), BF16), jax.ShapeDtypeStruct((T, H * QK_PAD), BF16),
                   jax.ShapeDtypeStruct((T, H * HEAD), BF16)],
        compiler_params=_params(("parallel",)),
    )(c, wq, wkv, ga_q.reshape(1, R), ga_kv.reshape(1, R), gq, gk, ct, s1, s2)


def _mla_prep_bwd(c, wq, wkv, ga_q, ga_kv, gq, gk, ct, s1, s2, dq, dk, dv, *, name, tm=256):
    T, CW = c.shape
    R = (CW - LANES) // 2
    H = wq.shape[1] // QK_PAD
    tm = min(tm, T)

    def body(c_ref, wq_ref, wkv_ref, gaq_ref, gakv_ref, gq_ref, gk_ref, ct_ref, s1_ref, s2_ref,
             dq_ref, dk_ref, dv_ref,
             dc_ref, dwq_ref, dwkv_ref, dgaq_ref, dgakv_ref, dgq_ref, dgk_ref, dqp_ref, dkvp_ref):
        i = pl.program_id(0)

        @pl.when(i == 0)
        def _():
            for ref in (dwq_ref, dwkv_ref, dgaq_ref, dgakv_ref, dgq_ref, dgk_ref):
                ref[...] = jnp.zeros_like(ref)

        cv = c_ref[...]
        cq, ckv, kr = cv[:, 0:R], cv[:, R:2 * R], cv[:, 2 * R:2 * R + LANES]
        rq = lax.rsqrt(jnp.mean(cq * cq, axis=-1, keepdims=True) + RMS_EPS)
        nq = cq * rq
        cqn = (nq * gaq_ref[...]).astype(BF16)
        rk = lax.rsqrt(jnp.mean(ckv * ckv, axis=-1, keepdims=True) + RMS_EPS)
        nkv = ckv * rk
        ckvn = (nkv * gakv_ref[...]).astype(BF16)
        qp = _dot(cqn, wq_ref[...], NN)
        kvp = _dot(ckvn, wkv_ref[...], NN)
        ctv, s1v, s2v = ct_ref[...], s1_ref[...], s2_ref[...]
        gqv, gkv = gq_ref[...], gk_ref[...]
        krs = jnp.sum(kr * kr, axis=-1, keepdims=True)
        dkr = jnp.zeros((tm, LANES), F32)
        dgq_n = jnp.zeros((1, HEAD), F32)
        dgq_r = jnp.zeros((1, HEAD), F32)
        dgk_n = jnp.zeros((1, HEAD), F32)
        dgk_r = jnp.zeros((1, HEAD), F32)
        for h in range(H):
            b = h * QK_PAD
            qn, qr = qp[:, b:b + HEAD], qp[:, b + HEAD:b + QK_PAD]
            ss = jnp.sum(qn * qn, axis=-1, keepdims=True) + jnp.sum(qr * qr, axis=-1, keepdims=True)
            rr = lax.rsqrt(ss * (1.0 / QK_HEAD) + RMS_EPS)
            un, ur = qn * rr, qr * rr
            dzn = dq_ref[:, b:b + HEAD]
            dzr = _rope_t(dq_ref[:, b + HEAD:b + QK_PAD], ctv, s1v, s2v)
            dgq_n = dgq_n + jnp.sum(dzn * un, axis=0, keepdims=True)
            dgq_r = dgq_r + jnp.sum(dzr * ur, axis=0, keepdims=True)
            dun, dur = dzn * gqv[:, 0:HEAD], dzr * gqv[:, HEAD:QK_PAD]
            m = (jnp.sum(dun * un, axis=-1, keepdims=True) + jnp.sum(dur * ur, axis=-1, keepdims=True)) \
                * (1.0 / QK_HEAD)
            dqp_ref[:, b:b + HEAD] = (rr * (dun - un * m)).astype(BF16)
            dqp_ref[:, b + HEAD:b + QK_PAD] = (rr * (dur - ur * m)).astype(BF16)
            kn = kvp[:, b:b + HEAD]
            ssk = jnp.sum(kn * kn, axis=-1, keepdims=True) + krs
            rrk = lax.rsqrt(ssk * (1.0 / QK_HEAD) + RMS_EPS)
            vn, vr = kn * rrk, kr * rrk
            dyn = dk_ref[:, b:b + HEAD]
            dyr = _rope_t(dk_ref[:, b + HEAD:b + QK_PAD], ctv, s1v, s2v)
            dgk_n = dgk_n + jnp.sum(dyn * vn, axis=0, keepdims=True)
            dgk_r = dgk_r + jnp.sum(dyr * vr, axis=0, keepdims=True)
            dvn, dvr = dyn * gkv[:, 0:HEAD], dyr * gkv[:, HEAD:QK_PAD]
            mk = (jnp.sum(dvn * vn, axis=-1, keepdims=True) + jnp.sum(dvr * vr, axis=-1, keepdims=True)) \
                * (1.0 / QK_HEAD)
            dkvp_ref[:, b:b + HEAD] = (rrk * (dvn - vn * mk)).astype(BF16)
            dkr = dkr + rrk * (dvr - vr * mk)
            dkvp_ref[:, b + HEAD:b + QK_PAD] = dv_ref[:, h * HEAD:(h + 1) * HEAD].astype(BF16)
        dgq_ref[:, 0:HEAD] += dgq_n
        dgq_ref[:, HEAD:QK_PAD] += dgq_r
        dgk_ref[:, 0:HEAD] += dgk_n
        dgk_ref[:, HEAD:QK_PAD] += dgk_r
        dqp = dqp_ref[...]
        dkvp = dkvp_ref[...]
        dwq_ref[...] += _dot(cqn, dqp, TN)
        dwkv_ref[...] += _dot(ckvn, dkvp, TN)
        dcqn = _dot(dqp, wq_ref[...], NT)
        dckvn = _dot(dkvp, wkv_ref[...], NT)
        dgaq_ref[...] += jnp.sum(dcqn * nq, axis=0, keepdims=True)
        dgakv_ref[...] += jnp.sum(dckvn * nkv, axis=0, keepdims=True)
        dnq = dcqn * gaq_ref[...]
        dnkv = dckvn * gakv_ref[...]
        dc_ref[:, 0:R] = (rq * (dnq - nq * jnp.mean(dnq * nq, axis=-1, keepdims=True))).astype(dc_ref.dtype)
        dc_ref[:, R:2 * R] = (rk * (dnkv - nkv * jnp.mean(dnkv * nkv, axis=-1, keepdims=True))).astype(dc_ref.dtype)
        dc_ref[:, 2 * R:2 * R + LANES] = dkr.astype(dc_ref.dtype)

    full = lambda shape: pl.BlockSpec(shape, lambda i: (0, 0))
    tok = lambda w: pl.BlockSpec((tm, w), lambda i: (i, 0))
    return pl.pallas_call(
        body, name=name, grid=(T // tm,),
        in_specs=[tok(CW), full(wq.shape), full(wkv.shape), full((1, R)), full((1, R)), full((1, QK_PAD)),
                  full((1, QK_PAD)), tok(LANES), tok(LANES), tok(LANES),
                  tok(H * QK_PAD), tok(H * QK_PAD), tok(H * HEAD)],
        out_specs=[tok(CW), full(wq.shape), full(wkv.shape), full((1, R)), full((1, R)), full((1, QK_PAD)),
                   full((1, QK_PAD))],
        out_shape=[jax.ShapeDtypeStruct((T, CW), BF16), jax.ShapeDtypeStruct(wq.shape, F32),
                   jax.ShapeDtypeStruct(wkv.shape, F32), jax.ShapeDtypeStruct((1, R), F32),
                   jax.ShapeDtypeStruct((1, R), F32), jax.ShapeDtypeStruct((1, QK_PAD), F32),
                   jax.ShapeDtypeStruct((1, QK_PAD), F32)],
        scratch_shapes=[pltpu.VMEM((tm, H * QK_PAD), BF16), pltpu.VMEM((tm, H * QK_PAD), BF16)],
        compiler_params=_params(("arbitrary",)),
    )(c, wq, wkv, ga_q.reshape(1, R), ga_kv.reshape(1, R), gq, gk, ct, s1, s2, dq, dk, dv)


NEG = -1e30
LOG2E = 1.4426950408889634


def _attn_fwd(q, k, v, *, name, tb=512, hp=2, ride=None):
    T = q.shape[0]
    H = q.shape[1] // QK_PAD
    tb = min(tb, T)
    nq = T // tb
    scale = QK_HEAD ** -0.5
    c2 = scale * LOG2E
    assert H % hp == 0

    def body(q_ref, k_ref, v_ref, ot_ref, lse_ref, m_ref, l_ref, acc_ref):
        i = pl.program_id(1)
        m_ref[...] = jnp.full_like(m_ref, NEG)
        l_ref[...] = jnp.zeros_like(l_ref)
        acc_ref[...] = jnp.zeros_like(acc_ref)

        def step(j, masked):
            off = pl.multiple_of(j * tb, tb)
            sts = [_dot(k_ref[pl.ds(off, tb), hh * QK_PAD:(hh + 1) * QK_PAD],
                        q_ref[:, hh * QK_PAD:(hh + 1) * QK_PAD], NT) for hh in range(hp)]
            for hh in range(hp):
                vs = slice(hh * HEAD, (hh + 1) * HEAD)
                vb = v_ref[pl.ds(off, tb), vs]
                st = sts[hh]
                if masked:
                    kpos = lax.broadcasted_iota(jnp.int32, (tb, tb), 0)
                    qpos = lax.broadcasted_iota(jnp.int32, (tb, tb), 1)
                    st = jnp.where(kpos <= qpos, st, NEG)
                m_old = m_ref[hh]
                m_new = jnp.maximum(m_old, jnp.max(st, axis=0, keepdims=True))
                alpha = jnp.exp2((m_old - m_new) * c2)
                pt = jnp.exp2((st - m_new) * c2)
                l_ref[hh] = l_ref[hh] * alpha + jnp.sum(pt, axis=0, keepdims=True)
                acc_ref[vs, :] = acc_ref[vs, :] * alpha + _dot(vb, pt.astype(BF16), TN)
                m_ref[hh] = m_new

        def loop_body(j, carry):
            step(j, False)
            return carry

        lax.fori_loop(0, i, loop_body, 0)
        step(i, True)
        for hh in range(hp):
            vs = slice(hh * HEAD, (hh + 1) * HEAD)
            l = l_ref[hh]
            ot_ref[vs, :] = (acc_ref[vs, :] / l).astype(ot_ref.dtype)
            lse_ref[hh] = m_ref[hh] * scale + jnp.log(l)

    return _call(
        body, name=name, grid=(H // hp, nq),
        in_specs=[pl.BlockSpec((tb, hp * QK_PAD), lambda g, i: (i, g)),
                  pl.BlockSpec((T, hp * QK_PAD), lambda g, i: (0, g)),
                  pl.BlockSpec((T, hp * HEAD), lambda g, i: (0, g))],
        out_specs=[pl.BlockSpec((None, hp * HEAD, tb), lambda g, i: (i, g, 0)),
                   pl.BlockSpec((hp, None, 1, tb), lambda g, i: (g, i, 0, 0))],
        out_shape=[jax.ShapeDtypeStruct((nq, H * HEAD, tb), BF16), jax.ShapeDtypeStruct((H, nq, 1, tb), F32)],
        scratch_shapes=[pltpu.VMEM((hp, 1, tb), F32), pltpu.VMEM((hp, 1, tb), F32),
                        pltpu.VMEM((hp * HEAD, tb), F32)],
        sem=("parallel", "arbitrary"), ride=ride, args=(q, k, v))


def _attn_bwd(q, k, v, ot, lse, dot_, *, name, ride=None):
    T = q.shape[0]
    H = q.shape[1] // QK_PAD
    nq, _, tb = ot.shape
    scale = QK_HEAD ** -0.5
    c2 = scale * LOG2E

    def body(q_ref, k_ref, v_ref, ot_ref, lse_ref, dot_ref, dq_ref, dk_ref, dv_ref, dka_ref, dva_ref):
        j = pl.program_id(1)

        @pl.when(j == 0)
        def _():
            dq_ref[...] = jnp.zeros_like(dq_ref)

        kb = k_ref[...]
        vb = v_ref[...]

        def step(i, masked):
            off = i * tb if isinstance(i, int) else pl.multiple_of(i * tb, tb)
            qb = q_ref[pl.ds(off, tb), :]
            dob = dot_ref[i]
            ob = ot_ref[i]
            st = _dot(kb, qb, NT)
            if masked:
                kpos = lax.broadcasted_iota(jnp.int32, (tb, tb), 0)
                qpos = lax.broadcasted_iota(jnp.int32, (tb, tb), 1)
                st = jnp.where(kpos <= qpos, st, NEG)
            pt = jnp.exp2(st * c2 - lse_ref[i] * LOG2E)
            dpt = _dot(vb, dob, NN)
            delta = jnp.sum(dob.astype(F32) * ob.astype(F32), axis=0, keepdims=True)
            dst = (pt * (dpt - delta)).astype(BF16)
            dq_ref[pl.ds(off, tb), :] += _dot(dst, kb, TN)
            return _dot(dst, qb, NN), _dot(pt.astype(BF16), dob, NT)

        dk0, dv0 = step(j, True)
        dka_ref[...] = dk0
        dva_ref[...] = dv0

        rest = nq - 1 - j

        def pair_body(t, carry):
            i0 = j + 1 + 2 * t
            dk1, dv1 = step(i0, False)
            dk2, dv2 = step(i0 + 1, False)
            dka_ref[...] += dk1 + dk2
            dva_ref[...] += dv1 + dv2
            return carry

        lax.fori_loop(0, rest // 2, pair_body, 0)

        @pl.when(rest % 2 == 1)
        def _():
            dk1, dv1 = step(nq - 1, False)
            dka_ref[...] += dk1
            dva_ref[...] += dv1

        dk_ref[...] = dka_ref[...] * scale
        dv_ref[...] = dva_ref[...]

        @pl.when(j == nq - 1)
        def _():
            dq_ref[...] = dq_ref[...] * scale

    return _call(
        body, name=name, grid=(H, nq),
        in_specs=[pl.BlockSpec((T, QK_PAD), lambda h, j: (0, h)), pl.BlockSpec((tb, QK_PAD), lambda h, j: (j, h)),
                  pl.BlockSpec((tb, HEAD), lambda h, j: (j, h)),
                  pl.BlockSpec((nq, HEAD, tb), lambda h, j: (0, h, 0)),
                  pl.BlockSpec((None, nq, 1, tb), lambda h, j: (h, 0, 0, 0)),
                  pl.BlockSpec((nq, HEAD, tb), lambda h, j: (0, h, 0))],
        out_specs=[pl.BlockSpec((T, QK_PAD), lambda h, j: (0, h)), pl.BlockSpec((tb, QK_PAD), lambda h, j: (j, h)),
                   pl.BlockSpec((tb, HEAD), lambda h, j: (j, h))],
        out_shape=[jax.ShapeDtypeStruct((T, H * QK_PAD), F32), jax.ShapeDtypeStruct((T, H * QK_PAD), F32),
                   jax.ShapeDtypeStruct((T, H * HEAD), F32)],
        scratch_shapes=[pltpu.VMEM((tb, QK_PAD), F32), pltpu.VMEM((tb, HEAD), F32)],
        sem=("parallel", "arbitrary"), ride=ride, args=(q, k, v, ot, lse, dot_))


def _conv_taps(u, prev6, prev7, rows):
    u1 = jnp.where(rows >= 1, pltpu.roll(u, 1, 0), prev7)
    u2 = jnp.where(rows >= 2, pltpu.roll(u, 2, 0), jnp.where(rows == 0, prev6, prev7))
    return u2, u1


def _ffn_up_fwd(h, w_up, conv_w, conv_b, *, name, tm=512):
    T, D = h.shape
    ns, _, fs = w_up.shape
    nh = ns // 2
    tm = min(tm, T)

    def body(h_ref, wg_ref, wu_ref, cwg_ref, cwu_ref, cbg_ref, cbu_ref, a_ref, u_ref, cg_ref, cu_ref):
        i = pl.program_id(1)

        @pl.when(i == 0)
        def _():
            cg_ref[...] = jnp.zeros_like(cg_ref)
            cu_ref[...] = jnp.zeros_like(cu_ref)

        hv = h_ref[...]
        rows = lax.broadcasted_iota(jnp.int32, (tm, 1), 0)
        ys = []
        for idx, (w_ref, cw_ref, cb_ref, carry) in enumerate(
                ((wg_ref, cwg_ref, cbg_ref, cg_ref), (wu_ref, cwu_ref, cbu_ref, cu_ref))):
            u = _dot(hv, w_ref[...], NN)
            u_ref[idx] = u.astype(u_ref.dtype)
            u2, u1 = _conv_taps(u, carry[6:7, :], carry[7:8, :], rows)
            y = cb_ref[...] + u2 * cw_ref[0:1, :]
            y = y + u1 * cw_ref[1:2, :]
            y = y + u * cw_ref[2:3, :]
            ys.append(y)
            carry[...] = u[tm - 8:tm, :]
        yg, yu = ys
        a_ref[...] = ((yg * _sigmoid(yg)) * yu).astype(a_ref.dtype)

    shard = lambda r, off: pl.BlockSpec((None, r, fs), lambda j, i: (j + off, 0, 0))
    return pl.pallas_call(
        body, name=name, grid=(nh, T // tm),
        in_specs=[pl.BlockSpec((tm, D), lambda j, i: (i, 0)), shard(D, 0), shard(D, nh),
                  shard(3, 0), shard(3, nh), shard(1, 0), shard(1, nh)],
        out_specs=[pl.BlockSpec((None, tm, fs), lambda j, i: (j, i, 0)),
                   pl.BlockSpec((2, None, tm, fs), lambda j, i: (0, j, i, 0))],
        out_shape=[jax.ShapeDtypeStruct((nh, T, fs), BF16), jax.ShapeDtypeStruct((2, nh, T, fs), BF16)],
        scratch_shapes=[pltpu.VMEM((8, fs), F32), pltpu.VMEM((8, fs), F32)],
        compiler_params=_params(("parallel", "arbitrary")),
    )(h, w_up, w_up, conv_w, conv_w, conv_b, conv_b)


def _ffn_act_bwd(dxo, w_down, u, conv_w, conv_b, *, name, tm=512):
    T, D = dxo.shape
    _, nh, _, fs = u.shape
    tm = min(tm, T)
    nt = T // tm
    hb = tm // 8

    def body(dx_ref, wd_ref, u_ref, up_ref, cwg_ref, cwu_ref, cbg_ref, cbu_ref,
             du_ref, dcw_ref, dcb_ref, cg_ref, cu_ref):
        i = pl.program_id(1)
        tile = nt - 1 - i

        @pl.when(i == 0)
        def _():
            cg_ref[...] = jnp.zeros_like(cg_ref)
            cu_ref[...] = jnp.zeros_like(cu_ref)
            dcw_ref[...] = jnp.zeros_like(dcw_ref)
            dcb_ref[...] = jnp.zeros_like(dcb_ref)

        rows = lax.broadcasted_iota(jnp.int32, (tm, 1), 0)
        da = _dot(dx_ref[...].astype(BF16), wd_ref[...], NT)
        has_prev = (tile > 0).astype(F32)
        us, ys, shifted = [], [], []
        for idx, (cw_ref, cb_ref) in enumerate(((cwg_ref, cbg_ref), (cwu_ref, cbu_ref))):
            uv = u_ref[idx].astype(F32)
            p6 = up_ref[idx, 6:7, :].astype(F32) * has_prev
            p7 = up_ref[idx, 7:8, :].astype(F32) * has_prev
            u2, u1 = _conv_taps(uv, p6, p7, rows)
            y = cb_ref[...] + u2 * cw_ref[0:1, :]
            y = y + u1 * cw_ref[1:2, :]
            y = y + uv * cw_ref[2:3, :]
            us.append(uv)
            ys.append(y)
            shifted.append((u2, u1))
        yg, yu = ys
        sg = _sigmoid(yg)
        dys = (da * yu * (sg * (1.0 + yg * (1.0 - sg))), da * (yg * sg))
        for idx, (cw_ref, carry) in enumerate(((cwg_ref, cg_ref), (cwu_ref, cu_ref))):
            dy = dys[idx]
            u2, u1 = shifted[idx]
            dcb_ref[idx] += jnp.sum(dy, axis=0, keepdims=True)
            dcw_ref[idx, 0:1, :] += jnp.sum(dy * u2, axis=0, keepdims=True)
            dcw_ref[idx, 1:2, :] += jnp.sum(dy * u1, axis=0, keepdims=True)
            dcw_ref[idx, 2:3, :] += jnp.sum(dy * us[idx], axis=0, keepdims=True)
            n0, n1 = carry[0:1, :], carry[1:2, :]
            d1 = jnp.where(rows < tm - 1, pltpu.roll(dy, tm - 1, 0), n0)
            d2 = jnp.where(rows < tm - 2, pltpu.roll(dy, tm - 2, 0), jnp.where(rows == tm - 2, n0, n1))
            du = dy * cw_ref[2:3, :] + d1 * cw_ref[1:2, :] + d2 * cw_ref[0:1, :]
            du_ref[idx] = du.astype(du_ref.dtype)
            carry[...] = dy[0:8, :]

    rt = lambda i: nt - 1 - i
    shard = lambda r, off: pl.BlockSpec((None, r, fs), lambda j, i: (j + off, 0, 0))
    return pl.pallas_call(
        body, name=name, grid=(nh, nt),
        in_specs=[pl.BlockSpec((tm, D), lambda j, i: (rt(i), 0)), pl.BlockSpec((fs, D), lambda j, i: (j, 0)),
                  pl.BlockSpec((2, None, tm, fs), lambda j, i: (0, j, rt(i), 0)),
                  pl.BlockSpec((2, None, 8, fs), lambda j, i: (0, j, jnp.maximum(rt(i) * hb - 1, 0), 0)),
                  shard(3, 0), shard(3, nh), shard(1, 0), shard(1, nh)],
        out_specs=[pl.BlockSpec((2, None, tm, fs), lambda j, i: (0, j, rt(i), 0)),
                   pl.BlockSpec((2, None, 3, fs), lambda j, i: (0, j, 0, 0)),
                   pl.BlockSpec((2, None, 1, fs), lambda j, i: (0, j, 0, 0))],
        out_shape=[jax.ShapeDtypeStruct((2, nh, T, fs), BF16), jax.ShapeDtypeStruct((2, nh, 3, fs), F32),
                   jax.ShapeDtypeStruct((2, nh, 1, fs), F32)],
        scratch_shapes=[pltpu.VMEM((8, fs), F32), pltpu.VMEM((8, fs), F32)],
        compiler_params=_params(("parallel", "arbitrary")),
    )(dxo, w_down, u, u, conv_w, conv_w, conv_b, conv_b)


def _pad_cols(w, n):
    return jnp.pad(w, [(0, 0)] * (w.ndim - 1) + [(0, n - w.shape[-1])])


def _q_up_padded(w):
    R = w.shape[0]
    H = w.shape[1] // QK_HEAD
    return _pad_cols(w.reshape(R, H, QK_HEAD), QK_PAD).reshape(R, H * QK_PAD)


def _q_up_unpadded(w):
    R = w.shape[0]
    H = w.shape[1] // QK_PAD
    return w.reshape(R, H, QK_PAD)[:, :, :QK_HEAD].reshape(R, H * QK_HEAD)


def _xchg_copies(src_refs, out_refs, kinds, send_sems, recv_sems, local_sems):
    x, y, c = lax.axis_index("x"), lax.axis_index("y"), lax.axis_index("c")
    me = 4 * x + 2 * y + c
    copies = []
    for b, kind in enumerate(kinds):
        gather = kind == "gather"
        own = src_refs[b] if gather else src_refs[b].at[me]
        copies.append(pltpu.make_async_copy(own, out_refs[b].at[me], local_sems.at[b]))
        for kk in range(1, N_DEV):
            px = 1 - x if kk & 4 else x
            py = 1 - y if kk & 2 else y
            pc = 1 - c if kk & 1 else c
            peer = 4 * px + 2 * py + pc
            src = src_refs[b] if gather else src_refs[b].at[peer]
            copies.append(pltpu.make_async_remote_copy(
                src_ref=src, dst_ref=out_refs[b].at[me],
                send_sem=send_sems.at[b * (N_DEV - 1) + kk - 1],
                recv_sem=recv_sems.at[b * (N_DEV - 1) + kk - 1],
                device_id=(px, py, pc), device_id_type=pl.DeviceIdType.MESH))
    return copies


def _xchg_out_shapes(srcs, kinds):
    return [jax.ShapeDtypeStruct((N_DEV,) + s.shape if kind == "gather" else s.shape, s.dtype)
            for s, kind in zip(srcs, kinds)]


def _xchg_scratch(n):
    return [pltpu.SemaphoreType.DMA((n * (N_DEV - 1),)), pltpu.SemaphoreType.DMA((n * (N_DEV - 1),)),
            pltpu.SemaphoreType.DMA((n,))]


def _exchange(srcs, kinds, *, name):
    n = len(srcs)

    def body(*refs):
        copies = _xchg_copies(refs[:n], refs[n:2 * n], kinds, *refs[2 * n:])
        for cp in copies:
            cp.start()
        for cp in copies:
            cp.wait()

    hbm = pl.BlockSpec(memory_space=pl.ANY)
    return pl.pallas_call(
        body, name=name, in_specs=[hbm] * n, out_specs=[hbm] * n, out_shape=_xchg_out_shapes(srcs, kinds),
        scratch_shapes=_xchg_scratch(n),
    )(*srcs)


def _call(body, *, name, grid, in_specs, out_specs, out_shape, scratch_shapes, args, sem, ride=None):
    if ride is None:
        outs = pl.pallas_call(body, name=name, grid=grid, in_specs=in_specs, out_specs=out_specs,
                              out_shape=out_shape, scratch_shapes=scratch_shapes,
                              compiler_params=_params(sem))(*args)
        return list(outs), []
    srcs, kinds = ride
    n_in, n_out, n_sc, nx = len(in_specs), len(out_specs), len(scratch_shapes), len(srcs)

    def wrapped(*refs):
        ins, xs = refs[:n_in], refs[n_in:n_in + nx]
        o0 = n_in + nx
        outs, xo = refs[o0:o0 + n_out], refs[o0 + n_out:o0 + n_out + nx]
        s0 = o0 + n_out + nx
        sc, sems = refs[s0:s0 + n_sc], refs[s0 + n_sc:]
        first = functools.reduce(jnp.logical_and, [pl.program_id(d) == 0 for d in range(len(grid))])
        last = functools.reduce(jnp.logical_and, [pl.program_id(d) == grid[d] - 1 for d in range(len(grid))])

        @pl.when(first)
        def _():
            for cp in _xchg_copies(xs, xo, kinds, *sems):
                cp.start()

        body(*ins, *outs, *sc)

        @pl.when(last)
        def _():
            for cp in _xchg_copies(xs, xo, kinds, *sems):
                cp.wait()

    hbm = pl.BlockSpec(memory_space=pl.ANY)
    outs = pl.pallas_call(
        wrapped, name=name, grid=grid, in_specs=list(in_specs) + [hbm] * nx,
        out_specs=list(out_specs) + [hbm] * nx, out_shape=list(out_shape) + _xchg_out_shapes(srcs, kinds),
        scratch_shapes=list(scratch_shapes) + _xchg_scratch(nx),
        compiler_params=_params(("arbitrary",) * len(grid)),
    )(*args, *srcs)
    return list(outs[:n_out]), list(outs[n_out:])


def _sum_slots(parts, *, name):
    _, Rr, C = parts.shape

    def body(p_ref, o_ref):
        acc = p_ref[0].astype(F32)
        for d in range(1, N_DEV):
            acc = acc + p_ref[d].astype(F32)
        o_ref[...] = acc

    return pl.pallas_call(
        body, name=name, grid=(1,),
        in_specs=[pl.BlockSpec((N_DEV, Rr, C), lambda i: (0, 0, 0))],
        out_specs=pl.BlockSpec((Rr, C), lambda i: (0, 0)),
        out_shape=jax.ShapeDtypeStruct((Rr, C), F32),
        compiler_params=_params(("arbitrary",)),
    )(parts)


def _row_tile(rows, cap=512):
    if rows <= cap:
        return rows
    d = (cap // 8) * 8
    while d >= 8:
        if rows % d == 0:
            return d
        d -= 8
    raise ValueError(f"no row tile for {rows}")


def _adamw(parts, w, m, v, *, name):
    S, Rr, C = parts.shape
    tr = _row_tile(Rr)
    c1 = 1.0 - ADAM_B1 ** ADAM_STEP
    c2 = 1.0 - ADAM_B2 ** ADAM_STEP

    def body(p_ref, w_ref, m_ref, v_ref, g_ref, d_ref, nm_ref, nv_ref):
        g = p_ref[0].astype(F32)
        for d in range(1, S):
            g = g + p_ref[d].astype(F32)
        mm = ADAM_B1 * m_ref[...] + (1.0 - ADAM_B1) * g
        vv = ADAM_B2 * v_ref[...] + (1.0 - ADAM_B2) * (g * g)
        m_hat = mm / c1
        v_hat = vv / c2
        g_ref[...] = g
        d_ref[...] = -ADAM_LR * (m_hat / (jnp.sqrt(v_hat) + ADAM_EPS) + ADAM_WD * w_ref[...])
        nm_ref[...] = mm
        nv_ref[...] = vv

    spec = pl.BlockSpec((tr, C), lambda i: (i, 0))
    shape = jax.ShapeDtypeStruct((Rr, C), F32)
    return pl.pallas_call(
        body, name=name, grid=(Rr // tr,),
        in_specs=[pl.BlockSpec((S, tr, C), lambda i: (0, i, 0)), spec, spec, spec],
        out_specs=[spec] * 4, out_shape=[shape] * 4,
        compiler_params=_params(("parallel",)),
    )(parts, w, m, v)


def _pack(arrs, dtype, row_mult):
    flat = jnp.concatenate([a.reshape(-1).astype(dtype) for a in arrs])
    per = row_mult * PACK_COLS
    total = -(-flat.shape[0] // per) * per
    return jnp.pad(flat, (0, total - flat.shape[0])).reshape(total // PACK_COLS, PACK_COLS)


def _unpack(packed, shapes, lead=()):
    flat = packed.reshape(lead + (-1,))
    out, off = [], 0
    for shp in shapes:
        n = 1
        for d in shp:
            n *= d
        out.append(flat[..., off:off + n].reshape(lead + tuple(shp)))
        off += n
    return out


HGRN_W = ("hgrn_w_in", "hgrn_w_out")
MLA_W = ("mla_w_in", "mla_w_q_up", "mla_w_kv_up", "mla_w_out")
FFN_W = ("ffn_w_up", "ffn_w_down")
BIG = HGRN_W + MLA_W + FFN_W
SMALL_SHARDED = {"ffn_conv_w": 2, "mla_q_a_norm": 1, "mla_kv_a_norm": 1}
REPLICATED = ["norm_mix", "norm_ffn", "hgrn_lower_bounds", "hgrn_out_norm", "mla_q_norm", "mla_k_norm",
              "ffn_conv_b"]
WEIGHTS = ["norm_mix", "norm_ffn", "hgrn_w_in", "hgrn_lower_bounds", "hgrn_out_norm", "hgrn_w_out", "mla_w_in",
           "mla_q_a_norm", "mla_w_q_up", "mla_kv_a_norm", "mla_w_kv_up", "mla_q_norm", "mla_k_norm", "mla_w_out",
           "ffn_w_up", "ffn_conv_w", "ffn_conv_b", "ffn_w_down"]


def _shards_to_cols(g):
    return g.transpose(1, 0, 2).reshape(g.shape[1], N_DEV * g.shape[2])


def _cols_to_shards(w):
    R = w.shape[0]
    return w.reshape(R, N_DEV, w.shape[1] // N_DEV).transpose(1, 0, 2)


def kernel(x, positions, norm_mix, norm_ffn, hgrn_w_in, hgrn_lower_bounds, hgrn_out_norm, hgrn_w_out, mla_w_in, mla_q_a_norm, mla_w_q_up, mla_kv_a_norm, mla_w_kv_up, mla_q_norm, mla_k_norm, mla_w_out, ffn_w_up, ffn_conv_w, ffn_conv_b, ffn_w_down, loss_target, m_norm_mix, m_norm_ffn, m_hgrn_w_in, m_hgrn_lower_bounds, m_hgrn_out_norm, m_hgrn_w_out, m_mla_w_in, m_mla_q_a_norm, m_mla_w_q_up, m_mla_kv_a_norm, m_mla_w_kv_up, m_mla_q_norm, m_mla_k_norm, m_mla_w_out, m_ffn_w_up, m_ffn_conv_w, m_ffn_conv_b, m_ffn_w_down, v_norm_mix, v_norm_ffn, v_hgrn_w_in, v_hgrn_lower_bounds, v_hgrn_out_norm, v_hgrn_w_out, v_mla_w_in, v_mla_q_a_norm, v_mla_w_q_up, v_mla_kv_a_norm, v_mla_w_kv_up, v_mla_q_norm, v_mla_k_norm, v_mla_w_out, v_ffn_w_up, v_ffn_conv_w, v_ffn_conv_b, v_ffn_w_down):
    local = dict(norm_mix=norm_mix, norm_ffn=norm_ffn, hgrn_w_in=hgrn_w_in, hgrn_lower_bounds=hgrn_lower_bounds,
                 hgrn_out_norm=hgrn_out_norm, hgrn_w_out=hgrn_w_out, mla_w_in=mla_w_in, mla_q_a_norm=mla_q_a_norm,
                 mla_w_q_up=mla_w_q_up, mla_kv_a_norm=mla_kv_a_norm, mla_w_kv_up=mla_w_kv_up, mla_q_norm=mla_q_norm,
                 mla_k_norm=mla_k_norm, mla_w_out=mla_w_out, ffn_w_up=ffn_w_up, ffn_conv_w=ffn_conv_w,
                 ffn_conv_b=ffn_conv_b, ffn_w_down=ffn_w_down)
    mom_m = dict(norm_mix=m_norm_mix, norm_ffn=m_norm_ffn, hgrn_w_in=m_hgrn_w_in,
                 hgrn_lower_bounds=m_hgrn_lower_bounds, hgrn_out_norm=m_hgrn_out_norm, hgrn_w_out=m_hgrn_w_out,
                 mla_w_in=m_mla_w_in, mla_q_a_norm=m_mla_q_a_norm, mla_w_q_up=m_mla_w_q_up,
                 mla_kv_a_norm=m_mla_kv_a_norm, mla_w_kv_up=m_mla_w_kv_up, mla_q_norm=m_mla_q_norm,
                 mla_k_norm=m_mla_k_norm, mla_w_out=m_mla_w_out, ffn_w_up=m_ffn_w_up, ffn_conv_w=m_ffn_conv_w,
                 ffn_conv_b=m_ffn_conv_b, ffn_w_down=m_ffn_w_down)
    mom_v = dict(norm_mix=v_norm_mix, norm_ffn=v_norm_ffn, hgrn_w_in=v_hgrn_w_in,
                 hgrn_lower_bounds=v_hgrn_lower_bounds, hgrn_out_norm=v_hgrn_out_norm, hgrn_w_out=v_hgrn_w_out,
                 mla_w_in=v_mla_w_in, mla_q_a_norm=v_mla_q_a_norm, mla_w_q_up=v_mla_w_q_up,
                 mla_kv_a_norm=v_mla_kv_a_norm, mla_w_kv_up=v_mla_w_kv_up, mla_q_norm=v_mla_q_norm,
                 mla_k_norm=v_mla_k_norm, mla_w_out=v_mla_w_out, ffn_w_up=v_ffn_w_up, ffn_conv_w=v_ffn_conv_w,
                 ffn_conv_b=v_ffn_conv_b, ffn_w_down=v_ffn_w_down)
    me = 4 * lax.axis_index("x") + 2 * lax.axis_index("y") + lax.axis_index("c")
    x, positions, target = x[0], positions[0], loss_target[0]
    T, D = x.shape
    depth = norm_mix.shape[0]
    R = mla_w_q_up.shape[1]
    cw = 2 * R + LANES
    small_names = list(SMALL_SHARDED)

    def block_of(kind, l):
        names = {"hgrn": HGRN_W, "mla": MLA_W, "ffn": FFN_W}[kind]
        idx = l if kind == "ffn" else l // 2
        return [(n, idx) for n in names]

    def mixer_kind(l):
        return "hgrn" if l % 2 == 0 else "mla"

    def riders(l):
        keys = block_of("ffn", l)
        if l + 1 < depth:
            keys += block_of(mixer_kind(l + 1), l + 1)
        return keys

    gathered = {}

    def gather_ride(l):
        keys = riders(l)
        return keys, ([local[n][i].astype(BF16) for n, i in keys], ["gather"] * len(keys))

    def take_gathered(keys, arrs):
        for key, a in zip(keys, arrs):
            gathered[key] = a

    keys0 = block_of("hgrn", 0)
    small_local = _pack([local[n] for n in small_names], F32, 8)
    got = _exchange([local[n][i].astype(BF16) for n, i in keys0] + [small_local],
                    ["gather"] * (len(keys0) + 1), name="gather_first")
    take_gathered(keys0, got[:-1])
    small_all = _unpack(got[-1], [local[n].shape for n in small_names], lead=(N_DEV,))
    conv_w_all = small_all[0].transpose(1, 0, 2, 3)
    qa_all = small_all[1].transpose(1, 0, 2).reshape(-1, R)
    kva_all = small_all[2].transpose(1, 0, 2).reshape(-1, R)
    fs = conv_w_all.shape[-1]
    conv_b_s = ffn_conv_b.reshape(depth, N_DEV, 1, fs)

    ct, s1, s2 = _rope_tables(positions)
    lb_soft = jax.nn.softmax(hgrn_lower_bounds.astype(F32), axis=0)
    lower_bounds = jnp.cumsum(lb_soft, axis=0) - lb_soft[0:1]

    def mla_views(j):
        w_in = _pad_cols(gathered["mla_w_in", j].reshape(D, -1), cw)
        wq = _q_up_padded(_shards_to_cols(gathered["mla_w_q_up", j]))
        wkv = _shards_to_cols(gathered["mla_w_kv_up", j])
        gq = _pad_cols(mla_q_norm[j].reshape(1, QK_HEAD), QK_PAD)
        gk = _pad_cols(mla_k_norm[j].reshape(1, QK_HEAD), QK_PAD)
        return w_in, wq, wkv, gq, gk

    saved = []
    h = _rmsnorm_fwd(x, norm_mix[0], name="norm_mix_fwd_0")
    for layer in range(depth):
        j = layer // 2
        s = {"x_in": x}
        s["h_mix"] = h
        keys, ride = gather_ride(layer)
        if layer % 2 == 0:
            p = _mm(h, gathered["hgrn_w_in", j], b_fmt="knb", bm=1024, name=f"hgrn_in_{layer}")
            (og, o, s0), got = _hgrn_fwd(p, lower_bounds[j], hgrn_out_norm[j], name=f"hgrn_fwd_{layer}", ride=ride)
            s.update(p=p, og=og, o=o, s0=s0)
            take_gathered(keys, got)
            x, h = _mm(og, gathered["hgrn_w_out", j].reshape(D, D), res=x, norm_out=norm_ffn[layer],
                       name=f"hgrn_out_{layer}")
        else:
            w_in, wq, wkv, gq, gk = mla_views(j)
            c = _mm(h, w_in, bm=1024, name=f"mla_in_{layer}")
            q, k, v = _mla_prep_fwd(c, wq, wkv, qa_all[j], kva_all[j], gq, gk, ct, s1, s2,
                                    name=f"mla_prep_fwd_{layer}")
            (ot, lse), got = _attn_fwd(q, k, v, name=f"attn_fwd_{layer}", ride=ride)
            s.update(c=c, q=q, k=k, v=v, ot=ot, lse=lse)
            take_gathered(keys, got)
            x, h = _mm(ot, gathered["mla_w_out", j].reshape(D, D), a_fmt="kmb", res=x, norm_out=norm_ffn[layer],
                       name=f"mla_out_{layer}")
        s["x_mid"] = x
        s["h_ffn"] = h
        a, u = _ffn_up_fwd(h, gathered["ffn_w_up", layer], conv_w_all[layer], conv_b_s[layer], name=f"ffn_up_{layer}")
        s.update(a=a, u=u)
        w_down = gathered["ffn_w_down", layer].reshape(-1, D)
        if layer + 1 < depth:
            x, h = _mm(a, w_down, a_fmt="mkb", res=x, norm_out=norm_mix[layer + 1], bk=fs, name=f"ffn_down_{layer}")
        else:
            x = _mm(a, w_down, a_fmt="mkb", res=x, bk=fs, name=f"ffn_down_{layer}")
        saved.append(s)

    dx, loss_part = _loss_head(x, target, name="loss_head")

    parts = {}
    received = {}
    g_small = {n: [None] * local[n].shape[0] for n in REPLICATED + small_names}

    def scatter_ride(l):
        keys = riders(l)
        return keys, ([parts[key] for key in keys], ["scatter"] * len(keys))

    def take_received(keys, arrs):
        for key, a in zip(keys, arrs):
            received[key] = a

    for layer in reversed(range(depth)):
        j = layer // 2
        s = saved[layer]
        parts["ffn_w_down", layer] = _mm(s["a"], dx, a_fmt="kmb", out_dtype=BF16, name=f"ffn_down_dw_{layer}"
                                         ).reshape(N_DEV, -1, D)
        du, dcw, dcb = _ffn_act_bwd(dx, gathered["ffn_w_down", layer].reshape(-1, D), s["u"], conv_w_all[layer],
                                    conv_b_s[layer], name=f"ffn_act_bwd_{layer}")
        g_small["ffn_conv_w"][layer] = dcw.reshape(N_DEV, 3, fs)
        g_small["ffn_conv_b"][layer] = dcb.reshape(N_DEV * fs)
        du8 = du.reshape(N_DEV, T, fs)
        parts["ffn_w_up", layer] = _mm(s["h_ffn"], du8, a_fmt="km", b_fmt="knb", out_fmt="mnb", out_dtype=BF16,
                                       bm=1024, name=f"ffn_up_dw_{layer}")
        dx, dgain = _mm(du8, gathered["ffn_w_up", layer], a_fmt="mkb", b_fmt="nkb", bm=1024, bk=fs,
                        norm_bwd=(s["x_mid"], norm_ffn[layer], dx), name=f"ffn_up_dh_{layer}")
        g_small["norm_ffn"][layer] = dgain.reshape(D)
        keys, ride = scatter_ride(layer)
        if layer % 2 == 0:
            w_out = gathered["hgrn_w_out", j].reshape(D, D)
            parts["hgrn_w_out", j] = _mm(s["og"], dx, a_fmt="km", out_dtype=BF16, bm=1024,
                                         name=f"hgrn_out_dw_{layer}").reshape(N_DEV, -1, D)
            dog = _mm(dx, w_out, b_fmt="nk", name=f"hgrn_out_dx_{layer}")
            (dp, dlb, dgn), got = _hgrn_bwd(s["p"], lower_bounds[j], hgrn_out_norm[j], s["s0"], s["o"], dog,
                                            name=f"hgrn_bwd_{layer}", ride=ride)
            take_received(keys, got)
            g_small["hgrn_lower_bounds"][j] = dlb.reshape(D)
            g_small["hgrn_out_norm"][j] = dgn.reshape(HEAD)
            w_in_s = gathered["hgrn_w_in", j]
            parts["hgrn_w_in", j] = _mm(s["h_mix"], dp, a_fmt="km", out_fmt="mnb", out_dtype=BF16, bm=1024,
                                        bn=w_in_s.shape[2], name=f"hgrn_in_dw_{layer}")
            dx, dgain = _mm(dp, w_in_s, b_fmt="nkb", bm=1024, norm_bwd=(s["x_in"], norm_mix[layer], dx),
                            name=f"hgrn_in_dx_{layer}")
        else:
            w_in, wq, wkv, gq, gk = mla_views(j)
            w_out = gathered["mla_w_out", j].reshape(D, D)
            tb = s["ot"].shape[2]
            parts["mla_w_out", j] = _mm(s["ot"], dx, a_fmt="mkb", bk=tb, out_dtype=BF16, bm=1024,
                                        name=f"mla_out_dw_{layer}").reshape(N_DEV, -1, D)
            dot_ = _mm(w_out, dx, b_fmt="nk", out_fmt="mnb", out_dtype=BF16, bm=D, bn=tb,
                       name=f"mla_out_dx_{layer}")
            (dq, dk, dv), got = _attn_bwd(s["q"], s["k"], s["v"], s["ot"], s["lse"], dot_,
                                          name=f"attn_bwd_{layer}", ride=ride)
            take_received(keys, got)
            dc, dwq, dwkv, dgaq, dgakv, dgq, dgk = _mla_prep_bwd(
                s["c"], wq, wkv, qa_all[j], kva_all[j], gq, gk, ct, s1, s2, dq, dk, dv,
                name=f"mla_prep_bwd_{layer}")
            parts["mla_w_q_up", j] = _cols_to_shards(_q_up_unpadded(dwq)).astype(BF16)
            parts["mla_w_kv_up", j] = _cols_to_shards(dwkv).astype(BF16)
            g_small["mla_q_a_norm"][j] = dgaq.reshape(R)
            g_small["mla_kv_a_norm"][j] = dgakv.reshape(R)
            g_small["mla_q_norm"][j] = dgq[0, :QK_HEAD]
            g_small["mla_k_norm"][j] = dgk[0, :QK_HEAD]
            win_cols = mla_w_in.shape[2]
            dw_in = _mm(s["h_mix"], dc, a_fmt="km", bm=1024, name=f"mla_in_dw_{layer}")
            parts["mla_w_in", j] = dw_in[:, :win_cols].astype(BF16).reshape(N_DEV, -1, win_cols)
            dx, dgain = _mm(dc, w_in, b_fmt="nk", norm_bwd=(s["x_in"], norm_mix[layer], dx),
                            name=f"mla_in_dx_{layer}")
        g_small["norm_mix"][layer] = dgain.reshape(D)
    grad_x = dx

    dlb_eff = jnp.stack(g_small["hgrn_lower_bounds"])
    dsoft = jnp.cumsum(dlb_eff[::-1], axis=0)[::-1]
    dsoft = dsoft.at[0].add(-jnp.sum(dlb_eff, axis=0))
    g_lb = lb_soft * (dsoft - jnp.sum(dsoft * lb_soft, axis=0, keepdims=True))
    small_grads = {n: (g_lb if n == "hgrn_lower_bounds" else jnp.stack(g_small[n])) for n in g_small}

    small_grad_names = REPLICATED + small_names
    small_part = _pack([small_grads[n] for n in small_grad_names] + [loss_part], F32, 8)
    got = _exchange([parts[key] for key in keys0] + [small_part], ["scatter"] * len(keys0) + ["gather"],
                    name="exchange_last")
    take_received(keys0, got[:-1])
    small_recv = got[-1]

    out = {}
    for n in BIG:
        layers = local[n].shape[0]
        shard = local[n].shape[1:]
        p2 = jnp.concatenate([received[n, i].reshape(N_DEV, -1, shard[-1]) for i in range(layers)], axis=1)
        flat = lambda a: a.reshape(-1, shard[-1])
        res = _adamw(p2, flat(local[n]), flat(mom_m[n]), flat(mom_v[n]), name=f"adamw_{n}")
        for kind, a in zip(("grad", "delta", "new_m", "new_v"), res):
            out[kind, n] = a.reshape(local[n].shape)
    small_sum = _sum_slots(small_recv, name="sum_small")
    small_full = _unpack(small_sum, [small_grads[n].shape for n in small_grad_names] + [(1, LANES)])
    loss = small_full[-1][0, 0]
    g_mine = {}
    for n, a in zip(small_grad_names, small_full[:-1]):
        if n == "ffn_conv_w":
            a = lax.dynamic_index_in_dim(a, me, axis=1, keepdims=False)
        elif n in SMALL_SHARDED:
            size = local[n].shape[1]
            a = lax.dynamic_slice_in_dim(a, me * size, size, axis=1)
        g_mine[n] = a
    small_shapes = [local[n].shape for n in small_grad_names]
    res = _adamw(_pack([g_mine[n] for n in small_grad_names], F32, 8)[None],
                 _pack([local[n] for n in small_grad_names], F32, 8),
                 _pack([mom_m[n] for n in small_grad_names], F32, 8),
                 _pack([mom_v[n] for n in small_grad_names], F32, 8), name="adamw_small")
    for kind, packed in zip(("grad", "delta", "new_m", "new_v"), res):
        for n, a in zip(small_grad_names, _unpack(packed, small_shapes)):
            out[kind, n] = a

    outs = [loss, grad_x[None]]
    for kind in ("grad", "delta", "new_m", "new_v"):
        outs += [out[kind, n] for n in WEIGHTS]
    return tuple(outs)
```

```python
import functools

import jax
import jax.numpy as jnp
from jax import lax
from jax.experimental import pallas as pl
from jax.experimental.pallas import tpu as pltpu

F32 = jnp.float32
BF16 = jnp.bfloat16

RMS_EPS = 1e-6
ROPE_THETA = 10000.0
HEAD = 128
ROPE = 64
QK_HEAD = HEAD + ROPE
QK_PAD = 256
CHUNK = 64
SUB = 16
EXP_CLAMP = 60.0
HGRN_CHUNKS_PER_STEP = 4
HGRN_BWD_CHUNKS_PER_STEP = 2

ADAM_LR = 0.001
ADAM_B1 = 0.9
ADAM_B2 = 0.999
ADAM_EPS = 1e-08
ADAM_WD = 0.01
ADAM_STEP = 10

N_DEV = 8
LANES = 128
PACK_COLS = 1024
V7X_VMEM_LIMIT = 56 * 1024 * 1024

HI = lax.Precision.HIGHEST


def _params(sem):
    return pltpu.CompilerParams(dimension_semantics=sem, vmem_limit_bytes=V7X_VMEM_LIMIT)


def _blk(n, cap):
    if n <= cap:
        return n
    d = (cap // LANES) * LANES
    while d >= LANES:
        if n % d == 0:
            return d
        d -= LANES
    raise ValueError(f"no lane-aligned block for {n} under {cap}")


def _sigmoid(x):
    return jax.nn.sigmoid(x)


def _dot(a, b, dims, precision=None):
    return lax.dot_general(a, b, (dims, ((), ())), preferred_element_type=F32, precision=precision)


NN = ((1,), (0,))
NT = ((1,), (1,))
TN = ((0,), (0,))


def _mm(a, b, *, a_fmt="mk", b_fmt="kn", out_fmt="mn", res=None, norm_out=None, norm_bwd=None, out_dtype=F32,
        bm=512, bn=1024, bk=1024, name):
    if a_fmt == "mk":
        M, K = a.shape
    elif a_fmt == "km":
        K, M = a.shape
    elif a_fmt == "kmb":
        nb, K, B = a.shape
        M = nb * B
    else:
        nb, M, B = a.shape
        K = nb * B
    if b_fmt == "kn":
        Kb, N = b.shape
    elif b_fmt == "nk":
        N, Kb = b.shape
    elif b_fmt == "knb":
        nbb, Kb, Bb = b.shape
        N = nbb * Bb
    else:
        nbb, N, Bb = b.shape
        Kb = nbb * Bb
    assert K == Kb, (a.shape, b.shape, a_fmt, b_fmt)
    bm = B if a_fmt == "kmb" else _blk(M, bm)
    bn = Bb if b_fmt == "knb" else _blk(N, bn)
    if a_fmt == "mkb" and b_fmt == "nkb":
        assert B == Bb
    bk = _blk(B, bk) if a_fmt == "mkb" else (_blk(Bb, bk) if b_fmt == "nkb" else _blk(K, bk))
    nm, nn, nk = M // bm, N // bn, K // bk

    if a_fmt == "mk":
        a_spec = pl.BlockSpec((bm, bk), lambda i, j, k: (i, k))
        a_dim = 1
    elif a_fmt == "km":
        a_spec = pl.BlockSpec((bk, bm), lambda i, j, k: (k, i))
        a_dim = 0
    elif a_fmt == "kmb":
        a_spec = pl.BlockSpec((None, bk, bm), lambda i, j, k: (i, k, 0))
        a_dim = 0
    else:
        per = B // bk
        a_spec = pl.BlockSpec((None, bm, bk), lambda i, j, k: (k // per, i, k % per))
        a_dim = 1
    if b_fmt == "kn":
        b_spec = pl.BlockSpec((bk, bn), lambda i, j, k: (k, j))
        b_dim = 0
    elif b_fmt == "nk":
        b_spec = pl.BlockSpec((bn, bk), lambda i, j, k: (j, k))
        b_dim = 1
    elif b_fmt == "knb":
        b_spec = pl.BlockSpec((None, bk, bn), lambda i, j, k: (j, k, 0))
        b_dim = 0
    else:
        perb = Bb // bk
        b_spec = pl.BlockSpec((None, bn, bk), lambda i, j, k: (k // perb, j, k % perb))
        b_dim = 1
    if out_fmt == "mn":
        o_spec = pl.BlockSpec((bm, bn), lambda i, j, k: (i, j))
        o_shape = jax.ShapeDtypeStruct((M, N), out_dtype)
    else:
        o_spec = pl.BlockSpec((None, bm, bn), lambda i, j, k: (j, i, 0))
        o_shape = jax.ShapeDtypeStruct((nn, M, bn), out_dtype)
    in_specs = [a_spec, b_spec]
    args = [a, b]
    row_tile = pl.BlockSpec((bm, bn), lambda i, j, k: (i, j))
    row_vec = pl.BlockSpec((1, bn), lambda i, j, k: (0, j))
    if res is not None:
        assert out_fmt == "mn"
        in_specs.append(row_tile)
        args.append(res)
    out_specs, out_shapes = [o_spec], [o_shape]
    if norm_out is not None:
        assert nn == 1 and out_fmt == "mn"
        in_specs.append(row_vec)
        args.append(norm_out.reshape(1, N))
        out_specs.append(row_tile)
        out_shapes.append(jax.ShapeDtypeStruct((M, N), BF16))
    if norm_bwd is not None:
        assert nn == 1 and out_fmt == "mn" and res is None and norm_out is None
        xin, gain, dres = norm_bwd
        in_specs += [row_tile, row_vec, row_tile]
        args += [xin, gain.reshape(1, N), dres]
        out_specs.append(row_vec)
        out_shapes.append(jax.ShapeDtypeStruct((1, N), F32))
    dims = ((a_dim,), (b_dim,))
    has_res = res is not None
    n_in = len(in_specs)

    def body(*refs):
        a_ref, b_ref = refs[0], refs[1]
        extra_in = list(refs[2:n_in])
        o_ref = refs[n_in]
        part = _dot(a_ref[...].astype(BF16), b_ref[...].astype(BF16), dims)

        def finish(out):
            if has_res:
                out = out + extra_in[0][...]
            if norm_out is not None:
                g_ref, h_ref = extra_in[-1], refs[n_in + 1]
                r = lax.rsqrt(jnp.mean(out * out, axis=-1, keepdims=True) + RMS_EPS)
                h_ref[...] = ((out * r) * g_ref[...]).astype(h_ref.dtype)
            if norm_bwd is not None:
                x_ref, g_ref, dr_ref = extra_in
                dg_ref = refs[n_in + 1]
                xv = x_ref[...]
                r = lax.rsqrt(jnp.mean(xv * xv, axis=-1, keepdims=True) + RMS_EPS)
                n = xv * r
                dn = out * g_ref[...]
                gpart = jnp.sum(out * n, axis=0, keepdims=True)
                i = pl.program_id(0)

                @pl.when(i == 0)
                def _():
                    dg_ref[...] = gpart

                @pl.when(i > 0)
                def _():
                    dg_ref[...] += gpart

                out = dr_ref[...] + r * (dn - n * jnp.mean(dn * n, axis=-1, keepdims=True))
            o_ref[...] = out.astype(o_ref.dtype)

        if nk == 1:
            finish(part)
            return
        acc_ref = refs[-1]
        k = pl.program_id(2)

        @pl.when(k == 0)
        def _():
            acc_ref[...] = part

        @pl.when(jnp.logical_and(k > 0, k < nk - 1))
        def _():
            acc_ref[...] += part

        @pl.when(k == nk - 1)
        def _():
            finish(acc_ref[...] + part)

    multi = len(out_specs) > 1
    return pl.pallas_call(
        body, name=name, grid=(nm, nn, nk), in_specs=in_specs,
        out_specs=out_specs if multi else o_spec, out_shape=out_shapes if multi else o_shape,
        scratch_shapes=[] if nk == 1 else [pltpu.VMEM((bm, bn), F32)],
        compiler_params=_params(("arbitrary",) * 3 if norm_bwd is not None else ("parallel", "parallel", "arbitrary")),
    )(*args)


def _rmsnorm_fwd(x, gain, *, name, tm=512):
    T, D = x.shape
    tm = min(tm, T)

    def body(x_ref, g_ref, o_ref):
        xv = x_ref[...]
        r = lax.rsqrt(jnp.mean(xv * xv, axis=-1, keepdims=True) + RMS_EPS)
        o_ref[...] = ((xv * r) * g_ref[...]).astype(o_ref.dtype)

    return pl.pallas_call(
        body, name=name, grid=(T // tm,),
        in_specs=[pl.BlockSpec((tm, D), lambda i: (i, 0)), pl.BlockSpec((1, D), lambda i: (0, 0))],
        out_specs=pl.BlockSpec((tm, D), lambda i: (i, 0)),
        out_shape=jax.ShapeDtypeStruct((T, D), BF16),
        compiler_params=_params(("parallel",)),
    )(x, gain.reshape(1, D))


def _loss_head(y, target, *, name, tm=512):
    T, D = y.shape
    tm = min(tm, T)

    def body(y_ref, t_ref, dy_ref, l_ref):
        i = pl.program_id(0)
        e = y_ref[...] - t_ref[...]
        dy_ref[...] = e * (1.0 / D)
        s = 0.5 * jnp.sum(jnp.mean(e * e, axis=-1, keepdims=True), axis=0, keepdims=True)
        part = jnp.broadcast_to(s, (1, LANES))

        @pl.when(i == 0)
        def _():
            l_ref[...] = part

        @pl.when(i > 0)
        def _():
            l_ref[...] += part

    return pl.pallas_call(
        body, name=name, grid=(T // tm,),
        in_specs=[pl.BlockSpec((tm, D), lambda i: (i, 0)), pl.BlockSpec((tm, D), lambda i: (i, 0))],
        out_specs=[pl.BlockSpec((tm, D), lambda i: (i, 0)), pl.BlockSpec((1, LANES), lambda i: (0, 0))],
        out_shape=[jax.ShapeDtypeStruct((T, D), F32), jax.ShapeDtypeStruct((1, LANES), F32)],
        compiler_params=_params(("arbitrary",)),
    )(y, target)


def _hgrn_selectors():
    t = jnp.arange(CHUNK)[:, None]
    s = jnp.arange(CHUNK)[None, :]
    mats = [s <= t, s < (t // SUB) * SUB]
    for i in range(1, CHUNK // SUB):
        mats.append(jnp.broadcast_to(s < i * SUB, (8, CHUNK)))
    mats.append(jnp.ones((8, CHUNK), bool))
    sel = jnp.concatenate([m.astype(BF16) for m in mats], axis=0)
    rev = (s >= t).astype(BF16)
    return sel, rev


def _select_sums(sel, x):
    hi = x.astype(BF16)
    r1 = x - hi.astype(F32)
    mid = r1.astype(BF16)
    lo = (r1 - mid.astype(F32)).astype(BF16)
    return _dot(sel, hi, NN) + (_dot(sel, mid, NN) + _dot(sel, lo, NN))


def _hgrn_cums(sel, logf):
    nsub = CHUNK // SUB
    cums = _select_sums(sel, logf)
    g = cums[0:CHUNK]
    rrow = cums[CHUNK:2 * CHUNK]
    base = 2 * CHUNK
    rsel = [None] + [jnp.max(cums[base + 8 * (i - 1):base + 8 * i], axis=0, keepdims=True) for i in range(1, nsub)]
    gl = jnp.max(cums[base + 8 * (nsub - 1):base + 8 * nsub], axis=0, keepdims=True)
    return g, rrow, rsel, gl


def _hgrn_gates(p, lb, D):
    qpre, fpre, iv, gpre = p[:, 0:D], p[:, D:2 * D], p[:, 2 * D:3 * D], p[:, 3 * D:4 * D]
    sig = _sigmoid(fpre)
    forget = lb + (1.0 - lb) * sig
    key = 1.0 - forget
    logf = jnp.log(forget)
    sq = _sigmoid(qpre)
    qs = qpre * sq
    return qpre, sq, qs, sig, forget, key, logf, iv, gpre


def _hgrn_fwd(p, lb, gn, *, name, ride=None):
    T, D4 = p.shape
    D = D4 // 4
    H = D // HEAD
    nc = T // CHUNK
    nb = min(HGRN_CHUNKS_PER_STEP, nc)
    assert nc % nb == 0
    sel, _ = _hgrn_selectors()
    nsel = sel.shape[0]
    nsub = CHUNK // SUB
    heads = [slice(h * HEAD, (h + 1) * HEAD) for h in range(H)]

    def body(p_ref, lb_ref, gn_ref, sel_ref, og_ref, o_ref, s0_ref, st_ref):
        @pl.when(pl.program_id(0) == 0)
        def _():
            st_ref[...] = jnp.zeros_like(st_ref)

        row = lax.broadcasted_iota(jnp.int32, (CHUNK, CHUNK), 0)
        col = lax.broadcasted_iota(jnp.int32, (CHUNK, CHUNK), 1)
        gnv = gn_ref[...]
        lbv = lb_ref[...]
        selv = sel_ref[...]
        pre = []
        for cc in range(nb):
            pv = p_ref[cc * CHUNK:(cc + 1) * CHUNK, :]
            _, _, qs, _, _, key, logf, iv, gpre = _hgrn_gates(pv, lbv, D)
            g, rrow, rsel_all, gl = _hgrn_cums(selv, logf)
            qgb = (qs * jnp.exp(g)).astype(BF16)
            qtb = (qs * jnp.exp(g - rrow)).astype(BF16)
            ktb = [(key * jnp.exp(jnp.minimum((0.0 if r is None else r) - g, EXP_CLAMP))).astype(BF16)
                   for r in rsel_all]
            kdb = (key * jnp.exp(gl - g)).astype(BF16)
            vb = iv.astype(BF16)
            blocks = [[_dot(qtb[i * SUB:(i + 1) * SUB, sl], ktb[i][:, sl], NT) for i in range(nsub)]
                      for sl in heads]
            amats = [jnp.where(col <= row, jnp.concatenate(bl, axis=0), 0.0).astype(BF16) for bl in blocks]
            pre.append(dict(qgb=qgb, egl=jnp.exp(gl), gate=gpre * _sigmoid(gpre),
                            intra=[_dot(a, vb[:, sl], NN) for a, sl in zip(amats, heads)],
                            upd=[_dot(vb[:, sl], kdb[:, sl], TN) for sl in heads]))
        sts = [st_ref[sl, :] for sl in heads]
        for cc, d in enumerate(pre):
            rows = slice(cc * CHUNK, (cc + 1) * CHUNK)
            inter = [_dot(d["qgb"][:, sl], st.astype(BF16), NT) for sl, st in zip(heads, sts)]
            for h, sl in enumerate(heads):
                s0_ref[cc, sl, :] = sts[h]
                o = inter[h] + d["intra"][h]
                o_ref[rows, sl] = o
                r = lax.rsqrt(jnp.mean(o * o, axis=-1, keepdims=True) + RMS_EPS)
                og_ref[rows, sl] = (((o * r) * gnv) * d["gate"][:, sl]).astype(og_ref.dtype)
            sts = [st * d["egl"][:, sl] + u for st, sl, u in zip(sts, heads, d["upd"])]
        for sl, st in zip(heads, sts):
            st_ref[sl, :] = st

    rb = nb * CHUNK
    return _call(
        body, name=name, grid=(nc // nb,),
        in_specs=[pl.BlockSpec((rb, D4), lambda c: (c, 0)), pl.BlockSpec((1, D), lambda c: (0, 0)),
                  pl.BlockSpec((1, HEAD), lambda c: (0, 0)), pl.BlockSpec((nsel, CHUNK), lambda c: (0, 0))],
        out_specs=[pl.BlockSpec((rb, D), lambda c: (c, 0)), pl.BlockSpec((rb, D), lambda c: (c, 0)),
                   pl.BlockSpec((nb, D, HEAD), lambda c: (c, 0, 0))],
        out_shape=[jax.ShapeDtypeStruct((T, D), BF16), jax.ShapeDtypeStruct((T, D), F32),
                   jax.ShapeDtypeStruct((nc, D, HEAD), F32)],
        scratch_shapes=[pltpu.VMEM((D, HEAD), F32)], sem=("arbitrary",), ride=ride,
        args=(p, lb.reshape(1, D), gn.reshape(1, HEAD), sel))


def _hgrn_bwd(p, lb, gn, s0, o_saved, dog, *, name, ride=None):
    T, D4 = p.shape
    D = D4 // 4
    H = D // HEAD
    nc = T // CHUNK
    nb = min(HGRN_BWD_CHUNKS_PER_STEP, nc)
    assert nc % nb == 0
    nsteps = nc // nb
    sel, rev = _hgrn_selectors()
    nsel = sel.shape[0]
    nsub = CHUNK // SUB
    heads = [slice(h * HEAD, (h + 1) * HEAD) for h in range(H)]
    cat = lambda xs: jnp.concatenate(xs, axis=1)

    def body(p_ref, lb_ref, gn_ref, sel_ref, rev_ref, s0_ref, s1_ref, o_ref, dog_ref,
             dp_ref, dlb_ref, dgn_ref, dst_ref):
        @pl.when(pl.program_id(0) == 0)
        def _():
            dst_ref[...] = jnp.zeros_like(dst_ref)
            dlb_ref[...] = jnp.zeros_like(dlb_ref)
            dgn_ref[...] = jnp.zeros_like(dgn_ref)

        row = lax.broadcasted_iota(jnp.int32, (CHUNK, CHUNK), 0)
        col = lax.broadcasted_iota(jnp.int32, (CHUNK, CHUNK), 1)
        causal = col <= row
        gnv, lbv, selv, revv = gn_ref[...], lb_ref[...], sel_ref[...], rev_ref[...]
        dgn_acc = jnp.zeros((1, HEAD), F32)
        pre = []
        for cc in range(nb):
            rows = slice(cc * CHUNK, (cc + 1) * CHUNK)
            qpre, sq, qs, sig, forget, key, logf, iv, gpre = _hgrn_gates(p_ref[rows, :], lbv, D)
            g, rrow, rsel_all, gl = _hgrn_cums(selv, logf)
            eg = jnp.exp(g)
            eqr = jnp.exp(g - rrow)
            eis = [jnp.exp(jnp.minimum((0.0 if r is None else r) - g, EXP_CLAMP)) for r in rsel_all]
            ekd = jnp.exp(gl - g)
            qgb = (qs * eg).astype(BF16)
            qtb = (qs * eqr).astype(BF16)
            ktb = [(key * e).astype(BF16) for e in eis]
            kdb = (key * ekd).astype(BF16)
            vb = iv.astype(BF16)
            sg = _sigmoid(gpre)
            gate = gpre * sg
            dgate = sg * (1.0 + gpre * (1.0 - sg))
            dobs, dgpres = [], []
            for sl in heads:
                o = o_ref[rows, sl]
                r = lax.rsqrt(jnp.mean(o * o, axis=-1, keepdims=True) + RMS_EPS)
                n = o * r
                dog_h = dog_ref[rows, sl]
                d_on = dog_h * gate[:, sl]
                dgpres.append(dog_h * (n * gnv) * dgate[:, sl])
                dgn_acc = dgn_acc + jnp.sum(d_on * n, axis=0, keepdims=True)
                dn = d_on * gnv
                dobs.append((r * (dn - n * jnp.mean(dn * n, axis=-1, keepdims=True))).astype(BF16))
            blocks = [[_dot(qtb[i * SUB:(i + 1) * SUB, sl], ktb[i][:, sl], NT) for i in range(nsub)]
                      for sl in heads]
            amats = [jnp.where(causal, jnp.concatenate(bl, axis=0), 0.0).astype(BF16) for bl in blocks]
            dabs = [jnp.where(causal, _dot(dob, vb[:, sl], NT), 0.0).astype(BF16) for dob, sl in zip(dobs, heads)]
            dq_inter = [_dot(dob, s0_ref[cc, sl, :].astype(BF16), NN) for dob, sl in zip(dobs, heads)]
            dqt = [jnp.concatenate([_dot(dab[i * SUB:(i + 1) * SUB], ktb[i][:, sl], NN) for i in range(nsub)],
                                   axis=0) for dab, sl in zip(dabs, heads)]
            dkt = [[_dot(dab[i * SUB:(i + 1) * SUB], qtb[i * SUB:(i + 1) * SUB, sl], TN) for i in range(nsub)]
                   for dab, sl in zip(dabs, heads)]
            dv_intra = [_dot(a, dob, TN) for a, dob in zip(amats, dobs)]
            upd = [_dot(dob, qgb[:, sl], TN) for dob, sl in zip(dobs, heads)]
            dq = cat(dq_inter) * eg + cat(dqt) * eqr
            dk_intra = cat([dkt[h][0] for h in range(H)]) * eis[0]
            for i in range(1, nsub):
                dk_intra = dk_intra + cat([dkt[h][i] for h in range(H)]) * eis[i]
            s1 = [s0_ref[cc + 1, sl, :] if cc + 1 < nb else s1_ref[sl, :] for sl in heads]
            pre.append(dict(qpre=qpre, sq=sq, qs=qs, sig=sig, forget=forget, key=key, ekd=ekd, egl=jnp.exp(gl),
                            kdb=kdb, vb=vb, dq=dq, dk_intra=dk_intra, dv_intra=dv_intra, upd=upd, s1=s1,
                            dgpre=cat(dgpres)))
        dsts = [dst_ref[sl, :] for sl in heads]
        dlb_acc = jnp.zeros((1, D), F32)
        for cc in reversed(range(nb)):
            d = pre[cc]
            rows = slice(cc * CHUNK, (cc + 1) * CHUNK)
            dstb = [x.astype(BF16) for x in dsts]
            dk_state = cat([_dot(d["vb"][:, sl], x, NN) for sl, x in zip(heads, dstb)])
            dv = cat([dvi + _dot(d["kdb"][:, sl], x, NT) for dvi, sl, x in zip(d["dv_intra"], heads, dstb)])
            term = cat([jnp.sum(x * s, axis=0, keepdims=True) for x, s in zip(dsts, d["s1"])])
            dsts = [x * d["egl"][:, sl] + u for x, sl, u in zip(dsts, heads, d["upd"])]
            dk = d["dk_intra"] + dk_state * d["ekd"]
            dq = d["dq"]
            dg = d["qs"] * dq - d["key"] * dk
            dlogf = _select_sums(revv, dg) + term
            sgf = d["sig"]
            dforget = dlogf / d["forget"] - dk
            dlb_acc = dlb_acc + jnp.sum(dforget * (1.0 - sgf), axis=0, keepdims=True)
            sqv = d["sq"]
            dp_ref[rows, 0:D] = (dq * (sqv * (1.0 + d["qpre"] * (1.0 - sqv)))).astype(dp_ref.dtype)
            dp_ref[rows, D:2 * D] = (dforget * (1.0 - lbv) * (sgf * (1.0 - sgf))).astype(dp_ref.dtype)
            dp_ref[rows, 2 * D:3 * D] = dv.astype(dp_ref.dtype)
            dp_ref[rows, 3 * D:4 * D] = d["dgpre"].astype(dp_ref.dtype)
        for sl, x in zip(heads, dsts):
            dst_ref[sl, :] = x
        dlb_ref[...] += dlb_acc
        dgn_ref[...] += dgn_acc

    rb = nb * CHUNK
    rc = lambda c: nsteps - 1 - c
    return _call(
        body, name=name, grid=(nsteps,),
        in_specs=[pl.BlockSpec((rb, D4), lambda c: (rc(c), 0)), pl.BlockSpec((1, D), lambda c: (0, 0)),
                  pl.BlockSpec((1, HEAD), lambda c: (0, 0)), pl.BlockSpec((nsel, CHUNK), lambda c: (0, 0)),
                  pl.BlockSpec((CHUNK, CHUNK), lambda c: (0, 0)),
                  pl.BlockSpec((nb, D, HEAD), lambda c: (rc(c), 0, 0)),
                  pl.BlockSpec((None, D, HEAD), lambda c: (jnp.minimum((rc(c) + 1) * nb, nc - 1), 0, 0)),
                  pl.BlockSpec((rb, D), lambda c: (rc(c), 0)), pl.BlockSpec((rb, D), lambda c: (rc(c), 0))],
        out_specs=[pl.BlockSpec((rb, D4), lambda c: (rc(c), 0)), pl.BlockSpec((1, D), lambda c: (0, 0)),
                   pl.BlockSpec((1, HEAD), lambda c: (0, 0))],
        out_shape=[jax.ShapeDtypeStruct((T, D4), BF16), jax.ShapeDtypeStruct((1, D), F32),
                   jax.ShapeDtypeStruct((1, HEAD), F32)],
        scratch_shapes=[pltpu.VMEM((D, HEAD), F32)], sem=("arbitrary",), ride=ride,
        args=(p, lb.reshape(1, D), gn.reshape(1, HEAD), sel, rev, s0, s0, o_saved, dog))


def _rope_tables(positions):
    inv_freq = ROPE_THETA ** (-jnp.arange(0, ROPE, 2, dtype=F32) / ROPE)
    ang = positions.astype(F32)[:, None] * inv_freq
    cos, sin = jnp.cos(ang), jnp.sin(ang)
    z = jnp.zeros_like(cos)
    ctab = jnp.concatenate([cos, cos, z, z], axis=-1)
    s1 = jnp.concatenate([-sin, z, z, z], axis=-1)
    s2 = jnp.concatenate([z, sin, z, z], axis=-1)
    return ctab, s1, s2


def _rope(z, ct, s1, s2):
    return z * ct + pltpu.roll(z, 96, 1) * s1 + pltpu.roll(z, 32, 1) * s2


def _rope_t(d, ct, s1, s2):
    return d * ct + pltpu.roll(d * s1, 32, 1) + pltpu.roll(d * s2, 96, 1)


def _mla_prep_fwd(c, wq, wkv, ga_q, ga_kv, gq, gk, ct, s1, s2, *, name, tm=256):
    T, CW = c.shape
    R = (CW - LANES) // 2
    H = wq.shape[1] // QK_PAD
    tm = min(tm, T)

    def body(c_ref, wq_ref, wkv_ref, gaq_ref, gakv_ref, gq_ref, gk_ref, ct_ref, s1_ref, s2_ref,
             q_ref, k_ref, v_ref):
        cv = c_ref[...]
        cq, ckv, kr = cv[:, 0:R], cv[:, R:2 * R], cv[:, 2 * R:2 * R + LANES]
        rq = lax.rsqrt(jnp.mean(cq * cq, axis=-1, keepdims=True) + RMS_EPS)
        cqn = ((cq * rq) * gaq_ref[...]).astype(BF16)
        rk = lax.rsqrt(jnp.mean(ckv * ckv, axis=-1, keepdims=True) + RMS_EPS)
        ckvn = ((ckv * rk) * gakv_ref[...]).astype(BF16)
        qp = _dot(cqn, wq_ref[...], NN)
        kvp = _dot(ckvn, wkv_ref[...], NN)
        ctv, s1v, s2v = ct_ref[...], s1_ref[...], s2_ref[...]
        gqv, gkv = gq_ref[...], gk_ref[...]
        krs = jnp.sum(kr * kr, axis=-1, keepdims=True)
        for h in range(H):
            b = h * QK_PAD
            qn, qr = qp[:, b:b + HEAD], qp[:, b + HEAD:b + QK_PAD]
            ss = jnp.sum(qn * qn, axis=-1, keepdims=True) + jnp.sum(qr * qr, axis=-1, keepdims=True)
            rr = lax.rsqrt(ss * (1.0 / QK_HEAD) + RMS_EPS)
            q_ref[:, b:b + HEAD] = ((qn * rr) * gqv[:, 0:HEAD]).astype(q_ref.dtype)
            q_ref[:, b + HEAD:b + QK_PAD] = _rope((qr * rr) * gqv[:, HEAD:QK_PAD], ctv, s1v, s2v).astype(q_ref.dtype)
            kn, vv = kvp[:, b:b + HEAD], kvp[:, b + HEAD:b + QK_PAD]
            ssk = jnp.sum(kn * kn, axis=-1, keepdims=True) + krs
            rrk = lax.rsqrt(ssk * (1.0 / QK_HEAD) + RMS_EPS)
            k_ref[:, b:b + HEAD] = ((kn * rrk) * gkv[:, 0:HEAD]).astype(k_ref.dtype)
            k_ref[:, b + HEAD:b + QK_PAD] = _rope((kr * rrk) * gkv[:, HEAD:QK_PAD], ctv, s1v, s2v).astype(k_ref.dtype)
            v_ref[:, h * HEAD:(h + 1) * HEAD] = vv.astype(v_ref.dtype)

    full = lambda shape: pl.BlockSpec(shape, lambda i: (0, 0))
    tok = lambda w: pl.BlockSpec((tm, w), lambda i: (i, 0))
    return pl.pallas_call(
        body, name=name, grid=(T // tm,),
        in_specs=[tok(CW), full(wq.shape), full(wkv.shape), full((1, R)), full((1, R)), full((1, QK_PAD)),
                  full((1, QK_PAD)), tok(LANES), tok(LANES), tok(LANES)],
        out_specs=[tok(H * QK_PAD), tok(H * QK_PAD), tok(H * HEAD)],
        out_shape=[jax.ShapeDtypeStruct((T, H * QK_PAD), BF16), jax.ShapeDtypeStruct((T, H * QK_PAD), BF16),
                   jax.ShapeDtypeStruct((T, H * HEAD), BF16)],
        compiler_params=_params(("parallel",)),
    )(c, wq, wkv, ga_q.reshape(1, R), ga_kv.reshape(1, R), gq, gk, ct, s1, s2)


def _mla_prep_bwd(c, wq, wkv, ga_q, ga_kv, gq, gk, ct, s1, s2, dq, dk, dv, *, name, tm=256):
    T, CW = c.shape
    R = (CW - LANES) // 2
    H = wq.shape[1] // QK_PAD
    tm = min(tm, T)

    def body(c_ref, wq_ref, wkv_ref, gaq_ref, gakv_ref, gq_ref, gk_ref, ct_ref, s1_ref, s2_ref,
             dq_ref, dk_ref, dv_ref,
             dc_ref, dwq_ref, dwkv_ref, dgaq_ref, dgakv_ref, dgq_ref, dgk_ref, dqp_ref, dkvp_ref):
        i = pl.program_id(0)

        @pl.when(i == 0)
        def _():
            for ref in (dwq_ref, dwkv_ref, dgaq_ref, dgakv_ref, dgq_ref, dgk_ref):
                ref[...] = jnp.zeros_like(ref)

        cv = c_ref[...]
        cq, ckv, kr = cv[:, 0:R], cv[:, R:2 * R], cv[:, 2 * R:2 * R + LANES]
        rq = lax.rsqrt(jnp.mean(cq * cq, axis=-1, keepdims=True) + RMS_EPS)
        nq = cq * rq
        cqn = (nq * gaq_ref[...]).astype(BF16)
        rk = lax.rsqrt(jnp.mean(ckv * ckv, axis=-1, keepdims=True) + RMS_EPS)
        nkv = ckv * rk
        ckvn = (nkv * gakv_ref[...]).astype(BF16)
        qp = _dot(cqn, wq_ref[...], NN)
        kvp = _dot(ckvn, wkv_ref[...], NN)
        ctv, s1v, s2v = ct_ref[...], s1_ref[...], s2_ref[...]
        gqv, gkv = gq_ref[...], gk_ref[...]
        krs = jnp.sum(kr * kr, axis=-1, keepdims=True)
        dkr = jnp.zeros((tm, LANES), F32)
        dgq_n = jnp.zeros((1, HEAD), F32)
        dgq_r = jnp.zeros((1, HEAD), F32)
        dgk_n = jnp.zeros((1, HEAD), F32)
        dgk_r = jnp.zeros((1, HEAD), F32)
        for h in range(H):
            b = h * QK_PAD
            qn, qr = qp[:, b:b + HEAD], qp[:, b + HEAD:b + QK_PAD]
            ss = jnp.sum(qn * qn, axis=-1, keepdims=True) + jnp.sum(qr * qr, axis=-1, keepdims=True)
            rr = lax.rsqrt(ss * (1.0 / QK_HEAD) + RMS_EPS)
            un, ur = qn * rr, qr * rr
            dzn = dq_ref[:, b:b + HEAD]
            dzr = _rope_t(dq_ref[:, b + HEAD:b + QK_PAD], ctv, s1v, s2v)
            dgq_n = dgq_n + jnp.sum(dzn * un, axis=0, keepdims=True)
            dgq_r = dgq_r + jnp.sum(dzr * ur, axis=0, keepdims=True)
            dun, dur = dzn * gqv[:, 0:HEAD], dzr * gqv[:, HEAD:QK_PAD]
            m = (jnp.sum(dun * un, axis=-1, keepdims=True) + jnp.sum(dur * ur, axis=-1, keepdims=True)) \
                * (1.0 / QK_HEAD)
            dqp_ref[:, b:b + HEAD] = (rr * (dun - un * m)).astype(BF16)
            dqp_ref[:, b + HEAD:b + QK_PAD] = (rr * (dur - ur * m)).astype(BF16)
            kn = kvp[:, b:b + HEAD]
            ssk = jnp.sum(kn * kn, axis=-1, keepdims=True) + krs
            rrk = lax.rsqrt(ssk * (1.0 / QK_HEAD) + RMS_EPS)
            vn, vr = kn * rrk, kr * rrk
            dyn = dk_ref[:, b:b + HEAD]
            dyr = _rope_t(dk_ref[:, b + HEAD:b + QK_PAD], ctv, s1v, s2v)
            dgk_n = dgk_n + jnp.sum(dyn * vn, axis=0, keepdims=True)
            dgk_r = dgk_r + jnp.sum(dyr * vr, axis=0, keepdims=True)
            dvn, dvr = dyn * gkv[:, 0:HEAD], dyr * gkv[:, HEAD:QK_PAD]
            mk = (jnp.sum(dvn * vn, axis=-1, keepdims=True) + jnp.sum(dvr * vr, axis=-1, keepdims=True)) \
                * (1.0 / QK_HEAD)
            dkvp_ref[:, b:b + HEAD] = (rrk * (dvn - vn * mk)).astype(BF16)
            dkr = dkr + rrk * (dvr - vr * mk)
            dkvp_ref[:, b + HEAD:b + QK_PAD] = dv_ref[:, h * HEAD:(h + 1) * HEAD].astype(BF16)
        dgq_ref[:, 0:HEAD] += dgq_n
        dgq_ref[:, HEAD:QK_PAD] += dgq_r
        dgk_ref[:, 0:HEAD] += dgk_n
        dgk_ref[:, HEAD:QK_PAD] += dgk_r
        dqp = dqp_ref[...]
        dkvp = dkvp_ref[...]
        dwq_ref[...] += _dot(cqn, dqp, TN)
        dwkv_ref[...] += _dot(ckvn, dkvp, TN)
        dcqn = _dot(dqp, wq_ref[...], NT)
        dckvn = _dot(dkvp, wkv_ref[...], NT)
        dgaq_ref[...] += jnp.sum(dcqn * nq, axis=0, keepdims=True)
        dgakv_ref[...] += jnp.sum(dckvn * nkv, axis=0, keepdims=True)
        dnq = dcqn * gaq_ref[...]
        dnkv = dckvn * gakv_ref[...]
        dc_ref[:, 0:R] = (rq * (dnq - nq * jnp.mean(dnq * nq, axis=-1, keepdims=True))).astype(dc_ref.dtype)
        dc_ref[:, R:2 * R] = (rk * (dnkv - nkv * jnp.mean(dnkv * nkv, axis=-1, keepdims=True))).astype(dc_ref.dtype)
        dc_ref[:, 2 * R:2 * R + LANES] = dkr.astype(dc_ref.dtype)

    full = lambda shape: pl.BlockSpec(shape, lambda i: (0, 0))
    tok = lambda w: pl.BlockSpec((tm, w), lambda i: (i, 0))
    return pl.pallas_call(
        body, name=name, grid=(T // tm,),
        in_specs=[tok(CW), full(wq.shape), full(wkv.shape), full((1, R)), full((1, R)), full((1, QK_PAD)),
                  full((1, QK_PAD)), tok(LANES), tok(LANES), tok(LANES),
                  tok(H * QK_PAD), tok(H * QK_PAD), tok(H * HEAD)],
        out_specs=[tok(CW), full(wq.shape), full(wkv.shape), full((1, R)), full((1, R)), full((1, QK_PAD)),
                   full((1, QK_PAD))],
        out_shape=[jax.ShapeDtypeStruct((T, CW), BF16), jax.ShapeDtypeStruct(wq.shape, F32),
                   jax.ShapeDtypeStruct(wkv.shape, F32), jax.ShapeDtypeStruct((1, R), F32),
                   jax.ShapeDtypeStruct((1, R), F32), jax.ShapeDtypeStruct((1, QK_PAD), F32),
                   jax.ShapeDtypeStruct((1, QK_PAD), F32)],
        scratch_shapes=[pltpu.VMEM((tm, H * QK_PAD), BF16), pltpu.VMEM((tm, H * QK_PAD), BF16)],
        compiler_params=_params(("arbitrary",)),
    )(c, wq, wkv, ga_q.reshape(1, R), ga_kv.reshape(1, R), gq, gk, ct, s1, s2, dq, dk, dv)


NEG = -1e30
LOG2E = 1.4426950408889634


def _attn_fwd(q, k, v, *, name, tb=512, hp=2, ride=None):
    T = q.shape[0]
    H = q.shape[1] // QK_PAD
    tb = min(tb, T)
    nq = T // tb
    scale = QK_HEAD ** -0.5
    c2 = scale * LOG2E
    assert H % hp == 0

    def body(q_ref, k_ref, v_ref, ot_ref, lse_ref, m_ref, l_ref, acc_ref):
        i = pl.program_id(1)
        m_ref[...] = jnp.full_like(m_ref, NEG)
        l_ref[...] = jnp.zeros_like(l_ref)
        acc_ref[...] = jnp.zeros_like(acc_ref)

        def step(j, masked):
            off = pl.multiple_of(j * tb, tb)
            sts = [_dot(k_ref[pl.ds(off, tb), hh * QK_PAD:(hh + 1) * QK_PAD],
                        q_ref[:, hh * QK_PAD:(hh + 1) * QK_PAD], NT) for hh in range(hp)]
            for hh in range(hp):
                vs = slice(hh * HEAD, (hh + 1) * HEAD)
                vb = v_ref[pl.ds(off, tb), vs]
                st = sts[hh]
                if masked:
                    kpos = lax.broadcasted_iota(jnp.int32, (tb, tb), 0)
                    qpos = lax.broadcasted_iota(jnp.int32, (tb, tb), 1)
                    st = jnp.where(kpos <= qpos, st, NEG)
                m_old = m_ref[hh]
                m_new = jnp.maximum(m_old, jnp.max(st, axis=0, keepdims=True))
                alpha = jnp.exp2((m_old - m_new) * c2)
                pt = jnp.exp2((st - m_new) * c2)
                l_ref[hh] = l_ref[hh] * alpha + jnp.sum(pt, axis=0, keepdims=True)
                acc_ref[vs, :] = acc_ref[vs, :] * alpha + _dot(vb, pt.astype(BF16), TN)
                m_ref[hh] = m_new

        def loop_body(j, carry):
            step(j, False)
            return carry

        lax.fori_loop(0, i, loop_body, 0)
        step(i, True)
        for hh in range(hp):
            vs = slice(hh * HEAD, (hh + 1) * HEAD)
            l = l_ref[hh]
            ot_ref[vs, :] = (acc_ref[vs, :] / l).astype(ot_ref.dtype)
            lse_ref[hh] = m_ref[hh] * scale + jnp.log(l)

    return _call(
        body, name=name, grid=(H // hp, nq),
        in_specs=[pl.BlockSpec((tb, hp * QK_PAD), lambda g, i: (i, g)),
                  pl.BlockSpec((T, hp * QK_PAD), lambda g, i: (0, g)),
                  pl.BlockSpec((T, hp * HEAD), lambda g, i: (0, g))],
        out_specs=[pl.BlockSpec((None, hp * HEAD, tb), lambda g, i: (i, g, 0)),
                   pl.BlockSpec((hp, None, 1, tb), lambda g, i: (g, i, 0, 0))],
        out_shape=[jax.ShapeDtypeStruct((nq, H * HEAD, tb), BF16), jax.ShapeDtypeStruct((H, nq, 1, tb), F32)],
        scratch_shapes=[pltpu.VMEM((hp, 1, tb), F32), pltpu.VMEM((hp, 1, tb), F32),
                        pltpu.VMEM((hp * HEAD, tb), F32)],
        sem=("parallel", "arbitrary"), ride=ride, args=(q, k, v))


def _attn_bwd(q, k, v, ot, lse, dot_, *, name, ride=None):
    T = q.shape[0]
    H = q.shape[1] // QK_PAD
    nq, _, tb = ot.shape
    scale = QK_HEAD ** -0.5
    c2 = scale * LOG2E

    def body(q_ref, k_ref, v_ref, ot_ref, lse_ref, dot_ref, dq_ref, dk_ref, dv_ref, dka_ref, dva_ref):
        j = pl.program_id(1)

        @pl.when(j == 0)
        def _():
            dq_ref[...] = jnp.zeros_like(dq_ref)

        kb = k_ref[...]
        vb = v_ref[...]

        def step(i, masked):
            off = i * tb if isinstance(i, int) else pl.multiple_of(i * tb, tb)
            qb = q_ref[pl.ds(off, tb), :]
            dob = dot_ref[i]
            ob = ot_ref[i]
            st = _dot(kb, qb, NT)
            if masked:
                kpos = lax.broadcasted_iota(jnp.int32, (tb, tb), 0)
                qpos = lax.broadcasted_iota(jnp.int32, (tb, tb), 1)
                st = jnp.where(kpos <= qpos, st, NEG)
            pt = jnp.exp2(st * c2 - lse_ref[i] * LOG2E)
            dpt = _dot(vb, dob, NN)
            delta = jnp.sum(dob.astype(F32) * ob.astype(F32), axis=0, keepdims=True)
            dst = (pt * (dpt - delta)).astype(BF16)
            dq_ref[pl.ds(off, tb), :] += _dot(dst, kb, TN)
            return _dot(dst, qb, NN), _dot(pt.astype(BF16), dob, NT)

        dk0, dv0 = step(j, True)
        dka_ref[...] = dk0
        dva_ref[...] = dv0

        rest = nq - 1 - j

        def pair_body(t, carry):
            i0 = j + 1 + 2 * t
            dk1, dv1 = step(i0, False)
            dk2, dv2 = step(i0 + 1, False)
            dka_ref[...] += dk1 + dk2
            dva_ref[...] += dv1 + dv2
            return carry

        lax.fori_loop(0, rest // 2, pair_body, 0)

        @pl.when(rest % 2 == 1)
        def _():
            dk1, dv1 = step(nq - 1, False)
            dka_ref[...] += dk1
            dva_ref[...] += dv1

        dk_ref[...] = dka_ref[...] * scale
        dv_ref[...] = dva_ref[...]

        @pl.when(j == nq - 1)
        def _():
            dq_ref[...] = dq_ref[...] * scale

    return _call(
        body, name=name, grid=(H, nq),
        in_specs=[pl.BlockSpec((T, QK_PAD), lambda h, j: (0, h)), pl.BlockSpec((tb, QK_PAD), lambda h, j: (j, h)),
                  pl.BlockSpec((tb, HEAD), lambda h, j: (j, h)),
                  pl.BlockSpec((nq, HEAD, tb), lambda h, j: (0, h, 0)),
                  pl.BlockSpec((None, nq, 1, tb), lambda h, j: (h, 0, 0, 0)),
                  pl.BlockSpec((nq, HEAD, tb), lambda h, j: (0, h, 0))],
        out_specs=[pl.BlockSpec((T, QK_PAD), lambda h, j: (0, h)), pl.BlockSpec((tb, QK_PAD), lambda h, j: (j, h)),
                   pl.BlockSpec((tb, HEAD), lambda h, j: (j, h))],
        out_shape=[jax.ShapeDtypeStruct((T, H * QK_PAD), F32), jax.ShapeDtypeStruct((T, H * QK_PAD), F32),
                   jax.ShapeDtypeStruct((T, H * HEAD), F32)],
        scratch_shapes=[pltpu.VMEM((tb, QK_PAD), F32), pltpu.VMEM((tb, HEAD), F32)],
        sem=("parallel", "arbitrary"), ride=ride, args=(q, k, v, ot, lse, dot_))


def _conv_taps(u, prev6, prev7):
    rows = lax.broadcasted_iota(jnp.int32, (u.shape[0], 1), 0)
    u1 = jnp.where(rows >= 1, pltpu.roll(u, 1, 0), prev7)
    u2 = jnp.where(rows >= 2, pltpu.roll(u, 2, 0), jnp.where(rows == 0, prev6, prev7))
    return u2, u1


def _conv_taps_ahead(d, next0, next1):
    tm = d.shape[0]
    rows = lax.broadcasted_iota(jnp.int32, (tm, 1), 0)
    d1 = jnp.where(rows < tm - 1, pltpu.roll(d, tm - 1, 0), next0)
    d2 = jnp.where(rows < tm - 2, pltpu.roll(d, tm - 2, 0), jnp.where(rows == tm - 2, next0, next1))
    return d1, d2


def _ffn_up_fwd(h, w_up, conv_w, conv_b, *, name, tm=512, ride=None):
    T, D = h.shape
    ns, _, fs = w_up.shape
    nh = ns // 2
    tm = min(tm, T)

    def body(h_ref, wg_ref, wu_ref, cwg_ref, cwu_ref, cbg_ref, cbu_ref, a_ref, u_ref, cg_ref, cu_ref):
        i = pl.program_id(1)

        @pl.when(i == 0)
        def _():
            cg_ref[...] = jnp.zeros_like(cg_ref)
            cu_ref[...] = jnp.zeros_like(cu_ref)

        hv = h_ref[...]
        ys = []
        for idx, (w_ref, cw_ref, cb_ref, carry) in enumerate(
                ((wg_ref, cwg_ref, cbg_ref, cg_ref), (wu_ref, cwu_ref, cbu_ref, cu_ref))):
            u = _dot(hv, w_ref[...], NN)
            u_ref[idx] = u.astype(u_ref.dtype)
            u2, u1 = _conv_taps(u, carry[6:7, :], carry[7:8, :])
            y = cb_ref[...] + u2 * cw_ref[0:1, :]
            y = y + u1 * cw_ref[1:2, :]
            y = y + u * cw_ref[2:3, :]
            ys.append(y)
            carry[...] = u[tm - 8:tm, :]
        yg, yu = ys
        a_ref[...] = ((yg * _sigmoid(yg)) * yu).astype(a_ref.dtype)

    shard = lambda r, off: pl.BlockSpec((None, r, fs), lambda j, i: (j + off, 0, 0))
    return _call(
        body, name=name, grid=(nh, T // tm),
        in_specs=[pl.BlockSpec((tm, D), lambda j, i: (i, 0)), shard(D, 0), shard(D, nh),
                  shard(3, 0), shard(3, nh), shard(1, 0), shard(1, nh)],
        out_specs=[pl.BlockSpec((None, tm, fs), lambda j, i: (j, i, 0)),
                   pl.BlockSpec((2, None, tm, fs), lambda j, i: (0, j, i, 0))],
        out_shape=[jax.ShapeDtypeStruct((nh, T, fs), BF16), jax.ShapeDtypeStruct((2, nh, T, fs), BF16)],
        scratch_shapes=[pltpu.VMEM((8, fs), F32), pltpu.VMEM((8, fs), F32)],
        sem=("parallel", "arbitrary"), ride=ride, args=(h, w_up, w_up, conv_w, conv_w, conv_b, conv_b))


def _ffn_act_bwd(dxo, w_down, u, conv_w, conv_b, *, name, tm=512):
    T, D = dxo.shape
    _, nh, _, fs = u.shape
    tm = min(tm, T)
    nt = T // tm
    hb = tm // 8

    def body(dx_ref, wd_ref, u_ref, up_ref, cwg_ref, cwu_ref, cbg_ref, cbu_ref,
             du_ref, dcw_ref, dcb_ref, cg_ref, cu_ref):
        i = pl.program_id(1)
        tile = nt - 1 - i

        @pl.when(i == 0)
        def _():
            cg_ref[...] = jnp.zeros_like(cg_ref)
            cu_ref[...] = jnp.zeros_like(cu_ref)
            dcw_ref[...] = jnp.zeros_like(dcw_ref)
            dcb_ref[...] = jnp.zeros_like(dcb_ref)

        da = _dot(dx_ref[...].astype(BF16), wd_ref[...], NT)
        has_prev = (tile > 0).astype(F32)
        us, ys, shifted = [], [], []
        for idx, (cw_ref, cb_ref) in enumerate(((cwg_ref, cbg_ref), (cwu_ref, cbu_ref))):
            uv = u_ref[idx].astype(F32)
            p6 = up_ref[idx, 6:7, :].astype(F32) * has_prev
            p7 = up_ref[idx, 7:8, :].astype(F32) * has_prev
            u2, u1 = _conv_taps(uv, p6, p7)
            y = cb_ref[...] + u2 * cw_ref[0:1, :]
            y = y + u1 * cw_ref[1:2, :]
            y = y + uv * cw_ref[2:3, :]
            us.append(uv)
            ys.append(y)
            shifted.append((u2, u1))
        yg, yu = ys
        sg = _sigmoid(yg)
        dys = (da * yu * (sg * (1.0 + yg * (1.0 - sg))), da * (yg * sg))
        for idx, (cw_ref, carry) in enumerate(((cwg_ref, cg_ref), (cwu_ref, cu_ref))):
            dy = dys[idx]
            u2, u1 = shifted[idx]
            dcb_ref[idx] += jnp.sum(dy, axis=0, keepdims=True)
            dcw_ref[idx, 0:1, :] += jnp.sum(dy * u2, axis=0, keepdims=True)
            dcw_ref[idx, 1:2, :] += jnp.sum(dy * u1, axis=0, keepdims=True)
            dcw_ref[idx, 2:3, :] += jnp.sum(dy * us[idx], axis=0, keepdims=True)
            d1, d2 = _conv_taps_ahead(dy, carry[0:1, :], carry[1:2, :])
            du = dy * cw_ref[2:3, :] + d1 * cw_ref[1:2, :] + d2 * cw_ref[0:1, :]
            du_ref[idx] = du.astype(du_ref.dtype)
            carry[...] = dy[0:8, :]

    rt = lambda i: nt - 1 - i
    shard = lambda r, off: pl.BlockSpec((None, r, fs), lambda j, i: (j + off, 0, 0))
    return pl.pallas_call(
        body, name=name, grid=(nh, nt),
        in_specs=[pl.BlockSpec((tm, D), lambda j, i: (rt(i), 0)), pl.BlockSpec((fs, D), lambda j, i: (j, 0)),
                  pl.BlockSpec((2, None, tm, fs), lambda j, i: (0, j, rt(i), 0)),
                  pl.BlockSpec((2, None, 8, fs), lambda j, i: (0, j, jnp.maximum(rt(i) * hb - 1, 0), 0)),
                  shard(3, 0), shard(3, nh), shard(1, 0), shard(1, nh)],
        out_specs=[pl.BlockSpec((2, None, tm, fs), lambda j, i: (0, j, rt(i), 0)),
                   pl.BlockSpec((2, None, 3, fs), lambda j, i: (0, j, 0, 0)),
                   pl.BlockSpec((2, None, 1, fs), lambda j, i: (0, j, 0, 0))],
        out_shape=[jax.ShapeDtypeStruct((2, nh, T, fs), BF16), jax.ShapeDtypeStruct((2, nh, 3, fs), F32),
                   jax.ShapeDtypeStruct((2, nh, 1, fs), F32)],
        scratch_shapes=[pltpu.VMEM((8, fs), F32), pltpu.VMEM((8, fs), F32)],
        compiler_params=_params(("parallel", "arbitrary")),
    )(dxo, w_down, u, u, conv_w, conv_w, conv_b, conv_b)


def _pad_cols(w, n):
    return jnp.pad(w, [(0, 0)] * (w.ndim - 1) + [(0, n - w.shape[-1])])


def _q_up_padded(w):
    R = w.shape[0]
    H = w.shape[1] // QK_HEAD
    return _pad_cols(w.reshape(R, H, QK_HEAD), QK_PAD).reshape(R, H * QK_PAD)


def _q_up_unpadded(w):
    R = w.shape[0]
    H = w.shape[1] // QK_PAD
    return w.reshape(R, H, QK_PAD)[:, :, :QK_HEAD].reshape(R, H * QK_HEAD)


def _xchg_copies(src_refs, out_refs, kinds, send_sems, recv_sems, local_sems):
    x, y, c = lax.axis_index("x"), lax.axis_index("y"), lax.axis_index("c")
    me = 4 * x + 2 * y + c
    copies = []
    for b, kind in enumerate(kinds):
        gather = kind == "gather"
        own = src_refs[b] if gather else src_refs[b].at[me]
        copies.append(pltpu.make_async_copy(own, out_refs[b].at[me], local_sems.at[b]))
        for kk in range(1, N_DEV):
            px = 1 - x if kk & 4 else x
            py = 1 - y if kk & 2 else y
            pc = 1 - c if kk & 1 else c
            peer = 4 * px + 2 * py + pc
            src = src_refs[b] if gather else src_refs[b].at[peer]
            copies.append(pltpu.make_async_remote_copy(
                src_ref=src, dst_ref=out_refs[b].at[me],
                send_sem=send_sems.at[b * (N_DEV - 1) + kk - 1],
                recv_sem=recv_sems.at[b * (N_DEV - 1) + kk - 1],
                device_id=(px, py, pc), device_id_type=pl.DeviceIdType.MESH))
    return copies


def _xchg_out_shapes(srcs, kinds):
    return [jax.ShapeDtypeStruct((N_DEV,) + s.shape if kind == "gather" else s.shape, s.dtype)
            for s, kind in zip(srcs, kinds)]


def _xchg_scratch(n):
    return [pltpu.SemaphoreType.DMA((n * (N_DEV - 1),)), pltpu.SemaphoreType.DMA((n * (N_DEV - 1),)),
            pltpu.SemaphoreType.DMA((n,))]


def _exchange(srcs, kinds, *, name):
    n = len(srcs)

    def body(*refs):
        copies = _xchg_copies(refs[:n], refs[n:2 * n], kinds, *refs[2 * n:])
        for cp in copies:
            cp.start()
        for cp in copies:
            cp.wait()

    hbm = pl.BlockSpec(memory_space=pl.ANY)
    return pl.pallas_call(
        body, name=name, in_specs=[hbm] * n, out_specs=[hbm] * n, out_shape=_xchg_out_shapes(srcs, kinds),
        scratch_shapes=_xchg_scratch(n),
    )(*srcs)


def _call(body, *, name, grid, in_specs, out_specs, out_shape, scratch_shapes, args, sem, ride=None):
    if ride is None:
        outs = pl.pallas_call(body, name=name, grid=grid, in_specs=in_specs, out_specs=out_specs,
                              out_shape=out_shape, scratch_shapes=scratch_shapes,
                              compiler_params=_params(sem))(*args)
        return list(outs), []
    srcs, kinds = ride
    n_in, n_out, n_sc, nx = len(in_specs), len(out_specs), len(scratch_shapes), len(srcs)

    def wrapped(*refs):
        ins, xs = refs[:n_in], refs[n_in:n_in + nx]
        o0 = n_in + nx
        outs, xo = refs[o0:o0 + n_out], refs[o0 + n_out:o0 + n_out + nx]
        s0 = o0 + n_out + nx
        sc, sems = refs[s0:s0 + n_sc], refs[s0 + n_sc:]
        first = functools.reduce(jnp.logical_and, [pl.program_id(d) == 0 for d in range(len(grid))])
        last = functools.reduce(jnp.logical_and, [pl.program_id(d) == grid[d] - 1 for d in range(len(grid))])

        @pl.when(first)
        def _():
            for cp in _xchg_copies(xs, xo, kinds, *sems):
                cp.start()

        body(*ins, *outs, *sc)

        @pl.when(last)
        def _():
            for cp in _xchg_copies(xs, xo, kinds, *sems):
                cp.wait()

    hbm = pl.BlockSpec(memory_space=pl.ANY)
    outs = pl.pallas_call(
        wrapped, name=name, grid=grid, in_specs=list(in_specs) + [hbm] * nx,
        out_specs=list(out_specs) + [hbm] * nx, out_shape=list(out_shape) + _xchg_out_shapes(srcs, kinds),
        scratch_shapes=list(scratch_shapes) + _xchg_scratch(nx),
        compiler_params=_params(("arbitrary",) * len(grid)),
    )(*args, *srcs)
    return list(outs[:n_out]), list(outs[n_out:])


def _sum_slots(parts, *, name):
    _, Rr, C = parts.shape

    def body(p_ref, o_ref):
        acc = p_ref[0].astype(F32)
        for d in range(1, N_DEV):
            acc = acc + p_ref[d].astype(F32)
        o_ref[...] = acc

    return pl.pallas_call(
        body, name=name, grid=(1,),
        in_specs=[pl.BlockSpec((N_DEV, Rr, C), lambda i: (0, 0, 0))],
        out_specs=pl.BlockSpec((Rr, C), lambda i: (0, 0)),
        out_shape=jax.ShapeDtypeStruct((Rr, C), F32),
        compiler_params=_params(("arbitrary",)),
    )(parts)


def _row_tile(rows, cap=512):
    if rows <= cap:
        return rows
    d = (cap // 8) * 8
    while d >= 8:
        if rows % d == 0:
            return d
        d -= 8
    raise ValueError(f"no row tile for {rows}")


def _adamw(parts, w, m, v, *, name):
    S, Rr, C = parts.shape
    tr = _row_tile(Rr)
    c1 = 1.0 - ADAM_B1 ** ADAM_STEP
    c2 = 1.0 - ADAM_B2 ** ADAM_STEP

    def body(p_ref, w_ref, m_ref, v_ref, g_ref, d_ref, nm_ref, nv_ref):
        g = p_ref[0].astype(F32)
        for d in range(1, S):
            g = g + p_ref[d].astype(F32)
        mm = ADAM_B1 * m_ref[...] + (1.0 - ADAM_B1) * g
        vv = ADAM_B2 * v_ref[...] + (1.0 - ADAM_B2) * (g * g)
        m_hat = mm / c1
        v_hat = vv / c2
        g_ref[...] = g
        d_ref[...] = -ADAM_LR * (m_hat / (jnp.sqrt(v_hat) + ADAM_EPS) + ADAM_WD * w_ref[...])
        nm_ref[...] = mm
        nv_ref[...] = vv

    spec = pl.BlockSpec((tr, C), lambda i: (i, 0))
    shape = jax.ShapeDtypeStruct((Rr, C), F32)
    return pl.pallas_call(
        body, name=name, grid=(Rr // tr,),
        in_specs=[pl.BlockSpec((S, tr, C), lambda i: (0, i, 0)), spec, spec, spec],
        out_specs=[spec] * 4, out_shape=[shape] * 4,
        compiler_params=_params(("parallel",)),
    )(parts, w, m, v)


def _pack(arrs, dtype, row_mult):
    flat = jnp.concatenate([a.reshape(-1).astype(dtype) for a in arrs])
    per = row_mult * PACK_COLS
    total = -(-flat.shape[0] // per) * per
    return jnp.pad(flat, (0, total - flat.shape[0])).reshape(total // PACK_COLS, PACK_COLS)


def _unpack(packed, shapes, lead=()):
    flat = packed.reshape(lead + (-1,))
    out, off = [], 0
    for shp in shapes:
        n = 1
        for d in shp:
            n *= d
        out.append(flat[..., off:off + n].reshape(lead + tuple(shp)))
        off += n
    return out


HGRN_W = ("hgrn_w_in", "hgrn_w_out")
MLA_W = ("mla_w_in", "mla_w_q_up", "mla_w_kv_up", "mla_w_out")
FFN_W = ("ffn_w_up", "ffn_w_down")
BIG = HGRN_W + MLA_W + FFN_W
SMALL_SHARDED = {"ffn_conv_w": 2, "mla_q_a_norm": 1, "mla_kv_a_norm": 1}
REPLICATED = ["norm_mix", "norm_ffn", "hgrn_lower_bounds", "hgrn_out_norm", "mla_q_norm", "mla_k_norm",
              "ffn_conv_b"]
WEIGHTS = ["norm_mix", "norm_ffn", "hgrn_w_in", "hgrn_lower_bounds", "hgrn_out_norm", "hgrn_w_out", "mla_w_in",
           "mla_q_a_norm", "mla_w_q_up", "mla_kv_a_norm", "mla_w_kv_up", "mla_q_norm", "mla_k_norm", "mla_w_out",
           "ffn_w_up", "ffn_conv_w", "ffn_conv_b", "ffn_w_down"]


def _shards_to_cols(g):
    return g.transpose(1, 0, 2).reshape(g.shape[1], N_DEV * g.shape[2])


def _cols_to_shards(w):
    R = w.shape[0]
    return w.reshape(R, N_DEV, w.shape[1] // N_DEV).transpose(1, 0, 2)


def kernel(x, positions, norm_mix, norm_ffn, hgrn_w_in, hgrn_lower_bounds, hgrn_out_norm, hgrn_w_out, mla_w_in, mla_q_a_norm, mla_w_q_up, mla_kv_a_norm, mla_w_kv_up, mla_q_norm, mla_k_norm, mla_w_out, ffn_w_up, ffn_conv_w, ffn_conv_b, ffn_w_down, loss_target, m_norm_mix, m_norm_ffn, m_hgrn_w_in, m_hgrn_lower_bounds, m_hgrn_out_norm, m_hgrn_w_out, m_mla_w_in, m_mla_q_a_norm, m_mla_w_q_up, m_mla_kv_a_norm, m_mla_w_kv_up, m_mla_q_norm, m_mla_k_norm, m_mla_w_out, m_ffn_w_up, m_ffn_conv_w, m_ffn_conv_b, m_ffn_w_down, v_norm_mix, v_norm_ffn, v_hgrn_w_in, v_hgrn_lower_bounds, v_hgrn_out_norm, v_hgrn_w_out, v_mla_w_in, v_mla_q_a_norm, v_mla_w_q_up, v_mla_kv_a_norm, v_mla_w_kv_up, v_mla_q_norm, v_mla_k_norm, v_mla_w_out, v_ffn_w_up, v_ffn_conv_w, v_ffn_conv_b, v_ffn_w_down):
    local = dict(norm_mix=norm_mix, norm_ffn=norm_ffn, hgrn_w_in=hgrn_w_in, hgrn_lower_bounds=hgrn_lower_bounds,
                 hgrn_out_norm=hgrn_out_norm, hgrn_w_out=hgrn_w_out, mla_w_in=mla_w_in, mla_q_a_norm=mla_q_a_norm,
                 mla_w_q_up=mla_w_q_up, mla_kv_a_norm=mla_kv_a_norm, mla_w_kv_up=mla_w_kv_up, mla_q_norm=mla_q_norm,
                 mla_k_norm=mla_k_norm, mla_w_out=mla_w_out, ffn_w_up=ffn_w_up, ffn_conv_w=ffn_conv_w,
                 ffn_conv_b=ffn_conv_b, ffn_w_down=ffn_w_down)
    mom_m = dict(norm_mix=m_norm_mix, norm_ffn=m_norm_ffn, hgrn_w_in=m_hgrn_w_in,
                 hgrn_lower_bounds=m_hgrn_lower_bounds, hgrn_out_norm=m_hgrn_out_norm, hgrn_w_out=m_hgrn_w_out,
                 mla_w_in=m_mla_w_in, mla_q_a_norm=m_mla_q_a_norm, mla_w_q_up=m_mla_w_q_up,
                 mla_kv_a_norm=m_mla_kv_a_norm, mla_w_kv_up=m_mla_w_kv_up, mla_q_norm=m_mla_q_norm,
                 mla_k_norm=m_mla_k_norm, mla_w_out=m_mla_w_out, ffn_w_up=m_ffn_w_up, ffn_conv_w=m_ffn_conv_w,
                 ffn_conv_b=m_ffn_conv_b, ffn_w_down=m_ffn_w_down)
    mom_v = dict(norm_mix=v_norm_mix, norm_ffn=v_norm_ffn, hgrn_w_in=v_hgrn_w_in,
                 hgrn_lower_bounds=v_hgrn_lower_bounds, hgrn_out_norm=v_hgrn_out_norm, hgrn_w_out=v_hgrn_w_out,
                 mla_w_in=v_mla_w_in, mla_q_a_norm=v_mla_q_a_norm, mla_w_q_up=v_mla_w_q_up,
                 mla_kv_a_norm=v_mla_kv_a_norm, mla_w_kv_up=v_mla_w_kv_up, mla_q_norm=v_mla_q_norm,
                 mla_k_norm=v_mla_k_norm, mla_w_out=v_mla_w_out, ffn_w_up=v_ffn_w_up, ffn_conv_w=v_ffn_conv_w,
                 ffn_conv_b=v_ffn_conv_b, ffn_w_down=v_ffn_w_down)
    me = 4 * lax.axis_index("x") + 2 * lax.axis_index("y") + lax.axis_index("c")
    x, positions, target = x[0], positions[0], loss_target[0]
    T, D = x.shape
    depth = norm_mix.shape[0]
    R = mla_w_q_up.shape[1]
    cw = 2 * R + LANES
    small_names = list(SMALL_SHARDED)

    def block_of(kind, l):
        names = {"hgrn": HGRN_W, "mla": MLA_W, "ffn": FFN_W}[kind]
        idx = l if kind == "ffn" else l // 2
        return [(n, idx) for n in names]

    def mixer_kind(l):
        return "hgrn" if l % 2 == 0 else "mla"

    def riders(l):
        keys = block_of("ffn", l)
        if l + 1 < depth:
            keys += block_of(mixer_kind(l + 1), l + 1)
        return keys

    gathered = {}

    def gather_ride(host, l):
        if host == "mixer" and l % 2 == 0:
            keys = [("ffn_w_up", l)]
        elif host == "mixer":
            keys = block_of("ffn", l)
            if l + 1 < depth:
                keys += block_of("hgrn", l + 1) + [("ffn_w_down", l + 1)]
            if l + 2 < depth:
                keys += block_of("mla", l + 2)
        elif l == 0:
            keys = [("ffn_w_down", 0)] + (block_of("mla", 1) if depth > 1 else [])
        else:
            keys = []
        if not keys:
            return keys, None
        return keys, ([local[n][i].astype(BF16) for n, i in keys], ["gather"] * len(keys))

    def take_gathered(keys, arrs):
        for key, a in zip(keys, arrs):
            gathered[key] = a

    keys0 = block_of("hgrn", 0)
    small_local = _pack([local[n] for n in small_names], F32, 8)
    got = _exchange([local[n][i].astype(BF16) for n, i in keys0] + [small_local],
                    ["gather"] * (len(keys0) + 1), name="gather_first")
    take_gathered(keys0, got[:-1])
    small_all = _unpack(got[-1], [local[n].shape for n in small_names], lead=(N_DEV,))
    conv_w_all = small_all[0].transpose(1, 0, 2, 3)
    qa_all = small_all[1].transpose(1, 0, 2).reshape(-1, R)
    kva_all = small_all[2].transpose(1, 0, 2).reshape(-1, R)
    fs = conv_w_all.shape[-1]
    conv_b_s = ffn_conv_b.reshape(depth, N_DEV, 1, fs)

    ct, s1, s2 = _rope_tables(positions)
    lb_soft = jax.nn.softmax(hgrn_lower_bounds.astype(F32), axis=0)
    lower_bounds = jnp.cumsum(lb_soft, axis=0) - lb_soft[0:1]

    def mla_views(j):
        w_in = _pad_cols(gathered["mla_w_in", j].reshape(D, -1), cw)
        wq = _q_up_padded(_shards_to_cols(gathered["mla_w_q_up", j]))
        wkv = _shards_to_cols(gathered["mla_w_kv_up", j])
        gq = _pad_cols(mla_q_norm[j].reshape(1, QK_HEAD), QK_PAD)
        gk = _pad_cols(mla_k_norm[j].reshape(1, QK_HEAD), QK_PAD)
        return w_in, wq, wkv, gq, gk

    saved = []
    h = _rmsnorm_fwd(x, norm_mix[0], name="norm_mix_fwd_0")
    for layer in range(depth):
        j = layer // 2
        s = {"x_in": x}
        s["h_mix"] = h
        keys, ride = gather_ride("mixer", layer)
        if layer % 2 == 0:
            p = _mm(h, gathered["hgrn_w_in", j], b_fmt="knb", bm=1024, name=f"hgrn_in_{layer}")
            (og, o, s0), got = _hgrn_fwd(p, lower_bounds[j], hgrn_out_norm[j], name=f"hgrn_fwd_{layer}", ride=ride)
            s.update(p=p, og=og, o=o, s0=s0)
            take_gathered(keys, got)
            x, h = _mm(og, gathered["hgrn_w_out", j].reshape(D, D), res=x, norm_out=norm_ffn[layer],
                       name=f"hgrn_out_{layer}")
        else:
            w_in, wq, wkv, gq, gk = mla_views(j)
            c = _mm(h, w_in, bm=1024, name=f"mla_in_{layer}")
            q, k, v = _mla_prep_fwd(c, wq, wkv, qa_all[j], kva_all[j], gq, gk, ct, s1, s2,
                                    name=f"mla_prep_fwd_{layer}")
            (ot, lse), got = _attn_fwd(q, k, v, name=f"attn_fwd_{layer}", ride=ride)
            s.update(c=c, q=q, k=k, v=v, ot=ot, lse=lse)
            take_gathered(keys, got)
            x, h = _mm(ot, gathered["mla_w_out", j].reshape(D, D), a_fmt="kmb", res=x, norm_out=norm_ffn[layer],
                       name=f"mla_out_{layer}")
        s["x_mid"] = x
        s["h_ffn"] = h
        keys, ride = gather_ride("ffn_up", layer)
        (a, u), got = _ffn_up_fwd(h, gathered["ffn_w_up", layer], conv_w_all[layer], conv_b_s[layer],
                                  name=f"ffn_up_{layer}", ride=ride)
        take_gathered(keys, got)
        s.update(a=a, u=u)
        w_down = gathered["ffn_w_down", layer].reshape(-1, D)
        if layer + 1 < depth:
            x, h = _mm(a, w_down, a_fmt="mkb", res=x, norm_out=norm_mix[layer + 1], bm=1024, bk=fs,
                       name=f"ffn_down_{layer}")
        else:
            x = _mm(a, w_down, a_fmt="mkb", res=x, bm=1024, bk=fs, name=f"ffn_down_{layer}")
        saved.append(s)

    dx, loss_part = _loss_head(x, target, name="loss_head")

    parts = {}
    received = {}
    g_small = {n: [None] * local[n].shape[0] for n in REPLICATED + small_names}

    def scatter_ride(l):
        keys = riders(l)
        return keys, ([parts[key] for key in keys], ["scatter"] * len(keys))

    def take_received(keys, arrs):
        for key, a in zip(keys, arrs):
            received[key] = a

    for layer in reversed(range(depth)):
        j = layer // 2
        s = saved[layer]
        parts["ffn_w_down", layer] = _mm(s["a"], dx, a_fmt="kmb", out_dtype=BF16, name=f"ffn_down_dw_{layer}"
                                         ).reshape(N_DEV, -1, D)
        du, dcw, dcb = _ffn_act_bwd(dx, gathered["ffn_w_down", layer].reshape(-1, D), s["u"], conv_w_all[layer],
                                    conv_b_s[layer], name=f"ffn_act_bwd_{layer}")
        g_small["ffn_conv_w"][layer] = dcw.reshape(N_DEV, 3, fs)
        g_small["ffn_conv_b"][layer] = dcb.reshape(N_DEV * fs)
        du8 = du.reshape(N_DEV, T, fs)
        parts["ffn_w_up", layer] = _mm(s["h_ffn"], du8, a_fmt="km", b_fmt="knb", out_fmt="mnb", out_dtype=BF16,
                                       bm=1024, name=f"ffn_up_dw_{layer}")
        dx, dgain = _mm(du8, gathered["ffn_w_up", layer], a_fmt="mkb", b_fmt="nkb", bm=1024, bk=fs,
                        norm_bwd=(s["x_mid"], norm_ffn[layer], dx), name=f"ffn_up_dh_{layer}")
        g_small["norm_ffn"][layer] = dgain.reshape(D)
        keys, ride = scatter_ride(layer)
        if layer % 2 == 0:
            w_out = gathered["hgrn_w_out", j].reshape(D, D)
            parts["hgrn_w_out", j] = _mm(s["og"], dx, a_fmt="km", out_dtype=BF16, bm=1024,
                                         name=f"hgrn_out_dw_{layer}").reshape(N_DEV, -1, D)
            dog = _mm(dx, w_out, b_fmt="nk", name=f"hgrn_out_dx_{layer}")
            (dp, dlb, dgn), got = _hgrn_bwd(s["p"], lower_bounds[j], hgrn_out_norm[j], s["s0"], s["o"], dog,
                                            name=f"hgrn_bwd_{layer}", ride=ride)
            take_received(keys, got)
            g_small["hgrn_lower_bounds"][j] = dlb.reshape(D)
            g_small["hgrn_out_norm"][j] = dgn.reshape(HEAD)
            w_in_s = gathered["hgrn_w_in", j]
            parts["hgrn_w_in", j] = _mm(s["h_mix"], dp, a_fmt="km", out_fmt="mnb", out_dtype=BF16, bm=1024,
                                        bn=w_in_s.shape[2], name=f"hgrn_in_dw_{layer}")
            dx, dgain = _mm(dp, w_in_s, b_fmt="nkb", bm=1024, norm_bwd=(s["x_in"], norm_mix[layer], dx),
                            name=f"hgrn_in_dx_{layer}")
        else:
            w_in, wq, wkv, gq, gk = mla_views(j)
            w_out = gathered["mla_w_out", j].reshape(D, D)
            tb = s["ot"].shape[2]
            parts["mla_w_out", j] = _mm(s["ot"], dx, a_fmt="mkb", bk=tb, out_dtype=BF16, bm=1024,
                                        name=f"mla_out_dw_{layer}").reshape(N_DEV, -1, D)
            dot_ = _mm(w_out, dx, b_fmt="nk", out_fmt="mnb", out_dtype=BF16, bm=D, bn=tb,
                       name=f"mla_out_dx_{layer}")
            (dq, dk, dv), got = _attn_bwd(s["q"], s["k"], s["v"], s["ot"], s["lse"], dot_,
                                          name=f"attn_bwd_{layer}", ride=ride)
            take_received(keys, got)
            dc, dwq, dwkv, dgaq, dgakv, dgq, dgk = _mla_prep_bwd(
                s["c"], wq, wkv, qa_all[j], kva_all[j], gq, gk, ct, s1, s2, dq, dk, dv,
                name=f"mla_prep_bwd_{layer}")
            parts["mla_w_q_up", j] = _cols_to_shards(_q_up_unpadded(dwq)).astype(BF16)
            parts["mla_w_kv_up", j] = _cols_to_shards(dwkv).astype(BF16)
            g_small["mla_q_a_norm"][j] = dgaq.reshape(R)
            g_small["mla_kv_a_norm"][j] = dgakv.reshape(R)
            g_small["mla_q_norm"][j] = dgq[0, :QK_HEAD]
            g_small["mla_k_norm"][j] = dgk[0, :QK_HEAD]
            win_cols = mla_w_in.shape[2]
            dw_in = _mm(s["h_mix"], dc, a_fmt="km", bm=1024, name=f"mla_in_dw_{layer}")
            parts["mla_w_in", j] = dw_in[:, :win_cols].astype(BF16).reshape(N_DEV, -1, win_cols)
            dx, dgain = _mm(dc, w_in, b_fmt="nk", norm_bwd=(s["x_in"], norm_mix[layer], dx),
                            name=f"mla_in_dx_{layer}")
        g_small["norm_mix"][layer] = dgain.reshape(D)
    grad_x = dx

    dlb_eff = jnp.stack(g_small["hgrn_lower_bounds"])
    dsoft = jnp.cumsum(dlb_eff[::-1], axis=0)[::-1]
    dsoft = dsoft.at[0].add(-jnp.sum(dlb_eff, axis=0))
    g_lb = lb_soft * (dsoft - jnp.sum(dsoft * lb_soft, axis=0, keepdims=True))
    small_grads = {n: (g_lb if n == "hgrn_lower_bounds" else jnp.stack(g_small[n])) for n in g_small}

    small_grad_names = REPLICATED + small_names
    small_part = _pack([small_grads[n] for n in small_grad_names] + [loss_part], F32, 8)
    got = _exchange([parts[key] for key in keys0] + [small_part], ["scatter"] * len(keys0) + ["gather"],
                    name="exchange_last")
    take_received(keys0, got[:-1])
    small_recv = got[-1]

    out = {}
    for n in BIG:
        layers = local[n].shape[0]
        shard = local[n].shape[1:]
        p2 = jnp.concatenate([received[n, i].reshape(N_DEV, -1, shard[-1]) for i in range(layers)], axis=1)
        flat = lambda a: a.reshape(-1, shard[-1])
        res = _adamw(p2, flat(local[n]), flat(mom_m[n]), flat(mom_v[n]), name=f"adamw_{n}")
        for kind, a in zip(("grad", "delta", "new_m", "new_v"), res):
            out[kind, n] = a.reshape(local[n].shape)
    small_sum = _sum_slots(small_recv, name="sum_small")
    small_full = _unpack(small_sum, [small_grads[n].shape for n in small_grad_names] + [(1, LANES)])
    loss = small_full[-1][0, 0]
    g_mine = {}
    for n, a in zip(small_grad_names, small_full[:-1]):
        if n == "ffn_conv_w":
            a = lax.dynamic_index_in_dim(a, me, axis=1, keepdims=False)
        elif n in SMALL_SHARDED:
            size = local[n].shape[1]
            a = lax.dynamic_slice_in_dim(a, me * size, size, axis=1)
        g_mine[n] = a
    small_shapes = [local[n].shape for n in small_grad_names]
    res = _adamw(_pack([g_mine[n] for n in small_grad_names], F32, 8)[None],
                 _pack([local[n] for n in small_grad_names], F32, 8),
                 _pack([mom_m[n] for n in small_grad_names], F32, 8),
                 _pack([mom_v[n] for n in small_grad_names], F32, 8), name="adamw_small")
    for kind, packed in zip(("grad", "delta", "new_m", "new_v"), res):
        for n, a in zip(small_grad_names, _unpack(packed, small_shapes)):
            out[kind, n] = a

    outs = [loss, grad_x[None]]
    for kind in ("grad", "delta", "new_m", "new_v"):
        outs += [out[kind, n] for n in WEIGHTS]
    return tuple(outs)
```

```python
import functools

import jax
import jax.numpy as jnp
from jax import lax
from jax.experimental import pallas as pl
from jax.experimental.pallas import tpu as pltpu

F32 = jnp.float32
BF16 = jnp.bfloat16

RMS_EPS = 1e-6
ROPE_THETA = 10000.0
HEAD = 128
ROPE = 64
QK_HEAD = HEAD + ROPE
QK_PAD = 256
CHUNK = 64
SUB = 16
EXP_CLAMP = 60.0
HGRN_CHUNKS_PER_STEP = 4
HGRN_BWD_CHUNKS_PER_STEP = 2

ADAM_LR = 0.001
ADAM_B1 = 0.9
ADAM_B2 = 0.999
ADAM_EPS = 1e-08
ADAM_WD = 0.01
ADAM_STEP = 10

N_DEV = 8
LANES = 128
PACK_COLS = 1024
V7X_VMEM_LIMIT = 56 * 1024 * 1024

HI = lax.Precision.HIGHEST


def _params(sem):
    return pltpu.CompilerParams(dimension_semantics=sem, vmem_limit_bytes=V7X_VMEM_LIMIT)


def _blk(n, cap):
    if n <= cap:
        return n
    d = (cap // LANES) * LANES
    while d >= LANES:
        if n % d == 0:
            return d
        d -= LANES
    raise ValueError(f"no lane-aligned block for {n} under {cap}")


def _sigmoid(x):
    return jax.nn.sigmoid(x)


def _dot(a, b, dims, precision=None):
    return lax.dot_general(a, b, (dims, ((), ())), preferred_element_type=F32, precision=precision)


NN = ((1,), (0,))
NT = ((1,), (1,))
TN = ((0,), (0,))


def _mm(a, b, *, a_fmt="mk", b_fmt="kn", out_fmt="mn", res=None, norm_out=None, norm_bwd=None, out_dtype=F32,
        bm=512, bn=1024, bk=1024, name):
    if a_fmt == "mk":
        M, K = a.shape
    elif a_fmt == "km":
        K, M = a.shape
    elif a_fmt == "kmb":
        nb, K, B = a.shape
        M = nb * B
    else:
        nb, M, B = a.shape
        K = nb * B
    if b_fmt == "kn":
        Kb, N = b.shape
    elif b_fmt == "nk":
        N, Kb = b.shape
    elif b_fmt == "knb":
        nbb, Kb, Bb = b.shape
        N = nbb * Bb
    else:
        nbb, N, Bb = b.shape
        Kb = nbb * Bb
    assert K == Kb, (a.shape, b.shape, a_fmt, b_fmt)
    bm = B if a_fmt == "kmb" else _blk(M, bm)
    bn = Bb if b_fmt == "knb" else _blk(N, bn)
    if a_fmt == "mkb" and b_fmt == "nkb":
        assert B == Bb
    bk = _blk(B, bk) if a_fmt == "mkb" else (_blk(Bb, bk) if b_fmt == "nkb" else _blk(K, bk))
    nm, nn, nk = M // bm, N // bn, K // bk

    if a_fmt == "mk":
        a_spec = pl.BlockSpec((bm, bk), lambda i, j, k: (i, k))
        a_dim = 1
    elif a_fmt == "km":
        a_spec = pl.BlockSpec((bk, bm), lambda i, j, k: (k, i))
        a_dim = 0
    elif a_fmt == "kmb":
        a_spec = pl.BlockSpec((None, bk, bm), lambda i, j, k: (i, k, 0))
        a_dim = 0
    else:
        per = B // bk
        a_spec = pl.BlockSpec((None, bm, bk), lambda i, j, k: (k // per, i, k % per))
        a_dim = 1
    if b_fmt == "kn":
        b_spec = pl.BlockSpec((bk, bn), lambda i, j, k: (k, j))
        b_dim = 0
    elif b_fmt == "nk":
        b_spec = pl.BlockSpec((bn, bk), lambda i, j, k: (j, k))
        b_dim = 1
    elif b_fmt == "knb":
        b_spec = pl.BlockSpec((None, bk, bn), lambda i, j, k: (j, k, 0))
        b_dim = 0
    else:
        perb = Bb // bk
        b_spec = pl.BlockSpec((None, bn, bk), lambda i, j, k: (k // perb, j, k % perb))
        b_dim = 1
    if out_fmt == "mn":
        o_spec = pl.BlockSpec((bm, bn), lambda i, j, k: (i, j))
        o_shape = jax.ShapeDtypeStruct((M, N), out_dtype)
    else:
        o_spec = pl.BlockSpec((None, bm, bn), lambda i, j, k: (j, i, 0))
        o_shape = jax.ShapeDtypeStruct((nn, M, bn), out_dtype)
    in_specs = [a_spec, b_spec]
    args = [a, b]
    row_tile = pl.BlockSpec((bm, bn), lambda i, j, k: (i, j))
    row_vec = pl.BlockSpec((1, bn), lambda i, j, k: (0, j))
    if res is not None:
        assert out_fmt == "mn"
        in_specs.append(row_tile)
        args.append(res)
    out_specs, out_shapes = [o_spec], [o_shape]
    if norm_out is not None:
        assert nn == 1 and out_fmt == "mn"
        in_specs.append(row_vec)
        args.append(norm_out.reshape(1, N))
        out_specs.append(row_tile)
        out_shapes.append(jax.ShapeDtypeStruct((M, N), BF16))
    if norm_bwd is not None:
        assert nn == 1 and out_fmt == "mn" and res is None and norm_out is None
        xin, gain, dres = norm_bwd
        in_specs += [row_tile, row_vec, row_tile]
        args += [xin, gain.reshape(1, N), dres]
        out_specs.append(row_vec)
        out_shapes.append(jax.ShapeDtypeStruct((1, N), F32))
    dims = ((a_dim,), (b_dim,))
    has_res = res is not None
    n_in = len(in_specs)

    def body(*refs):
        a_ref, b_ref = refs[0], refs[1]
        extra_in = list(refs[2:n_in])
        o_ref = refs[n_in]
        part = _dot(a_ref[...].astype(BF16), b_ref[...].astype(BF16), dims)

        def finish(out):
            if has_res:
                out = out + extra_in[0][...]
            if norm_out is not None:
                g_ref, h_ref = extra_in[-1], refs[n_in + 1]
                r = lax.rsqrt(jnp.mean(out * out, axis=-1, keepdims=True) + RMS_EPS)
                h_ref[...] = ((out * r) * g_ref[...]).astype(h_ref.dtype)
            if norm_bwd is not None:
                x_ref, g_ref, dr_ref = extra_in
                dg_ref = refs[n_in + 1]
                xv = x_ref[...]
                r = lax.rsqrt(jnp.mean(xv * xv, axis=-1, keepdims=True) + RMS_EPS)
                n = xv * r
                dn = out * g_ref[...]
                gpart = jnp.sum(out * n, axis=0, keepdims=True)
                i = pl.program_id(0)

                @pl.when(i == 0)
                def _():
                    dg_ref[...] = gpart

                @pl.when(i > 0)
                def _():
                    dg_ref[...] += gpart

                out = dr_ref[...] + r * (dn - n * jnp.mean(dn * n, axis=-1, keepdims=True))
            o_ref[...] = out.astype(o_ref.dtype)

        if nk == 1:
            finish(part)
            return
        acc_ref = refs[-1]
        k = pl.program_id(2)

        @pl.when(k == 0)
        def _():
            acc_ref[...] = part

        @pl.when(jnp.logical_and(k > 0, k < nk - 1))
        def _():
            acc_ref[...] += part

        @pl.when(k == nk - 1)
        def _():
            finish(acc_ref[...] + part)

    multi = len(out_specs) > 1
    return pl.pallas_call(
        body, name=name, grid=(nm, nn, nk), in_specs=in_specs,
        out_specs=out_specs if multi else o_spec, out_shape=out_shapes if multi else o_shape,
        scratch_shapes=[] if nk == 1 else [pltpu.VMEM((bm, bn), F32)],
        compiler_params=_params(("arbitrary",) * 3 if norm_bwd is not None else ("parallel", "parallel", "arbitrary")),
    )(*args)


def _rmsnorm_fwd(x, gain, *, name, tm=512):
    T, D = x.shape
    tm = min(tm, T)

    def body(x_ref, g_ref, o_ref):
        xv = x_ref[...]
        r = lax.rsqrt(jnp.mean(xv * xv, axis=-1, keepdims=True) + RMS_EPS)
        o_ref[...] = ((xv * r) * g_ref[...]).astype(o_ref.dtype)

    return pl.pallas_call(
        body, name=name, grid=(T // tm,),
        in_specs=[pl.BlockSpec((tm, D), lambda i: (i, 0)), pl.BlockSpec((1, D), lambda i: (0, 0))],
        out_specs=pl.BlockSpec((tm, D), lambda i: (i, 0)),
        out_shape=jax.ShapeDtypeStruct((T, D), BF16),
        compiler_params=_params(("parallel",)),
    )(x, gain.reshape(1, D))


def _loss_head(y, target, *, name, tm=512):
    T, D = y.shape
    tm = min(tm, T)

    def body(y_ref, t_ref, dy_ref, l_ref):
        i = pl.program_id(0)
        e = y_ref[...] - t_ref[...]
        dy_ref[...] = e * (1.0 / D)
        s = 0.5 * jnp.sum(jnp.mean(e * e, axis=-1, keepdims=True), axis=0, keepdims=True)
        part = jnp.broadcast_to(s, (1, LANES))

        @pl.when(i == 0)
        def _():
            l_ref[...] = part

        @pl.when(i > 0)
        def _():
            l_ref[...] += part

    return pl.pallas_call(
        body, name=name, grid=(T // tm,),
        in_specs=[pl.BlockSpec((tm, D), lambda i: (i, 0)), pl.BlockSpec((tm, D), lambda i: (i, 0))],
        out_specs=[pl.BlockSpec((tm, D), lambda i: (i, 0)), pl.BlockSpec((1, LANES), lambda i: (0, 0))],
        out_shape=[jax.ShapeDtypeStruct((T, D), F32), jax.ShapeDtypeStruct((1, LANES), F32)],
        compiler_params=_params(("arbitrary",)),
    )(y, target)


def _hgrn_selectors():
    t = jnp.arange(CHUNK)[:, None]
    s = jnp.arange(CHUNK)[None, :]
    mats = [s <= t, s < (t // SUB) * SUB]
    for i in range(1, CHUNK // SUB):
        mats.append(jnp.broadcast_to(s < i * SUB, (8, CHUNK)))
    mats.append(jnp.ones((8, CHUNK), bool))
    sel = jnp.concatenate([m.astype(BF16) for m in mats], axis=0)
    rev = (s >= t).astype(BF16)
    return sel, rev


def _select_sums(sel, x):
    hi = x.astype(BF16)
    r1 = x - hi.astype(F32)
    mid = r1.astype(BF16)
    lo = (r1 - mid.astype(F32)).astype(BF16)
    return _dot(sel, hi, NN) + (_dot(sel, mid, NN) + _dot(sel, lo, NN))


def _hgrn_cums(sel, logf):
    nsub = CHUNK // SUB
    cums = _select_sums(sel, logf)
    g = cums[0:CHUNK]
    rrow = cums[CHUNK:2 * CHUNK]
    base = 2 * CHUNK
    rsel = [None] + [jnp.max(cums[base + 8 * (i - 1):base + 8 * i], axis=0, keepdims=True) for i in range(1, nsub)]
    gl = jnp.max(cums[base + 8 * (nsub - 1):base + 8 * nsub], axis=0, keepdims=True)
    return g, rrow, rsel, gl


def _hgrn_gates(p, lb, D):
    qpre, fpre, iv, gpre = p[:, 0:D], p[:, D:2 * D], p[:, 2 * D:3 * D], p[:, 3 * D:4 * D]
    sig = _sigmoid(fpre)
    forget = lb + (1.0 - lb) * sig
    key = 1.0 - forget
    logf = jnp.log(forget)
    sq = _sigmoid(qpre)
    qs = qpre * sq
    return qpre, sq, qs, sig, forget, key, logf, iv, gpre


def _hgrn_fwd(p, lb, gn, *, name, ride=None):
    T, D4 = p.shape
    D = D4 // 4
    H = D // HEAD
    nc = T // CHUNK
    nb = min(HGRN_CHUNKS_PER_STEP, nc)
    assert nc % nb == 0
    sel, _ = _hgrn_selectors()
    nsel = sel.shape[0]
    nsub = CHUNK // SUB
    heads = [slice(h * HEAD, (h + 1) * HEAD) for h in range(H)]

    def body(p_ref, lb_ref, gn_ref, sel_ref, og_ref, o_ref, s0_ref, st_ref):
        @pl.when(pl.program_id(0) == 0)
        def _():
            st_ref[...] = jnp.zeros_like(st_ref)

        row = lax.broadcasted_iota(jnp.int32, (CHUNK, CHUNK), 0)
        col = lax.broadcasted_iota(jnp.int32, (CHUNK, CHUNK), 1)
        gnv = gn_ref[...]
        lbv = lb_ref[...]
        selv = sel_ref[...]
        pre = []
        for cc in range(nb):
            pv = p_ref[cc * CHUNK:(cc + 1) * CHUNK, :]
            _, _, qs, _, _, key, logf, iv, gpre = _hgrn_gates(pv, lbv, D)
            g, rrow, rsel_all, gl = _hgrn_cums(selv, logf)
            qgb = (qs * jnp.exp(g)).astype(BF16)
            qtb = (qs * jnp.exp(g - rrow)).astype(BF16)
            ktb = [(key * jnp.exp(jnp.minimum((0.0 if r is None else r) - g, EXP_CLAMP))).astype(BF16)
                   for r in rsel_all]
            kdb = (key * jnp.exp(gl - g)).astype(BF16)
            vb = iv.astype(BF16)
            blocks = [[_dot(qtb[i * SUB:(i + 1) * SUB, sl], ktb[i][:, sl], NT) for i in range(nsub)]
                      for sl in heads]
            amats = [jnp.where(col <= row, jnp.concatenate(bl, axis=0), 0.0).astype(BF16) for bl in blocks]
            pre.append(dict(qgb=qgb, egl=jnp.exp(gl), gate=gpre * _sigmoid(gpre),
                            intra=[_dot(a, vb[:, sl], NN) for a, sl in zip(amats, heads)],
                            upd=[_dot(vb[:, sl], kdb[:, sl], TN) for sl in heads]))
        sts = [st_ref[sl, :] for sl in heads]
        for cc, d in enumerate(pre):
            rows = slice(cc * CHUNK, (cc + 1) * CHUNK)
            inter = [_dot(d["qgb"][:, sl], st.astype(BF16), NT) for sl, st in zip(heads, sts)]
            for h, sl in enumerate(heads):
                s0_ref[cc, sl, :] = sts[h]
                o = inter[h] + d["intra"][h]
                o_ref[rows, sl] = o
                r = lax.rsqrt(jnp.mean(o * o, axis=-1, keepdims=True) + RMS_EPS)
                og_ref[rows, sl] = (((o * r) * gnv) * d["gate"][:, sl]).astype(og_ref.dtype)
            sts = [st * d["egl"][:, sl] + u for st, sl, u in zip(sts, heads, d["upd"])]
        for sl, st in zip(heads, sts):
            st_ref[sl, :] = st

    rb = nb * CHUNK
    return _call(
        body, name=name, grid=(nc // nb,),
        in_specs=[pl.BlockSpec((rb, D4), lambda c: (c, 0)), pl.BlockSpec((1, D), lambda c: (0, 0)),
                  pl.BlockSpec((1, HEAD), lambda c: (0, 0)), pl.BlockSpec((nsel, CHUNK), lambda c: (0, 0))],
        out_specs=[pl.BlockSpec((rb, D), lambda c: (c, 0)), pl.BlockSpec((rb, D), lambda c: (c, 0)),
                   pl.BlockSpec((nb, D, HEAD), lambda c: (c, 0, 0))],
        out_shape=[jax.ShapeDtypeStruct((T, D), BF16), jax.ShapeDtypeStruct((T, D), F32),
                   jax.ShapeDtypeStruct((nc, D, HEAD), F32)],
        scratch_shapes=[pltpu.VMEM((D, HEAD), F32)], sem=("arbitrary",), ride=ride,
        args=(p, lb.reshape(1, D), gn.reshape(1, HEAD), sel))


def _hgrn_bwd(p, lb, gn, s0, o_saved, dog, *, name, ride=None):
    T, D4 = p.shape
    D = D4 // 4
    H = D // HEAD
    nc = T // CHUNK
    nb = min(HGRN_BWD_CHUNKS_PER_STEP, nc)
    assert nc % nb == 0
    nsteps = nc // nb
    sel, rev = _hgrn_selectors()
    nsel = sel.shape[0]
    nsub = CHUNK // SUB
    heads = [slice(h * HEAD, (h + 1) * HEAD) for h in range(H)]
    cat = lambda xs: jnp.concatenate(xs, axis=1)

    def body(p_ref, lb_ref, gn_ref, sel_ref, rev_ref, s0_ref, s1_ref, o_ref, dog_ref,
             dp_ref, dlb_ref, dgn_ref, dst_ref):
        @pl.when(pl.program_id(0) == 0)
        def _():
            dst_ref[...] = jnp.zeros_like(dst_ref)
            dlb_ref[...] = jnp.zeros_like(dlb_ref)
            dgn_ref[...] = jnp.zeros_like(dgn_ref)

        row = lax.broadcasted_iota(jnp.int32, (CHUNK, CHUNK), 0)
        col = lax.broadcasted_iota(jnp.int32, (CHUNK, CHUNK), 1)
        causal = col <= row
        gnv, lbv, selv, revv = gn_ref[...], lb_ref[...], sel_ref[...], rev_ref[...]
        dgn_acc = jnp.zeros((1, HEAD), F32)
        pre = []
        for cc in range(nb):
            rows = slice(cc * CHUNK, (cc + 1) * CHUNK)
            qpre, sq, qs, sig, forget, key, logf, iv, gpre = _hgrn_gates(p_ref[rows, :], lbv, D)
            g, rrow, rsel_all, gl = _hgrn_cums(selv, logf)
            eg = jnp.exp(g)
            eqr = jnp.exp(g - rrow)
            eis = [jnp.exp(jnp.minimum((0.0 if r is None else r) - g, EXP_CLAMP)) for r in rsel_all]
            ekd = jnp.exp(gl - g)
            qgb = (qs * eg).astype(BF16)
            qtb = (qs * eqr).astype(BF16)
            ktb = [(key * e).astype(BF16) for e in eis]
            kdb = (key * ekd).astype(BF16)
            vb = iv.astype(BF16)
            sg = _sigmoid(gpre)
            gate = gpre * sg
            dgate = sg * (1.0 + gpre * (1.0 - sg))
            dobs, dgpres = [], []
            for sl in heads:
                o = o_ref[rows, sl]
                r = lax.rsqrt(jnp.mean(o * o, axis=-1, keepdims=True) + RMS_EPS)
                n = o * r
                dog_h = dog_ref[rows, sl]
                d_on = dog_h * gate[:, sl]
                dgpres.append(dog_h * (n * gnv) * dgate[:, sl])
                dgn_acc = dgn_acc + jnp.sum(d_on * n, axis=0, keepdims=True)
                dn = d_on * gnv
                dobs.append((r * (dn - n * jnp.mean(dn * n, axis=-1, keepdims=True))).astype(BF16))
            blocks = [[_dot(qtb[i * SUB:(i + 1) * SUB, sl], ktb[i][:, sl], NT) for i in range(nsub)]
                      for sl in heads]
            amats = [jnp.where(causal, jnp.concatenate(bl, axis=0), 0.0).astype(BF16) for bl in blocks]
            dabs = [jnp.where(causal, _dot(dob, vb[:, sl], NT), 0.0).astype(BF16) for dob, sl in zip(dobs, heads)]
            dq_inter = [_dot(dob, s0_ref[cc, sl, :].astype(BF16), NN) for dob, sl in zip(dobs, heads)]
            dqt = [jnp.concatenate([_dot(dab[i * SUB:(i + 1) * SUB], ktb[i][:, sl], NN) for i in range(nsub)],
                                   axis=0) for dab, sl in zip(dabs, heads)]
            dkt = [[_dot(dab[i * SUB:(i + 1) * SUB], qtb[i * SUB:(i + 1) * SUB, sl], TN) for i in range(nsub)]
                   for dab, sl in zip(dabs, heads)]
            dv_intra = [_dot(a, dob, TN) for a, dob in zip(amats, dobs)]
            upd = [_dot(dob, qgb[:, sl], TN) for dob, sl in zip(dobs, heads)]
            dq = cat(dq_inter) * eg + cat(dqt) * eqr
            dk_intra = cat([dkt[h][0] for h in range(H)]) * eis[0]
            for i in range(1, nsub):
                dk_intra = dk_intra + cat([dkt[h][i] for h in range(H)]) * eis[i]
            s1 = [s0_ref[cc + 1, sl, :] if cc + 1 < nb else s1_ref[sl, :] for sl in heads]
            pre.append(dict(qpre=qpre, sq=sq, qs=qs, sig=sig, forget=forget, key=key, ekd=ekd, egl=jnp.exp(gl),
                            kdb=kdb, vb=vb, dq=dq, dk_intra=dk_intra, dv_intra=dv_intra, upd=upd, s1=s1,
                            dgpre=cat(dgpres)))
        dsts = [dst_ref[sl, :] for sl in heads]
        dlb_acc = jnp.zeros((1, D), F32)
        for cc in reversed(range(nb)):
            d = pre[cc]
            rows = slice(cc * CHUNK, (cc + 1) * CHUNK)
            dstb = [x.astype(BF16) for x in dsts]
            dk_state = cat([_dot(d["vb"][:, sl], x, NN) for sl, x in zip(heads, dstb)])
            dv = cat([dvi + _dot(d["kdb"][:, sl], x, NT) for dvi, sl, x in zip(d["dv_intra"], heads, dstb)])
            term = cat([jnp.sum(x * s, axis=0, keepdims=True) for x, s in zip(dsts, d["s1"])])
            dsts = [x * d["egl"][:, sl] + u for x, sl, u in zip(dsts, heads, d["upd"])]
            dk = d["dk_intra"] + dk_state * d["ekd"]
            dq = d["dq"]
            dg = d["qs"] * dq - d["key"] * dk
            dlogf = _select_sums(revv, dg) + term
            sgf = d["sig"]
            dforget = dlogf / d["forget"] - dk
            dlb_acc = dlb_acc + jnp.sum(dforget * (1.0 - sgf), axis=0, keepdims=True)
            sqv = d["sq"]
            dp_ref[rows, 0:D] = (dq * (sqv * (1.0 + d["qpre"] * (1.0 - sqv)))).astype(dp_ref.dtype)
            dp_ref[rows, D:2 * D] = (dforget * (1.0 - lbv) * (sgf * (1.0 - sgf))).astype(dp_ref.dtype)
            dp_ref[rows, 2 * D:3 * D] = dv.astype(dp_ref.dtype)
            dp_ref[rows, 3 * D:4 * D] = d["dgpre"].astype(dp_ref.dtype)
        for sl, x in zip(heads, dsts):
            dst_ref[sl, :] = x
        dlb_ref[...] += dlb_acc
        dgn_ref[...] += dgn_acc

    rb = nb * CHUNK
    rc = lambda c: nsteps - 1 - c
    return _call(
        body, name=name, grid=(nsteps,),
        in_specs=[pl.BlockSpec((rb, D4), lambda c: (rc(c), 0)), pl.BlockSpec((1, D), lambda c: (0, 0)),
                  pl.BlockSpec((1, HEAD), lambda c: (0, 0)), pl.BlockSpec((nsel, CHUNK), lambda c: (0, 0)),
                  pl.BlockSpec((CHUNK, CHUNK), lambda c: (0, 0)),
                  pl.BlockSpec((nb, D, HEAD), lambda c: (rc(c), 0, 0)),
                  pl.BlockSpec((None, D, HEAD), lambda c: (jnp.minimum((rc(c) + 1) * nb, nc - 1), 0, 0)),
                  pl.BlockSpec((rb, D), lambda c: (rc(c), 0)), pl.BlockSpec((rb, D), lambda c: (rc(c), 0))],
        out_specs=[pl.BlockSpec((rb, D4), lambda c: (rc(c), 0)), pl.BlockSpec((1, D), lambda c: (0, 0)),
                   pl.BlockSpec((1, HEAD), lambda c: (0, 0))],
        out_shape=[jax.ShapeDtypeStruct((T, D4), BF16), jax.ShapeDtypeStruct((1, D), F32),
                   jax.ShapeDtypeStruct((1, HEAD), F32)],
        scratch_shapes=[pltpu.VMEM((D, HEAD), F32)], sem=("arbitrary",), ride=ride,
        args=(p, lb.reshape(1, D), gn.reshape(1, HEAD), sel, rev, s0, s0, o_saved, dog))


def _rope_tables(positions):
    inv_freq = ROPE_THETA ** (-jnp.arange(0, ROPE, 2, dtype=F32) / ROPE)
    ang = positions.astype(F32)[:, None] * inv_freq
    cos, sin = jnp.cos(ang), jnp.sin(ang)
    z = jnp.zeros_like(cos)
    ctab = jnp.concatenate([cos, cos, z, z], axis=-1)
    s1 = jnp.concatenate([-sin, z, z, z], axis=-1)
    s2 = jnp.concatenate([z, sin, z, z], axis=-1)
    return ctab, s1, s2


def _rope(z, ct, s1, s2):
    return z * ct + pltpu.roll(z, 96, 1) * s1 + pltpu.roll(z, 32, 1) * s2


def _rope_t(d, ct, s1, s2):
    return d * ct + pltpu.roll(d * s1, 32, 1) + pltpu.roll(d * s2, 96, 1)


def _mla_prep_fwd(c, wq, wkv, ga_q, ga_kv, gq, gk, ct, s1, s2, *, name, tm=256):
    T, CW = c.shape
    R = (CW - LANES) // 2
    H = wq.shape[1] // QK_PAD
    tm = min(tm, T)

    def body(c_ref, wq_ref, wkv_ref, gaq_ref, gakv_ref, gq_ref, gk_ref, ct_ref, s1_ref, s2_ref,
             q_ref, k_ref, v_ref):
        cv = c_ref[...]
        cq, ckv, kr = cv[:, 0:R], cv[:, R:2 * R], cv[:, 2 * R:2 * R + LANES]
        rq = lax.rsqrt(jnp.mean(cq * cq, axis=-1, keepdims=True) + RMS_EPS)
        cqn = ((cq * rq) * gaq_ref[...]).astype(BF16)
        rk = lax.rsqrt(jnp.mean(ckv * ckv, axis=-1, keepdims=True) + RMS_EPS)
        ckvn = ((ckv * rk) * gakv_ref[...]).astype(BF16)
        qp = _dot(cqn, wq_ref[...], NN)
        kvp = _dot(ckvn, wkv_ref[...], NN)
        ctv, s1v, s2v = ct_ref[...], s1_ref[...], s2_ref[...]
        gqv, gkv = gq_ref[...], gk_ref[...]
        krs = jnp.sum(kr * kr, axis=-1, keepdims=True)
        for h in range(H):
            b = h * QK_PAD
            qn, qr = qp[:, b:b + HEAD], qp[:, b + HEAD:b + QK_PAD]
            ss = jnp.sum(qn * qn, axis=-1, keepdims=True) + jnp.sum(qr * qr, axis=-1, keepdims=True)
            rr = lax.rsqrt(ss * (1.0 / QK_HEAD) + RMS_EPS)
            q_ref[:, b:b + HEAD] = ((qn * rr) * gqv[:, 0:HEAD]).astype(q_ref.dtype)
            q_ref[:, b + HEAD:b + QK_PAD] = _rope((qr * rr) * gqv[:, HEAD:QK_PAD], ctv, s1v, s2v).astype(q_ref.dtype)
            kn, vv = kvp[:, b:b + HEAD], kvp[:, b + HEAD:b + QK_PAD]
            ssk = jnp.sum(kn * kn, axis=-1, keepdims=True) + krs
            rrk = lax.rsqrt(ssk * (1.0 / QK_HEAD) + RMS_EPS)
            k_ref[:, b:b + HEAD] = ((kn * rrk) * gkv[:, 0:HEAD]).astype(k_ref.dtype)
            k_ref[:, b + HEAD:b + QK_PAD] = _rope((kr * rrk) * gkv[:, HEAD:QK_PAD], ctv, s1v, s2v).astype(k_ref.dtype)
            v_ref[:, h * HEAD:(h + 1) * HEAD] = vv.astype(v_ref.dtype)

    full = lambda shape: pl.BlockSpec(shape, lambda i: (0, 0))
    tok = lambda w: pl.BlockSpec((tm, w), lambda i: (i, 0))
    return pl.pallas_call(
        body, name=name, grid=(T // tm,),
        in_specs=[tok(CW), full(wq.shape), full(wkv.shape), full((1, R)), full((1, R)), full((1, QK_PAD)),
                  full((1, QK_PAD)), tok(LANES), tok(LANES), tok(LANES)],
        out_specs=[tok(H * QK_PAD), tok(H * QK_PAD), tok(H * HEAD)],
        out_shape=[jax.ShapeDtypeStruct((T, H * QK_PAD), BF16), jax.ShapeDtypeStruct((T, H * QK_PAD), BF16),
                   jax.ShapeDtypeStruct((T, H * HEAD), BF16)],
        compiler_params=_params(("parallel",)),
    )(c, wq, wkv, ga_q.reshape(1, R), ga_kv.reshape(1, R), gq, gk, ct, s1, s2)


def _mla_prep_bwd(c, wq, wkv, ga_q, ga_kv, gq, gk, ct, s1, s2, dq, dk, dv, *, name, tm=256):
    T, CW = c.shape
    R = (CW - LANES) // 2
    H = wq.shape[1] // QK_PAD
    tm = min(tm, T)

    def body(c_ref, wq_ref, wkv_ref, gaq_ref, gakv_ref, gq_ref, gk_ref, ct_ref, s1_ref, s2_ref,
             dq_ref, dk_ref, dv_ref,
             dc_ref, dwq_ref, dwkv_ref, dgaq_ref, dgakv_ref, dgq_ref, dgk_ref, dqp_ref, dkvp_ref):
        i = pl.program_id(0)

        @pl.when(i == 0)
        def _():
            for ref in (dwq_ref, dwkv_ref, dgaq_ref, dgakv_ref, dgq_ref, dgk_ref):
                ref[...] = jnp.zeros_like(ref)

        cv = c_ref[...]
        cq, ckv, kr = cv[:, 0:R], cv[:, R:2 * R], cv[:, 2 * R:2 * R + LANES]
        rq = lax.rsqrt(jnp.mean(cq * cq, axis=-1, keepdims=True) + RMS_EPS)
        nq = cq * rq
        cqn = (nq * gaq_ref[...]).astype(BF16)
        rk = lax.rsqrt(jnp.mean(ckv * ckv, axis=-1, keepdims=True) + RMS_EPS)
        nkv = ckv * rk
        ckvn = (nkv * gakv_ref[...]).astype(BF16)
        qp = _dot(cqn, wq_ref[...], NN)
        kvp = _dot(ckvn, wkv_ref[...], NN)
        ctv, s1v, s2v = ct_ref[...], s1_ref[...], s2_ref[...]
        gqv, gkv = gq_ref[...], gk_ref[...]
        krs = jnp.sum(kr * kr, axis=-1, keepdims=True)
        dkr = jnp.zeros((tm, LANES), F32)
        dgq_n = jnp.zeros((1, HEAD), F32)
        dgq_r = jnp.zeros((1, HEAD), F32)
        dgk_n = jnp.zeros((1, HEAD), F32)
        dgk_r = jnp.zeros((1, HEAD), F32)
        for h in range(H):
            b = h * QK_PAD
            qn, qr = qp[:, b:b + HEAD], qp[:, b + HEAD:b + QK_PAD]
            ss = jnp.sum(qn * qn, axis=-1, keepdims=True) + jnp.sum(qr * qr, axis=-1, keepdims=True)
            rr = lax.rsqrt(ss * (1.0 / QK_HEAD) + RMS_EPS)
            un, ur = qn * rr, qr * rr
            dzn = dq_ref[:, b:b + HEAD]
            dzr = _rope_t(dq_ref[:, b + HEAD:b + QK_PAD], ctv, s1v, s2v)
            dgq_n = dgq_n + jnp.sum(dzn * un, axis=0, keepdims=True)
            dgq_r = dgq_r + jnp.sum(dzr * ur, axis=0, keepdims=True)
            dun, dur = dzn * gqv[:, 0:HEAD], dzr * gqv[:, HEAD:QK_PAD]
            m = (jnp.sum(dun * un, axis=-1, keepdims=True) + jnp.sum(dur * ur, axis=-1, keepdims=True)) \
                * (1.0 / QK_HEAD)
            dqp_ref[:, b:b + HEAD] = (rr * (dun - un * m)).astype(BF16)
            dqp_ref[:, b + HEAD:b + QK_PAD] = (rr * (dur - ur * m)).astype(BF16)
            kn = kvp[:, b:b + HEAD]
            ssk = jnp.sum(kn * kn, axis=-1, keepdims=True) + krs
            rrk = lax.rsqrt(ssk * (1.0 / QK_HEAD) + RMS_EPS)
            vn, vr = kn * rrk, kr * rrk
            dyn = dk_ref[:, b:b + HEAD]
            dyr = _rope_t(dk_ref[:, b + HEAD:b + QK_PAD], ctv, s1v, s2v)
            dgk_n = dgk_n + jnp.sum(dyn * vn, axis=0, keepdims=True)
            dgk_r = dgk_r + jnp.sum(dyr * vr, axis=0, keepdims=True)
            dvn, dvr = dyn * gkv[:, 0:HEAD], dyr * gkv[:, HEAD:QK_PAD]
            mk = (jnp.sum(dvn * vn, axis=-1, keepdims=True) + jnp.sum(dvr * vr, axis=-1, keepdims=True)) \
                * (1.0 / QK_HEAD)
            dkvp_ref[:, b:b + HEAD] = (rrk * (dvn - vn * mk)).astype(BF16)
            dkr = dkr + rrk * (dvr - vr * mk)
            dkvp_ref[:, b + HEAD:b + QK_PAD] = dv_ref[:, h * HEAD:(h + 1) * HEAD].astype(BF16)
        dgq_ref[:, 0:HEAD] += dgq_n
        dgq_ref[:, HEAD:QK_PAD] += dgq_r
        dgk_ref[:, 0:HEAD] += dgk_n
        dgk_ref[:, HEAD:QK_PAD] += dgk_r
        dqp = dqp_ref[...]
        dkvp = dkvp_ref[...]
        dwq_ref[...] += _dot(cqn, dqp, TN)
        dwkv_ref[...] += _dot(ckvn, dkvp, TN)
        dcqn = _dot(dqp, wq_ref[...], NT)
        dckvn = _dot(dkvp, wkv_ref[...], NT)
        dgaq_ref[...] += jnp.sum(dcqn * nq, axis=0, keepdims=True)
        dgakv_ref[...] += jnp.sum(dckvn * nkv, axis=0, keepdims=True)
        dnq = dcqn * gaq_ref[...]
        dnkv = dckvn * gakv_ref[...]
        dc_ref[:, 0:R] = (rq * (dnq - nq * jnp.mean(dnq * nq, axis=-1, keepdims=True))).astype(dc_ref.dtype)
        dc_ref[:, R:2 * R] = (rk * (dnkv - nkv * jnp.mean(dnkv * nkv, axis=-1, keepdims=True))).astype(dc_ref.dtype)
        dc_ref[:, 2 * R:2 * R + LANES] = dkr.astype(dc_ref.dtype)

    full = lambda shape: pl.BlockSpec(shape, lambda i: (0, 0))
    tok = lambda w: pl.BlockSpec((tm, w), lambda i: (i, 0))
    return pl.pallas_call(
        body, name=name, grid=(T // tm,),
        in_specs=[tok(CW), full(wq.shape), full(wkv.shape), full((1, R)), full((1, R)), full((1, QK_PAD)),
                  full((1, QK_PAD)), tok(LANES), tok(LANES), tok(LANES),
                  tok(H * QK_PAD), tok(H * QK_PAD), tok(H * HEAD)],
        out_specs=[tok(CW), full(wq.shape), full(wkv.shape), full((1, R)), full((1, R)), full((1, QK_PAD)),
                   full((1, QK_PAD))],
        out_shape=[jax.ShapeDtypeStruct((T, CW), BF16), jax.ShapeDtypeStruct(wq.shape, F32),
                   jax.ShapeDtypeStruct(wkv.shape, F32), jax.ShapeDtypeStruct((1, R), F32),
                   jax.ShapeDtypeStruct((1, R), F32), jax.ShapeDtypeStruct((1, QK_PAD), F32),
                   jax.ShapeDtypeStruct((1, QK_PAD), F32)],
        scratch_shapes=[pltpu.VMEM((tm, H * QK_PAD), BF16), pltpu.VMEM((tm, H * QK_PAD), BF16)],
        compiler_params=_params(("arbitrary",)),
    )(c, wq, wkv, ga_q.reshape(1, R), ga_kv.reshape(1, R), gq, gk, ct, s1, s2, dq, dk, dv)


NEG = -1e30
LOG2E = 1.4426950408889634


def _attn_fwd(q, k, v, *, name, tb=512, hp=2, ride=None):
    T = q.shape[0]
    H = q.shape[1] // QK_PAD
    tb = min(tb, T)
    nq = T // tb
    scale = QK_HEAD ** -0.5
    c2 = scale * LOG2E
    assert H % hp == 0

    def body(q_ref, k_ref, v_ref, ot_ref, lse_ref, m_ref, l_ref, acc_ref):
        i = pl.program_id(1)
        m_ref[...] = jnp.full_like(m_ref, NEG)
        l_ref[...] = jnp.zeros_like(l_ref)
        acc_ref[...] = jnp.zeros_like(acc_ref)

        def step(js, masked):
            offs = [pl.multiple_of(j * tb, tb) for j in js]
            sts = [[_dot(k_ref[pl.ds(off, tb), hh * QK_PAD:(hh + 1) * QK_PAD],
                         q_ref[:, hh * QK_PAD:(hh + 1) * QK_PAD], NT) for hh in range(hp)] for off in offs]
            for b, hh in [(b, hh) for b in range(len(js)) for hh in range(hp)]:
                off = offs[b]
                vs = slice(hh * HEAD, (hh + 1) * HEAD)
                vb = v_ref[pl.ds(off, tb), vs]
                st = sts[b][hh]
                if masked:
                    kpos = lax.broadcasted_iota(jnp.int32, (tb, tb), 0)
                    qpos = lax.broadcasted_iota(jnp.int32, (tb, tb), 1)
                    st = jnp.where(kpos <= qpos, st, NEG)
                m_old = m_ref[hh]
                m_new = jnp.maximum(m_old, jnp.max(st, axis=0, keepdims=True))
                alpha = jnp.exp2((m_old - m_new) * c2)
                pt = jnp.exp2((st - m_new) * c2)
                l_ref[hh] = l_ref[hh] * alpha + jnp.sum(pt, axis=0, keepdims=True)
                acc_ref[vs, :] = acc_ref[vs, :] * alpha + _dot(vb, pt.astype(BF16), TN)
                m_ref[hh] = m_new

        def pair_body(t, carry):
            step([2 * t, 2 * t + 1], False)
            return carry

        lax.fori_loop(0, i // 2, pair_body, 0)

        @pl.when(i % 2 == 1)
        def _():
            step([i - 1], False)

        step([i], True)
        for hh in range(hp):
            vs = slice(hh * HEAD, (hh + 1) * HEAD)
            l = l_ref[hh]
            ot_ref[vs, :] = (acc_ref[vs, :] / l).astype(ot_ref.dtype)
            lse_ref[hh] = m_ref[hh] * scale + jnp.log(l)

    return _call(
        body, name=name, grid=(H // hp, nq),
        in_specs=[pl.BlockSpec((tb, hp * QK_PAD), lambda g, i: (i, g)),
                  pl.BlockSpec((T, hp * QK_PAD), lambda g, i: (0, g)),
                  pl.BlockSpec((T, hp * HEAD), lambda g, i: (0, g))],
        out_specs=[pl.BlockSpec((None, hp * HEAD, tb), lambda g, i: (i, g, 0)),
                   pl.BlockSpec((hp, None, 1, tb), lambda g, i: (g, i, 0, 0))],
        out_shape=[jax.ShapeDtypeStruct((nq, H * HEAD, tb), BF16), jax.ShapeDtypeStruct((H, nq, 1, tb), F32)],
        scratch_shapes=[pltpu.VMEM((hp, 1, tb), F32), pltpu.VMEM((hp, 1, tb), F32),
                        pltpu.VMEM((hp * HEAD, tb), F32)],
        sem=("parallel", "arbitrary"), ride=ride, args=(q, k, v))


def _attn_bwd(q, k, v, ot, lse, dot_, *, name, ride=None):
    T = q.shape[0]
    H = q.shape[1] // QK_PAD
    nq, _, tb = ot.shape
    scale = QK_HEAD ** -0.5
    c2 = scale * LOG2E

    def body(q_ref, k_ref, v_ref, ot_ref, lse_ref, dot_ref, dq_ref, dk_ref, dv_ref, dka_ref, dva_ref):
        j = pl.program_id(1)

        @pl.when(j == 0)
        def _():
            dq_ref[...] = jnp.zeros_like(dq_ref)

        kb = k_ref[...]
        vb = v_ref[...]

        def step(i, masked):
            off = i * tb if isinstance(i, int) else pl.multiple_of(i * tb, tb)
            qb = q_ref[pl.ds(off, tb), :]
            dob = dot_ref[i]
            ob = ot_ref[i]
            st = _dot(kb, qb, NT)
            if masked:
                kpos = lax.broadcasted_iota(jnp.int32, (tb, tb), 0)
                qpos = lax.broadcasted_iota(jnp.int32, (tb, tb), 1)
                st = jnp.where(kpos <= qpos, st, NEG)
            pt = jnp.exp2(st * c2 - lse_ref[i] * LOG2E)
            dpt = _dot(vb, dob, NN)
            delta = jnp.sum(dob.astype(F32) * ob.astype(F32), axis=0, keepdims=True)
            dst = (pt * (dpt - delta)).astype(BF16)
            dq_ref[pl.ds(off, tb), :] += _dot(dst, kb, TN)
            return _dot(dst, qb, NN), _dot(pt.astype(BF16), dob, NT)

        dk0, dv0 = step(j, True)
        dka_ref[...] = dk0
        dva_ref[...] = dv0

        rest = nq - 1 - j

        def pair_body(t, carry):
            i0 = j + 1 + 2 * t
            dk1, dv1 = step(i0, False)
            dk2, dv2 = step(i0 + 1, False)
            dka_ref[...] += dk1 + dk2
            dva_ref[...] += dv1 + dv2
            return carry

        lax.fori_loop(0, rest // 2, pair_body, 0)

        @pl.when(rest % 2 == 1)
        def _():
            dk1, dv1 = step(nq - 1, False)
            dka_ref[...] += dk1
            dva_ref[...] += dv1

        dk_ref[...] = dka_ref[...] * scale
        dv_ref[...] = dva_ref[...]

        @pl.when(j == nq - 1)
        def _():
            dq_ref[...] = dq_ref[...] * scale

    return _call(
        body, name=name, grid=(H, nq),
        in_specs=[pl.BlockSpec((T, QK_PAD), lambda h, j: (0, h)), pl.BlockSpec((tb, QK_PAD), lambda h, j: (j, h)),
                  pl.BlockSpec((tb, HEAD), lambda h, j: (j, h)),
                  pl.BlockSpec((nq, HEAD, tb), lambda h, j: (0, h, 0)),
                  pl.BlockSpec((None, nq, 1, tb), lambda h, j: (h, 0, 0, 0)),
                  pl.BlockSpec((nq, HEAD, tb), lambda h, j: (0, h, 0))],
        out_specs=[pl.BlockSpec((T, QK_PAD), lambda h, j: (0, h)), pl.BlockSpec((tb, QK_PAD), lambda h, j: (j, h)),
                   pl.BlockSpec((tb, HEAD), lambda h, j: (j, h))],
        out_shape=[jax.ShapeDtypeStruct((T, H * QK_PAD), F32), jax.ShapeDtypeStruct((T, H * QK_PAD), F32),
                   jax.ShapeDtypeStruct((T, H * HEAD), F32)],
        scratch_shapes=[pltpu.VMEM((tb, QK_PAD), F32), pltpu.VMEM((tb, HEAD), F32)],
        sem=("parallel", "arbitrary"), ride=ride, args=(q, k, v, ot, lse, dot_))


def _conv_taps(u, prev6, prev7):
    rows = lax.broadcasted_iota(jnp.int32, (u.shape[0], 1), 0)
    u1 = jnp.where(rows >= 1, pltpu.roll(u, 1, 0), prev7)
    u2 = jnp.where(rows >= 2, pltpu.roll(u, 2, 0), jnp.where(rows == 0, prev6, prev7))
    return u2, u1


def _conv_taps_ahead(d, next0, next1):
    tm = d.shape[0]
    rows = lax.broadcasted_iota(jnp.int32, (tm, 1), 0)
    d1 = jnp.where(rows < tm - 1, pltpu.roll(d, tm - 1, 0), next0)
    d2 = jnp.where(rows < tm - 2, pltpu.roll(d, tm - 2, 0), jnp.where(rows == tm - 2, next0, next1))
    return d1, d2


def _ffn_up_fwd(h, w_up, conv_w, conv_b, *, name, tm=512, ride=None):
    T, D = h.shape
    ns, _, fs = w_up.shape
    nh = ns // 2
    tm = min(tm, T)

    def body(h_ref, wg_ref, wu_ref, cwg_ref, cwu_ref, cbg_ref, cbu_ref, a_ref, u_ref, y_ref, cg_ref, cu_ref):
        i = pl.program_id(1)

        @pl.when(i == 0)
        def _():
            cg_ref[...] = jnp.zeros_like(cg_ref)
            cu_ref[...] = jnp.zeros_like(cu_ref)

        hv = h_ref[...]
        ys = []
        for idx, (w_ref, cw_ref, cb_ref, carry) in enumerate(
                ((wg_ref, cwg_ref, cbg_ref, cg_ref), (wu_ref, cwu_ref, cbu_ref, cu_ref))):
            u = _dot(hv, w_ref[...], NN)
            u_ref[idx] = u.astype(u_ref.dtype)
            u2, u1 = _conv_taps(u, carry[6:7, :], carry[7:8, :])
            y = cb_ref[...] + u2 * cw_ref[0:1, :]
            y = y + u1 * cw_ref[1:2, :]
            y = y + u * cw_ref[2:3, :]
            y_ref[idx] = y.astype(y_ref.dtype)
            ys.append(y)
            carry[...] = u[tm - 8:tm, :]
        yg, yu = ys
        a_ref[...] = ((yg * _sigmoid(yg)) * yu).astype(a_ref.dtype)

    shard = lambda r, off: pl.BlockSpec((None, r, fs), lambda j, i: (j + off, 0, 0))
    return _call(
        body, name=name, grid=(nh, T // tm),
        in_specs=[pl.BlockSpec((tm, D), lambda j, i: (i, 0)), shard(D, 0), shard(D, nh),
                  shard(3, 0), shard(3, nh), shard(1, 0), shard(1, nh)],
        out_specs=[pl.BlockSpec((None, tm, fs), lambda j, i: (j, i, 0)),
                   pl.BlockSpec((2, None, tm, fs), lambda j, i: (0, j, i, 0)),
                   pl.BlockSpec((2, None, tm, fs), lambda j, i: (0, j, i, 0))],
        out_shape=[jax.ShapeDtypeStruct((nh, T, fs), BF16), jax.ShapeDtypeStruct((2, nh, T, fs), BF16),
                   jax.ShapeDtypeStruct((2, nh, T, fs), BF16)],
        scratch_shapes=[pltpu.VMEM((8, fs), F32), pltpu.VMEM((8, fs), F32)],
        sem=("parallel", "arbitrary"), ride=ride, args=(h, w_up, w_up, conv_w, conv_w, conv_b, conv_b))


def _ffn_act_bwd(dxo, w_down, u, y, conv_w, *, name, tm=512):
    T, D = dxo.shape
    _, nh, _, fs = u.shape
    tm = min(tm, T)
    nt = T // tm

    def body(dx_ref, wd_ref, u_ref, y_ref, cwg_ref, cwu_ref, du_ref, dcw_ref, dcb_ref, cg_ref, cu_ref):
        i = pl.program_id(1)

        @pl.when(i == 0)
        def _():
            cg_ref[...] = jnp.zeros_like(cg_ref)
            cu_ref[...] = jnp.zeros_like(cu_ref)
            dcw_ref[...] = jnp.zeros_like(dcw_ref)
            dcb_ref[...] = jnp.zeros_like(dcb_ref)

        da = _dot(dx_ref[...].astype(BF16), wd_ref[...], NT)
        yg, yu = y_ref[0].astype(F32), y_ref[1].astype(F32)
        sg = _sigmoid(yg)
        dys = (da * yu * (sg * (1.0 + yg * (1.0 - sg))), da * (yg * sg))
        for idx, (cw_ref, carry) in enumerate(((cwg_ref, cg_ref), (cwu_ref, cu_ref))):
            dy = dys[idx]
            uv = u_ref[idx].astype(F32)
            d1, d2 = _conv_taps_ahead(dy, carry[0:1, :], carry[1:2, :])
            dcb_ref[idx] += jnp.sum(dy, axis=0, keepdims=True)
            dcw_ref[idx, 0:1, :] += jnp.sum(d2 * uv, axis=0, keepdims=True)
            dcw_ref[idx, 1:2, :] += jnp.sum(d1 * uv, axis=0, keepdims=True)
            dcw_ref[idx, 2:3, :] += jnp.sum(dy * uv, axis=0, keepdims=True)
            du = dy * cw_ref[2:3, :] + d1 * cw_ref[1:2, :] + d2 * cw_ref[0:1, :]
            du_ref[idx] = du.astype(du_ref.dtype)
            carry[...] = dy[0:8, :]

    rt = lambda i: nt - 1 - i
    shard = lambda r, off: pl.BlockSpec((None, r, fs), lambda j, i: (j + off, 0, 0))
    tile = pl.BlockSpec((2, None, tm, fs), lambda j, i: (0, j, rt(i), 0))
    return pl.pallas_call(
        body, name=name, grid=(nh, nt),
        in_specs=[pl.BlockSpec((tm, D), lambda j, i: (rt(i), 0)), pl.BlockSpec((fs, D), lambda j, i: (j, 0)),
                  tile, tile, shard(3, 0), shard(3, nh)],
        out_specs=[tile, pl.BlockSpec((2, None, 3, fs), lambda j, i: (0, j, 0, 0)),
                   pl.BlockSpec((2, None, 1, fs), lambda j, i: (0, j, 0, 0))],
        out_shape=[jax.ShapeDtypeStruct((2, nh, T, fs), BF16), jax.ShapeDtypeStruct((2, nh, 3, fs), F32),
                   jax.ShapeDtypeStruct((2, nh, 1, fs), F32)],
        scratch_shapes=[pltpu.VMEM((8, fs), F32), pltpu.VMEM((8, fs), F32)],
        compiler_params=_params(("parallel", "arbitrary")),
    )(dxo, w_down, u, y, conv_w, conv_w)


def _pad_cols(w, n):
    return jnp.pad(w, [(0, 0)] * (w.ndim - 1) + [(0, n - w.shape[-1])])


def _q_up_padded(w):
    R = w.shape[0]
    H = w.shape[1] // QK_HEAD
    return _pad_cols(w.reshape(R, H, QK_HEAD), QK_PAD).reshape(R, H * QK_PAD)


def _q_up_unpadded(w):
    R = w.shape[0]
    H = w.shape[1] // QK_PAD
    return w.reshape(R, H, QK_PAD)[:, :, :QK_HEAD].reshape(R, H * QK_HEAD)


def _xchg_copies(src_refs, out_refs, kinds, send_sems, recv_sems, local_sems):
    x, y, c = lax.axis_index("x"), lax.axis_index("y"), lax.axis_index("c")
    me = 4 * x + 2 * y + c
    copies = []
    for b, kind in enumerate(kinds):
        gather = kind == "gather"
        own = src_refs[b] if gather else src_refs[b].at[me]
        copies.append(pltpu.make_async_copy(own, out_refs[b].at[me], local_sems.at[b]))
        for kk in range(1, N_DEV):
            px = 1 - x if kk & 4 else x
            py = 1 - y if kk & 2 else y
            pc = 1 - c if kk & 1 else c
            peer = 4 * px + 2 * py + pc
            src = src_refs[b] if gather else src_refs[b].at[peer]
            copies.append(pltpu.make_async_remote_copy(
                src_ref=src, dst_ref=out_refs[b].at[me],
                send_sem=send_sems.at[b * (N_DEV - 1) + kk - 1],
                recv_sem=recv_sems.at[b * (N_DEV - 1) + kk - 1],
                device_id=(px, py, pc), device_id_type=pl.DeviceIdType.MESH))
    return copies


def _xchg_out_shapes(srcs, kinds):
    return [jax.ShapeDtypeStruct((N_DEV,) + s.shape if kind == "gather" else s.shape, s.dtype)
            for s, kind in zip(srcs, kinds)]


def _xchg_scratch(n):
    return [pltpu.SemaphoreType.DMA((n * (N_DEV - 1),)), pltpu.SemaphoreType.DMA((n * (N_DEV - 1),)),
            pltpu.SemaphoreType.DMA((n,))]


def _exchange(srcs, kinds, *, name):
    n = len(srcs)

    def body(*refs):
        copies = _xchg_copies(refs[:n], refs[n:2 * n], kinds, *refs[2 * n:])
        for cp in copies:
            cp.start()
        for cp in copies:
            cp.wait()

    hbm = pl.BlockSpec(memory_space=pl.ANY)
    return pl.pallas_call(
        body, name=name, in_specs=[hbm] * n, out_specs=[hbm] * n, out_shape=_xchg_out_shapes(srcs, kinds),
        scratch_shapes=_xchg_scratch(n),
    )(*srcs)


def _call(body, *, name, grid, in_specs, out_specs, out_shape, scratch_shapes, args, sem, ride=None):
    if ride is None:
        outs = pl.pallas_call(body, name=name, grid=grid, in_specs=in_specs, out_specs=out_specs,
                              out_shape=out_shape, scratch_shapes=scratch_shapes,
                              compiler_params=_params(sem))(*args)
        return list(outs), []
    srcs, kinds = ride
    n_in, n_out, n_sc, nx = len(in_specs), len(out_specs), len(scratch_shapes), len(srcs)

    def wrapped(*refs):
        ins, xs = refs[:n_in], refs[n_in:n_in + nx]
        o0 = n_in + nx
        outs, xo = refs[o0:o0 + n_out], refs[o0 + n_out:o0 + n_out + nx]
        s0 = o0 + n_out + nx
        sc, sems = refs[s0:s0 + n_sc], refs[s0 + n_sc:]
        first = functools.reduce(jnp.logical_and, [pl.program_id(d) == 0 for d in range(len(grid))])
        last = functools.reduce(jnp.logical_and, [pl.program_id(d) == grid[d] - 1 for d in range(len(grid))])

        @pl.when(first)
        def _():
            for cp in _xchg_copies(xs, xo, kinds, *sems):
                cp.start()

        body(*ins, *outs, *sc)

        @pl.when(last)
        def _():
            for cp in _xchg_copies(xs, xo, kinds, *sems):
                cp.wait()

    hbm = pl.BlockSpec(memory_space=pl.ANY)
    outs = pl.pallas_call(
        wrapped, name=name, grid=grid, in_specs=list(in_specs) + [hbm] * nx,
        out_specs=list(out_specs) + [hbm] * nx, out_shape=list(out_shape) + _xchg_out_shapes(srcs, kinds),
        scratch_shapes=list(scratch_shapes) + _xchg_scratch(nx),
        compiler_params=_params(("arbitrary",) * len(grid)),
    )(*args, *srcs)
    return list(outs[:n_out]), list(outs[n_out:])


def _sum_slots(parts, *, name):
    _, Rr, C = parts.shape

    def body(p_ref, o_ref):
        acc = p_ref[0].astype(F32)
        for d in range(1, N_DEV):
            acc = acc + p_ref[d].astype(F32)
        o_ref[...] = acc

    return pl.pallas_call(
        body, name=name, grid=(1,),
        in_specs=[pl.BlockSpec((N_DEV, Rr, C), lambda i: (0, 0, 0))],
        out_specs=pl.BlockSpec((Rr, C), lambda i: (0, 0)),
        out_shape=jax.ShapeDtypeStruct((Rr, C), F32),
        compiler_params=_params(("arbitrary",)),
    )(parts)


def _row_tile(rows, cap=512):
    if rows <= cap:
        return rows
    d = (cap // 8) * 8
    while d >= 8:
        if rows % d == 0:
            return d
        d -= 8
    raise ValueError(f"no row tile for {rows}")


def _adamw(parts, w, m, v, *, name):
    S, Rr, C = parts.shape
    tr = _row_tile(Rr)
    c1 = 1.0 - ADAM_B1 ** ADAM_STEP
    c2 = 1.0 - ADAM_B2 ** ADAM_STEP

    def body(p_ref, w_ref, m_ref, v_ref, g_ref, d_ref, nm_ref, nv_ref):
        g = p_ref[0].astype(F32)
        for d in range(1, S):
            g = g + p_ref[d].astype(F32)
        mm = ADAM_B1 * m_ref[...] + (1.0 - ADAM_B1) * g
        vv = ADAM_B2 * v_ref[...] + (1.0 - ADAM_B2) * (g * g)
        m_hat = mm / c1
        v_hat = vv / c2
        g_ref[...] = g
        d_ref[...] = -ADAM_LR * (m_hat / (jnp.sqrt(v_hat) + ADAM_EPS) + ADAM_WD * w_ref[...])
        nm_ref[...] = mm
        nv_ref[...] = vv

    spec = pl.BlockSpec((tr, C), lambda i: (i, 0))
    shape = jax.ShapeDtypeStruct((Rr, C), F32)
    return pl.pallas_call(
        body, name=name, grid=(Rr // tr,),
        in_specs=[pl.BlockSpec((S, tr, C), lambda i: (0, i, 0)), spec, spec, spec],
        out_specs=[spec] * 4, out_shape=[shape] * 4,
        compiler_params=_params(("parallel",)),
    )(parts, w, m, v)


def _pack(arrs, dtype, row_mult):
    flat = jnp.concatenate([a.reshape(-1).astype(dtype) for a in arrs])
    per = row_mult * PACK_COLS
    total = -(-flat.shape[0] // per) * per
    return jnp.pad(flat, (0, total - flat.shape[0])).reshape(total // PACK_COLS, PACK_COLS)


def _unpack(packed, shapes, lead=()):
    flat = packed.reshape(lead + (-1,))
    out, off = [], 0
    for shp in shapes:
        n = 1
        for d in shp:
            n *= d
        out.append(flat[..., off:off + n].reshape(lead + tuple(shp)))
        off += n
    return out


HGRN_W = ("hgrn_w_in", "hgrn_w_out")
MLA_W = ("mla_w_in", "mla_w_q_up", "mla_w_kv_up", "mla_w_out")
FFN_W = ("ffn_w_up", "ffn_w_down")
BIG = HGRN_W + MLA_W + FFN_W
SMALL_SHARDED = {"ffn_conv_w": 2, "mla_q_a_norm": 1, "mla_kv_a_norm": 1}
REPLICATED = ["norm_mix", "norm_ffn", "hgrn_lower_bounds", "hgrn_out_norm", "mla_q_norm", "mla_k_norm",
              "ffn_conv_b"]
WEIGHTS = ["norm_mix", "norm_ffn", "hgrn_w_in", "hgrn_lower_bounds", "hgrn_out_norm", "hgrn_w_out", "mla_w_in",
           "mla_q_a_norm", "mla_w_q_up", "mla_kv_a_norm", "mla_w_kv_up", "mla_q_norm", "mla_k_norm", "mla_w_out",
           "ffn_w_up", "ffn_conv_w", "ffn_conv_b", "ffn_w_down"]


def _shards_to_cols(g):
    return g.transpose(1, 0, 2).reshape(g.shape[1], N_DEV * g.shape[2])


def _cols_to_shards(w):
    R = w.shape[0]
    return w.reshape(R, N_DEV, w.shape[1] // N_DEV).transpose(1, 0, 2)


def kernel(x, positions, norm_mix, norm_ffn, hgrn_w_in, hgrn_lower_bounds, hgrn_out_norm, hgrn_w_out, mla_w_in, mla_q_a_norm, mla_w_q_up, mla_kv_a_norm, mla_w_kv_up, mla_q_norm, mla_k_norm, mla_w_out, ffn_w_up, ffn_conv_w, ffn_conv_b, ffn_w_down, loss_target, m_norm_mix, m_norm_ffn, m_hgrn_w_in, m_hgrn_lower_bounds, m_hgrn_out_norm, m_hgrn_w_out, m_mla_w_in, m_mla_q_a_norm, m_mla_w_q_up, m_mla_kv_a_norm, m_mla_w_kv_up, m_mla_q_norm, m_mla_k_norm, m_mla_w_out, m_ffn_w_up, m_ffn_conv_w, m_ffn_conv_b, m_ffn_w_down, v_norm_mix, v_norm_ffn, v_hgrn_w_in, v_hgrn_lower_bounds, v_hgrn_out_norm, v_hgrn_w_out, v_mla_w_in, v_mla_q_a_norm, v_mla_w_q_up, v_mla_kv_a_norm, v_mla_w_kv_up, v_mla_q_norm, v_mla_k_norm, v_mla_w_out, v_ffn_w_up, v_ffn_conv_w, v_ffn_conv_b, v_ffn_w_down):
    local = dict(norm_mix=norm_mix, norm_ffn=norm_ffn, hgrn_w_in=hgrn_w_in, hgrn_lower_bounds=hgrn_lower_bounds,
                 hgrn_out_norm=hgrn_out_norm, hgrn_w_out=hgrn_w_out, mla_w_in=mla_w_in, mla_q_a_norm=mla_q_a_norm,
                 mla_w_q_up=mla_w_q_up, mla_kv_a_norm=mla_kv_a_norm, mla_w_kv_up=mla_w_kv_up, mla_q_norm=mla_q_norm,
                 mla_k_norm=mla_k_norm, mla_w_out=mla_w_out, ffn_w_up=ffn_w_up, ffn_conv_w=ffn_conv_w,
                 ffn_conv_b=ffn_conv_b, ffn_w_down=ffn_w_down)
    mom_m = dict(norm_mix=m_norm_mix, norm_ffn=m_norm_ffn, hgrn_w_in=m_hgrn_w_in,
                 hgrn_lower_bounds=m_hgrn_lower_bounds, hgrn_out_norm=m_hgrn_out_norm, hgrn_w_out=m_hgrn_w_out,
                 mla_w_in=m_mla_w_in, mla_q_a_norm=m_mla_q_a_norm, mla_w_q_up=m_mla_w_q_up,
                 mla_kv_a_norm=m_mla_kv_a_norm, mla_w_kv_up=m_mla_w_kv_up, mla_q_norm=m_mla_q_norm,
                 mla_k_norm=m_mla_k_norm, mla_w_out=m_mla_w_out, ffn_w_up=m_ffn_w_up, ffn_conv_w=m_ffn_conv_w,
                 ffn_conv_b=m_ffn_conv_b, ffn_w_down=m_ffn_w_down)
    mom_v = dict(norm_mix=v_norm_mix, norm_ffn=v_norm_ffn, hgrn_w_in=v_hgrn_w_in,
                 hgrn_lower_bounds=v_hgrn_lower_bounds, hgrn_out_norm=v_hgrn_out_norm, hgrn_w_out=v_hgrn_w_out,
                 mla_w_in=v_mla_w_in, mla_q_a_norm=v_mla_q_a_norm, mla_w_q_up=v_mla_w_q_up,
                 mla_kv_a_norm=v_mla_kv_a_norm, mla_w_kv_up=v_mla_w_kv_up, mla_q_norm=v_mla_q_norm,
                 mla_k_norm=v_mla_k_norm, mla_w_out=v_mla_w_out, ffn_w_up=v_ffn_w_up, ffn_conv_w=v_ffn_conv_w,
                 ffn_conv_b=v_ffn_conv_b, ffn_w_down=v_ffn_w_down)
    me = 4 * lax.axis_index("x") + 2 * lax.axis_index("y") + lax.axis_index("c")
    x, positions, target = x[0], positions[0], loss_target[0]
    T, D = x.shape
    depth = norm_mix.shape[0]
    R = mla_w_q_up.shape[1]
    cw = 2 * R + LANES
    small_names = list(SMALL_SHARDED)

    def block_of(kind, l):
        names = {"hgrn": HGRN_W, "mla": MLA_W, "ffn": FFN_W}[kind]
        idx = l if kind == "ffn" else l // 2
        return [(n, idx) for n in names]

    def mixer_kind(l):
        return "hgrn" if l % 2 == 0 else "mla"

    def riders(l):
        keys = block_of("ffn", l)
        if l + 1 < depth:
            keys += block_of(mixer_kind(l + 1), l + 1)
        return keys

    gathered = {}

    def gather_ride(host, l):
        if host == "mixer" and l % 2 == 0:
            keys = [("ffn_w_up", l)]
        elif host == "mixer":
            keys = block_of("ffn", l)
            if l + 1 < depth:
                keys += block_of("hgrn", l + 1) + [("ffn_w_down", l + 1)]
            if l + 2 < depth:
                keys += block_of("mla", l + 2)
        elif l == 0:
            keys = [("ffn_w_down", 0)] + (block_of("mla", 1) if depth > 1 else [])
        else:
            keys = []
        if not keys:
            return keys, None
        return keys, ([local[n][i].astype(BF16) for n, i in keys], ["gather"] * len(keys))

    def take_gathered(keys, arrs):
        for key, a in zip(keys, arrs):
            gathered[key] = a

    keys0 = block_of("hgrn", 0)
    small_local = _pack([local[n] for n in small_names], F32, 8)
    got = _exchange([local[n][i].astype(BF16) for n, i in keys0] + [small_local],
                    ["gather"] * (len(keys0) + 1), name="gather_first")
    take_gathered(keys0, got[:-1])
    small_all = _unpack(got[-1], [local[n].shape for n in small_names], lead=(N_DEV,))
    conv_w_all = small_all[0].transpose(1, 0, 2, 3)
    qa_all = small_all[1].transpose(1, 0, 2).reshape(-1, R)
    kva_all = small_all[2].transpose(1, 0, 2).reshape(-1, R)
    fs = conv_w_all.shape[-1]
    conv_b_s = ffn_conv_b.reshape(depth, N_DEV, 1, fs)

    ct, s1, s2 = _rope_tables(positions)
    lb_soft = jax.nn.softmax(hgrn_lower_bounds.astype(F32), axis=0)
    lower_bounds = jnp.cumsum(lb_soft, axis=0) - lb_soft[0:1]

    def mla_views(j):
        w_in = _pad_cols(gathered["mla_w_in", j].reshape(D, -1), cw)
        wq = _q_up_padded(_shards_to_cols(gathered["mla_w_q_up", j]))
        wkv = _shards_to_cols(gathered["mla_w_kv_up", j])
        gq = _pad_cols(mla_q_norm[j].reshape(1, QK_HEAD), QK_PAD)
        gk = _pad_cols(mla_k_norm[j].reshape(1, QK_HEAD), QK_PAD)
        return w_in, wq, wkv, gq, gk

    saved = []
    h = _rmsnorm_fwd(x, norm_mix[0], name="norm_mix_fwd_0")
    for layer in range(depth):
        j = layer // 2
        s = {"x_in": x}
        s["h_mix"] = h
        keys, ride = gather_ride("mixer", layer)
        if layer % 2 == 0:
            p = _mm(h, gathered["hgrn_w_in", j], b_fmt="knb", bm=1024, name=f"hgrn_in_{layer}")
            (og, o, s0), got = _hgrn_fwd(p, lower_bounds[j], hgrn_out_norm[j], name=f"hgrn_fwd_{layer}", ride=ride)
            s.update(p=p, og=og, o=o, s0=s0)
            take_gathered(keys, got)
            x, h = _mm(og, gathered["hgrn_w_out", j].reshape(D, D), res=x, norm_out=norm_ffn[layer],
                       name=f"hgrn_out_{layer}")
        else:
            w_in, wq, wkv, gq, gk = mla_views(j)
            c = _mm(h, w_in, bm=1024, name=f"mla_in_{layer}")
            q, k, v = _mla_prep_fwd(c, wq, wkv, qa_all[j], kva_all[j], gq, gk, ct, s1, s2,
                                    name=f"mla_prep_fwd_{layer}")
            (ot, lse), got = _attn_fwd(q, k, v, name=f"attn_fwd_{layer}", ride=ride)
            s.update(c=c, q=q, k=k, v=v, ot=ot, lse=lse)
            take_gathered(keys, got)
            x, h = _mm(ot, gathered["mla_w_out", j].reshape(D, D), a_fmt="kmb", res=x, norm_out=norm_ffn[layer],
                       name=f"mla_out_{layer}")
        s["x_mid"] = x
        s["h_ffn"] = h
        keys, ride = gather_ride("ffn_up", layer)
        (a, u, y), got = _ffn_up_fwd(h, gathered["ffn_w_up", layer], conv_w_all[layer], conv_b_s[layer],
                                     name=f"ffn_up_{layer}", ride=ride)
        take_gathered(keys, got)
        s.update(a=a, u=u, y=y)
        w_down = gathered["ffn_w_down", layer].reshape(-1, D)
        if layer + 1 < depth:
            x, h = _mm(a, w_down, a_fmt="mkb", res=x, norm_out=norm_mix[layer + 1], bm=1024, bk=fs,
                       name=f"ffn_down_{layer}")
        else:
            x = _mm(a, w_down, a_fmt="mkb", res=x, bm=1024, bk=fs, name=f"ffn_down_{layer}")
        saved.append(s)

    dx, loss_part = _loss_head(x, target, name="loss_head")

    parts = {}
    received = {}
    g_small = {n: [None] * local[n].shape[0] for n in REPLICATED + small_names}

    def scatter_ride(l):
        keys = riders(l)
        return keys, ([parts[key] for key in keys], ["scatter"] * len(keys))

    def take_received(keys, arrs):
        for key, a in zip(keys, arrs):
            received[key] = a

    for layer in reversed(range(depth)):
        j = layer // 2
        s = saved[layer]
        parts["ffn_w_down", layer] = _mm(s["a"], dx, a_fmt="kmb", out_dtype=BF16, name=f"ffn_down_dw_{layer}"
                                         ).reshape(N_DEV, -1, D)
        du, dcw, dcb = _ffn_act_bwd(dx, gathered["ffn_w_down", layer].reshape(-1, D), s["u"], s["y"],
                                    conv_w_all[layer], name=f"ffn_act_bwd_{layer}")
        g_small["ffn_conv_w"][layer] = dcw.reshape(N_DEV, 3, fs)
        g_small["ffn_conv_b"][layer] = dcb.reshape(N_DEV * fs)
        du8 = du.reshape(N_DEV, T, fs)
        parts["ffn_w_up", layer] = _mm(s["h_ffn"], du8, a_fmt="km", b_fmt="knb", out_fmt="mnb", out_dtype=BF16,
                                       bm=1024, name=f"ffn_up_dw_{layer}")
        dx, dgain = _mm(du8, gathered["ffn_w_up", layer], a_fmt="mkb", b_fmt="nkb", bm=1024, bk=fs,
                        norm_bwd=(s["x_mid"], norm_ffn[layer], dx), name=f"ffn_up_dh_{layer}")
        g_small["norm_ffn"][layer] = dgain.reshape(D)
        keys, ride = scatter_ride(layer)
        if layer % 2 == 0:
            w_out = gathered["hgrn_w_out", j].reshape(D, D)
            parts["hgrn_w_out", j] = _mm(s["og"], dx, a_fmt="km", out_dtype=BF16, bm=1024,
                                         name=f"hgrn_out_dw_{layer}").reshape(N_DEV, -1, D)
            dog = _mm(dx, w_out, b_fmt="nk", name=f"hgrn_out_dx_{layer}")
            (dp, dlb, dgn), got = _hgrn_bwd(s["p"], lower_bounds[j], hgrn_out_norm[j], s["s0"], s["o"], dog,
                                            name=f"hgrn_bwd_{layer}", ride=ride)
            take_received(keys, got)
            g_small["hgrn_lower_bounds"][j] = dlb.reshape(D)
            g_small["hgrn_out_norm"][j] = dgn.reshape(HEAD)
            w_in_s = gathered["hgrn_w_in", j]
            parts["hgrn_w_in", j] = _mm(s["h_mix"], dp, a_fmt="km", out_fmt="mnb", out_dtype=BF16, bm=1024,
                                        bn=w_in_s.shape[2], name=f"hgrn_in_dw_{layer}")
            dx, dgain = _mm(dp, w_in_s, b_fmt="nkb", bm=1024, norm_bwd=(s["x_in"], norm_mix[layer], dx),
                            name=f"hgrn_in_dx_{layer}")
        else:
            w_in, wq, wkv, gq, gk = mla_views(j)
            w_out = gathered["mla_w_out", j].reshape(D, D)
            tb = s["ot"].shape[2]
            parts["mla_w_out", j] = _mm(s["ot"], dx, a_fmt="mkb", bk=tb, out_dtype=BF16, bm=1024,
                                        name=f"mla_out_dw_{layer}").reshape(N_DEV, -1, D)
            dot_ = _mm(w_out, dx, b_fmt="nk", out_fmt="mnb", out_dtype=BF16, bm=D, bn=tb,
                       name=f"mla_out_dx_{layer}")
            (dq, dk, dv), got = _attn_bwd(s["q"], s["k"], s["v"], s["ot"], s["lse"], dot_,
                                          name=f"attn_bwd_{layer}", ride=ride)
            take_received(keys, got)
            dc, dwq, dwkv, dgaq, dgakv, dgq, dgk = _mla_prep_bwd(
                s["c"], wq, wkv, qa_all[j], kva_all[j], gq, gk, ct, s1, s2, dq, dk, dv,
                name=f"mla_prep_bwd_{layer}")
            parts["mla_w_q_up", j] = _cols_to_shards(_q_up_unpadded(dwq)).astype(BF16)
            parts["mla_w_kv_up", j] = _cols_to_shards(dwkv).astype(BF16)
            g_small["mla_q_a_norm"][j] = dgaq.reshape(R)
            g_small["mla_kv_a_norm"][j] = dgakv.reshape(R)
            g_small["mla_q_norm"][j] = dgq[0, :QK_HEAD]
            g_small["mla_k_norm"][j] = dgk[0, :QK_HEAD]
            win_cols = mla_w_in.shape[2]
            dw_in = _mm(s["h_mix"], dc, a_fmt="km", bm=1024, name=f"mla_in_dw_{layer}")
            parts["mla_w_in", j] = dw_in[:, :win_cols].astype(BF16).reshape(N_DEV, -1, win_cols)
            dx, dgain = _mm(dc, w_in, b_fmt="nk", norm_bwd=(s["x_in"], norm_mix[layer], dx),
                            name=f"mla_in_dx_{layer}")
        g_small["norm_mix"][layer] = dgain.reshape(D)
    grad_x = dx

    dlb_eff = jnp.stack(g_small["hgrn_lower_bounds"])
    dsoft = jnp.cumsum(dlb_eff[::-1], axis=0)[::-1]
    dsoft = dsoft.at[0].add(-jnp.sum(dlb_eff, axis=0))
    g_lb = lb_soft * (dsoft - jnp.sum(dsoft * lb_soft, axis=0, keepdims=True))
    small_grads = {n: (g_lb if n == "hgrn_lower_bounds" else jnp.stack(g_small[n])) for n in g_small}

    small_grad_names = REPLICATED + small_names
    small_part = _pack([small_grads[n] for n in small_grad_names] + [loss_part], F32, 8)
    got = _exchange([parts[key] for key in keys0] + [small_part], ["scatter"] * len(keys0) + ["gather"],
                    name="exchange_last")
    take_received(keys0, got[:-1])
    small_recv = got[-1]

    out = {}
    for n in BIG:
        layers = local[n].shape[0]
        shard = local[n].shape[1:]
        p2 = jnp.concatenate([received[n, i].reshape(N_DEV, -1, shard[-1]) for i in range(layers)], axis=1)
        flat = lambda a: a.reshape(-1, shard[-1])
        res = _adamw(p2, flat(local[n]), flat(mom_m[n]), flat(mom_v[n]), name=f"adamw_{n}")
        for kind, a in zip(("grad", "delta", "new_m", "new_v"), res):
            out[kind, n] = a.reshape(local[n].shape)
    small_sum = _sum_slots(small_recv, name="sum_small")
    small_full = _unpack(small_sum, [small_grads[n].shape for n in small_grad_names] + [(1, LANES)])
    loss = small_full[-1][0, 0]
    g_mine = {}
    for n, a in zip(small_grad_names, small_full[:-1]):
        if n == "ffn_conv_w":
            a = lax.dynamic_index_in_dim(a, me, axis=1, keepdims=False)
        elif n in SMALL_SHARDED:
            size = local[n].shape[1]
            a = lax.dynamic_slice_in_dim(a, me * size, size, axis=1)
        g_mine[n] = a
    small_shapes = [local[n].shape for n in small_grad_names]
    res = _adamw(_pack([g_mine[n] for n in small_grad_names], F32, 8)[None],
                 _pack([local[n] for n in small_grad_names], F32, 8),
                 _pack([mom_m[n] for n in small_grad_names], F32, 8),
                 _pack([mom_v[n] for n in small_grad_names], F32, 8), name="adamw_small")
    for kind, packed in zip(("grad", "delta", "new_m", "new_v"), res):
        for n, a in zip(small_grad_names, _unpack(packed, small_shapes)):
            out[kind, n] = a

    outs = [loss, grad_x[None]]
    for kind in ("grad", "delta", "new_m", "new_v"):
        outs += [out[kind, n] for n in WEIGHTS]
    return tuple(outs)
```

```python
import functools

import jax
import jax.numpy as jnp
from jax import lax
from jax.experimental import pallas as pl
from jax.experimental.pallas import tpu as pltpu

F32 = jnp.float32
BF16 = jnp.bfloat16

RMS_EPS = 1e-6
ROPE_THETA = 10000.0
HEAD = 128
ROPE = 64
QK_HEAD = HEAD + ROPE
QK_PAD = 256
CHUNK = 64
SUB = 16
EXP_CLAMP = 60.0
HGRN_CHUNKS_PER_STEP = 8
HGRN_BWD_CHUNKS_PER_STEP = 4

ADAM_LR = 0.001
ADAM_B1 = 0.9
ADAM_B2 = 0.999
ADAM_EPS = 1e-08
ADAM_WD = 0.01
ADAM_STEP = 10

N_DEV = 8
LANES = 128
PACK_COLS = 1024
V7X_VMEM_LIMIT = 56 * 1024 * 1024

HI = lax.Precision.HIGHEST


def _params(sem):
    return pltpu.CompilerParams(dimension_semantics=sem, vmem_limit_bytes=V7X_VMEM_LIMIT)


def _blk(n, cap):
    if n <= cap:
        return n
    d = (cap // LANES) * LANES
    while d >= LANES:
        if n % d == 0:
            return d
        d -= LANES
    raise ValueError(f"no lane-aligned block for {n} under {cap}")


def _sigmoid(x):
    return jax.nn.sigmoid(x)


def _dot(a, b, dims, precision=None):
    return lax.dot_general(a, b, (dims, ((), ())), preferred_element_type=F32, precision=precision)


NN = ((1,), (0,))
NT = ((1,), (1,))
TN = ((0,), (0,))


def _mm(a, b, *, a_fmt="mk", b_fmt="kn", out_fmt="mn", res=None, norm_out=None, norm_bwd=None, out_dtype=F32,
        bm=512, bn=1024, bk=1024, name):
    if a_fmt == "mk":
        M, K = a.shape
    elif a_fmt == "km":
        K, M = a.shape
    elif a_fmt == "kmb":
        nb, K, B = a.shape
        M = nb * B
    else:
        nb, M, B = a.shape
        K = nb * B
    if b_fmt == "kn":
        Kb, N = b.shape
    elif b_fmt == "nk":
        N, Kb = b.shape
    elif b_fmt == "knb":
        nbb, Kb, Bb = b.shape
        N = nbb * Bb
    else:
        nbb, N, Bb = b.shape
        Kb = nbb * Bb
    assert K == Kb, (a.shape, b.shape, a_fmt, b_fmt)
    bm = B if a_fmt == "kmb" else _blk(M, bm)
    bn = Bb if b_fmt == "knb" else _blk(N, bn)
    if a_fmt == "mkb" and b_fmt == "nkb":
        assert B == Bb
    bk = _blk(B, bk) if a_fmt == "mkb" else (_blk(Bb, bk) if b_fmt == "nkb" else _blk(K, bk))
    nm, nn, nk = M // bm, N // bn, K // bk

    if a_fmt == "mk":
        a_spec = pl.BlockSpec((bm, bk), lambda i, j, k: (i, k))
        a_dim = 1
    elif a_fmt == "km":
        a_spec = pl.BlockSpec((bk, bm), lambda i, j, k: (k, i))
        a_dim = 0
    elif a_fmt == "kmb":
        a_spec = pl.BlockSpec((None, bk, bm), lambda i, j, k: (i, k, 0))
        a_dim = 0
    else:
        per = B // bk
        a_spec = pl.BlockSpec((None, bm, bk), lambda i, j, k: (k // per, i, k % per))
        a_dim = 1
    if b_fmt == "kn":
        b_spec = pl.BlockSpec((bk, bn), lambda i, j, k: (k, j))
        b_dim = 0
    elif b_fmt == "nk":
        b_spec = pl.BlockSpec((bn, bk), lambda i, j, k: (j, k))
        b_dim = 1
    elif b_fmt == "knb":
        b_spec = pl.BlockSpec((None, bk, bn), lambda i, j, k: (j, k, 0))
        b_dim = 0
    else:
        perb = Bb // bk
        b_spec = pl.BlockSpec((None, bn, bk), lambda i, j, k: (k // perb, j, k % perb))
        b_dim = 1
    if out_fmt == "mn":
        o_spec = pl.BlockSpec((bm, bn), lambda i, j, k: (i, j))
        o_shape = jax.ShapeDtypeStruct((M, N), out_dtype)
    else:
        o_spec = pl.BlockSpec((None, bm, bn), lambda i, j, k: (j, i, 0))
        o_shape = jax.ShapeDtypeStruct((nn, M, bn), out_dtype)
    in_specs = [a_spec, b_spec]
    args = [a, b]
    row_tile = pl.BlockSpec((bm, bn), lambda i, j, k: (i, j))
    row_vec = pl.BlockSpec((1, bn), lambda i, j, k: (0, j))
    if res is not None:
        assert out_fmt == "mn"
        in_specs.append(row_tile)
        args.append(res)
    out_specs, out_shapes = [o_spec], [o_shape]
    if norm_out is not None:
        assert nn == 1 and out_fmt == "mn"
        in_specs.append(row_vec)
        args.append(norm_out.reshape(1, N))
        out_specs.append(row_tile)
        out_shapes.append(jax.ShapeDtypeStruct((M, N), BF16))
    if norm_bwd is not None:
        assert nn == 1 and out_fmt == "mn" and res is None and norm_out is None
        xin, gain, dres = norm_bwd
        in_specs += [row_tile, row_vec, row_tile]
        args += [xin, gain.reshape(1, N), dres]
        out_specs.append(row_vec)
        out_shapes.append(jax.ShapeDtypeStruct((1, N), F32))
    dims = ((a_dim,), (b_dim,))
    has_res = res is not None
    n_in = len(in_specs)

    def body(*refs):
        a_ref, b_ref = refs[0], refs[1]
        extra_in = list(refs[2:n_in])
        o_ref = refs[n_in]
        part = _dot(a_ref[...].astype(BF16), b_ref[...].astype(BF16), dims)

        def finish(out):
            if has_res:
                out = out + extra_in[0][...]
            if norm_out is not None:
                g_ref, h_ref = extra_in[-1], refs[n_in + 1]
                r = lax.rsqrt(jnp.mean(out * out, axis=-1, keepdims=True) + RMS_EPS)
                h_ref[...] = ((out * r) * g_ref[...]).astype(h_ref.dtype)
            if norm_bwd is not None:
                x_ref, g_ref, dr_ref = extra_in
                dg_ref = refs[n_in + 1]
                xv = x_ref[...]
                r = lax.rsqrt(jnp.mean(xv * xv, axis=-1, keepdims=True) + RMS_EPS)
                n = xv * r
                dn = out * g_ref[...]
                gpart = jnp.sum(out * n, axis=0, keepdims=True)
                i = pl.program_id(0)

                @pl.when(i == 0)
                def _():
                    dg_ref[...] = gpart

                @pl.when(i > 0)
                def _():
                    dg_ref[...] += gpart

                out = dr_ref[...] + r * (dn - n * jnp.mean(dn * n, axis=-1, keepdims=True))
            o_ref[...] = out.astype(o_ref.dtype)

        if nk == 1:
            finish(part)
            return
        acc_ref = refs[-1]
        k = pl.program_id(2)

        @pl.when(k == 0)
        def _():
            acc_ref[...] = part

        @pl.when(jnp.logical_and(k > 0, k < nk - 1))
        def _():
            acc_ref[...] += part

        @pl.when(k == nk - 1)
        def _():
            finish(acc_ref[...] + part)

    multi = len(out_specs) > 1
    return pl.pallas_call(
        body, name=name, grid=(nm, nn, nk), in_specs=in_specs,
        out_specs=out_specs if multi else o_spec, out_shape=out_shapes if multi else o_shape,
        scratch_shapes=[] if nk == 1 else [pltpu.VMEM((bm, bn), F32)],
        compiler_params=_params(("arbitrary",) * 3 if norm_bwd is not None else ("parallel", "parallel", "arbitrary")),
    )(*args)


def _rmsnorm_fwd(x, gain, *, name, tm=512):
    T, D = x.shape
    tm = min(tm, T)

    def body(x_ref, g_ref, o_ref):
        xv = x_ref[...]
        r = lax.rsqrt(jnp.mean(xv * xv, axis=-1, keepdims=True) + RMS_EPS)
        o_ref[...] = ((xv * r) * g_ref[...]).astype(o_ref.dtype)

    return pl.pallas_call(
        body, name=name, grid=(T // tm,),
        in_specs=[pl.BlockSpec((tm, D), lambda i: (i, 0)), pl.BlockSpec((1, D), lambda i: (0, 0))],
        out_specs=pl.BlockSpec((tm, D), lambda i: (i, 0)),
        out_shape=jax.ShapeDtypeStruct((T, D), BF16),
        compiler_params=_params(("parallel",)),
    )(x, gain.reshape(1, D))


def _loss_head(y, target, *, name, tm=512):
    T, D = y.shape
    tm = min(tm, T)

    def body(y_ref, t_ref, dy_ref, l_ref):
        i = pl.program_id(0)
        e = y_ref[...] - t_ref[...]
        dy_ref[...] = e * (1.0 / D)
        s = 0.5 * jnp.sum(jnp.mean(e * e, axis=-1, keepdims=True), axis=0, keepdims=True)
        part = jnp.broadcast_to(s, (1, LANES))

        @pl.when(i == 0)
        def _():
            l_ref[...] = part

        @pl.when(i > 0)
        def _():
            l_ref[...] += part

    return pl.pallas_call(
        body, name=name, grid=(T // tm,),
        in_specs=[pl.BlockSpec((tm, D), lambda i: (i, 0)), pl.BlockSpec((tm, D), lambda i: (i, 0))],
        out_specs=[pl.BlockSpec((tm, D), lambda i: (i, 0)), pl.BlockSpec((1, LANES), lambda i: (0, 0))],
        out_shape=[jax.ShapeDtypeStruct((T, D), F32), jax.ShapeDtypeStruct((1, LANES), F32)],
        compiler_params=_params(("arbitrary",)),
    )(y, target)


def _hgrn_selectors():
    t = jnp.arange(CHUNK)[:, None]
    s = jnp.arange(CHUNK)[None, :]
    mats = [s <= t, s < (t // SUB) * SUB]
    for i in range(1, CHUNK // SUB):
        mats.append(jnp.broadcast_to(s < i * SUB, (8, CHUNK)))
    mats.append(jnp.ones((8, CHUNK), bool))
    sel = jnp.concatenate([m.astype(BF16) for m in mats], axis=0)
    rev = (s >= t).astype(BF16)
    return sel, rev


def _select_sums(sel, x):
    hi = x.astype(BF16)
    r1 = x - hi.astype(F32)
    mid = r1.astype(BF16)
    lo = (r1 - mid.astype(F32)).astype(BF16)
    return _dot(sel, hi, NN) + (_dot(sel, mid, NN) + _dot(sel, lo, NN))


def _hgrn_cums(sel, logf):
    nsub = CHUNK // SUB
    cums = _select_sums(sel, logf)
    g = cums[0:CHUNK]
    rrow = cums[CHUNK:2 * CHUNK]
    base = 2 * CHUNK
    rsel = [None] + [jnp.max(cums[base + 8 * (i - 1):base + 8 * i], axis=0, keepdims=True) for i in range(1, nsub)]
    gl = jnp.max(cums[base + 8 * (nsub - 1):base + 8 * nsub], axis=0, keepdims=True)
    return g, rrow, rsel, gl


def _hgrn_gates(p, lb, D):
    qpre, fpre, iv, gpre = p[:, 0:D], p[:, D:2 * D], p[:, 2 * D:3 * D], p[:, 3 * D:4 * D]
    sig = _sigmoid(fpre)
    forget = lb + (1.0 - lb) * sig
    key = 1.0 - forget
    logf = jnp.log(forget)
    sq = _sigmoid(qpre)
    qs = qpre * sq
    return qpre, sq, qs, sig, forget, key, logf, iv, gpre


def _hgrn_fwd(p, lb, gn, *, name, ride=None):
    T, D4 = p.shape
    D = D4 // 4
    H = D // HEAD
    nc = T // CHUNK
    nb = min(HGRN_CHUNKS_PER_STEP, nc)
    assert nc % nb == 0
    sel, _ = _hgrn_selectors()
    nsel = sel.shape[0]
    nsub = CHUNK // SUB
    heads = [slice(h * HEAD, (h + 1) * HEAD) for h in range(H)]

    def body(p_ref, lb_ref, gn_ref, sel_ref, og_ref, o_ref, s0_ref, st_ref):
        @pl.when(pl.program_id(0) == 0)
        def _():
            st_ref[...] = jnp.zeros_like(st_ref)

        row = lax.broadcasted_iota(jnp.int32, (CHUNK, CHUNK), 0)
        col = lax.broadcasted_iota(jnp.int32, (CHUNK, CHUNK), 1)
        gnv = gn_ref[...]
        lbv = lb_ref[...]
        selv = sel_ref[...]
        pre = []
        for cc in range(nb):
            pv = p_ref[cc * CHUNK:(cc + 1) * CHUNK, :]
            _, _, qs, _, _, key, logf, iv, gpre = _hgrn_gates(pv, lbv, D)
            g, rrow, rsel_all, gl = _hgrn_cums(selv, logf)
            qgb = (qs * jnp.exp(g)).astype(BF16)
            qtb = (qs * jnp.exp(g - rrow)).astype(BF16)
            ktb = [(key * jnp.exp(jnp.minimum((0.0 if r is None else r) - g, EXP_CLAMP))).astype(BF16)
                   for r in rsel_all]
            kdb = (key * jnp.exp(gl - g)).astype(BF16)
            vb = iv.astype(BF16)
            blocks = [[_dot(qtb[i * SUB:(i + 1) * SUB, sl], ktb[i][:, sl], NT) for i in range(nsub)]
                      for sl in heads]
            amats = [jnp.where(col <= row, jnp.concatenate(bl, axis=0), 0.0).astype(BF16) for bl in blocks]
            pre.append(dict(qgb=qgb, egl=jnp.exp(gl), gate=gpre * _sigmoid(gpre),
                            intra=[_dot(a, vb[:, sl], NN) for a, sl in zip(amats, heads)],
                            upd=[_dot(vb[:, sl], kdb[:, sl], TN) for sl in heads]))
        sts = [st_ref[sl, :] for sl in heads]
        for cc, d in enumerate(pre):
            rows = slice(cc * CHUNK, (cc + 1) * CHUNK)
            inter = [_dot(d["qgb"][:, sl], st.astype(BF16), NT) for sl, st in zip(heads, sts)]
            for h, sl in enumerate(heads):
                s0_ref[cc, sl, :] = sts[h]
                o = inter[h] + d["intra"][h]
                o_ref[rows, sl] = o
                r = lax.rsqrt(jnp.mean(o * o, axis=-1, keepdims=True) + RMS_EPS)
                og_ref[rows, sl] = (((o * r) * gnv) * d["gate"][:, sl]).astype(og_ref.dtype)
            sts = [st * d["egl"][:, sl] + u for st, sl, u in zip(sts, heads, d["upd"])]
        for sl, st in zip(heads, sts):
            st_ref[sl, :] = st

    rb = nb * CHUNK
    return _call(
        body, name=name, grid=(nc // nb,),
        in_specs=[pl.BlockSpec((rb, D4), lambda c: (c, 0)), pl.BlockSpec((1, D), lambda c: (0, 0)),
                  pl.BlockSpec((1, HEAD), lambda c: (0, 0)), pl.BlockSpec((nsel, CHUNK), lambda c: (0, 0))],
        out_specs=[pl.BlockSpec((rb, D), lambda c: (c, 0)), pl.BlockSpec((rb, D), lambda c: (c, 0)),
                   pl.BlockSpec((nb, D, HEAD), lambda c: (c, 0, 0))],
        out_shape=[jax.ShapeDtypeStruct((T, D), BF16), jax.ShapeDtypeStruct((T, D), F32),
                   jax.ShapeDtypeStruct((nc, D, HEAD), F32)],
        scratch_shapes=[pltpu.VMEM((D, HEAD), F32)], sem=("arbitrary",), ride=ride,
        args=(p, lb.reshape(1, D), gn.reshape(1, HEAD), sel))


def _hgrn_bwd(p, lb, gn, s0, o_saved, dog, *, name, ride=None):
    T, D4 = p.shape
    D = D4 // 4
    H = D // HEAD
    nc = T // CHUNK
    nb = min(HGRN_BWD_CHUNKS_PER_STEP, nc)
    assert nc % nb == 0
    nsteps = nc // nb
    sel, rev = _hgrn_selectors()
    nsel = sel.shape[0]
    nsub = CHUNK // SUB
    heads = [slice(h * HEAD, (h + 1) * HEAD) for h in range(H)]
    cat = lambda xs: jnp.concatenate(xs, axis=1)

    def body(p_ref, lb_ref, gn_ref, sel_ref, rev_ref, s0_ref, s1_ref, o_ref, dog_ref,
             dp_ref, dlb_ref, dgn_ref, dst_ref):
        @pl.when(pl.program_id(0) == 0)
        def _():
            dst_ref[...] = jnp.zeros_like(dst_ref)
            dlb_ref[...] = jnp.zeros_like(dlb_ref)
            dgn_ref[...] = jnp.zeros_like(dgn_ref)

        row = lax.broadcasted_iota(jnp.int32, (CHUNK, CHUNK), 0)
        col = lax.broadcasted_iota(jnp.int32, (CHUNK, CHUNK), 1)
        causal = col <= row
        gnv, lbv, selv, revv = gn_ref[...], lb_ref[...], sel_ref[...], rev_ref[...]
        dgn_acc = jnp.zeros((1, HEAD), F32)
        pre = []
        for cc in range(nb):
            rows = slice(cc * CHUNK, (cc + 1) * CHUNK)
            qpre, sq, qs, sig, forget, key, logf, iv, gpre = _hgrn_gates(p_ref[rows, :], lbv, D)
            g, rrow, rsel_all, gl = _hgrn_cums(selv, logf)
            eg = jnp.exp(g)
            eqr = jnp.exp(g - rrow)
            eis = [jnp.exp(jnp.minimum((0.0 if r is None else r) - g, EXP_CLAMP)) for r in rsel_all]
            ekd = jnp.exp(gl - g)
            qgb = (qs * eg).astype(BF16)
            qtb = (qs * eqr).astype(BF16)
            ktb = [(key * e).astype(BF16) for e in eis]
            kdb = (key * ekd).astype(BF16)
            vb = iv.astype(BF16)
            sg = _sigmoid(gpre)
            gate = gpre * sg
            dgate = sg * (1.0 + gpre * (1.0 - sg))
            dobs, dgpres = [], []
            for sl in heads:
                o = o_ref[rows, sl]
                r = lax.rsqrt(jnp.mean(o * o, axis=-1, keepdims=True) + RMS_EPS)
                n = o * r
                dog_h = dog_ref[rows, sl]
                d_on = dog_h * gate[:, sl]
                dgpres.append(dog_h * (n * gnv) * dgate[:, sl])
                dgn_acc = dgn_acc + jnp.sum(d_on * n, axis=0, keepdims=True)
                dn = d_on * gnv
                dobs.append((r * (dn - n * jnp.mean(dn * n, axis=-1, keepdims=True))).astype(BF16))
            blocks = [[_dot(qtb[i * SUB:(i + 1) * SUB, sl], ktb[i][:, sl], NT) for i in range(nsub)]
                      for sl in heads]
            amats = [jnp.where(causal, jnp.concatenate(bl, axis=0), 0.0).astype(BF16) for bl in blocks]
            dabs = [jnp.where(causal, _dot(dob, vb[:, sl], NT), 0.0).astype(BF16) for dob, sl in zip(dobs, heads)]
            dq_inter = [_dot(dob, s0_ref[cc, sl, :].astype(BF16), NN) for dob, sl in zip(dobs, heads)]
            dqt = [jnp.concatenate([_dot(dab[i * SUB:(i + 1) * SUB], ktb[i][:, sl], NN) for i in range(nsub)],
                                   axis=0) for dab, sl in zip(dabs, heads)]
            dkt = [[_dot(dab[i * SUB:(i + 1) * SUB], qtb[i * SUB:(i + 1) * SUB, sl], TN) for i in range(nsub)]
                   for dab, sl in zip(dabs, heads)]
            dv_intra = [_dot(a, dob, TN) for a, dob in zip(amats, dobs)]
            upd = [_dot(dob, qgb[:, sl], TN) for dob, sl in zip(dobs, heads)]
            dq = cat(dq_inter) * eg + cat(dqt) * eqr
            dk_intra = cat([dkt[h][0] for h in range(H)]) * eis[0]
            for i in range(1, nsub):
                dk_intra = dk_intra + cat([dkt[h][i] for h in range(H)]) * eis[i]
            s1 = [s0_ref[cc + 1, sl, :] if cc + 1 < nb else s1_ref[sl, :] for sl in heads]
            pre.append(dict(qpre=qpre, sq=sq, qs=qs, sig=sig, forget=forget, key=key, ekd=ekd, egl=jnp.exp(gl),
                            kdb=kdb, vb=vb, dq=dq, dk_intra=dk_intra, dv_intra=dv_intra, upd=upd, s1=s1,
                            dgpre=cat(dgpres)))
        dsts = [dst_ref[sl, :] for sl in heads]
        dlb_acc = jnp.zeros((1, D), F32)
        for cc in reversed(range(nb)):
            d = pre[cc]
            rows = slice(cc * CHUNK, (cc + 1) * CHUNK)
            dstb = [x.astype(BF16) for x in dsts]
            dk_state = cat([_dot(d["vb"][:, sl], x, NN) for sl, x in zip(heads, dstb)])
            dv = cat([dvi + _dot(d["kdb"][:, sl], x, NT) for dvi, sl, x in zip(d["dv_intra"], heads, dstb)])
            term = cat([jnp.sum(x * s, axis=0, keepdims=True) for x, s in zip(dsts, d["s1"])])
            dsts = [x * d["egl"][:, sl] + u for x, sl, u in zip(dsts, heads, d["upd"])]
            dk = d["dk_intra"] + dk_state * d["ekd"]
            dq = d["dq"]
            dg = d["qs"] * dq - d["key"] * dk
            dlogf = _select_sums(revv, dg) + term
            sgf = d["sig"]
            dforget = dlogf / d["forget"] - dk
            dlb_acc = dlb_acc + jnp.sum(dforget * (1.0 - sgf), axis=0, keepdims=True)
            sqv = d["sq"]
            dp_ref[rows, 0:D] = (dq * (sqv * (1.0 + d["qpre"] * (1.0 - sqv)))).astype(dp_ref.dtype)
            dp_ref[rows, D:2 * D] = (dforget * (1.0 - lbv) * (sgf * (1.0 - sgf))).astype(dp_ref.dtype)
            dp_ref[rows, 2 * D:3 * D] = dv.astype(dp_ref.dtype)
            dp_ref[rows, 3 * D:4 * D] = d["dgpre"].astype(dp_ref.dtype)
        for sl, x in zip(heads, dsts):
            dst_ref[sl, :] = x
        dlb_ref[...] += dlb_acc
        dgn_ref[...] += dgn_acc

    rb = nb * CHUNK
    rc = lambda c: nsteps - 1 - c
    return _call(
        body, name=name, grid=(nsteps,),
        in_specs=[pl.BlockSpec((rb, D4), lambda c: (rc(c), 0)), pl.BlockSpec((1, D), lambda c: (0, 0)),
                  pl.BlockSpec((1, HEAD), lambda c: (0, 0)), pl.BlockSpec((nsel, CHUNK), lambda c: (0, 0)),
                  pl.BlockSpec((CHUNK, CHUNK), lambda c: (0, 0)),
                  pl.BlockSpec((nb, D, HEAD), lambda c: (rc(c), 0, 0)),
                  pl.BlockSpec((None, D, HEAD), lambda c: (jnp.minimum((rc(c) + 1) * nb, nc - 1), 0, 0)),
                  pl.BlockSpec((rb, D), lambda c: (rc(c), 0)), pl.BlockSpec((rb, D), lambda c: (rc(c), 0))],
        out_specs=[pl.BlockSpec((rb, D4), lambda c: (rc(c), 0)), pl.BlockSpec((1, D), lambda c: (0, 0)),
                   pl.BlockSpec((1, HEAD), lambda c: (0, 0))],
        out_shape=[jax.ShapeDtypeStruct((T, D4), BF16), jax.ShapeDtypeStruct((1, D), F32),
                   jax.ShapeDtypeStruct((1, HEAD), F32)],
        scratch_shapes=[pltpu.VMEM((D, HEAD), F32)], sem=("arbitrary",), ride=ride,
        args=(p, lb.reshape(1, D), gn.reshape(1, HEAD), sel, rev, s0, s0, o_saved, dog))


def _rope_tables(positions):
    inv_freq = ROPE_THETA ** (-jnp.arange(0, ROPE, 2, dtype=F32) / ROPE)
    ang = positions.astype(F32)[:, None] * inv_freq
    cos, sin = jnp.cos(ang), jnp.sin(ang)
    z = jnp.zeros_like(cos)
    ctab = jnp.concatenate([cos, cos, z, z], axis=-1)
    s1 = jnp.concatenate([-sin, z, z, z], axis=-1)
    s2 = jnp.concatenate([z, sin, z, z], axis=-1)
    return ctab, s1, s2


def _rope(z, ct, s1, s2):
    return z * ct + pltpu.roll(z, 96, 1) * s1 + pltpu.roll(z, 32, 1) * s2


def _rope_t(d, ct, s1, s2):
    return d * ct + pltpu.roll(d * s1, 32, 1) + pltpu.roll(d * s2, 96, 1)


def _mla_prep_fwd(c, wq, wkv, ga_q, ga_kv, gq, gk, ct, s1, s2, *, name, tm=512):
    T, CW = c.shape
    R = (CW - LANES) // 2
    H = wq.shape[1] // QK_PAD
    tm = min(tm, T)

    def body(c_ref, wq_ref, wkv_ref, gaq_ref, gakv_ref, gq_ref, gk_ref, ct_ref, s1_ref, s2_ref,
             q_ref, k_ref, v_ref):
        cv = c_ref[...]
        cq, ckv, kr = cv[:, 0:R], cv[:, R:2 * R], cv[:, 2 * R:2 * R + LANES]
        rq = lax.rsqrt(jnp.mean(cq * cq, axis=-1, keepdims=True) + RMS_EPS)
        cqn = ((cq * rq) * gaq_ref[...]).astype(BF16)
        rk = lax.rsqrt(jnp.mean(ckv * ckv, axis=-1, keepdims=True) + RMS_EPS)
        ckvn = ((ckv * rk) * gakv_ref[...]).astype(BF16)
        qp = _dot(cqn, wq_ref[...], NN)
        kvp = _dot(ckvn, wkv_ref[...], NN)
        ctv, s1v, s2v = ct_ref[...], s1_ref[...], s2_ref[...]
        gqv, gkv = gq_ref[...], gk_ref[...]
        krs = jnp.sum(kr * kr, axis=-1, keepdims=True)
        for h in range(H):
            b = h * QK_PAD
            qn, qr = qp[:, b:b + HEAD], qp[:, b + HEAD:b + QK_PAD]
            ss = jnp.sum(qn * qn + qr * qr, axis=-1, keepdims=True)
            rr = lax.rsqrt(ss * (1.0 / QK_HEAD) + RMS_EPS)
            q_ref[:, b:b + HEAD] = ((qn * rr) * gqv[:, 0:HEAD]).astype(q_ref.dtype)
            q_ref[:, b + HEAD:b + QK_PAD] = _rope((qr * rr) * gqv[:, HEAD:QK_PAD], ctv, s1v, s2v).astype(q_ref.dtype)
            kn, vv = kvp[:, b:b + HEAD], kvp[:, b + HEAD:b + QK_PAD]
            ssk = jnp.sum(kn * kn, axis=-1, keepdims=True) + krs
            rrk = lax.rsqrt(ssk * (1.0 / QK_HEAD) + RMS_EPS)
            k_ref[:, b:b + HEAD] = ((kn * rrk) * gkv[:, 0:HEAD]).astype(k_ref.dtype)
            k_ref[:, b + HEAD:b + QK_PAD] = _rope((kr * rrk) * gkv[:, HEAD:QK_PAD], ctv, s1v, s2v).astype(k_ref.dtype)
            v_ref[:, h * HEAD:(h + 1) * HEAD] = vv.astype(v_ref.dtype)

    full = lambda shape: pl.BlockSpec(shape, lambda i: (0, 0))
    tok = lambda w: pl.BlockSpec((tm, w), lambda i: (i, 0))
    return pl.pallas_call(
        body, name=name, grid=(T // tm,),
        in_specs=[tok(CW), full(wq.shape), full(wkv.shape), full((1, R)), full((1, R)), full((1, QK_PAD)),
                  full((1, QK_PAD)), tok(LANES), tok(LANES), tok(LANES)],
        out_specs=[tok(H * QK_PAD), tok(H * QK_PAD), tok(H * HEAD)],
        out_shape=[jax.ShapeDtypeStruct((T, H * QK_PAD), BF16), jax.ShapeDtypeStruct((T, H * QK_PAD), BF16),
                   jax.ShapeDtypeStruct((T, H * HEAD), BF16)],
        compiler_params=_params(("parallel",)),
    )(c, wq, wkv, ga_q.reshape(1, R), ga_kv.reshape(1, R), gq, gk, ct, s1, s2)


def _mla_prep_bwd(c, wq, wkv, ga_q, ga_kv, gq, gk, ct, s1, s2, dq, dk, dv, *, name, tm=512):
    T, CW = c.shape
    R = (CW - LANES) // 2
    H = wq.shape[1] // QK_PAD
    tm = min(tm, T)

    def body(c_ref, wq_ref, wkv_ref, gaq_ref, gakv_ref, gq_ref, gk_ref, ct_ref, s1_ref, s2_ref,
             dq_ref, dk_ref, dv_ref,
             dc_ref, dwq_ref, dwkv_ref, dgaq_ref, dgakv_ref, dgq_ref, dgk_ref, dqp_ref, dkvp_ref):
        i = pl.program_id(0)

        @pl.when(i == 0)
        def _():
            for ref in (dwq_ref, dwkv_ref, dgaq_ref, dgakv_ref, dgq_ref, dgk_ref):
                ref[...] = jnp.zeros_like(ref)

        cv = c_ref[...]
        cq, ckv, kr = cv[:, 0:R], cv[:, R:2 * R], cv[:, 2 * R:2 * R + LANES]
        rq = lax.rsqrt(jnp.mean(cq * cq, axis=-1, keepdims=True) + RMS_EPS)
        nq = cq * rq
        cqn = (nq * gaq_ref[...]).astype(BF16)
        rk = lax.rsqrt(jnp.mean(ckv * ckv, axis=-1, keepdims=True) + RMS_EPS)
        nkv = ckv * rk
        ckvn = (nkv * gakv_ref[...]).astype(BF16)
        qp = _dot(cqn, wq_ref[...], NN)
        kvp = _dot(ckvn, wkv_ref[...], NN)
        ctv, s1v, s2v = ct_ref[...], s1_ref[...], s2_ref[...]
        gqv, gkv = gq_ref[...], gk_ref[...]
        krs = jnp.sum(kr * kr, axis=-1, keepdims=True)
        dkr = jnp.zeros((tm, LANES), F32)
        dgq_n = jnp.zeros((1, HEAD), F32)
        dgq_r = jnp.zeros((1, HEAD), F32)
        dgk_n = jnp.zeros((1, HEAD), F32)
        dgk_r = jnp.zeros((1, HEAD), F32)
        for h in range(H):
            b = h * QK_PAD
            qn, qr = qp[:, b:b + HEAD], qp[:, b + HEAD:b + QK_PAD]
            ss = jnp.sum(qn * qn + qr * qr, axis=-1, keepdims=True)
            rr = lax.rsqrt(ss * (1.0 / QK_HEAD) + RMS_EPS)
            un, ur = qn * rr, qr * rr
            dzn = dq_ref[:, b:b + HEAD]
            dzr = _rope_t(dq_ref[:, b + HEAD:b + QK_PAD], ctv, s1v, s2v)
            dgq_n = dgq_n + jnp.sum(dzn * un, axis=0, keepdims=True)
            dgq_r = dgq_r + jnp.sum(dzr * ur, axis=0, keepdims=True)
            dun, dur = dzn * gqv[:, 0:HEAD], dzr * gqv[:, HEAD:QK_PAD]
            m = jnp.sum(dun * un + dur * ur, axis=-1, keepdims=True) * (1.0 / QK_HEAD)
            dqp_ref[:, b:b + HEAD] = (rr * (dun - un * m)).astype(BF16)
            dqp_ref[:, b + HEAD:b + QK_PAD] = (rr * (dur - ur * m)).astype(BF16)
            kn = kvp[:, b:b + HEAD]
            ssk = jnp.sum(kn * kn, axis=-1, keepdims=True) + krs
            rrk = lax.rsqrt(ssk * (1.0 / QK_HEAD) + RMS_EPS)
            vn, vr = kn * rrk, kr * rrk
            dyn = dk_ref[:, b:b + HEAD]
            dyr = _rope_t(dk_ref[:, b + HEAD:b + QK_PAD], ctv, s1v, s2v)
            dgk_n = dgk_n + jnp.sum(dyn * vn, axis=0, keepdims=True)
            dgk_r = dgk_r + jnp.sum(dyr * vr, axis=0, keepdims=True)
            dvn, dvr = dyn * gkv[:, 0:HEAD], dyr * gkv[:, HEAD:QK_PAD]
            mk = jnp.sum(dvn * vn + dvr * vr, axis=-1, keepdims=True) * (1.0 / QK_HEAD)
            dkvp_ref[:, b:b + HEAD] = (rrk * (dvn - vn * mk)).astype(BF16)
            dkr = dkr + rrk * (dvr - vr * mk)
            dkvp_ref[:, b + HEAD:b + QK_PAD] = dv_ref[:, h * HEAD:(h + 1) * HEAD].astype(BF16)
        dgq_ref[:, 0:HEAD] += dgq_n
        dgq_ref[:, HEAD:QK_PAD] += dgq_r
        dgk_ref[:, 0:HEAD] += dgk_n
        dgk_ref[:, HEAD:QK_PAD] += dgk_r
        dqp = dqp_ref[...]
        dkvp = dkvp_ref[...]
        dwq_ref[...] += _dot(cqn, dqp, TN)
        dwkv_ref[...] += _dot(ckvn, dkvp, TN)
        dcqn = _dot(dqp, wq_ref[...], NT)
        dckvn = _dot(dkvp, wkv_ref[...], NT)
        dgaq_ref[...] += jnp.sum(dcqn * nq, axis=0, keepdims=True)
        dgakv_ref[...] += jnp.sum(dckvn * nkv, axis=0, keepdims=True)
        dnq = dcqn * gaq_ref[...]
        dnkv = dckvn * gakv_ref[...]
        dc_ref[:, 0:R] = (rq * (dnq - nq * jnp.mean(dnq * nq, axis=-1, keepdims=True))).astype(dc_ref.dtype)
        dc_ref[:, R:2 * R] = (rk * (dnkv - nkv * jnp.mean(dnkv * nkv, axis=-1, keepdims=True))).astype(dc_ref.dtype)
        dc_ref[:, 2 * R:2 * R + LANES] = dkr.astype(dc_ref.dtype)

    full = lambda shape: pl.BlockSpec(shape, lambda i: (0, 0))
    tok = lambda w: pl.BlockSpec((tm, w), lambda i: (i, 0))
    return pl.pallas_call(
        body, name=name, grid=(T // tm,),
        in_specs=[tok(CW), full(wq.shape), full(wkv.shape), full((1, R)), full((1, R)), full((1, QK_PAD)),
                  full((1, QK_PAD)), tok(LANES), tok(LANES), tok(LANES),
                  tok(H * QK_PAD), tok(H * QK_PAD), tok(H * HEAD)],
        out_specs=[tok(CW), full(wq.shape), full(wkv.shape), full((1, R)), full((1, R)), full((1, QK_PAD)),
                   full((1, QK_PAD))],
        out_shape=[jax.ShapeDtypeStruct((T, CW), BF16), jax.ShapeDtypeStruct(wq.shape, F32),
                   jax.ShapeDtypeStruct(wkv.shape, F32), jax.ShapeDtypeStruct((1, R), F32),
                   jax.ShapeDtypeStruct((1, R), F32), jax.ShapeDtypeStruct((1, QK_PAD), F32),
                   jax.ShapeDtypeStruct((1, QK_PAD), F32)],
        scratch_shapes=[pltpu.VMEM((tm, H * QK_PAD), BF16), pltpu.VMEM((tm, H * QK_PAD), BF16)],
        compiler_params=_params(("arbitrary",)),
    )(c, wq, wkv, ga_q.reshape(1, R), ga_kv.reshape(1, R), gq, gk, ct, s1, s2, dq, dk, dv)


NEG = -1e30
LOG2E = 1.4426950408889634


def _attn_fwd(q, k, v, *, name, tb=512, hp=2, ride=None):
    T = q.shape[0]
    H = q.shape[1] // QK_PAD
    tb = min(tb, T)
    nq = T // tb
    scale = QK_HEAD ** -0.5
    c2 = scale * LOG2E
    assert H % hp == 0

    def body(q_ref, k_ref, v_ref, ot_ref, lse_ref, m_ref, l_ref, acc_ref):
        i = pl.program_id(1)
        m_ref[...] = jnp.full_like(m_ref, NEG)
        l_ref[...] = jnp.zeros_like(l_ref)
        acc_ref[...] = jnp.zeros_like(acc_ref)

        def step(js, masked):
            offs = [pl.multiple_of(j * tb, tb) for j in js]
            sts = [[_dot(k_ref[pl.ds(off, tb), hh * QK_PAD:(hh + 1) * QK_PAD],
                         q_ref[:, hh * QK_PAD:(hh + 1) * QK_PAD], NT) for hh in range(hp)] for off in offs]
            for b, hh in [(b, hh) for b in range(len(js)) for hh in range(hp)]:
                off = offs[b]
                vs = slice(hh * HEAD, (hh + 1) * HEAD)
                vb = v_ref[pl.ds(off, tb), vs]
                st = sts[b][hh]
                if masked:
                    kpos = lax.broadcasted_iota(jnp.int32, (tb, tb), 0)
                    qpos = lax.broadcasted_iota(jnp.int32, (tb, tb), 1)
                    st = jnp.where(kpos <= qpos, st, NEG)
                m_old = m_ref[hh]
                m_new = jnp.maximum(m_old, jnp.max(st, axis=0, keepdims=True))
                alpha = jnp.exp2((m_old - m_new) * c2)
                pt = jnp.exp2((st - m_new) * c2)
                l_ref[hh] = l_ref[hh] * alpha + jnp.sum(pt, axis=0, keepdims=True)
                acc_ref[vs, :] = acc_ref[vs, :] * alpha + _dot(vb, pt.astype(BF16), TN)
                m_ref[hh] = m_new

        def pair_body(t, carry):
            step([2 * t, 2 * t + 1], False)
            return carry

        lax.fori_loop(0, i // 2, pair_body, 0)

        @pl.when(i % 2 == 1)
        def _():
            step([i - 1], False)

        step([i], True)
        for hh in range(hp):
            vs = slice(hh * HEAD, (hh + 1) * HEAD)
            l = l_ref[hh]
            ot_ref[vs, :] = (acc_ref[vs, :] / l).astype(ot_ref.dtype)
            lse_ref[hh] = m_ref[hh] * scale + jnp.log(l)

    return _call(
        body, name=name, grid=(H // hp, nq),
        in_specs=[pl.BlockSpec((tb, hp * QK_PAD), lambda g, i: (i, g)),
                  pl.BlockSpec((T, hp * QK_PAD), lambda g, i: (0, g)),
                  pl.BlockSpec((T, hp * HEAD), lambda g, i: (0, g))],
        out_specs=[pl.BlockSpec((None, hp * HEAD, tb), lambda g, i: (i, g, 0)),
                   pl.BlockSpec((hp, None, 1, tb), lambda g, i: (g, i, 0, 0))],
        out_shape=[jax.ShapeDtypeStruct((nq, H * HEAD, tb), BF16), jax.ShapeDtypeStruct((H, nq, 1, tb), F32)],
        scratch_shapes=[pltpu.VMEM((hp, 1, tb), F32), pltpu.VMEM((hp, 1, tb), F32),
                        pltpu.VMEM((hp * HEAD, tb), F32)],
        sem=("parallel", "arbitrary"), ride=ride, args=(q, k, v))


def _attn_bwd(q, k, v, ot, lse, dot_, *, name, ride=None):
    T = q.shape[0]
    H = q.shape[1] // QK_PAD
    nq, _, tb = ot.shape
    scale = QK_HEAD ** -0.5
    c2 = scale * LOG2E

    def body(q_ref, k_ref, v_ref, ot_ref, lse_ref, dot_ref, dq_ref, dk_ref, dv_ref, dka_ref, dva_ref):
        j = pl.program_id(1)

        @pl.when(j == 0)
        def _():
            dq_ref[...] = jnp.zeros_like(dq_ref)

        kb = k_ref[...]
        vb = v_ref[...]

        def step(i, masked):
            off = i * tb if isinstance(i, int) else pl.multiple_of(i * tb, tb)
            qb = q_ref[pl.ds(off, tb), :]
            dob = dot_ref[i]
            ob = ot_ref[i]
            st = _dot(kb, qb, NT)
            if masked:
                kpos = lax.broadcasted_iota(jnp.int32, (tb, tb), 0)
                qpos = lax.broadcasted_iota(jnp.int32, (tb, tb), 1)
                st = jnp.where(kpos <= qpos, st, NEG)
            pt = jnp.exp2(st * c2 - lse_ref[i] * LOG2E)
            dpt = _dot(vb, dob, NN)
            delta = jnp.sum(dob.astype(F32) * ob.astype(F32), axis=0, keepdims=True)
            dst = (pt * (dpt - delta)).astype(BF16)
            dq_ref[pl.ds(off, tb), :] += _dot(dst, kb, TN)
            return _dot(dst, qb, NN), _dot(pt.astype(BF16), dob, NT)

        dk0, dv0 = step(j, True)
        dka_ref[...] = dk0
        dva_ref[...] = dv0

        rest = nq - 1 - j

        def pair_body(t, carry):
            i0 = j + 1 + 2 * t
            dk1, dv1 = step(i0, False)
            dk2, dv2 = step(i0 + 1, False)
            dka_ref[...] += dk1 + dk2
            dva_ref[...] += dv1 + dv2
            return carry

        lax.fori_loop(0, rest // 2, pair_body, 0)

        @pl.when(rest % 2 == 1)
        def _():
            dk1, dv1 = step(nq - 1, False)
            dka_ref[...] += dk1
            dva_ref[...] += dv1

        dk_ref[...] = dka_ref[...] * scale
        dv_ref[...] = dva_ref[...]

        @pl.when(j == nq - 1)
        def _():
            dq_ref[...] = dq_ref[...] * scale

    return _call(
        body, name=name, grid=(H, nq),
        in_specs=[pl.BlockSpec((T, QK_PAD), lambda h, j: (0, h)), pl.BlockSpec((tb, QK_PAD), lambda h, j: (j, h)),
                  pl.BlockSpec((tb, HEAD), lambda h, j: (j, h)),
                  pl.BlockSpec((nq, HEAD, tb), lambda h, j: (0, h, 0)),
                  pl.BlockSpec((None, nq, 1, tb), lambda h, j: (h, 0, 0, 0)),
                  pl.BlockSpec((nq, HEAD, tb), lambda h, j: (0, h, 0))],
        out_specs=[pl.BlockSpec((T, QK_PAD), lambda h, j: (0, h)), pl.BlockSpec((tb, QK_PAD), lambda h, j: (j, h)),
                   pl.BlockSpec((tb, HEAD), lambda h, j: (j, h))],
        out_shape=[jax.ShapeDtypeStruct((T, H * QK_PAD), F32), jax.ShapeDtypeStruct((T, H * QK_PAD), F32),
                   jax.ShapeDtypeStruct((T, H * HEAD), F32)],
        scratch_shapes=[pltpu.VMEM((tb, QK_PAD), F32), pltpu.VMEM((tb, HEAD), F32)],
        sem=("parallel", "arbitrary"), ride=ride, args=(q, k, v, ot, lse, dot_))


def _conv_taps(u, prev6, prev7):
    rows = lax.broadcasted_iota(jnp.int32, (u.shape[0], 1), 0)
    u1 = jnp.where(rows >= 1, pltpu.roll(u, 1, 0), prev7)
    u2 = jnp.where(rows >= 2, pltpu.roll(u, 2, 0), jnp.where(rows == 0, prev6, prev7))
    return u2, u1


def _conv_taps_ahead(d, next0, next1):
    tm = d.shape[0]
    rows = lax.broadcasted_iota(jnp.int32, (tm, 1), 0)
    d1 = jnp.where(rows < tm - 1, pltpu.roll(d, tm - 1, 0), next0)
    d2 = jnp.where(rows < tm - 2, pltpu.roll(d, tm - 2, 0), jnp.where(rows == tm - 2, next0, next1))
    return d1, d2


def _ffn_up_fwd(h, w_up, conv_w, conv_b, *, name, tm=512, ride=None):
    T, D = h.shape
    ns, _, fs = w_up.shape
    nh = ns // 2
    tm = min(tm, T)

    def body(h_ref, wg_ref, wu_ref, cwg_ref, cwu_ref, cbg_ref, cbu_ref, a_ref, u_ref, y_ref, cg_ref, cu_ref):
        i = pl.program_id(1)

        @pl.when(i == 0)
        def _():
            cg_ref[...] = jnp.zeros_like(cg_ref)
            cu_ref[...] = jnp.zeros_like(cu_ref)

        hv = h_ref[...]
        ys = []
        for idx, (w_ref, cw_ref, cb_ref, carry) in enumerate(
                ((wg_ref, cwg_ref, cbg_ref, cg_ref), (wu_ref, cwu_ref, cbu_ref, cu_ref))):
            u = _dot(hv, w_ref[...], NN)
            u_ref[idx] = u.astype(u_ref.dtype)
            u2, u1 = _conv_taps(u, carry[6:7, :], carry[7:8, :])
            y = cb_ref[...] + u2 * cw_ref[0:1, :]
            y = y + u1 * cw_ref[1:2, :]
            y = y + u * cw_ref[2:3, :]
            y_ref[idx] = y.astype(y_ref.dtype)
            ys.append(y)
            carry[...] = u[tm - 8:tm, :]
        yg, yu = ys
        a_ref[...] = ((yg * _sigmoid(yg)) * yu).astype(a_ref.dtype)

    shard = lambda r, off: pl.BlockSpec((None, r, fs), lambda j, i: (j + off, 0, 0))
    return _call(
        body, name=name, grid=(nh, T // tm),
        in_specs=[pl.BlockSpec((tm, D), lambda j, i: (i, 0)), shard(D, 0), shard(D, nh),
                  shard(3, 0), shard(3, nh), shard(1, 0), shard(1, nh)],
        out_specs=[pl.BlockSpec((None, tm, fs), lambda j, i: (j, i, 0)),
                   pl.BlockSpec((2, None, tm, fs), lambda j, i: (0, j, i, 0)),
                   pl.BlockSpec((2, None, tm, fs), lambda j, i: (0, j, i, 0))],
        out_shape=[jax.ShapeDtypeStruct((nh, T, fs), BF16), jax.ShapeDtypeStruct((2, nh, T, fs), BF16),
                   jax.ShapeDtypeStruct((2, nh, T, fs), BF16)],
        scratch_shapes=[pltpu.VMEM((8, fs), F32), pltpu.VMEM((8, fs), F32)],
        sem=("parallel", "arbitrary"), ride=ride, args=(h, w_up, w_up, conv_w, conv_w, conv_b, conv_b))


def _ffn_act_bwd(dxo, w_down, u, y, conv_w, *, name, tm=512):
    T, D = dxo.shape
    _, nh, _, fs = u.shape
    tm = min(tm, T)
    nt = T // tm

    def body(dx_ref, wd_ref, u_ref, y_ref, cwg_ref, cwu_ref, du_ref, dcw_ref, dcb_ref, cg_ref, cu_ref):
        i = pl.program_id(1)

        @pl.when(i == 0)
        def _():
            cg_ref[...] = jnp.zeros_like(cg_ref)
            cu_ref[...] = jnp.zeros_like(cu_ref)
            dcw_ref[...] = jnp.zeros_like(dcw_ref)
            dcb_ref[...] = jnp.zeros_like(dcb_ref)

        da = _dot(dx_ref[...].astype(BF16), wd_ref[...], NT)
        yg, yu = y_ref[0].astype(F32), y_ref[1].astype(F32)
        sg = _sigmoid(yg)
        dys = (da * yu * (sg * (1.0 + yg * (1.0 - sg))), da * (yg * sg))
        for idx, (cw_ref, carry) in enumerate(((cwg_ref, cg_ref), (cwu_ref, cu_ref))):
            dy = dys[idx]
            uv = u_ref[idx].astype(F32)
            d1, d2 = _conv_taps_ahead(dy, carry[0:1, :], carry[1:2, :])
            dcb_ref[idx] += jnp.sum(dy, axis=0, keepdims=True)
            dcw_ref[idx, 0:1, :] += jnp.sum(d2 * uv, axis=0, keepdims=True)
            dcw_ref[idx, 1:2, :] += jnp.sum(d1 * uv, axis=0, keepdims=True)
            dcw_ref[idx, 2:3, :] += jnp.sum(dy * uv, axis=0, keepdims=True)
            du = dy * cw_ref[2:3, :] + d1 * cw_ref[1:2, :] + d2 * cw_ref[0:1, :]
            du_ref[idx] = du.astype(du_ref.dtype)
            carry[...] = dy[0:8, :]

    rt = lambda i: nt - 1 - i
    shard = lambda r, off: pl.BlockSpec((None, r, fs), lambda j, i: (j + off, 0, 0))
    tile = pl.BlockSpec((2, None, tm, fs), lambda j, i: (0, j, rt(i), 0))
    return pl.pallas_call(
        body, name=name, grid=(nh, nt),
        in_specs=[pl.BlockSpec((tm, D), lambda j, i: (rt(i), 0)), pl.BlockSpec((fs, D), lambda j, i: (j, 0)),
                  tile, tile, shard(3, 0), shard(3, nh)],
        out_specs=[tile, pl.BlockSpec((2, None, 3, fs), lambda j, i: (0, j, 0, 0)),
                   pl.BlockSpec((2, None, 1, fs), lambda j, i: (0, j, 0, 0))],
        out_shape=[jax.ShapeDtypeStruct((2, nh, T, fs), BF16), jax.ShapeDtypeStruct((2, nh, 3, fs), F32),
                   jax.ShapeDtypeStruct((2, nh, 1, fs), F32)],
        scratch_shapes=[pltpu.VMEM((8, fs), F32), pltpu.VMEM((8, fs), F32)],
        compiler_params=_params(("parallel", "arbitrary")),
    )(dxo, w_down, u, y, conv_w, conv_w)


def _pad_cols(w, n):
    return jnp.pad(w, [(0, 0)] * (w.ndim - 1) + [(0, n - w.shape[-1])])


def _q_up_padded(w):
    R = w.shape[0]
    H = w.shape[1] // QK_HEAD
    return _pad_cols(w.reshape(R, H, QK_HEAD), QK_PAD).reshape(R, H * QK_PAD)


def _q_up_unpadded(w):
    R = w.shape[0]
    H = w.shape[1] // QK_PAD
    return w.reshape(R, H, QK_PAD)[:, :, :QK_HEAD].reshape(R, H * QK_HEAD)


def _xchg_copies(src_refs, out_refs, kinds, send_sems, recv_sems, local_sems):
    x, y, c = lax.axis_index("x"), lax.axis_index("y"), lax.axis_index("c")
    me = 4 * x + 2 * y + c
    copies = []
    for b, kind in enumerate(kinds):
        gather = kind == "gather"
        own = src_refs[b] if gather else src_refs[b].at[me]
        copies.append(pltpu.make_async_copy(own, out_refs[b].at[me], local_sems.at[b]))
        for kk in range(1, N_DEV):
            px = 1 - x if kk & 4 else x
            py = 1 - y if kk & 2 else y
            pc = 1 - c if kk & 1 else c
            peer = 4 * px + 2 * py + pc
            src = src_refs[b] if gather else src_refs[b].at[peer]
            copies.append(pltpu.make_async_remote_copy(
                src_ref=src, dst_ref=out_refs[b].at[me],
                send_sem=send_sems.at[b * (N_DEV - 1) + kk - 1],
                recv_sem=recv_sems.at[b * (N_DEV - 1) + kk - 1],
                device_id=(px, py, pc), device_id_type=pl.DeviceIdType.MESH))
    return copies


def _xchg_out_shapes(srcs, kinds):
    return [jax.ShapeDtypeStruct((N_DEV,) + s.shape if kind == "gather" else s.shape, s.dtype)
            for s, kind in zip(srcs, kinds)]


def _xchg_scratch(n):
    return [pltpu.SemaphoreType.DMA((n * (N_DEV - 1),)), pltpu.SemaphoreType.DMA((n * (N_DEV - 1),)),
            pltpu.SemaphoreType.DMA((n,))]


def _exchange(srcs, kinds, *, name):
    n = len(srcs)

    def body(*refs):
        copies = _xchg_copies(refs[:n], refs[n:2 * n], kinds, *refs[2 * n:])
        for cp in copies:
            cp.start()
        for cp in copies:
            cp.wait()

    hbm = pl.BlockSpec(memory_space=pl.ANY)
    return pl.pallas_call(
        body, name=name, in_specs=[hbm] * n, out_specs=[hbm] * n, out_shape=_xchg_out_shapes(srcs, kinds),
        scratch_shapes=_xchg_scratch(n),
    )(*srcs)


def _call(body, *, name, grid, in_specs, out_specs, out_shape, scratch_shapes, args, sem, ride=None):
    if ride is None:
        outs = pl.pallas_call(body, name=name, grid=grid, in_specs=in_specs, out_specs=out_specs,
                              out_shape=out_shape, scratch_shapes=scratch_shapes,
                              compiler_params=_params(sem))(*args)
        return list(outs), []
    srcs, kinds = ride
    n_in, n_out, n_sc, nx = len(in_specs), len(out_specs), len(scratch_shapes), len(srcs)

    def wrapped(*refs):
        ins, xs = refs[:n_in], refs[n_in:n_in + nx]
        o0 = n_in + nx
        outs, xo = refs[o0:o0 + n_out], refs[o0 + n_out:o0 + n_out + nx]
        s0 = o0 + n_out + nx
        sc, sems = refs[s0:s0 + n_sc], refs[s0 + n_sc:]
        first = functools.reduce(jnp.logical_and, [pl.program_id(d) == 0 for d in range(len(grid))])
        last = functools.reduce(jnp.logical_and, [pl.program_id(d) == grid[d] - 1 for d in range(len(grid))])

        @pl.when(first)
        def _():
            for cp in _xchg_copies(xs, xo, kinds, *sems):
                cp.start()

        body(*ins, *outs, *sc)

        @pl.when(last)
        def _():
            for cp in _xchg_copies(xs, xo, kinds, *sems):
                cp.wait()

    hbm = pl.BlockSpec(memory_space=pl.ANY)
    outs = pl.pallas_call(
        wrapped, name=name, grid=grid, in_specs=list(in_specs) + [hbm] * nx,
        out_specs=list(out_specs) + [hbm] * nx, out_shape=list(out_shape) + _xchg_out_shapes(srcs, kinds),
        scratch_shapes=list(scratch_shapes) + _xchg_scratch(nx),
        compiler_params=_params(("arbitrary",) * len(grid)),
    )(*args, *srcs)
    return list(outs[:n_out]), list(outs[n_out:])


def _sum_slots(parts, *, name):
    _, Rr, C = parts.shape

    def body(p_ref, o_ref):
        acc = p_ref[0].astype(F32)
        for d in range(1, N_DEV):
            acc = acc + p_ref[d].astype(F32)
        o_ref[...] = acc

    return pl.pallas_call(
        body, name=name, grid=(1,),
        in_specs=[pl.BlockSpec((N_DEV, Rr, C), lambda i: (0, 0, 0))],
        out_specs=pl.BlockSpec((Rr, C), lambda i: (0, 0)),
        out_shape=jax.ShapeDtypeStruct((Rr, C), F32),
        compiler_params=_params(("arbitrary",)),
    )(parts)


def _row_tile(rows, cap=512):
    if rows <= cap:
        return rows
    d = (cap // 8) * 8
    while d >= 8:
        if rows % d == 0:
            return d
        d -= 8
    raise ValueError(f"no row tile for {rows}")


def _adamw(parts, w, m, v, *, name):
    S, Rr, C = parts.shape
    tr = _row_tile(Rr)
    c1 = 1.0 - ADAM_B1 ** ADAM_STEP
    c2 = 1.0 - ADAM_B2 ** ADAM_STEP

    def body(p_ref, w_ref, m_ref, v_ref, g_ref, d_ref, nm_ref, nv_ref):
        g = p_ref[0].astype(F32)
        for d in range(1, S):
            g = g + p_ref[d].astype(F32)
        mm = ADAM_B1 * m_ref[...] + (1.0 - ADAM_B1) * g
        vv = ADAM_B2 * v_ref[...] + (1.0 - ADAM_B2) * (g * g)
        m_hat = mm / c1
        v_hat = vv / c2
        g_ref[...] = g
        d_ref[...] = -ADAM_LR * (m_hat / (jnp.sqrt(v_hat) + ADAM_EPS) + ADAM_WD * w_ref[...])
        nm_ref[...] = mm
        nv_ref[...] = vv

    spec = pl.BlockSpec((tr, C), lambda i: (i, 0))
    shape = jax.ShapeDtypeStruct((Rr, C), F32)
    return pl.pallas_call(
        body, name=name, grid=(Rr // tr,),
        in_specs=[pl.BlockSpec((S, tr, C), lambda i: (0, i, 0)), spec, spec, spec],
        out_specs=[spec] * 4, out_shape=[shape] * 4,
        compiler_params=_params(("parallel",)),
    )(parts, w, m, v)


def _pack(arrs, dtype, row_mult):
    flat = jnp.concatenate([a.reshape(-1).astype(dtype) for a in arrs])
    per = row_mult * PACK_COLS
    total = -(-flat.shape[0] // per) * per
    return jnp.pad(flat, (0, total - flat.shape[0])).reshape(total // PACK_COLS, PACK_COLS)


def _unpack(packed, shapes, lead=()):
    flat = packed.reshape(lead + (-1,))
    out, off = [], 0
    for shp in shapes:
        n = 1
        for d in shp:
            n *= d
        out.append(flat[..., off:off + n].reshape(lead + tuple(shp)))
        off += n
    return out


HGRN_W = ("hgrn_w_in", "hgrn_w_out")
MLA_W = ("mla_w_in", "mla_w_q_up", "mla_w_kv_up", "mla_w_out")
FFN_W = ("ffn_w_up", "ffn_w_down")
BIG = HGRN_W + MLA_W + FFN_W
SMALL_SHARDED = {"ffn_conv_w": 2, "mla_q_a_norm": 1, "mla_kv_a_norm": 1}
REPLICATED = ["norm_mix", "norm_ffn", "hgrn_lower_bounds", "hgrn_out_norm", "mla_q_norm", "mla_k_norm",
              "ffn_conv_b"]
WEIGHTS = ["norm_mix", "norm_ffn", "hgrn_w_in", "hgrn_lower_bounds", "hgrn_out_norm", "hgrn_w_out", "mla_w_in",
           "mla_q_a_norm", "mla_w_q_up", "mla_kv_a_norm", "mla_w_kv_up", "mla_q_norm", "mla_k_norm", "mla_w_out",
           "ffn_w_up", "ffn_conv_w", "ffn_conv_b", "ffn_w_down"]


def _shards_to_cols(g):
    return g.transpose(1, 0, 2).reshape(g.shape[1], N_DEV * g.shape[2])


def _cols_to_shards(w):
    R = w.shape[0]
    return w.reshape(R, N_DEV, w.shape[1] // N_DEV).transpose(1, 0, 2)


def kernel(x, positions, norm_mix, norm_ffn, hgrn_w_in, hgrn_lower_bounds, hgrn_out_norm, hgrn_w_out, mla_w_in, mla_q_a_norm, mla_w_q_up, mla_kv_a_norm, mla_w_kv_up, mla_q_norm, mla_k_norm, mla_w_out, ffn_w_up, ffn_conv_w, ffn_conv_b, ffn_w_down, loss_target, m_norm_mix, m_norm_ffn, m_hgrn_w_in, m_hgrn_lower_bounds, m_hgrn_out_norm, m_hgrn_w_out, m_mla_w_in, m_mla_q_a_norm, m_mla_w_q_up, m_mla_kv_a_norm, m_mla_w_kv_up, m_mla_q_norm, m_mla_k_norm, m_mla_w_out, m_ffn_w_up, m_ffn_conv_w, m_ffn_conv_b, m_ffn_w_down, v_norm_mix, v_norm_ffn, v_hgrn_w_in, v_hgrn_lower_bounds, v_hgrn_out_norm, v_hgrn_w_out, v_mla_w_in, v_mla_q_a_norm, v_mla_w_q_up, v_mla_kv_a_norm, v_mla_w_kv_up, v_mla_q_norm, v_mla_k_norm, v_mla_w_out, v_ffn_w_up, v_ffn_conv_w, v_ffn_conv_b, v_ffn_w_down):
    local = dict(norm_mix=norm_mix, norm_ffn=norm_ffn, hgrn_w_in=hgrn_w_in, hgrn_lower_bounds=hgrn_lower_bounds,
                 hgrn_out_norm=hgrn_out_norm, hgrn_w_out=hgrn_w_out, mla_w_in=mla_w_in, mla_q_a_norm=mla_q_a_norm,
                 mla_w_q_up=mla_w_q_up, mla_kv_a_norm=mla_kv_a_norm, mla_w_kv_up=mla_w_kv_up, mla_q_norm=mla_q_norm,
                 mla_k_norm=mla_k_norm, mla_w_out=mla_w_out, ffn_w_up=ffn_w_up, ffn_conv_w=ffn_conv_w,
                 ffn_conv_b=ffn_conv_b, ffn_w_down=ffn_w_down)
    mom_m = dict(norm_mix=m_norm_mix, norm_ffn=m_norm_ffn, hgrn_w_in=m_hgrn_w_in,
                 hgrn_lower_bounds=m_hgrn_lower_bounds, hgrn_out_norm=m_hgrn_out_norm, hgrn_w_out=m_hgrn_w_out,
                 mla_w_in=m_mla_w_in, mla_q_a_norm=m_mla_q_a_norm, mla_w_q_up=m_mla_w_q_up,
                 mla_kv_a_norm=m_mla_kv_a_norm, mla_w_kv_up=m_mla_w_kv_up, mla_q_norm=m_mla_q_norm,
                 mla_k_norm=m_mla_k_norm, mla_w_out=m_mla_w_out, ffn_w_up=m_ffn_w_up, ffn_conv_w=m_ffn_conv_w,
                 ffn_conv_b=m_ffn_conv_b, ffn_w_down=m_ffn_w_down)
    mom_v = dict(norm_mix=v_norm_mix, norm_ffn=v_norm_ffn, hgrn_w_in=v_hgrn_w_in,
                 hgrn_lower_bounds=v_hgrn_lower_bounds, hgrn_out_norm=v_hgrn_out_norm, hgrn_w_out=v_hgrn_w_out,
                 mla_w_in=v_mla_w_in, mla_q_a_norm=v_mla_q_a_norm, mla_w_q_up=v_mla_w_q_up,
                 mla_kv_a_norm=v_mla_kv_a_norm, mla_w_kv_up=v_mla_w_kv_up, mla_q_norm=v_mla_q_norm,
                 mla_k_norm=v_mla_k_norm, mla_w_out=v_mla_w_out, ffn_w_up=v_ffn_w_up, ffn_conv_w=v_ffn_conv_w,
                 ffn_conv_b=v_ffn_conv_b, ffn_w_down=v_ffn_w_down)
    me = 4 * lax.axis_index("x") + 2 * lax.axis_index("y") + lax.axis_index("c")
    x, positions, target = x[0], positions[0], loss_target[0]
    T, D = x.shape
    depth = norm_mix.shape[0]
    R = mla_w_q_up.shape[1]
    cw = 2 * R + LANES
    small_names = list(SMALL_SHARDED)

    def block_of(kind, l):
        names = {"hgrn": HGRN_W, "mla": MLA_W, "ffn": FFN_W}[kind]
        idx = l if kind == "ffn" else l // 2
        return [(n, idx) for n in names]

    def mixer_kind(l):
        return "hgrn" if l % 2 == 0 else "mla"

    def riders(l):
        keys = block_of("ffn", l)
        if l + 1 < depth:
            keys += block_of(mixer_kind(l + 1), l + 1)
        return keys

    gathered = {}

    def gather_ride(host, l):
        if host == "mixer" and l % 2 == 0:
            keys = [("ffn_w_up", l)]
        elif host == "mixer":
            keys = block_of("ffn", l)
            if l + 1 < depth:
                keys += block_of("hgrn", l + 1) + [("ffn_w_down", l + 1)]
            if l + 2 < depth:
                keys += block_of("mla", l + 2)
        elif l == 0:
            keys = [("ffn_w_down", 0)] + (block_of("mla", 1) if depth > 1 else [])
        else:
            keys = []
        if not keys:
            return keys, None
        return keys, ([local[n][i].astype(BF16) for n, i in keys], ["gather"] * len(keys))

    def take_gathered(keys, arrs):
        for key, a in zip(keys, arrs):
            gathered[key] = a

    keys0 = block_of("hgrn", 0)
    small_local = _pack([local[n] for n in small_names], F32, 8)
    got = _exchange([local[n][i].astype(BF16) for n, i in keys0] + [small_local],
                    ["gather"] * (len(keys0) + 1), name="gather_first")
    take_gathered(keys0, got[:-1])
    small_all = _unpack(got[-1], [local[n].shape for n in small_names], lead=(N_DEV,))
    conv_w_all = small_all[0].transpose(1, 0, 2, 3)
    qa_all = small_all[1].transpose(1, 0, 2).reshape(-1, R)
    kva_all = small_all[2].transpose(1, 0, 2).reshape(-1, R)
    fs = conv_w_all.shape[-1]
    conv_b_s = ffn_conv_b.reshape(depth, N_DEV, 1, fs)

    ct, s1, s2 = _rope_tables(positions)
    lb_soft = jax.nn.softmax(hgrn_lower_bounds.astype(F32), axis=0)
    lower_bounds = jnp.cumsum(lb_soft, axis=0) - lb_soft[0:1]

    def mla_views(j):
        w_in = _pad_cols(gathered["mla_w_in", j].reshape(D, -1), cw)
        wq = _q_up_padded(_shards_to_cols(gathered["mla_w_q_up", j]))
        wkv = _shards_to_cols(gathered["mla_w_kv_up", j])
        gq = _pad_cols(mla_q_norm[j].reshape(1, QK_HEAD), QK_PAD)
        gk = _pad_cols(mla_k_norm[j].reshape(1, QK_HEAD), QK_PAD)
        return w_in, wq, wkv, gq, gk

    saved = []
    h = _rmsnorm_fwd(x, norm_mix[0], name="norm_mix_fwd_0")
    for layer in range(depth):
        j = layer // 2
        s = {"x_in": x}
        s["h_mix"] = h
        keys, ride = gather_ride("mixer", layer)
        if layer % 2 == 0:
            p = _mm(h, gathered["hgrn_w_in", j], b_fmt="knb", bm=1024, name=f"hgrn_in_{layer}")
            (og, o, s0), got = _hgrn_fwd(p, lower_bounds[j], hgrn_out_norm[j], name=f"hgrn_fwd_{layer}", ride=ride)
            s.update(p=p, og=og, o=o, s0=s0)
            take_gathered(keys, got)
            x, h = _mm(og, gathered["hgrn_w_out", j].reshape(D, D), res=x, norm_out=norm_ffn[layer],
                       name=f"hgrn_out_{layer}")
        else:
            w_in, wq, wkv, gq, gk = mla_views(j)
            c = _mm(h, w_in, bm=1024, name=f"mla_in_{layer}")
            q, k, v = _mla_prep_fwd(c, wq, wkv, qa_all[j], kva_all[j], gq, gk, ct, s1, s2,
                                    name=f"mla_prep_fwd_{layer}")
            (ot, lse), got = _attn_fwd(q, k, v, name=f"attn_fwd_{layer}", ride=ride)
            s.update(c=c, q=q, k=k, v=v, ot=ot, lse=lse)
            take_gathered(keys, got)
            x, h = _mm(ot, gathered["mla_w_out", j].reshape(D, D), a_fmt="kmb", res=x, norm_out=norm_ffn[layer],
                       name=f"mla_out_{layer}")
        s["x_mid"] = x
        s["h_ffn"] = h
        keys, ride = gather_ride("ffn_up", layer)
        (a, u, y), got = _ffn_up_fwd(h, gathered["ffn_w_up", layer], conv_w_all[layer], conv_b_s[layer],
                                     name=f"ffn_up_{layer}", ride=ride)
        take_gathered(keys, got)
        s.update(a=a, u=u, y=y)
        w_down = gathered["ffn_w_down", layer].reshape(-1, D)
        if layer + 1 < depth:
            x, h = _mm(a, w_down, a_fmt="mkb", res=x, norm_out=norm_mix[layer + 1], bm=1024, bk=fs,
                       name=f"ffn_down_{layer}")
        else:
            x = _mm(a, w_down, a_fmt="mkb", res=x, bm=1024, bk=fs, name=f"ffn_down_{layer}")
        saved.append(s)

    dx, loss_part = _loss_head(x, target, name="loss_head")

    parts = {}
    received = {}
    g_small = {n: [None] * local[n].shape[0] for n in REPLICATED + small_names}

    def scatter_ride(l):
        keys = riders(l)
        return keys, ([parts[key] for key in keys], ["scatter"] * len(keys))

    def take_received(keys, arrs):
        for key, a in zip(keys, arrs):
            received[key] = a

    for layer in reversed(range(depth)):
        j = layer // 2
        s = saved[layer]
        parts["ffn_w_down", layer] = _mm(s["a"], dx, a_fmt="kmb", out_dtype=BF16, name=f"ffn_down_dw_{layer}"
                                         ).reshape(N_DEV, -1, D)
        du, dcw, dcb = _ffn_act_bwd(dx, gathered["ffn_w_down", layer].reshape(-1, D), s["u"], s["y"],
                                    conv_w_all[layer], name=f"ffn_act_bwd_{layer}")
        g_small["ffn_conv_w"][layer] = dcw.reshape(N_DEV, 3, fs)
        g_small["ffn_conv_b"][layer] = dcb.reshape(N_DEV * fs)
        du8 = du.reshape(N_DEV, T, fs)
        parts["ffn_w_up", layer] = _mm(s["h_ffn"], du8, a_fmt="km", b_fmt="knb", out_fmt="mnb", out_dtype=BF16,
                                       bm=1024, name=f"ffn_up_dw_{layer}")
        dx, dgain = _mm(du8, gathered["ffn_w_up", layer], a_fmt="mkb", b_fmt="nkb", bm=1024, bk=fs,
                        norm_bwd=(s["x_mid"], norm_ffn[layer], dx), name=f"ffn_up_dh_{layer}")
        g_small["norm_ffn"][layer] = dgain.reshape(D)
        keys, ride = scatter_ride(layer)
        if layer % 2 == 0:
            w_out = gathered["hgrn_w_out", j].reshape(D, D)
            parts["hgrn_w_out", j] = _mm(s["og"], dx, a_fmt="km", out_dtype=BF16, bm=1024,
                                         name=f"hgrn_out_dw_{layer}").reshape(N_DEV, -1, D)
            dog = _mm(dx, w_out, b_fmt="nk", name=f"hgrn_out_dx_{layer}")
            (dp, dlb, dgn), got = _hgrn_bwd(s["p"], lower_bounds[j], hgrn_out_norm[j], s["s0"], s["o"], dog,
                                            name=f"hgrn_bwd_{layer}", ride=ride)
            take_received(keys, got)
            g_small["hgrn_lower_bounds"][j] = dlb.reshape(D)
            g_small["hgrn_out_norm"][j] = dgn.reshape(HEAD)
            w_in_s = gathered["hgrn_w_in", j]
            parts["hgrn_w_in", j] = _mm(s["h_mix"], dp, a_fmt="km", out_fmt="mnb", out_dtype=BF16, bm=1024,
                                        bn=w_in_s.shape[2], name=f"hgrn_in_dw_{layer}")
            dx, dgain = _mm(dp, w_in_s, b_fmt="nkb", bm=1024, norm_bwd=(s["x_in"], norm_mix[layer], dx),
                            name=f"hgrn_in_dx_{layer}")
        else:
            w_in, wq, wkv, gq, gk = mla_views(j)
            w_out = gathered["mla_w_out", j].reshape(D, D)
            tb = s["ot"].shape[2]
            parts["mla_w_out", j] = _mm(s["ot"], dx, a_fmt="mkb", bk=tb, out_dtype=BF16, bm=1024,
                                        name=f"mla_out_dw_{layer}").reshape(N_DEV, -1, D)
            dot_ = _mm(w_out, dx, b_fmt="nk", out_fmt="mnb", out_dtype=BF16, bm=D, bn=tb,
                       name=f"mla_out_dx_{layer}")
            (dq, dk, dv), got = _attn_bwd(s["q"], s["k"], s["v"], s["ot"], s["lse"], dot_,
                                          name=f"attn_bwd_{layer}", ride=ride)
            take_received(keys, got)
            dc, dwq, dwkv, dgaq, dgakv, dgq, dgk = _mla_prep_bwd(
                s["c"], wq, wkv, qa_all[j], kva_all[j], gq, gk, ct, s1, s2, dq, dk, dv,
                name=f"mla_prep_bwd_{layer}")
            parts["mla_w_q_up", j] = _cols_to_shards(_q_up_unpadded(dwq)).astype(BF16)
            parts["mla_w_kv_up", j] = _cols_to_shards(dwkv).astype(BF16)
            g_small["mla_q_a_norm"][j] = dgaq.reshape(R)
            g_small["mla_kv_a_norm"][j] = dgakv.reshape(R)
            g_small["mla_q_norm"][j] = dgq[0, :QK_HEAD]
            g_small["mla_k_norm"][j] = dgk[0, :QK_HEAD]
            win_cols = mla_w_in.shape[2]
            dw_in = _mm(s["h_mix"], dc, a_fmt="km", bm=1024, name=f"mla_in_dw_{layer}")
            parts["mla_w_in", j] = dw_in[:, :win_cols].astype(BF16).reshape(N_DEV, -1, win_cols)
            dx, dgain = _mm(dc, w_in, b_fmt="nk", norm_bwd=(s["x_in"], norm_mix[layer], dx),
                            name=f"mla_in_dx_{layer}")
        g_small["norm_mix"][layer] = dgain.reshape(D)
    grad_x = dx

    dlb_eff = jnp.stack(g_small["hgrn_lower_bounds"])
    dsoft = jnp.cumsum(dlb_eff[::-1], axis=0)[::-1]
    dsoft = dsoft.at[0].add(-jnp.sum(dlb_eff, axis=0))
    g_lb = lb_soft * (dsoft - jnp.sum(dsoft * lb_soft, axis=0, keepdims=True))
    small_grads = {n: (g_lb if n == "hgrn_lower_bounds" else jnp.stack(g_small[n])) for n in g_small}

    small_grad_names = REPLICATED + small_names
    small_part = _pack([small_grads[n] for n in small_grad_names] + [loss_part], F32, 8)
    got = _exchange([parts[key] for key in keys0] + [small_part], ["scatter"] * len(keys0) + ["gather"],
                    name="exchange_last")
    take_received(keys0, got[:-1])
    small_recv = got[-1]

    out = {}
    for n in BIG:
        layers = local[n].shape[0]
        shard = local[n].shape[1:]
        p2 = jnp.concatenate([received[n, i].reshape(N_DEV, -1, shard[-1]) for i in range(layers)], axis=1)
        flat = lambda a: a.reshape(-1, shard[-1])
        res = _adamw(p2, flat(local[n]), flat(mom_m[n]), flat(mom_v[n]), name=f"adamw_{n}")
        for kind, a in zip(("grad", "delta", "new_m", "new_v"), res):
            out[kind, n] = a.reshape(local[n].shape)
    small_sum = _sum_slots(small_recv, name="sum_small")
    small_full = _unpack(small_sum, [small_grads[n].shape for n in small_grad_names] + [(1, LANES)])
    loss = small_full[-1][0, 0]
    g_mine = {}
    for n, a in zip(small_grad_names, small_full[:-1]):
        if n == "ffn_conv_w":
            a = lax.dynamic_index_in_dim(a, me, axis=1, keepdims=False)
        elif n in SMALL_SHARDED:
            size = local[n].shape[1]
            a = lax.dynamic_slice_in_dim(a, me * size, size, axis=1)
        g_mine[n] = a
    small_shapes = [local[n].shape for n in small_grad_names]
    res = _adamw(_pack([g_mine[n] for n in small_grad_names], F32, 8)[None],
                 _pack([local[n] for n in small_grad_names], F32, 8),
                 _pack([mom_m[n] for n in small_grad_names], F32, 8),
                 _pack([mom_v[n] for n in small_grad_names], F32, 8), name="adamw_small")
    for kind, packed in zip(("grad", "delta", "new_m", "new_v"), res):
        for n, a in zip(small_grad_names, _unpack(packed, small_shapes)):
            out[kind, n] = a

    outs = [loss, grad_x[None]]
    for kind in ("grad", "delta", "new_m", "new_v"):
        outs += [out[kind, n] for n in WEIGHTS]
    return tuple(outs)
```

```python
import functools

import jax
import jax.numpy as jnp
from jax import lax
from jax.experimental import pallas as pl
from jax.experimental.pallas import tpu as pltpu

F32 = jnp.float32
BF16 = jnp.bfloat16

RMS_EPS = 1e-6
ROPE_THETA = 10000.0
HEAD = 128
ROPE = 64
QK_HEAD = HEAD + ROPE
QK_PAD = 256
CHUNK = 64
SUB = 16
EXP_CLAMP = 60.0
HGRN_CHUNKS_PER_STEP = 8
HGRN_BWD_CHUNKS_PER_STEP = 4

ADAM_LR = 0.001
ADAM_B1 = 0.9
ADAM_B2 = 0.999
ADAM_EPS = 1e-08
ADAM_WD = 0.01
ADAM_STEP = 10

N_DEV = 8
LANES = 128
PACK_COLS = 1024
V7X_VMEM_LIMIT = 56 * 1024 * 1024

HI = lax.Precision.HIGHEST


def _params(sem):
    return pltpu.CompilerParams(dimension_semantics=sem, vmem_limit_bytes=V7X_VMEM_LIMIT)


def _blk(n, cap):
    if n <= cap:
        return n
    d = (cap // LANES) * LANES
    while d >= LANES:
        if n % d == 0:
            return d
        d -= LANES
    raise ValueError(f"no lane-aligned block for {n} under {cap}")


def _sigmoid(x):
    return jax.nn.sigmoid(x)


def _dot(a, b, dims, precision=None):
    return lax.dot_general(a, b, (dims, ((), ())), preferred_element_type=F32, precision=precision)


NN = ((1,), (0,))
NT = ((1,), (1,))
TN = ((0,), (0,))


def _mm(a, b, *, a_fmt="mk", b_fmt="kn", out_fmt="mn", res=None, norm_out=None, norm_bwd=None, out_dtype=F32,
        bm=512, bn=1024, bk=1024, name):
    if a_fmt == "mk":
        M, K = a.shape
    elif a_fmt == "km":
        K, M = a.shape
    elif a_fmt == "kmb":
        nb, K, B = a.shape
        M = nb * B
    else:
        nb, M, B = a.shape
        K = nb * B
    if b_fmt == "kn":
        Kb, N = b.shape
    elif b_fmt == "nk":
        N, Kb = b.shape
    elif b_fmt == "knb":
        nbb, Kb, Bb = b.shape
        N = nbb * Bb
    else:
        nbb, N, Bb = b.shape
        Kb = nbb * Bb
    assert K == Kb, (a.shape, b.shape, a_fmt, b_fmt)
    bm = B if a_fmt == "kmb" else _blk(M, bm)
    bn = Bb if b_fmt == "knb" else _blk(N, bn)
    if a_fmt == "mkb" and b_fmt == "nkb":
        assert B == Bb
    bk = _blk(B, bk) if a_fmt == "mkb" else (_blk(Bb, bk) if b_fmt == "nkb" else _blk(K, bk))
    nm, nn, nk = M // bm, N // bn, K // bk

    if a_fmt == "mk":
        a_spec = pl.BlockSpec((bm, bk), lambda i, j, k: (i, k))
        a_dim = 1
    elif a_fmt == "km":
        a_spec = pl.BlockSpec((bk, bm), lambda i, j, k: (k, i))
        a_dim = 0
    elif a_fmt == "kmb":
        a_spec = pl.BlockSpec((None, bk, bm), lambda i, j, k: (i, k, 0))
        a_dim = 0
    else:
        per = B // bk
        a_spec = pl.BlockSpec((None, bm, bk), lambda i, j, k: (k // per, i, k % per))
        a_dim = 1
    if b_fmt == "kn":
        b_spec = pl.BlockSpec((bk, bn), lambda i, j, k: (k, j))
        b_dim = 0
    elif b_fmt == "nk":
        b_spec = pl.BlockSpec((bn, bk), lambda i, j, k: (j, k))
        b_dim = 1
    elif b_fmt == "knb":
        b_spec = pl.BlockSpec((None, bk, bn), lambda i, j, k: (j, k, 0))
        b_dim = 0
    else:
        perb = Bb // bk
        b_spec = pl.BlockSpec((None, bn, bk), lambda i, j, k: (k // perb, j, k % perb))
        b_dim = 1
    if out_fmt == "mn":
        o_spec = pl.BlockSpec((bm, bn), lambda i, j, k: (i, j))
        o_shape = jax.ShapeDtypeStruct((M, N), out_dtype)
    else:
        o_spec = pl.BlockSpec((None, bm, bn), lambda i, j, k: (j, i, 0))
        o_shape = jax.ShapeDtypeStruct((nn, M, bn), out_dtype)
    in_specs = [a_spec, b_spec]
    args = [a, b]
    row_tile = pl.BlockSpec((bm, bn), lambda i, j, k: (i, j))
    row_vec = pl.BlockSpec((1, bn), lambda i, j, k: (0, j))
    if res is not None:
        assert out_fmt == "mn"
        in_specs.append(row_tile)
        args.append(res)
    out_specs, out_shapes = [o_spec], [o_shape]
    if norm_out is not None:
        assert nn == 1 and out_fmt == "mn"
        in_specs.append(row_vec)
        args.append(norm_out.reshape(1, N))
        out_specs.append(row_tile)
        out_shapes.append(jax.ShapeDtypeStruct((M, N), BF16))
    if norm_bwd is not None:
        assert nn == 1 and out_fmt == "mn" and res is None and norm_out is None
        xin, gain, dres = norm_bwd
        in_specs += [row_tile, row_vec, row_tile]
        args += [xin, gain.reshape(1, N), dres]
        out_specs.append(row_vec)
        out_shapes.append(jax.ShapeDtypeStruct((1, N), F32))
    dims = ((a_dim,), (b_dim,))
    has_res = res is not None
    n_in = len(in_specs)

    def body(*refs):
        a_ref, b_ref = refs[0], refs[1]
        extra_in = list(refs[2:n_in])
        o_ref = refs[n_in]
        part = _dot(a_ref[...].astype(BF16), b_ref[...].astype(BF16), dims)

        def finish(out):
            if has_res:
                out = out + extra_in[0][...]
            if norm_out is not None:
                g_ref, h_ref = extra_in[-1], refs[n_in + 1]
                r = lax.rsqrt(jnp.mean(out * out, axis=-1, keepdims=True) + RMS_EPS)
                h_ref[...] = ((out * r) * g_ref[...]).astype(h_ref.dtype)
            if norm_bwd is not None:
                x_ref, g_ref, dr_ref = extra_in
                dg_ref = refs[n_in + 1]
                xv = x_ref[...]
                r = lax.rsqrt(jnp.mean(xv * xv, axis=-1, keepdims=True) + RMS_EPS)
                n = xv * r
                dn = out * g_ref[...]
                gpart = jnp.sum(out * n, axis=0, keepdims=True)
                i = pl.program_id(0)

                @pl.when(i == 0)
                def _():
                    dg_ref[...] = gpart

                @pl.when(i > 0)
                def _():
                    dg_ref[...] += gpart

                out = dr_ref[...] + r * (dn - n * jnp.mean(dn * n, axis=-1, keepdims=True))
            o_ref[...] = out.astype(o_ref.dtype)

        if nk == 1:
            finish(part)
            return
        acc_ref = refs[-1]
        k = pl.program_id(2)

        @pl.when(k == 0)
        def _():
            acc_ref[...] = part

        @pl.when(jnp.logical_and(k > 0, k < nk - 1))
        def _():
            acc_ref[...] += part

        @pl.when(k == nk - 1)
        def _():
            finish(acc_ref[...] + part)

    multi = len(out_specs) > 1
    return pl.pallas_call(
        body, name=name, grid=(nm, nn, nk), in_specs=in_specs,
        out_specs=out_specs if multi else o_spec, out_shape=out_shapes if multi else o_shape,
        scratch_shapes=[] if nk == 1 else [pltpu.VMEM((bm, bn), F32)],
        compiler_params=_params(("arbitrary",) * 3 if norm_bwd is not None else ("parallel", "parallel", "arbitrary")),
    )(*args)


def _rmsnorm_fwd(x, gain, *, name, tm=512):
    T, D = x.shape
    tm = min(tm, T)

    def body(x_ref, g_ref, o_ref):
        xv = x_ref[...]
        r = lax.rsqrt(jnp.mean(xv * xv, axis=-1, keepdims=True) + RMS_EPS)
        o_ref[...] = ((xv * r) * g_ref[...]).astype(o_ref.dtype)

    return pl.pallas_call(
        body, name=name, grid=(T // tm,),
        in_specs=[pl.BlockSpec((tm, D), lambda i: (i, 0)), pl.BlockSpec((1, D), lambda i: (0, 0))],
        out_specs=pl.BlockSpec((tm, D), lambda i: (i, 0)),
        out_shape=jax.ShapeDtypeStruct((T, D), BF16),
        compiler_params=_params(("parallel",)),
    )(x, gain.reshape(1, D))


def _loss_head(y, target, *, name, tm=512):
    T, D = y.shape
    tm = min(tm, T)

    def body(y_ref, t_ref, dy_ref, l_ref):
        i = pl.program_id(0)
        e = y_ref[...] - t_ref[...]
        dy_ref[...] = e * (1.0 / D)
        s = 0.5 * jnp.sum(jnp.mean(e * e, axis=-1, keepdims=True), axis=0, keepdims=True)
        part = jnp.broadcast_to(s, (1, LANES))

        @pl.when(i == 0)
        def _():
            l_ref[...] = part

        @pl.when(i > 0)
        def _():
            l_ref[...] += part

    return pl.pallas_call(
        body, name=name, grid=(T // tm,),
        in_specs=[pl.BlockSpec((tm, D), lambda i: (i, 0)), pl.BlockSpec((tm, D), lambda i: (i, 0))],
        out_specs=[pl.BlockSpec((tm, D), lambda i: (i, 0)), pl.BlockSpec((1, LANES), lambda i: (0, 0))],
        out_shape=[jax.ShapeDtypeStruct((T, D), F32), jax.ShapeDtypeStruct((1, LANES), F32)],
        compiler_params=_params(("arbitrary",)),
    )(y, target)


def _hgrn_selectors():
    t = jnp.arange(CHUNK)[:, None]
    s = jnp.arange(CHUNK)[None, :]
    mats = [s <= t, s < (t // SUB) * SUB]
    for i in range(1, CHUNK // SUB):
        mats.append(jnp.broadcast_to(s < i * SUB, (8, CHUNK)))
    mats.append(jnp.ones((8, CHUNK), bool))
    sel = jnp.concatenate([m.astype(BF16) for m in mats], axis=0)
    rev = (s >= t).astype(BF16)
    return sel, rev


def _select_sums(sel, x):
    hi = x.astype(BF16)
    r1 = x - hi.astype(F32)
    mid = r1.astype(BF16)
    lo = (r1 - mid.astype(F32)).astype(BF16)
    return _dot(sel, hi, NN) + (_dot(sel, mid, NN) + _dot(sel, lo, NN))


def _hgrn_cums(sel, logf):
    nsub = CHUNK // SUB
    cums = _select_sums(sel, logf)
    g = cums[0:CHUNK]
    rrow = cums[CHUNK:2 * CHUNK]
    base = 2 * CHUNK
    rsel = [None] + [jnp.max(cums[base + 8 * (i - 1):base + 8 * i], axis=0, keepdims=True) for i in range(1, nsub)]
    gl = jnp.max(cums[base + 8 * (nsub - 1):base + 8 * nsub], axis=0, keepdims=True)
    return g, rrow, rsel, gl


def _hgrn_gates(p, lb, D):
    qpre, fpre, iv, gpre = p[:, 0:D], p[:, D:2 * D], p[:, 2 * D:3 * D], p[:, 3 * D:4 * D]
    sig = _sigmoid(fpre)
    forget = lb + (1.0 - lb) * sig
    key = 1.0 - forget
    logf = jnp.log(forget)
    sq = _sigmoid(qpre)
    qs = qpre * sq
    return qpre, sq, qs, sig, forget, key, logf, iv, gpre


def _hgrn_fwd(p, lb, gn, *, name, ride=None):
    T, D4 = p.shape
    D = D4 // 4
    H = D // HEAD
    nc = T // CHUNK
    nb = min(HGRN_CHUNKS_PER_STEP, nc)
    assert nc % nb == 0
    sel, _ = _hgrn_selectors()
    nsel = sel.shape[0]
    nsub = CHUNK // SUB
    heads = [slice(h * HEAD, (h + 1) * HEAD) for h in range(H)]

    def body(p_ref, lb_ref, gn_ref, sel_ref, og_ref, o_ref, s0_ref, st_ref):
        @pl.when(pl.program_id(0) == 0)
        def _():
            st_ref[...] = jnp.zeros_like(st_ref)

        row = lax.broadcasted_iota(jnp.int32, (CHUNK, CHUNK), 0)
        col = lax.broadcasted_iota(jnp.int32, (CHUNK, CHUNK), 1)
        gnv = gn_ref[...]
        lbv = lb_ref[...]
        selv = sel_ref[...]
        pre = []
        for cc in range(nb):
            pv = p_ref[cc * CHUNK:(cc + 1) * CHUNK, :]
            _, _, qs, _, _, key, logf, iv, gpre = _hgrn_gates(pv, lbv, D)
            g, rrow, rsel_all, gl = _hgrn_cums(selv, logf)
            qgb = (qs * jnp.exp(g)).astype(BF16)
            qtb = (qs * jnp.exp(g - rrow)).astype(BF16)
            ktb = [(key * jnp.exp(jnp.minimum((0.0 if r is None else r) - g, EXP_CLAMP))).astype(BF16)
                   for r in rsel_all]
            kdb = (key * jnp.exp(gl - g)).astype(BF16)
            vb = iv.astype(BF16)
            blocks = [[_dot(qtb[i * SUB:(i + 1) * SUB, sl], ktb[i][:, sl], NT) for i in range(nsub)]
                      for sl in heads]
            amats = [jnp.where(col <= row, jnp.concatenate(bl, axis=0), 0.0).astype(BF16) for bl in blocks]
            pre.append(dict(qgb=qgb, egl=jnp.exp(gl), gate=gpre * _sigmoid(gpre),
                            intra=[_dot(a, vb[:, sl], NN) for a, sl in zip(amats, heads)],
                            upd=[_dot(vb[:, sl], kdb[:, sl], TN) for sl in heads]))
        sts = [st_ref[sl, :] for sl in heads]
        for cc, d in enumerate(pre):
            rows = slice(cc * CHUNK, (cc + 1) * CHUNK)
            inter = [_dot(d["qgb"][:, sl], st.astype(BF16), NT) for sl, st in zip(heads, sts)]
            for h, sl in enumerate(heads):
                s0_ref[cc, sl, :] = sts[h]
                o = inter[h] + d["intra"][h]
                o_ref[rows, sl] = o
                r = lax.rsqrt(jnp.mean(o * o, axis=-1, keepdims=True) + RMS_EPS)
                og_ref[rows, sl] = (((o * r) * gnv) * d["gate"][:, sl]).astype(og_ref.dtype)
            sts = [st * d["egl"][:, sl] + u for st, sl, u in zip(sts, heads, d["upd"])]
        for sl, st in zip(heads, sts):
            st_ref[sl, :] = st

    rb = nb * CHUNK
    return _call(
        body, name=name, grid=(nc // nb,),
        in_specs=[pl.BlockSpec((rb, D4), lambda c: (c, 0)), pl.BlockSpec((1, D), lambda c: (0, 0)),
                  pl.BlockSpec((1, HEAD), lambda c: (0, 0)), pl.BlockSpec((nsel, CHUNK), lambda c: (0, 0))],
        out_specs=[pl.BlockSpec((rb, D), lambda c: (c, 0)), pl.BlockSpec((rb, D), lambda c: (c, 0)),
                   pl.BlockSpec((nb, D, HEAD), lambda c: (c, 0, 0))],
        out_shape=[jax.ShapeDtypeStruct((T, D), BF16), jax.ShapeDtypeStruct((T, D), F32),
                   jax.ShapeDtypeStruct((nc, D, HEAD), F32)],
        scratch_shapes=[pltpu.VMEM((D, HEAD), F32)], sem=("arbitrary",), ride=ride,
        args=(p, lb.reshape(1, D), gn.reshape(1, HEAD), sel))


def _hgrn_bwd(p, lb, gn, s0, o_saved, dog, *, name, ride=None):
    T, D4 = p.shape
    D = D4 // 4
    H = D // HEAD
    nc = T // CHUNK
    nb = min(HGRN_BWD_CHUNKS_PER_STEP, nc)
    assert nc % nb == 0
    nsteps = nc // nb
    sel, rev = _hgrn_selectors()
    nsel = sel.shape[0]
    nsub = CHUNK // SUB
    heads = [slice(h * HEAD, (h + 1) * HEAD) for h in range(H)]
    cat = lambda xs: jnp.concatenate(xs, axis=1)

    def body(p_ref, lb_ref, gn_ref, sel_ref, rev_ref, s0_ref, s1_ref, o_ref, dog_ref,
             dp_ref, dlb_ref, dgn_ref, dst_ref):
        @pl.when(pl.program_id(0) == 0)
        def _():
            dst_ref[...] = jnp.zeros_like(dst_ref)
            dlb_ref[...] = jnp.zeros_like(dlb_ref)
            dgn_ref[...] = jnp.zeros_like(dgn_ref)

        row = lax.broadcasted_iota(jnp.int32, (CHUNK, CHUNK), 0)
        col = lax.broadcasted_iota(jnp.int32, (CHUNK, CHUNK), 1)
        causal = col <= row
        gnv, lbv, selv, revv = gn_ref[...], lb_ref[...], sel_ref[...], rev_ref[...]
        dgn_acc = jnp.zeros((1, HEAD), F32)
        pre = []
        for cc in range(nb):
            rows = slice(cc * CHUNK, (cc + 1) * CHUNK)
            qpre, sq, qs, sig, forget, key, logf, iv, gpre = _hgrn_gates(p_ref[rows, :], lbv, D)
            g, rrow, rsel_all, gl = _hgrn_cums(selv, logf)
            eg = jnp.exp(g)
            eqr = jnp.exp(g - rrow)
            eis = [jnp.exp(jnp.minimum((0.0 if r is None else r) - g, EXP_CLAMP)) for r in rsel_all]
            ekd = jnp.exp(gl - g)
            qgb = (qs * eg).astype(BF16)
            qtb = (qs * eqr).astype(BF16)
            ktb = [(key * e).astype(BF16) for e in eis]
            kdb = (key * ekd).astype(BF16)
            vb = iv.astype(BF16)
            sg = _sigmoid(gpre)
            gate = gpre * sg
            dgate = sg * (1.0 + gpre * (1.0 - sg))
            dobs, dgpres = [], []
            for sl in heads:
                o = o_ref[rows, sl]
                r = lax.rsqrt(jnp.mean(o * o, axis=-1, keepdims=True) + RMS_EPS)
                n = o * r
                dog_h = dog_ref[rows, sl]
                d_on = dog_h * gate[:, sl]
                dgpres.append(dog_h * (n * gnv) * dgate[:, sl])
                dgn_acc = dgn_acc + jnp.sum(d_on * n, axis=0, keepdims=True)
                dn = d_on * gnv
                dobs.append((r * (dn - n * jnp.mean(dn * n, axis=-1, keepdims=True))).astype(BF16))
            blocks = [[_dot(qtb[i * SUB:(i + 1) * SUB, sl], ktb[i][:, sl], NT) for i in range(nsub)]
                      for sl in heads]
            amats = [jnp.where(causal, jnp.concatenate(bl, axis=0), 0.0).astype(BF16) for bl in blocks]
            dabs = [jnp.where(causal, _dot(dob, vb[:, sl], NT), 0.0).astype(BF16) for dob, sl in zip(dobs, heads)]
            dq_inter = [_dot(dob, s0_ref[cc, sl, :].astype(BF16), NN) for dob, sl in zip(dobs, heads)]
            dqt = [jnp.concatenate([_dot(dab[i * SUB:(i + 1) * SUB], ktb[i][:, sl], NN) for i in range(nsub)],
                                   axis=0) for dab, sl in zip(dabs, heads)]
            dkt = [[_dot(dab[i * SUB:(i + 1) * SUB], qtb[i * SUB:(i + 1) * SUB, sl], TN) for i in range(nsub)]
                   for dab, sl in zip(dabs, heads)]
            dv_intra = [_dot(a, dob, TN) for a, dob in zip(amats, dobs)]
            upd = [_dot(dob, qgb[:, sl], TN) for dob, sl in zip(dobs, heads)]
            dq = cat(dq_inter) * eg + cat(dqt) * eqr
            dk_intra = cat([dkt[h][0] for h in range(H)]) * eis[0]
            for i in range(1, nsub):
                dk_intra = dk_intra + cat([dkt[h][i] for h in range(H)]) * eis[i]
            s1 = [s0_ref[cc + 1, sl, :] if cc + 1 < nb else s1_ref[sl, :] for sl in heads]
            pre.append(dict(qpre=qpre, sq=sq, qs=qs, sig=sig, forget=forget, key=key, ekd=ekd, egl=jnp.exp(gl),
                            kdb=kdb, vb=vb, dq=dq, dk_intra=dk_intra, dv_intra=dv_intra, upd=upd, s1=s1,
                            dgpre=cat(dgpres)))
        dsts = [dst_ref[sl, :] for sl in heads]
        dlb_acc = jnp.zeros((1, D), F32)
        for cc in reversed(range(nb)):
            d = pre[cc]
            rows = slice(cc * CHUNK, (cc + 1) * CHUNK)
            dstb = [x.astype(BF16) for x in dsts]
            dk_state = cat([_dot(d["vb"][:, sl], x, NN) for sl, x in zip(heads, dstb)])
            dv = cat([dvi + _dot(d["kdb"][:, sl], x, NT) for dvi, sl, x in zip(d["dv_intra"], heads, dstb)])
            term = cat([jnp.sum(x * s, axis=0, keepdims=True) for x, s in zip(dsts, d["s1"])])
            dsts = [x * d["egl"][:, sl] + u for x, sl, u in zip(dsts, heads, d["upd"])]
            dk = d["dk_intra"] + dk_state * d["ekd"]
            dq = d["dq"]
            dg = d["qs"] * dq - d["key"] * dk
            dlogf = _select_sums(revv, dg) + term
            sgf = d["sig"]
            dforget = dlogf / d["forget"] - dk
            dlb_acc = dlb_acc + jnp.sum(dforget * (1.0 - sgf), axis=0, keepdims=True)
            sqv = d["sq"]
            dp_ref[rows, 0:D] = (dq * (sqv * (1.0 + d["qpre"] * (1.0 - sqv)))).astype(dp_ref.dtype)
            dp_ref[rows, D:2 * D] = (dforget * (1.0 - lbv) * (sgf * (1.0 - sgf))).astype(dp_ref.dtype)
            dp_ref[rows, 2 * D:3 * D] = dv.astype(dp_ref.dtype)
            dp_ref[rows, 3 * D:4 * D] = d["dgpre"].astype(dp_ref.dtype)
        for sl, x in zip(heads, dsts):
            dst_ref[sl, :] = x
        dlb_ref[...] += dlb_acc
        dgn_ref[...] += dgn_acc

    rb = nb * CHUNK
    rc = lambda c: nsteps - 1 - c
    return _call(
        body, name=name, grid=(nsteps,),
        in_specs=[pl.BlockSpec((rb, D4), lambda c: (rc(c), 0)), pl.BlockSpec((1, D), lambda c: (0, 0)),
                  pl.BlockSpec((1, HEAD), lambda c: (0, 0)), pl.BlockSpec((nsel, CHUNK), lambda c: (0, 0)),
                  pl.BlockSpec((CHUNK, CHUNK), lambda c: (0, 0)),
                  pl.BlockSpec((nb, D, HEAD), lambda c: (rc(c), 0, 0)),
                  pl.BlockSpec((None, D, HEAD), lambda c: (jnp.minimum((rc(c) + 1) * nb, nc - 1), 0, 0)),
                  pl.BlockSpec((rb, D), lambda c: (rc(c), 0)), pl.BlockSpec((rb, D), lambda c: (rc(c), 0))],
        out_specs=[pl.BlockSpec((rb, D4), lambda c: (rc(c), 0)), pl.BlockSpec((1, D), lambda c: (0, 0)),
                   pl.BlockSpec((1, HEAD), lambda c: (0, 0))],
        out_shape=[jax.ShapeDtypeStruct((T, D4), BF16), jax.ShapeDtypeStruct((1, D), F32),
                   jax.ShapeDtypeStruct((1, HEAD), F32)],
        scratch_shapes=[pltpu.VMEM((D, HEAD), F32)], sem=("arbitrary",), ride=ride,
        args=(p, lb.reshape(1, D), gn.reshape(1, HEAD), sel, rev, s0, s0, o_saved, dog))


def _rope_tables(positions):
    inv_freq = ROPE_THETA ** (-jnp.arange(0, ROPE, 2, dtype=F32) / ROPE)
    ang = positions.astype(F32)[:, None] * inv_freq
    cos, sin = jnp.cos(ang), jnp.sin(ang)
    z = jnp.zeros_like(cos)
    ctab = jnp.concatenate([cos, cos, z, z], axis=-1)
    s1 = jnp.concatenate([-sin, z, z, z], axis=-1)
    s2 = jnp.concatenate([z, sin, z, z], axis=-1)
    return ctab, s1, s2


def _rope(z, ct, s1, s2):
    return z * ct + pltpu.roll(z, 96, 1) * s1 + pltpu.roll(z, 32, 1) * s2


def _rope_t(d, ct, s1, s2):
    return d * ct + pltpu.roll(d * s1, 32, 1) + pltpu.roll(d * s2, 96, 1)


def _mla_prep_fwd(c, wq, wkv, ga_q, ga_kv, gq, gk, ct, s1, s2, *, name, tm=512):
    T, CW = c.shape
    R = (CW - LANES) // 2
    H = wq.shape[1] // QK_PAD
    tm = min(tm, T)

    def body(c_ref, wq_ref, wkv_ref, gaq_ref, gakv_ref, gq_ref, gk_ref, ct_ref, s1_ref, s2_ref,
             q_ref, k_ref, v_ref):
        cv = c_ref[...]
        cq, ckv, kr = cv[:, 0:R], cv[:, R:2 * R], cv[:, 2 * R:2 * R + LANES]
        rq = lax.rsqrt(jnp.mean(cq * cq, axis=-1, keepdims=True) + RMS_EPS)
        cqn = ((cq * rq) * gaq_ref[...]).astype(BF16)
        rk = lax.rsqrt(jnp.mean(ckv * ckv, axis=-1, keepdims=True) + RMS_EPS)
        ckvn = ((ckv * rk) * gakv_ref[...]).astype(BF16)
        qp = _dot(cqn, wq_ref[...], NN)
        kvp = _dot(ckvn, wkv_ref[...], NN)
        ctv, s1v, s2v = ct_ref[...], s1_ref[...], s2_ref[...]
        gqv, gkv = gq_ref[...], gk_ref[...]
        krs = jnp.sum(kr * kr, axis=-1, keepdims=True)
        for h in range(H):
            b = h * QK_PAD
            qn, qr = qp[:, b:b + HEAD], qp[:, b + HEAD:b + QK_PAD]
            ss = jnp.sum(qn * qn + qr * qr, axis=-1, keepdims=True)
            rr = lax.rsqrt(ss * (1.0 / QK_HEAD) + RMS_EPS)
            q_ref[:, b:b + HEAD] = ((qn * rr) * gqv[:, 0:HEAD]).astype(q_ref.dtype)
            q_ref[:, b + HEAD:b + QK_PAD] = _rope((qr * rr) * gqv[:, HEAD:QK_PAD], ctv, s1v, s2v).astype(q_ref.dtype)
            kn, vv = kvp[:, b:b + HEAD], kvp[:, b + HEAD:b + QK_PAD]
            ssk = jnp.sum(kn * kn, axis=-1, keepdims=True) + krs
            rrk = lax.rsqrt(ssk * (1.0 / QK_HEAD) + RMS_EPS)
            k_ref[:, b:b + HEAD] = ((kn * rrk) * gkv[:, 0:HEAD]).astype(k_ref.dtype)
            k_ref[:, b + HEAD:b + QK_PAD] = _rope((kr * rrk) * gkv[:, HEAD:QK_PAD], ctv, s1v, s2v).astype(k_ref.dtype)
            v_ref[:, h * HEAD:(h + 1) * HEAD] = vv.astype(v_ref.dtype)

    full = lambda shape: pl.BlockSpec(shape, lambda i: (0, 0))
    tok = lambda w: pl.BlockSpec((tm, w), lambda i: (i, 0))
    return pl.pallas_call(
        body, name=name, grid=(T // tm,),
        in_specs=[tok(CW), full(wq.shape), full(wkv.shape), full((1, R)), full((1, R)), full((1, QK_PAD)),
                  full((1, QK_PAD)), tok(LANES), tok(LANES), tok(LANES)],
        out_specs=[tok(H * QK_PAD), tok(H * QK_PAD), tok(H * HEAD)],
        out_shape=[jax.ShapeDtypeStruct((T, H * QK_PAD), BF16), jax.ShapeDtypeStruct((T, H * QK_PAD), BF16),
                   jax.ShapeDtypeStruct((T, H * HEAD), BF16)],
        compiler_params=_params(("parallel",)),
    )(c, wq, wkv, ga_q.reshape(1, R), ga_kv.reshape(1, R), gq, gk, ct, s1, s2)


def _mla_prep_bwd(c, wq, wkv, ga_q, ga_kv, gq, gk, ct, s1, s2, dq, dk, dv, *, name, tm=512):
    T, CW = c.shape
    R = (CW - LANES) // 2
    H = wq.shape[1] // QK_PAD
    tm = min(tm, T)

    def body(c_ref, wq_ref, wkv_ref, gaq_ref, gakv_ref, gq_ref, gk_ref, ct_ref, s1_ref, s2_ref,
             dq_ref, dk_ref, dv_ref,
             dc_ref, dwq_ref, dwkv_ref, dgaq_ref, dgakv_ref, dgq_ref, dgk_ref, dqp_ref, dkvp_ref):
        i = pl.program_id(0)

        @pl.when(i == 0)
        def _():
            for ref in (dwq_ref, dwkv_ref, dgaq_ref, dgakv_ref, dgq_ref, dgk_ref):
                ref[...] = jnp.zeros_like(ref)

        cv = c_ref[...]
        cq, ckv, kr = cv[:, 0:R], cv[:, R:2 * R], cv[:, 2 * R:2 * R + LANES]
        rq = lax.rsqrt(jnp.mean(cq * cq, axis=-1, keepdims=True) + RMS_EPS)
        nq = cq * rq
        cqn = (nq * gaq_ref[...]).astype(BF16)
        rk = lax.rsqrt(jnp.mean(ckv * ckv, axis=-1, keepdims=True) + RMS_EPS)
        nkv = ckv * rk
        ckvn = (nkv * gakv_ref[...]).astype(BF16)
        qp = _dot(cqn, wq_ref[...], NN)
        kvp = _dot(ckvn, wkv_ref[...], NN)
        ctv, s1v, s2v = ct_ref[...], s1_ref[...], s2_ref[...]
        gqv, gkv = gq_ref[...], gk_ref[...]
        krs = jnp.sum(kr * kr, axis=-1, keepdims=True)
        dkr = jnp.zeros((tm, LANES), F32)
        dgq_n = jnp.zeros((1, HEAD), F32)
        dgq_r = jnp.zeros((1, HEAD), F32)
        dgk_n = jnp.zeros((1, HEAD), F32)
        dgk_r = jnp.zeros((1, HEAD), F32)
        for h in range(H):
            b = h * QK_PAD
            qn, qr = qp[:, b:b + HEAD], qp[:, b + HEAD:b + QK_PAD]
            ss = jnp.sum(qn * qn + qr * qr, axis=-1, keepdims=True)
            rr = lax.rsqrt(ss * (1.0 / QK_HEAD) + RMS_EPS)
            un, ur = qn * rr, qr * rr
            dzn = dq_ref[:, b:b + HEAD]
            dzr = _rope_t(dq_ref[:, b + HEAD:b + QK_PAD], ctv, s1v, s2v)
            dgq_n = dgq_n + jnp.sum(dzn * un, axis=0, keepdims=True)
            dgq_r = dgq_r + jnp.sum(dzr * ur, axis=0, keepdims=True)
            dun, dur = dzn * gqv[:, 0:HEAD], dzr * gqv[:, HEAD:QK_PAD]
            m = jnp.sum(dun * un + dur * ur, axis=-1, keepdims=True) * (1.0 / QK_HEAD)
            dqp_ref[:, b:b + HEAD] = (rr * (dun - un * m)).astype(BF16)
            dqp_ref[:, b + HEAD:b + QK_PAD] = (rr * (dur - ur * m)).astype(BF16)
            kn = kvp[:, b:b + HEAD]
            ssk = jnp.sum(kn * kn, axis=-1, keepdims=True) + krs
            rrk = lax.rsqrt(ssk * (1.0 / QK_HEAD) + RMS_EPS)
            vn, vr = kn * rrk, kr * rrk
            dyn = dk_ref[:, b:b + HEAD]
            dyr = _rope_t(dk_ref[:, b + HEAD:b + QK_PAD], ctv, s1v, s2v)
            dgk_n = dgk_n + jnp.sum(dyn * vn, axis=0, keepdims=True)
            dgk_r = dgk_r + jnp.sum(dyr * vr, axis=0, keepdims=True)
            dvn, dvr = dyn * gkv[:, 0:HEAD], dyr * gkv[:, HEAD:QK_PAD]
            mk = jnp.sum(dvn * vn + dvr * vr, axis=-1, keepdims=True) * (1.0 / QK_HEAD)
            dkvp_ref[:, b:b + HEAD] = (rrk * (dvn - vn * mk)).astype(BF16)
            dkr = dkr + rrk * (dvr - vr * mk)
            dkvp_ref[:, b + HEAD:b + QK_PAD] = dv_ref[:, h * HEAD:(h + 1) * HEAD].astype(BF16)
        dgq_ref[:, 0:HEAD] += dgq_n
        dgq_ref[:, HEAD:QK_PAD] += dgq_r
        dgk_ref[:, 0:HEAD] += dgk_n
        dgk_ref[:, HEAD:QK_PAD] += dgk_r
        dqp = dqp_ref[...]
        dkvp = dkvp_ref[...]
        dwq_ref[...] += _dot(cqn, dqp, TN)
        dwkv_ref[...] += _dot(ckvn, dkvp, TN)
        dcqn = _dot(dqp, wq_ref[...], NT)
        dckvn = _dot(dkvp, wkv_ref[...], NT)
        dgaq_ref[...] += jnp.sum(dcqn * nq, axis=0, keepdims=True)
        dgakv_ref[...] += jnp.sum(dckvn * nkv, axis=0, keepdims=True)
        dnq = dcqn * gaq_ref[...]
        dnkv = dckvn * gakv_ref[...]
        dc_ref[:, 0:R] = (rq * (dnq - nq * jnp.mean(dnq * nq, axis=-1, keepdims=True))).astype(dc_ref.dtype)
        dc_ref[:, R:2 * R] = (rk * (dnkv - nkv * jnp.mean(dnkv * nkv, axis=-1, keepdims=True))).astype(dc_ref.dtype)
        dc_ref[:, 2 * R:2 * R + LANES] = dkr.astype(dc_ref.dtype)

    full = lambda shape: pl.BlockSpec(shape, lambda i: (0, 0))
    tok = lambda w: pl.BlockSpec((tm, w), lambda i: (i, 0))
    return pl.pallas_call(
        body, name=name, grid=(T // tm,),
        in_specs=[tok(CW), full(wq.shape), full(wkv.shape), full((1, R)), full((1, R)), full((1, QK_PAD)),
                  full((1, QK_PAD)), tok(LANES), tok(LANES), tok(LANES),
                  tok(H * QK_PAD), tok(H * QK_PAD), tok(H * HEAD)],
        out_specs=[tok(CW), full(wq.shape), full(wkv.shape), full((1, R)), full((1, R)), full((1, QK_PAD)),
                   full((1, QK_PAD))],
        out_shape=[jax.ShapeDtypeStruct((T, CW), BF16), jax.ShapeDtypeStruct(wq.shape, F32),
                   jax.ShapeDtypeStruct(wkv.shape, F32), jax.ShapeDtypeStruct((1, R), F32),
                   jax.ShapeDtypeStruct((1, R), F32), jax.ShapeDtypeStruct((1, QK_PAD), F32),
                   jax.ShapeDtypeStruct((1, QK_PAD), F32)],
        scratch_shapes=[pltpu.VMEM((tm, H * QK_PAD), BF16), pltpu.VMEM((tm, H * QK_PAD), BF16)],
        compiler_params=_params(("arbitrary",)),
    )(c, wq, wkv, ga_q.reshape(1, R), ga_kv.reshape(1, R), gq, gk, ct, s1, s2, dq, dk, dv)


NEG = -1e30
LOG2E = 1.4426950408889634


def _attn_fwd(q, k, v, *, name, tb=512, hp=2, ride=None):
    T = q.shape[0]
    H = q.shape[1] // QK_PAD
    tb = min(tb, T)
    nq = T // tb
    scale = QK_HEAD ** -0.5
    c2 = scale * LOG2E
    assert H % hp == 0

    def body(q_ref, k_ref, v_ref, ot_ref, lse_ref, m_ref, l_ref, acc_ref):
        i = pl.program_id(1)
        m_ref[...] = jnp.full_like(m_ref, NEG)
        l_ref[...] = jnp.zeros_like(l_ref)
        acc_ref[...] = jnp.zeros_like(acc_ref)

        def step(js, masked):
            offs = [pl.multiple_of(j * tb, tb) for j in js]
            sts = [[_dot(k_ref[pl.ds(off, tb), hh * QK_PAD:(hh + 1) * QK_PAD],
                         q_ref[:, hh * QK_PAD:(hh + 1) * QK_PAD], NT) for hh in range(hp)] for off in offs]
            for b, hh in [(b, hh) for b in range(len(js)) for hh in range(hp)]:
                off = offs[b]
                vs = slice(hh * HEAD, (hh + 1) * HEAD)
                vb = v_ref[pl.ds(off, tb), vs]
                st = sts[b][hh]
                if masked:
                    kpos = lax.broadcasted_iota(jnp.int32, (tb, tb), 0)
                    qpos = lax.broadcasted_iota(jnp.int32, (tb, tb), 1)
                    st = jnp.where(kpos <= qpos, st, NEG)
                m_old = m_ref[hh]
                m_new = jnp.maximum(m_old, jnp.max(st, axis=0, keepdims=True))
                alpha = jnp.exp2((m_old - m_new) * c2)
                pt = jnp.exp2((st - m_new) * c2)
                l_ref[hh] = l_ref[hh] * alpha + jnp.sum(pt, axis=0, keepdims=True)
                acc_ref[vs, :] = acc_ref[vs, :] * alpha + _dot(vb, pt.astype(BF16), TN)
                m_ref[hh] = m_new

        def pair_body(t, carry):
            step([2 * t, 2 * t + 1], False)
            return carry

        lax.fori_loop(0, i // 2, pair_body, 0)

        @pl.when(i % 2 == 1)
        def _():
            step([i - 1], False)

        step([i], True)
        for hh in range(hp):
            vs = slice(hh * HEAD, (hh + 1) * HEAD)
            l = l_ref[hh]
            ot_ref[vs, :] = (acc_ref[vs, :] / l).astype(ot_ref.dtype)
            lse_ref[hh] = m_ref[hh] * scale + jnp.log(l)

    return _call(
        body, name=name, grid=(H // hp, nq),
        in_specs=[pl.BlockSpec((tb, hp * QK_PAD), lambda g, i: (i, g)),
                  pl.BlockSpec((T, hp * QK_PAD), lambda g, i: (0, g)),
                  pl.BlockSpec((T, hp * HEAD), lambda g, i: (0, g))],
        out_specs=[pl.BlockSpec((None, hp * HEAD, tb), lambda g, i: (i, g, 0)),
                   pl.BlockSpec((hp, None, 1, tb), lambda g, i: (g, i, 0, 0))],
        out_shape=[jax.ShapeDtypeStruct((nq, H * HEAD, tb), BF16), jax.ShapeDtypeStruct((H, nq, 1, tb), F32)],
        scratch_shapes=[pltpu.VMEM((hp, 1, tb), F32), pltpu.VMEM((hp, 1, tb), F32),
                        pltpu.VMEM((hp * HEAD, tb), F32)],
        sem=("parallel", "arbitrary"), ride=ride, args=(q, k, v))


def _attn_bwd(q, k, v, ot, lse, dot_, *, name, ride=None):
    T = q.shape[0]
    H = q.shape[1] // QK_PAD
    nq, _, tb = ot.shape
    scale = QK_HEAD ** -0.5
    c2 = scale * LOG2E

    def body(q_ref, k_ref, v_ref, ot_ref, lse_ref, dot_ref, dq_ref, dk_ref, dv_ref, dka_ref, dva_ref):
        j = pl.program_id(1)

        @pl.when(j == 0)
        def _():
            dq_ref[...] = jnp.zeros_like(dq_ref)

        kb = k_ref[...]
        vb = v_ref[...]

        def step(i, masked):
            off = i * tb if isinstance(i, int) else pl.multiple_of(i * tb, tb)
            qb = q_ref[pl.ds(off, tb), :]
            dob = dot_ref[i]
            ob = ot_ref[i]
            st = _dot(kb, qb, NT)
            if masked:
                kpos = lax.broadcasted_iota(jnp.int32, (tb, tb), 0)
                qpos = lax.broadcasted_iota(jnp.int32, (tb, tb), 1)
                st = jnp.where(kpos <= qpos, st, NEG)
            pt = jnp.exp2(st * c2 - lse_ref[i] * LOG2E)
            dpt = _dot(vb, dob, NN)
            delta = jnp.sum(dob.astype(F32) * ob.astype(F32), axis=0, keepdims=True)
            dst = (pt * (dpt - delta)).astype(BF16)
            dq_ref[pl.ds(off, tb), :] += _dot(dst, kb, TN)
            return _dot(dst, qb, NN), _dot(pt.astype(BF16), dob, NT)

        dk0, dv0 = step(j, True)
        dka_ref[...] = dk0
        dva_ref[...] = dv0

        rest = nq - 1 - j

        def pair_body(t, carry):
            i0 = j + 1 + 2 * t
            dk1, dv1 = step(i0, False)
            dk2, dv2 = step(i0 + 1, False)
            dka_ref[...] += dk1 + dk2
            dva_ref[...] += dv1 + dv2
            return carry

        lax.fori_loop(0, rest // 2, pair_body, 0)

        @pl.when(rest % 2 == 1)
        def _():
            dk1, dv1 = step(nq - 1, False)
            dka_ref[...] += dk1
            dva_ref[...] += dv1

        dk_ref[...] = dka_ref[...] * scale
        dv_ref[...] = dva_ref[...]

        @pl.when(j == nq - 1)
        def _():
            dq_ref[...] = dq_ref[...] * scale

    return _call(
        body, name=name, grid=(H, nq),
        in_specs=[pl.BlockSpec((T, QK_PAD), lambda h, j: (0, h)), pl.BlockSpec((tb, QK_PAD), lambda h, j: (j, h)),
                  pl.BlockSpec((tb, HEAD), lambda h, j: (j, h)),
                  pl.BlockSpec((nq, HEAD, tb), lambda h, j: (0, h, 0)),
                  pl.BlockSpec((None, nq, 1, tb), lambda h, j: (h, 0, 0, 0)),
                  pl.BlockSpec((nq, HEAD, tb), lambda h, j: (0, h, 0))],
        out_specs=[pl.BlockSpec((T, QK_PAD), lambda h, j: (0, h)), pl.BlockSpec((tb, QK_PAD), lambda h, j: (j, h)),
                   pl.BlockSpec((tb, HEAD), lambda h, j: (j, h))],
        out_shape=[jax.ShapeDtypeStruct((T, H * QK_PAD), F32), jax.ShapeDtypeStruct((T, H * QK_PAD), F32),
                   jax.ShapeDtypeStruct((T, H * HEAD), F32)],
        scratch_shapes=[pltpu.VMEM((tb, QK_PAD), F32), pltpu.VMEM((tb, HEAD), F32)],
        sem=("parallel", "arbitrary"), ride=ride, args=(q, k, v, ot, lse, dot_))


def _conv_taps(u, prev6, prev7):
    rows = lax.broadcasted_iota(jnp.int32, (u.shape[0], 1), 0)
    u1 = jnp.where(rows >= 1, pltpu.roll(u, 1, 0), prev7)
    u2 = jnp.where(rows >= 2, pltpu.roll(u, 2, 0), jnp.where(rows == 0, prev6, prev7))
    return u2, u1


def _conv_taps_ahead(d, next0, next1):
    tm = d.shape[0]
    rows = lax.broadcasted_iota(jnp.int32, (tm, 1), 0)
    d1 = jnp.where(rows < tm - 1, pltpu.roll(d, tm - 1, 0), next0)
    d2 = jnp.where(rows < tm - 2, pltpu.roll(d, tm - 2, 0), jnp.where(rows == tm - 2, next0, next1))
    return d1, d2


def _ffn_up_fwd(h, w_up, conv_w, conv_b, *, name, tm=512, ride=None):
    T, D = h.shape
    ns, _, fs = w_up.shape
    nh = ns // 2
    tm = min(tm, T)

    def body(h_ref, wg_ref, wu_ref, cwg_ref, cwu_ref, cbg_ref, cbu_ref, a_ref, u_ref, y_ref, cg_ref, cu_ref):
        i = pl.program_id(1)

        @pl.when(i == 0)
        def _():
            cg_ref[...] = jnp.zeros_like(cg_ref)
            cu_ref[...] = jnp.zeros_like(cu_ref)

        hv = h_ref[...]
        ys = []
        for idx, (w_ref, cw_ref, cb_ref, carry) in enumerate(
                ((wg_ref, cwg_ref, cbg_ref, cg_ref), (wu_ref, cwu_ref, cbu_ref, cu_ref))):
            u = _dot(hv, w_ref[...], NN)
            u_ref[idx] = u.astype(u_ref.dtype)
            u2, u1 = _conv_taps(u, carry[6:7, :], carry[7:8, :])
            y = cb_ref[...] + u2 * cw_ref[0:1, :]
            y = y + u1 * cw_ref[1:2, :]
            y = y + u * cw_ref[2:3, :]
            y_ref[idx] = y.astype(y_ref.dtype)
            ys.append(y)
            carry[...] = u[tm - 8:tm, :]
        yg, yu = ys
        a_ref[...] = ((yg * _sigmoid(yg)) * yu).astype(a_ref.dtype)

    shard = lambda r, off: pl.BlockSpec((None, r, fs), lambda j, i: (j + off, 0, 0))
    return _call(
        body, name=name, grid=(nh, T // tm),
        in_specs=[pl.BlockSpec((tm, D), lambda j, i: (i, 0)), shard(D, 0), shard(D, nh),
                  shard(3, 0), shard(3, nh), shard(1, 0), shard(1, nh)],
        out_specs=[pl.BlockSpec((None, tm, fs), lambda j, i: (j, i, 0)),
                   pl.BlockSpec((2, None, tm, fs), lambda j, i: (0, j, i, 0)),
                   pl.BlockSpec((2, None, tm, fs), lambda j, i: (0, j, i, 0))],
        out_shape=[jax.ShapeDtypeStruct((nh, T, fs), BF16), jax.ShapeDtypeStruct((2, nh, T, fs), BF16),
                   jax.ShapeDtypeStruct((2, nh, T, fs), BF16)],
        scratch_shapes=[pltpu.VMEM((8, fs), F32), pltpu.VMEM((8, fs), F32)],
        sem=("parallel", "arbitrary"), ride=ride, args=(h, w_up, w_up, conv_w, conv_w, conv_b, conv_b))


def _ffn_act_bwd(dxo, w_down, u, y, conv_w, *, name, tm=512, ride=None):
    T, D = dxo.shape
    _, nh, _, fs = u.shape
    tm = min(tm, T)
    nt = T // tm

    def body(dx_ref, wd_ref, u_ref, y_ref, cwg_ref, cwu_ref, du_ref, dcw_ref, dcb_ref, cg_ref, cu_ref):
        i = pl.program_id(1)

        @pl.when(i == 0)
        def _():
            cg_ref[...] = jnp.zeros_like(cg_ref)
            cu_ref[...] = jnp.zeros_like(cu_ref)
            dcw_ref[...] = jnp.zeros_like(dcw_ref)
            dcb_ref[...] = jnp.zeros_like(dcb_ref)

        da = _dot(dx_ref[...].astype(BF16), wd_ref[...], NT)
        yg, yu = y_ref[0].astype(F32), y_ref[1].astype(F32)
        sg = _sigmoid(yg)
        dys = (da * yu * (sg * (1.0 + yg * (1.0 - sg))), da * (yg * sg))
        for idx, (cw_ref, carry) in enumerate(((cwg_ref, cg_ref), (cwu_ref, cu_ref))):
            dy = dys[idx]
            uv = u_ref[idx].astype(F32)
            d1, d2 = _conv_taps_ahead(dy, carry[0:1, :], carry[1:2, :])
            dcb_ref[idx] += jnp.sum(dy, axis=0, keepdims=True)
            dcw_ref[idx, 0:1, :] += jnp.sum(d2 * uv, axis=0, keepdims=True)
            dcw_ref[idx, 1:2, :] += jnp.sum(d1 * uv, axis=0, keepdims=True)
            dcw_ref[idx, 2:3, :] += jnp.sum(dy * uv, axis=0, keepdims=True)
            du = dy * cw_ref[2:3, :] + d1 * cw_ref[1:2, :] + d2 * cw_ref[0:1, :]
            du_ref[idx] = du.astype(du_ref.dtype)
            carry[...] = dy[0:8, :]

    rt = lambda i: nt - 1 - i
    shard = lambda r, off: pl.BlockSpec((None, r, fs), lambda j, i: (j + off, 0, 0))
    tile = pl.BlockSpec((2, None, tm, fs), lambda j, i: (0, j, rt(i), 0))
    return _call(
        body, name=name, grid=(nh, nt),
        in_specs=[pl.BlockSpec((tm, D), lambda j, i: (rt(i), 0)), pl.BlockSpec((fs, D), lambda j, i: (j, 0)),
                  tile, tile, shard(3, 0), shard(3, nh)],
        out_specs=[tile, pl.BlockSpec((2, None, 3, fs), lambda j, i: (0, j, 0, 0)),
                   pl.BlockSpec((2, None, 1, fs), lambda j, i: (0, j, 0, 0))],
        out_shape=[jax.ShapeDtypeStruct((2, nh, T, fs), BF16), jax.ShapeDtypeStruct((2, nh, 3, fs), F32),
                   jax.ShapeDtypeStruct((2, nh, 1, fs), F32)],
        scratch_shapes=[pltpu.VMEM((8, fs), F32), pltpu.VMEM((8, fs), F32)],
        sem=("parallel", "arbitrary"), ride=ride, args=(dxo, w_down, u, y, conv_w, conv_w))


def _pad_cols(w, n):
    return jnp.pad(w, [(0, 0)] * (w.ndim - 1) + [(0, n - w.shape[-1])])


def _q_up_padded(w):
    R = w.shape[0]
    H = w.shape[1] // QK_HEAD
    return _pad_cols(w.reshape(R, H, QK_HEAD), QK_PAD).reshape(R, H * QK_PAD)


def _q_up_unpadded(w):
    R = w.shape[0]
    H = w.shape[1] // QK_PAD
    return w.reshape(R, H, QK_PAD)[:, :, :QK_HEAD].reshape(R, H * QK_HEAD)


def _xchg_copies(src_refs, out_refs, kinds, send_sems, recv_sems, local_sems):
    x, y, c = lax.axis_index("x"), lax.axis_index("y"), lax.axis_index("c")
    me = 4 * x + 2 * y + c
    copies = []
    for b, kind in enumerate(kinds):
        gather = kind == "gather"
        own = src_refs[b] if gather else src_refs[b].at[me]
        copies.append(pltpu.make_async_copy(own, out_refs[b].at[me], local_sems.at[b]))
        for kk in range(1, N_DEV):
            px = 1 - x if kk & 4 else x
            py = 1 - y if kk & 2 else y
            pc = 1 - c if kk & 1 else c
            peer = 4 * px + 2 * py + pc
            src = src_refs[b] if gather else src_refs[b].at[peer]
            copies.append(pltpu.make_async_remote_copy(
                src_ref=src, dst_ref=out_refs[b].at[me],
                send_sem=send_sems.at[b * (N_DEV - 1) + kk - 1],
                recv_sem=recv_sems.at[b * (N_DEV - 1) + kk - 1],
                device_id=(px, py, pc), device_id_type=pl.DeviceIdType.MESH))
    return copies


def _xchg_out_shapes(srcs, kinds):
    return [jax.ShapeDtypeStruct((N_DEV,) + s.shape if kind == "gather" else s.shape, s.dtype)
            for s, kind in zip(srcs, kinds)]


def _xchg_scratch(n):
    return [pltpu.SemaphoreType.DMA((n * (N_DEV - 1),)), pltpu.SemaphoreType.DMA((n * (N_DEV - 1),)),
            pltpu.SemaphoreType.DMA((n,))]


def _exchange(srcs, kinds, *, name):
    n = len(srcs)

    def body(*refs):
        copies = _xchg_copies(refs[:n], refs[n:2 * n], kinds, *refs[2 * n:])
        for cp in copies:
            cp.start()
        for cp in copies:
            cp.wait()

    hbm = pl.BlockSpec(memory_space=pl.ANY)
    return pl.pallas_call(
        body, name=name, in_specs=[hbm] * n, out_specs=[hbm] * n, out_shape=_xchg_out_shapes(srcs, kinds),
        scratch_shapes=_xchg_scratch(n),
    )(*srcs)


def _call(body, *, name, grid, in_specs, out_specs, out_shape, scratch_shapes, args, sem, ride=None):
    if ride is None:
        outs = pl.pallas_call(body, name=name, grid=grid, in_specs=in_specs, out_specs=out_specs,
                              out_shape=out_shape, scratch_shapes=scratch_shapes,
                              compiler_params=_params(sem))(*args)
        return list(outs), []
    srcs, kinds = ride
    n_in, n_out, n_sc, nx = len(in_specs), len(out_specs), len(scratch_shapes), len(srcs)

    def wrapped(*refs):
        ins, xs = refs[:n_in], refs[n_in:n_in + nx]
        o0 = n_in + nx
        outs, xo = refs[o0:o0 + n_out], refs[o0 + n_out:o0 + n_out + nx]
        s0 = o0 + n_out + nx
        sc, sems = refs[s0:s0 + n_sc], refs[s0 + n_sc:]
        first = functools.reduce(jnp.logical_and, [pl.program_id(d) == 0 for d in range(len(grid))])
        last = functools.reduce(jnp.logical_and, [pl.program_id(d) == grid[d] - 1 for d in range(len(grid))])

        @pl.when(first)
        def _():
            for cp in _xchg_copies(xs, xo, kinds, *sems):
                cp.start()

        body(*ins, *outs, *sc)

        @pl.when(last)
        def _():
            for cp in _xchg_copies(xs, xo, kinds, *sems):
                cp.wait()

    hbm = pl.BlockSpec(memory_space=pl.ANY)
    outs = pl.pallas_call(
        wrapped, name=name, grid=grid, in_specs=list(in_specs) + [hbm] * nx,
        out_specs=list(out_specs) + [hbm] * nx, out_shape=list(out_shape) + _xchg_out_shapes(srcs, kinds),
        scratch_shapes=list(scratch_shapes) + _xchg_scratch(nx),
        compiler_params=_params(("arbitrary",) * len(grid)),
    )(*args, *srcs)
    return list(outs[:n_out]), list(outs[n_out:])


def _sum_slots(parts, *, name):
    _, Rr, C = parts.shape

    def body(p_ref, o_ref):
        acc = p_ref[0].astype(F32)
        for d in range(1, N_DEV):
            acc = acc + p_ref[d].astype(F32)
        o_ref[...] = acc

    return pl.pallas_call(
        body, name=name, grid=(1,),
        in_specs=[pl.BlockSpec((N_DEV, Rr, C), lambda i: (0, 0, 0))],
        out_specs=pl.BlockSpec((Rr, C), lambda i: (0, 0)),
        out_shape=jax.ShapeDtypeStruct((Rr, C), F32),
        compiler_params=_params(("arbitrary",)),
    )(parts)


def _row_tile(rows, cap=512):
    if rows <= cap:
        return rows
    d = (cap // 8) * 8
    while d >= 8:
        if rows % d == 0:
            return d
        d -= 8
    raise ValueError(f"no row tile for {rows}")


def _adamw(parts, w, m, v, *, name):
    S, Rr, C = parts.shape
    tr = _row_tile(Rr)
    c1 = 1.0 - ADAM_B1 ** ADAM_STEP
    c2 = 1.0 - ADAM_B2 ** ADAM_STEP

    def body(p_ref, w_ref, m_ref, v_ref, g_ref, d_ref, nm_ref, nv_ref):
        g = p_ref[0].astype(F32)
        for d in range(1, S):
            g = g + p_ref[d].astype(F32)
        mm = ADAM_B1 * m_ref[...] + (1.0 - ADAM_B1) * g
        vv = ADAM_B2 * v_ref[...] + (1.0 - ADAM_B2) * (g * g)
        m_hat = mm / c1
        v_hat = vv / c2
        g_ref[...] = g
        d_ref[...] = -ADAM_LR * (m_hat / (jnp.sqrt(v_hat) + ADAM_EPS) + ADAM_WD * w_ref[...])
        nm_ref[...] = mm
        nv_ref[...] = vv

    spec = pl.BlockSpec((tr, C), lambda i: (i, 0))
    shape = jax.ShapeDtypeStruct((Rr, C), F32)
    return pl.pallas_call(
        body, name=name, grid=(Rr // tr,),
        in_specs=[pl.BlockSpec((S, tr, C), lambda i: (0, i, 0)), spec, spec, spec],
        out_specs=[spec] * 4, out_shape=[shape] * 4,
        compiler_params=_params(("parallel",)),
    )(parts, w, m, v)


def _pack(arrs, dtype, row_mult):
    flat = jnp.concatenate([a.reshape(-1).astype(dtype) for a in arrs])
    per = row_mult * PACK_COLS
    total = -(-flat.shape[0] // per) * per
    return jnp.pad(flat, (0, total - flat.shape[0])).reshape(total // PACK_COLS, PACK_COLS)


def _unpack(packed, shapes, lead=()):
    flat = packed.reshape(lead + (-1,))
    out, off = [], 0
    for shp in shapes:
        n = 1
        for d in shp:
            n *= d
        out.append(flat[..., off:off + n].reshape(lead + tuple(shp)))
        off += n
    return out


HGRN_W = ("hgrn_w_in", "hgrn_w_out")
MLA_W = ("mla_w_in", "mla_w_q_up", "mla_w_kv_up", "mla_w_out")
FFN_W = ("ffn_w_up", "ffn_w_down")
BIG = HGRN_W + MLA_W + FFN_W
SMALL_SHARDED = {"ffn_conv_w": 2, "mla_q_a_norm": 1, "mla_kv_a_norm": 1}
REPLICATED = ["norm_mix", "norm_ffn", "hgrn_lower_bounds", "hgrn_out_norm", "mla_q_norm", "mla_k_norm",
              "ffn_conv_b"]
WEIGHTS = ["norm_mix", "norm_ffn", "hgrn_w_in", "hgrn_lower_bounds", "hgrn_out_norm", "hgrn_w_out", "mla_w_in",
           "mla_q_a_norm", "mla_w_q_up", "mla_kv_a_norm", "mla_w_kv_up", "mla_q_norm", "mla_k_norm", "mla_w_out",
           "ffn_w_up", "ffn_conv_w", "ffn_conv_b", "ffn_w_down"]


def _shards_to_cols(g):
    return g.transpose(1, 0, 2).reshape(g.shape[1], N_DEV * g.shape[2])


def _cols_to_shards(w):
    R = w.shape[0]
    return w.reshape(R, N_DEV, w.shape[1] // N_DEV).transpose(1, 0, 2)


def kernel(x, positions, norm_mix, norm_ffn, hgrn_w_in, hgrn_lower_bounds, hgrn_out_norm, hgrn_w_out, mla_w_in, mla_q_a_norm, mla_w_q_up, mla_kv_a_norm, mla_w_kv_up, mla_q_norm, mla_k_norm, mla_w_out, ffn_w_up, ffn_conv_w, ffn_conv_b, ffn_w_down, loss_target, m_norm_mix, m_norm_ffn, m_hgrn_w_in, m_hgrn_lower_bounds, m_hgrn_out_norm, m_hgrn_w_out, m_mla_w_in, m_mla_q_a_norm, m_mla_w_q_up, m_mla_kv_a_norm, m_mla_w_kv_up, m_mla_q_norm, m_mla_k_norm, m_mla_w_out, m_ffn_w_up, m_ffn_conv_w, m_ffn_conv_b, m_ffn_w_down, v_norm_mix, v_norm_ffn, v_hgrn_w_in, v_hgrn_lower_bounds, v_hgrn_out_norm, v_hgrn_w_out, v_mla_w_in, v_mla_q_a_norm, v_mla_w_q_up, v_mla_kv_a_norm, v_mla_w_kv_up, v_mla_q_norm, v_mla_k_norm, v_mla_w_out, v_ffn_w_up, v_ffn_conv_w, v_ffn_conv_b, v_ffn_w_down):
    local = dict(norm_mix=norm_mix, norm_ffn=norm_ffn, hgrn_w_in=hgrn_w_in, hgrn_lower_bounds=hgrn_lower_bounds,
                 hgrn_out_norm=hgrn_out_norm, hgrn_w_out=hgrn_w_out, mla_w_in=mla_w_in, mla_q_a_norm=mla_q_a_norm,
                 mla_w_q_up=mla_w_q_up, mla_kv_a_norm=mla_kv_a_norm, mla_w_kv_up=mla_w_kv_up, mla_q_norm=mla_q_norm,
                 mla_k_norm=mla_k_norm, mla_w_out=mla_w_out, ffn_w_up=ffn_w_up, ffn_conv_w=ffn_conv_w,
                 ffn_conv_b=ffn_conv_b, ffn_w_down=ffn_w_down)
    mom_m = dict(norm_mix=m_norm_mix, norm_ffn=m_norm_ffn, hgrn_w_in=m_hgrn_w_in,
                 hgrn_lower_bounds=m_hgrn_lower_bounds, hgrn_out_norm=m_hgrn_out_norm, hgrn_w_out=m_hgrn_w_out,
                 mla_w_in=m_mla_w_in, mla_q_a_norm=m_mla_q_a_norm, mla_w_q_up=m_mla_w_q_up,
                 mla_kv_a_norm=m_mla_kv_a_norm, mla_w_kv_up=m_mla_w_kv_up, mla_q_norm=m_mla_q_norm,
                 mla_k_norm=m_mla_k_norm, mla_w_out=m_mla_w_out, ffn_w_up=m_ffn_w_up, ffn_conv_w=m_ffn_conv_w,
                 ffn_conv_b=m_ffn_conv_b, ffn_w_down=m_ffn_w_down)
    mom_v = dict(norm_mix=v_norm_mix, norm_ffn=v_norm_ffn, hgrn_w_in=v_hgrn_w_in,
                 hgrn_lower_bounds=v_hgrn_lower_bounds, hgrn_out_norm=v_hgrn_out_norm, hgrn_w_out=v_hgrn_w_out,
                 mla_w_in=v_mla_w_in, mla_q_a_norm=v_mla_q_a_norm, mla_w_q_up=v_mla_w_q_up,
                 mla_kv_a_norm=v_mla_kv_a_norm, mla_w_kv_up=v_mla_w_kv_up, mla_q_norm=v_mla_q_norm,
                 mla_k_norm=v_mla_k_norm, mla_w_out=v_mla_w_out, ffn_w_up=v_ffn_w_up, ffn_conv_w=v_ffn_conv_w,
                 ffn_conv_b=v_ffn_conv_b, ffn_w_down=v_ffn_w_down)
    me = 4 * lax.axis_index("x") + 2 * lax.axis_index("y") + lax.axis_index("c")
    x, positions, target = x[0], positions[0], loss_target[0]
    T, D = x.shape
    depth = norm_mix.shape[0]
    R = mla_w_q_up.shape[1]
    cw = 2 * R + LANES
    small_names = list(SMALL_SHARDED)

    def block_of(kind, l):
        names = {"hgrn": HGRN_W, "mla": MLA_W, "ffn": FFN_W}[kind]
        idx = l if kind == "ffn" else l // 2
        return [(n, idx) for n in names]


    gathered = {}

    def gather_ride(host, l):
        if host == "mixer" and l % 2 == 0:
            keys = [("ffn_w_up", l)]
        elif host == "mixer":
            keys = block_of("ffn", l)
            if l + 1 < depth:
                keys += block_of("hgrn", l + 1) + [("ffn_w_down", l + 1)]
            if l + 2 < depth:
                keys += block_of("mla", l + 2)
        elif l == 0:
            keys = [("ffn_w_down", 0)] + (block_of("mla", 1) if depth > 1 else [])
        else:
            keys = []
        if not keys:
            return keys, None
        return keys, ([local[n][i].astype(BF16) for n, i in keys], ["gather"] * len(keys))

    def take_gathered(keys, arrs):
        for key, a in zip(keys, arrs):
            gathered[key] = a

    keys0 = block_of("hgrn", 0)
    small_local = _pack([local[n] for n in small_names], F32, 8)
    got = _exchange([local[n][i].astype(BF16) for n, i in keys0] + [small_local],
                    ["gather"] * (len(keys0) + 1), name="gather_first")
    take_gathered(keys0, got[:-1])
    small_all = _unpack(got[-1], [local[n].shape for n in small_names], lead=(N_DEV,))
    conv_w_all = small_all[0].transpose(1, 0, 2, 3)
    qa_all = small_all[1].transpose(1, 0, 2).reshape(-1, R)
    kva_all = small_all[2].transpose(1, 0, 2).reshape(-1, R)
    fs = conv_w_all.shape[-1]
    conv_b_s = ffn_conv_b.reshape(depth, N_DEV, 1, fs)

    ct, s1, s2 = _rope_tables(positions)
    lb_soft = jax.nn.softmax(hgrn_lower_bounds.astype(F32), axis=0)
    lower_bounds = jnp.cumsum(lb_soft, axis=0) - lb_soft[0:1]

    def mla_views(j):
        w_in = _pad_cols(gathered["mla_w_in", j].reshape(D, -1), cw)
        wq = _q_up_padded(_shards_to_cols(gathered["mla_w_q_up", j]))
        wkv = _shards_to_cols(gathered["mla_w_kv_up", j])
        gq = _pad_cols(mla_q_norm[j].reshape(1, QK_HEAD), QK_PAD)
        gk = _pad_cols(mla_k_norm[j].reshape(1, QK_HEAD), QK_PAD)
        return w_in, wq, wkv, gq, gk

    saved = []
    h = _rmsnorm_fwd(x, norm_mix[0], name="norm_mix_fwd_0")
    for layer in range(depth):
        j = layer // 2
        s = {"x_in": x}
        s["h_mix"] = h
        keys, ride = gather_ride("mixer", layer)
        if layer % 2 == 0:
            p = _mm(h, gathered["hgrn_w_in", j], b_fmt="knb", bm=1024, name=f"hgrn_in_{layer}")
            (og, o, s0), got = _hgrn_fwd(p, lower_bounds[j], hgrn_out_norm[j], name=f"hgrn_fwd_{layer}", ride=ride)
            s.update(p=p, og=og, o=o, s0=s0)
            take_gathered(keys, got)
            x, h = _mm(og, gathered["hgrn_w_out", j].reshape(D, D), res=x, norm_out=norm_ffn[layer],
                       name=f"hgrn_out_{layer}")
        else:
            w_in, wq, wkv, gq, gk = mla_views(j)
            c = _mm(h, w_in, bm=1024, name=f"mla_in_{layer}")
            q, k, v = _mla_prep_fwd(c, wq, wkv, qa_all[j], kva_all[j], gq, gk, ct, s1, s2,
                                    name=f"mla_prep_fwd_{layer}")
            (ot, lse), got = _attn_fwd(q, k, v, name=f"attn_fwd_{layer}", ride=ride)
            s.update(c=c, q=q, k=k, v=v, ot=ot, lse=lse)
            take_gathered(keys, got)
            x, h = _mm(ot, gathered["mla_w_out", j].reshape(D, D), a_fmt="kmb", res=x, norm_out=norm_ffn[layer],
                       name=f"mla_out_{layer}")
        s["x_mid"] = x
        s["h_ffn"] = h
        keys, ride = gather_ride("ffn_up", layer)
        (a, u, y), got = _ffn_up_fwd(h, gathered["ffn_w_up", layer], conv_w_all[layer], conv_b_s[layer],
                                     name=f"ffn_up_{layer}", ride=ride)
        take_gathered(keys, got)
        s.update(a=a, u=u, y=y)
        w_down = gathered["ffn_w_down", layer].reshape(-1, D)
        if layer + 1 < depth:
            x, h = _mm(a, w_down, a_fmt="mkb", res=x, norm_out=norm_mix[layer + 1], bm=1024, bk=fs,
                       name=f"ffn_down_{layer}")
        else:
            x = _mm(a, w_down, a_fmt="mkb", res=x, bm=1024, bk=fs, name=f"ffn_down_{layer}")
        saved.append(s)

    dx, loss_part = _loss_head(x, target, name="loss_head")

    parts = {}
    received = {}
    g_small = {n: [None] * local[n].shape[0] for n in REPLICATED + small_names}

    def scatter_ride(host, l):
        if host == "mixer" and l % 2 == 1:
            keys = block_of("ffn", l)
            if l + 1 < depth:
                keys += block_of("hgrn", l + 1) + block_of("ffn", l + 1)
        elif host == "mixer":
            keys = block_of("ffn", 0) if l == 0 else []
            if l > 0 and l + 1 < depth:
                keys += block_of("mla", l + 1)
        elif l == 0 and depth > 1:
            keys = block_of("mla", 1)
        else:
            keys = []
        if not keys:
            return keys, None
        return keys, ([parts[key] for key in keys], ["scatter"] * len(keys))

    def take_received(keys, arrs):
        for key, a in zip(keys, arrs):
            received[key] = a

    for layer in reversed(range(depth)):
        j = layer // 2
        s = saved[layer]
        parts["ffn_w_down", layer] = _mm(s["a"], dx, a_fmt="kmb", out_dtype=BF16, name=f"ffn_down_dw_{layer}"
                                         ).reshape(N_DEV, -1, D)
        keys, ride = scatter_ride("ffn_act", layer)
        (du, dcw, dcb), got = _ffn_act_bwd(dx, gathered["ffn_w_down", layer].reshape(-1, D), s["u"], s["y"],
                                           conv_w_all[layer], name=f"ffn_act_bwd_{layer}", ride=ride)
        take_received(keys, got)
        g_small["ffn_conv_w"][layer] = dcw.reshape(N_DEV, 3, fs)
        g_small["ffn_conv_b"][layer] = dcb.reshape(N_DEV * fs)
        du8 = du.reshape(N_DEV, T, fs)
        parts["ffn_w_up", layer] = _mm(s["h_ffn"], du8, a_fmt="km", b_fmt="knb", out_fmt="mnb", out_dtype=BF16,
                                       bm=1024, name=f"ffn_up_dw_{layer}")
        dx, dgain = _mm(du8, gathered["ffn_w_up", layer], a_fmt="mkb", b_fmt="nkb", bm=1024, bk=fs,
                        norm_bwd=(s["x_mid"], norm_ffn[layer], dx), name=f"ffn_up_dh_{layer}")
        g_small["norm_ffn"][layer] = dgain.reshape(D)
        keys, ride = scatter_ride("mixer", layer)
        if layer % 2 == 0:
            w_out = gathered["hgrn_w_out", j].reshape(D, D)
            parts["hgrn_w_out", j] = _mm(s["og"], dx, a_fmt="km", out_dtype=BF16, bm=1024,
                                         name=f"hgrn_out_dw_{layer}").reshape(N_DEV, -1, D)
            dog = _mm(dx, w_out, b_fmt="nk", name=f"hgrn_out_dx_{layer}")
            (dp, dlb, dgn), got = _hgrn_bwd(s["p"], lower_bounds[j], hgrn_out_norm[j], s["s0"], s["o"], dog,
                                            name=f"hgrn_bwd_{layer}", ride=ride)
            take_received(keys, got)
            g_small["hgrn_lower_bounds"][j] = dlb.reshape(D)
            g_small["hgrn_out_norm"][j] = dgn.reshape(HEAD)
            w_in_s = gathered["hgrn_w_in", j]
            parts["hgrn_w_in", j] = _mm(s["h_mix"], dp, a_fmt="km", out_fmt="mnb", out_dtype=BF16, bm=1024,
                                        bn=w_in_s.shape[2], name=f"hgrn_in_dw_{layer}")
            dx, dgain = _mm(dp, w_in_s, b_fmt="nkb", bm=1024, norm_bwd=(s["x_in"], norm_mix[layer], dx),
                            name=f"hgrn_in_dx_{layer}")
        else:
            w_in, wq, wkv, gq, gk = mla_views(j)
            w_out = gathered["mla_w_out", j].reshape(D, D)
            tb = s["ot"].shape[2]
            parts["mla_w_out", j] = _mm(s["ot"], dx, a_fmt="mkb", bk=tb, out_dtype=BF16, bm=1024,
                                        name=f"mla_out_dw_{layer}").reshape(N_DEV, -1, D)
            dot_ = _mm(w_out, dx, b_fmt="nk", out_fmt="mnb", out_dtype=BF16, bm=D, bn=tb,
                       name=f"mla_out_dx_{layer}")
            (dq, dk, dv), got = _attn_bwd(s["q"], s["k"], s["v"], s["ot"], s["lse"], dot_,
                                          name=f"attn_bwd_{layer}", ride=ride)
            take_received(keys, got)
            dc, dwq, dwkv, dgaq, dgakv, dgq, dgk = _mla_prep_bwd(
                s["c"], wq, wkv, qa_all[j], kva_all[j], gq, gk, ct, s1, s2, dq, dk, dv,
                name=f"mla_prep_bwd_{layer}")
            parts["mla_w_q_up", j] = _cols_to_shards(_q_up_unpadded(dwq)).astype(BF16)
            parts["mla_w_kv_up", j] = _cols_to_shards(dwkv).astype(BF16)
            g_small["mla_q_a_norm"][j] = dgaq.reshape(R)
            g_small["mla_kv_a_norm"][j] = dgakv.reshape(R)
            g_small["mla_q_norm"][j] = dgq[0, :QK_HEAD]
            g_small["mla_k_norm"][j] = dgk[0, :QK_HEAD]
            win_cols = mla_w_in.shape[2]
            dw_in = _mm(s["h_mix"], dc, a_fmt="km", bm=1024, name=f"mla_in_dw_{layer}")
            parts["mla_w_in", j] = dw_in[:, :win_cols].astype(BF16).reshape(N_DEV, -1, win_cols)
            dx, dgain = _mm(dc, w_in, b_fmt="nk", norm_bwd=(s["x_in"], norm_mix[layer], dx),
                            name=f"mla_in_dx_{layer}")
        g_small["norm_mix"][layer] = dgain.reshape(D)
    grad_x = dx

    dlb_eff = jnp.stack(g_small["hgrn_lower_bounds"])
    dsoft = jnp.cumsum(dlb_eff[::-1], axis=0)[::-1]
    dsoft = dsoft.at[0].add(-jnp.sum(dlb_eff, axis=0))
    g_lb = lb_soft * (dsoft - jnp.sum(dsoft * lb_soft, axis=0, keepdims=True))
    small_grads = {n: (g_lb if n == "hgrn_lower_bounds" else jnp.stack(g_small[n])) for n in g_small}

    small_grad_names = REPLICATED + small_names
    small_part = _pack([small_grads[n] for n in small_grad_names] + [loss_part], F32, 8)
    got = _exchange([parts[key] for key in keys0] + [small_part], ["scatter"] * len(keys0) + ["gather"],
                    name="exchange_last")
    take_received(keys0, got[:-1])
    small_recv = got[-1]

    out = {}
    for n in BIG:
        layers = local[n].shape[0]
        shard = local[n].shape[1:]
        p2 = jnp.concatenate([received[n, i].reshape(N_DEV, -1, shard[-1]) for i in range(layers)], axis=1)
        flat = lambda a: a.reshape(-1, shard[-1])
        res = _adamw(p2, flat(local[n]), flat(mom_m[n]), flat(mom_v[n]), name=f"adamw_{n}")
        for kind, a in zip(("grad", "delta", "new_m", "new_v"), res):
            out[kind, n] = a.reshape(local[n].shape)
    small_sum = _sum_slots(small_recv, name="sum_small")
    small_full = _unpack(small_sum, [small_grads[n].shape for n in small_grad_names] + [(1, LANES)])
    loss = small_full[-1][0, 0]
    g_mine = {}
    for n, a in zip(small_grad_names, small_full[:-1]):
        if n == "ffn_conv_w":
            a = lax.dynamic_index_in_dim(a, me, axis=1, keepdims=False)
        elif n in SMALL_SHARDED:
            size = local[n].shape[1]
            a = lax.dynamic_slice_in_dim(a, me * size, size, axis=1)
        g_mine[n] = a
    small_shapes = [local[n].shape for n in small_grad_names]
    res = _adamw(_pack([g_mine[n] for n in small_grad_names], F32, 8)[None],
                 _pack([local[n] for n in small_grad_names], F32, 8),
                 _pack([mom_m[n] for n in small_grad_names], F32, 8),
                 _pack([mom_v[n] for n in small_grad_names], F32, 8), name="adamw_small")
    for kind, packed in zip(("grad", "delta", "new_m", "new_v"), res):
        for n, a in zip(small_grad_names, _unpack(packed, small_shapes)):
            out[kind, n] = a

    outs = [loss, grad_x[None]]
    for kind in ("grad", "delta", "new_m", "new_v"):
        outs += [out[kind, n] for n in WEIGHTS]
    return tuple(outs)
```

```python
import functools

import jax
import jax.numpy as jnp
from jax import lax
from jax.experimental import pallas as pl
from jax.experimental.pallas import tpu as pltpu

F32 = jnp.float32
BF16 = jnp.bfloat16

RMS_EPS = 1e-6
ROPE_THETA = 10000.0
HEAD = 128
ROPE = 64
QK_HEAD = HEAD + ROPE
QK_PAD = 256
CHUNK = 64
SUB = 16
EXP_CLAMP = 60.0
HGRN_CHUNKS_PER_STEP = 8
HGRN_BWD_CHUNKS_PER_STEP = 4

ADAM_LR = 0.001
ADAM_B1 = 0.9
ADAM_B2 = 0.999
ADAM_EPS = 1e-08
ADAM_WD = 0.01
ADAM_STEP = 10

N_DEV = 8
LANES = 128
PACK_COLS = 1024
V7X_VMEM_LIMIT = 56 * 1024 * 1024

HI = lax.Precision.HIGHEST


def _params(sem):
    return pltpu.CompilerParams(dimension_semantics=sem, vmem_limit_bytes=V7X_VMEM_LIMIT)


def _blk(n, cap):
    if n <= cap:
        return n
    d = (cap // LANES) * LANES
    while d >= LANES:
        if n % d == 0:
            return d
        d -= LANES
    raise ValueError(f"no lane-aligned block for {n} under {cap}")


def _sigmoid(x):
    return jax.nn.sigmoid(x)


def _dot(a, b, dims, precision=None):
    return lax.dot_general(a, b, (dims, ((), ())), preferred_element_type=F32, precision=precision)


NN = ((1,), (0,))
NT = ((1,), (1,))
TN = ((0,), (0,))


def _mm(a, b, *, a_fmt="mk", b_fmt="kn", out_fmt="mn", res=None, norm_out=None, norm_bwd=None, out_dtype=F32,
        bm=512, bn=1024, bk=1024, name):
    if a_fmt == "mk":
        M, K = a.shape
    elif a_fmt == "km":
        K, M = a.shape
    elif a_fmt == "kmb":
        nb, K, B = a.shape
        M = nb * B
    else:
        nb, M, B = a.shape
        K = nb * B
    if b_fmt == "kn":
        Kb, N = b.shape
    elif b_fmt == "nk":
        N, Kb = b.shape
    elif b_fmt == "knb":
        nbb, Kb, Bb = b.shape
        N = nbb * Bb
    else:
        nbb, N, Bb = b.shape
        Kb = nbb * Bb
    assert K == Kb, (a.shape, b.shape, a_fmt, b_fmt)
    bm = B if a_fmt == "kmb" else _blk(M, bm)
    bn = Bb if b_fmt == "knb" else _blk(N, bn)
    if a_fmt == "mkb" and b_fmt == "nkb":
        assert B == Bb
    bk = _blk(B, bk) if a_fmt == "mkb" else (_blk(Bb, bk) if b_fmt == "nkb" else _blk(K, bk))
    nm, nn, nk = M // bm, N // bn, K // bk

    if a_fmt == "mk":
        a_spec = pl.BlockSpec((bm, bk), lambda i, j, k: (i, k))
        a_dim = 1
    elif a_fmt == "km":
        a_spec = pl.BlockSpec((bk, bm), lambda i, j, k: (k, i))
        a_dim = 0
    elif a_fmt == "kmb":
        a_spec = pl.BlockSpec((None, bk, bm), lambda i, j, k: (i, k, 0))
        a_dim = 0
    else:
        per = B // bk
        a_spec = pl.BlockSpec((None, bm, bk), lambda i, j, k: (k // per, i, k % per))
        a_dim = 1
    if b_fmt == "kn":
        b_spec = pl.BlockSpec((bk, bn), lambda i, j, k: (k, j))
        b_dim = 0
    elif b_fmt == "nk":
        b_spec = pl.BlockSpec((bn, bk), lambda i, j, k: (j, k))
        b_dim = 1
    elif b_fmt == "knb":
        b_spec = pl.BlockSpec((None, bk, bn), lambda i, j, k: (j, k, 0))
        b_dim = 0
    else:
        perb = Bb // bk
        b_spec = pl.BlockSpec((None, bn, bk), lambda i, j, k: (k // perb, j, k % perb))
        b_dim = 1
    if out_fmt == "mn":
        o_spec = pl.BlockSpec((bm, bn), lambda i, j, k: (i, j))
        o_shape = jax.ShapeDtypeStruct((M, N), out_dtype)
    else:
        o_spec = pl.BlockSpec((None, bm, bn), lambda i, j, k: (j, i, 0))
        o_shape = jax.ShapeDtypeStruct((nn, M, bn), out_dtype)
    in_specs = [a_spec, b_spec]
    args = [a, b]
    row_tile = pl.BlockSpec((bm, bn), lambda i, j, k: (i, j))
    row_vec = pl.BlockSpec((1, bn), lambda i, j, k: (0, j))
    if res is not None:
        assert out_fmt == "mn"
        in_specs.append(row_tile)
        args.append(res)
    out_specs, out_shapes = [o_spec], [o_shape]
    if norm_out is not None:
        assert nn == 1 and out_fmt == "mn"
        in_specs.append(row_vec)
        args.append(norm_out.reshape(1, N))
        out_specs.append(row_tile)
        out_shapes.append(jax.ShapeDtypeStruct((M, N), BF16))
    if norm_bwd is not None:
        assert nn == 1 and out_fmt == "mn" and res is None and norm_out is None
        xin, gain, dres = norm_bwd
        in_specs += [row_tile, row_vec, row_tile]
        args += [xin, gain.reshape(1, N), dres]
        out_specs.append(row_vec)
        out_shapes.append(jax.ShapeDtypeStruct((1, N), F32))
    dims = ((a_dim,), (b_dim,))
    has_res = res is not None
    n_in = len(in_specs)

    def body(*refs):
        a_ref, b_ref = refs[0], refs[1]
        extra_in = list(refs[2:n_in])
        o_ref = refs[n_in]
        part = _dot(a_ref[...].astype(BF16), b_ref[...].astype(BF16), dims)

        def finish(out):
            if has_res:
                out = out + extra_in[0][...]
            if norm_out is not None:
                g_ref, h_ref = extra_in[-1], refs[n_in + 1]
                r = lax.rsqrt(jnp.mean(out * out, axis=-1, keepdims=True) + RMS_EPS)
                h_ref[...] = ((out * r) * g_ref[...]).astype(h_ref.dtype)
            if norm_bwd is not None:
                x_ref, g_ref, dr_ref = extra_in
                dg_ref = refs[n_in + 1]
                xv = x_ref[...]
                r = lax.rsqrt(jnp.mean(xv * xv, axis=-1, keepdims=True) + RMS_EPS)
                n = xv * r
                dn = out * g_ref[...]
                gpart = jnp.sum(out * n, axis=0, keepdims=True)
                i = pl.program_id(0)

                @pl.when(i == 0)
                def _():
                    dg_ref[...] = gpart

                @pl.when(i > 0)
                def _():
                    dg_ref[...] += gpart

                out = dr_ref[...] + r * (dn - n * jnp.mean(dn * n, axis=-1, keepdims=True))
            o_ref[...] = out.astype(o_ref.dtype)

        if nk == 1:
            finish(part)
            return
        acc_ref = refs[-1]
        k = pl.program_id(2)

        @pl.when(k == 0)
        def _():
            acc_ref[...] = part

        @pl.when(jnp.logical_and(k > 0, k < nk - 1))
        def _():
            acc_ref[...] += part

        @pl.when(k == nk - 1)
        def _():
            finish(acc_ref[...] + part)

    multi = len(out_specs) > 1
    return pl.pallas_call(
        body, name=name, grid=(nm, nn, nk), in_specs=in_specs,
        out_specs=out_specs if multi else o_spec, out_shape=out_shapes if multi else o_shape,
        scratch_shapes=[] if nk == 1 else [pltpu.VMEM((bm, bn), F32)],
        compiler_params=_params(("arbitrary",) * 3 if norm_bwd is not None else ("parallel", "parallel", "arbitrary")),
    )(*args)


def _rmsnorm_fwd(x, gain, *, name, tm=512):
    T, D = x.shape
    tm = min(tm, T)

    def body(x_ref, g_ref, o_ref):
        xv = x_ref[...]
        r = lax.rsqrt(jnp.mean(xv * xv, axis=-1, keepdims=True) + RMS_EPS)
        o_ref[...] = ((xv * r) * g_ref[...]).astype(o_ref.dtype)

    return pl.pallas_call(
        body, name=name, grid=(T // tm,),
        in_specs=[pl.BlockSpec((tm, D), lambda i: (i, 0)), pl.BlockSpec((1, D), lambda i: (0, 0))],
        out_specs=pl.BlockSpec((tm, D), lambda i: (i, 0)),
        out_shape=jax.ShapeDtypeStruct((T, D), BF16),
        compiler_params=_params(("parallel",)),
    )(x, gain.reshape(1, D))


def _loss_head(y, target, *, name, tm=512):
    T, D = y.shape
    tm = min(tm, T)

    def body(y_ref, t_ref, dy_ref, l_ref):
        i = pl.program_id(0)
        e = y_ref[...] - t_ref[...]
        dy_ref[...] = e * (1.0 / D)
        s = 0.5 * jnp.sum(jnp.mean(e * e, axis=-1, keepdims=True), axis=0, keepdims=True)
        part = jnp.broadcast_to(s, (1, LANES))

        @pl.when(i == 0)
        def _():
            l_ref[...] = part

        @pl.when(i > 0)
        def _():
            l_ref[...] += part

    return pl.pallas_call(
        body, name=name, grid=(T // tm,),
        in_specs=[pl.BlockSpec((tm, D), lambda i: (i, 0)), pl.BlockSpec((tm, D), lambda i: (i, 0))],
        out_specs=[pl.BlockSpec((tm, D), lambda i: (i, 0)), pl.BlockSpec((1, LANES), lambda i: (0, 0))],
        out_shape=[jax.ShapeDtypeStruct((T, D), F32), jax.ShapeDtypeStruct((1, LANES), F32)],
        compiler_params=_params(("arbitrary",)),
    )(y, target)


def _hgrn_selectors():
    t = jnp.arange(CHUNK)[:, None]
    s = jnp.arange(CHUNK)[None, :]
    mats = [s <= t, s < (t // SUB) * SUB]
    for i in range(1, CHUNK // SUB):
        mats.append(jnp.broadcast_to(s < i * SUB, (8, CHUNK)))
    mats.append(jnp.ones((8, CHUNK), bool))
    sel = jnp.concatenate([m.astype(BF16) for m in mats], axis=0)
    rev = (s >= t).astype(BF16)
    return sel, rev


def _select_sums(sel, x):
    hi = x.astype(BF16)
    r1 = x - hi.astype(F32)
    mid = r1.astype(BF16)
    lo = (r1 - mid.astype(F32)).astype(BF16)
    return _dot(sel, hi, NN) + (_dot(sel, mid, NN) + _dot(sel, lo, NN))


def _hgrn_cums(sel, logf):
    nsub = CHUNK // SUB
    cums = _select_sums(sel, logf)
    g = cums[0:CHUNK]
    rrow = cums[CHUNK:2 * CHUNK]
    base = 2 * CHUNK
    rsel = [None] + [jnp.max(cums[base + 8 * (i - 1):base + 8 * i], axis=0, keepdims=True) for i in range(1, nsub)]
    gl = jnp.max(cums[base + 8 * (nsub - 1):base + 8 * nsub], axis=0, keepdims=True)
    return g, rrow, rsel, gl


def _hgrn_gates(p, lb, D):
    qpre, fpre, iv, gpre = p[:, 0:D], p[:, D:2 * D], p[:, 2 * D:3 * D], p[:, 3 * D:4 * D]
    sig = _sigmoid(fpre)
    forget = lb + (1.0 - lb) * sig
    key = 1.0 - forget
    logf = jnp.log(forget)
    sq = _sigmoid(qpre)
    qs = qpre * sq
    return qpre, sq, qs, sig, forget, key, logf, iv, gpre


def _hgrn_fwd(p, lb, gn, *, name, ride=None):
    T, D4 = p.shape
    D = D4 // 4
    H = D // HEAD
    nc = T // CHUNK
    nb = min(HGRN_CHUNKS_PER_STEP, nc)
    assert nc % nb == 0
    sel, _ = _hgrn_selectors()
    nsel = sel.shape[0]
    nsub = CHUNK // SUB
    heads = [slice(h * HEAD, (h + 1) * HEAD) for h in range(H)]

    def body(p_ref, lb_ref, gn_ref, sel_ref, og_ref, o_ref, s0_ref, st_ref):
        @pl.when(pl.program_id(0) == 0)
        def _():
            st_ref[...] = jnp.zeros_like(st_ref)

        row = lax.broadcasted_iota(jnp.int32, (CHUNK, CHUNK), 0)
        col = lax.broadcasted_iota(jnp.int32, (CHUNK, CHUNK), 1)
        gnv = gn_ref[...]
        lbv = lb_ref[...]
        selv = sel_ref[...]
        pre = []
        for cc in range(nb):
            pv = p_ref[cc * CHUNK:(cc + 1) * CHUNK, :]
            _, _, qs, _, _, key, logf, iv, gpre = _hgrn_gates(pv, lbv, D)
            g, rrow, rsel_all, gl = _hgrn_cums(selv, logf)
            qgb = (qs * jnp.exp(g)).astype(BF16)
            qtb = (qs * jnp.exp(g - rrow)).astype(BF16)
            ktb = [(key * jnp.exp(jnp.minimum((0.0 if r is None else r) - g, EXP_CLAMP))).astype(BF16)
                   for r in rsel_all]
            kdb = (key * jnp.exp(gl - g)).astype(BF16)
            vb = iv.astype(BF16)
            blocks = [[_dot(qtb[i * SUB:(i + 1) * SUB, sl], ktb[i][:, sl], NT) for i in range(nsub)]
                      for sl in heads]
            amats = [jnp.where(col <= row, jnp.concatenate(bl, axis=0), 0.0).astype(BF16) for bl in blocks]
            pre.append(dict(qgb=qgb, egl=jnp.exp(gl), gate=gpre * _sigmoid(gpre),
                            intra=[_dot(a, vb[:, sl], NN) for a, sl in zip(amats, heads)],
                            upd=[_dot(vb[:, sl], kdb[:, sl], TN) for sl in heads]))
        sts = [st_ref[sl, :] for sl in heads]
        for cc, d in enumerate(pre):
            rows = slice(cc * CHUNK, (cc + 1) * CHUNK)
            inter = [_dot(d["qgb"][:, sl], st.astype(BF16), NT) for sl, st in zip(heads, sts)]
            for h, sl in enumerate(heads):
                s0_ref[cc, sl, :] = sts[h]
                o = inter[h] + d["intra"][h]
                o_ref[rows, sl] = o
                r = lax.rsqrt(jnp.mean(o * o, axis=-1, keepdims=True) + RMS_EPS)
                og_ref[rows, sl] = (((o * r) * gnv) * d["gate"][:, sl]).astype(og_ref.dtype)
            sts = [st * d["egl"][:, sl] + u for st, sl, u in zip(sts, heads, d["upd"])]
        for sl, st in zip(heads, sts):
            st_ref[sl, :] = st

    rb = nb * CHUNK
    return _call(
        body, name=name, grid=(nc // nb,),
        in_specs=[pl.BlockSpec((rb, D4), lambda c: (c, 0)), pl.BlockSpec((1, D), lambda c: (0, 0)),
                  pl.BlockSpec((1, HEAD), lambda c: (0, 0)), pl.BlockSpec((nsel, CHUNK), lambda c: (0, 0))],
        out_specs=[pl.BlockSpec((rb, D), lambda c: (c, 0)), pl.BlockSpec((rb, D), lambda c: (c, 0)),
                   pl.BlockSpec((nb, D, HEAD), lambda c: (c, 0, 0))],
        out_shape=[jax.ShapeDtypeStruct((T, D), BF16), jax.ShapeDtypeStruct((T, D), F32),
                   jax.ShapeDtypeStruct((nc, D, HEAD), F32)],
        scratch_shapes=[pltpu.VMEM((D, HEAD), F32)], sem=("arbitrary",), ride=ride,
        args=(p, lb.reshape(1, D), gn.reshape(1, HEAD), sel))


def _hgrn_bwd(p, lb, gn, s0, o_saved, dog, *, name, ride=None):
    T, D4 = p.shape
    D = D4 // 4
    H = D // HEAD
    nc = T // CHUNK
    nb = min(HGRN_BWD_CHUNKS_PER_STEP, nc)
    assert nc % nb == 0
    nsteps = nc // nb
    sel, rev = _hgrn_selectors()
    nsel = sel.shape[0]
    nsub = CHUNK // SUB
    heads = [slice(h * HEAD, (h + 1) * HEAD) for h in range(H)]
    cat = lambda xs: jnp.concatenate(xs, axis=1)

    def body(p_ref, lb_ref, gn_ref, sel_ref, rev_ref, s0_ref, s1_ref, o_ref, dog_ref,
             dp_ref, dlb_ref, dgn_ref, dst_ref):
        @pl.when(pl.program_id(0) == 0)
        def _():
            dst_ref[...] = jnp.zeros_like(dst_ref)
            dlb_ref[...] = jnp.zeros_like(dlb_ref)
            dgn_ref[...] = jnp.zeros_like(dgn_ref)

        row = lax.broadcasted_iota(jnp.int32, (CHUNK, CHUNK), 0)
        col = lax.broadcasted_iota(jnp.int32, (CHUNK, CHUNK), 1)
        causal = col <= row
        gnv, lbv, selv, revv = gn_ref[...], lb_ref[...], sel_ref[...], rev_ref[...]
        dgn_acc = jnp.zeros((1, HEAD), F32)
        pre = []
        for cc in range(nb):
            rows = slice(cc * CHUNK, (cc + 1) * CHUNK)
            qpre, sq, qs, sig, forget, key, logf, iv, gpre = _hgrn_gates(p_ref[rows, :], lbv, D)
            g, rrow, rsel_all, gl = _hgrn_cums(selv, logf)
            eg = jnp.exp(g)
            eqr = jnp.exp(g - rrow)
            eis = [jnp.exp(jnp.minimum((0.0 if r is None else r) - g, EXP_CLAMP)) for r in rsel_all]
            ekd = jnp.exp(gl - g)
            qgb = (qs * eg).astype(BF16)
            qtb = (qs * eqr).astype(BF16)
            ktb = [(key * e).astype(BF16) for e in eis]
            kdb = (key * ekd).astype(BF16)
            vb = iv.astype(BF16)
            sg = _sigmoid(gpre)
            gate = gpre * sg
            dgate = sg * (1.0 + gpre * (1.0 - sg))
            dobs, dgpres = [], []
            for sl in heads:
                o = o_ref[rows, sl]
                r = lax.rsqrt(jnp.mean(o * o, axis=-1, keepdims=True) + RMS_EPS)
                n = o * r
                dog_h = dog_ref[rows, sl]
                d_on = dog_h * gate[:, sl]
                dgpres.append(dog_h * (n * gnv) * dgate[:, sl])
                dgn_acc = dgn_acc + jnp.sum(d_on * n, axis=0, keepdims=True)
                dn = d_on * gnv
                dobs.append((r * (dn - n * jnp.mean(dn * n, axis=-1, keepdims=True))).astype(BF16))
            blocks = [[_dot(qtb[i * SUB:(i + 1) * SUB, sl], ktb[i][:, sl], NT) for i in range(nsub)]
                      for sl in heads]
            amats = [jnp.where(causal, jnp.concatenate(bl, axis=0), 0.0).astype(BF16) for bl in blocks]
            dabs = [jnp.where(causal, _dot(dob, vb[:, sl], NT), 0.0).astype(BF16) for dob, sl in zip(dobs, heads)]
            dq_inter = [_dot(dob, s0_ref[cc, sl, :].astype(BF16), NN) for dob, sl in zip(dobs, heads)]
            dqt = [jnp.concatenate([_dot(dab[i * SUB:(i + 1) * SUB], ktb[i][:, sl], NN) for i in range(nsub)],
                                   axis=0) for dab, sl in zip(dabs, heads)]
            dkt = [[_dot(dab[i * SUB:(i + 1) * SUB], qtb[i * SUB:(i + 1) * SUB, sl], TN) for i in range(nsub)]
                   for dab, sl in zip(dabs, heads)]
            dv_intra = [_dot(a, dob, TN) for a, dob in zip(amats, dobs)]
            upd = [_dot(dob, qgb[:, sl], TN) for dob, sl in zip(dobs, heads)]
            dq = cat(dq_inter) * eg + cat(dqt) * eqr
            dk_intra = cat([dkt[h][0] for h in range(H)]) * eis[0]
            for i in range(1, nsub):
                dk_intra = dk_intra + cat([dkt[h][i] for h in range(H)]) * eis[i]
            s1 = [s0_ref[cc + 1, sl, :] if cc + 1 < nb else s1_ref[sl, :] for sl in heads]
            pre.append(dict(qpre=qpre, sq=sq, qs=qs, sig=sig, forget=forget, key=key, ekd=ekd, egl=jnp.exp(gl),
                            kdb=kdb, vb=vb, dq=dq, dk_intra=dk_intra, dv_intra=dv_intra, upd=upd, s1=s1,
                            dgpre=cat(dgpres)))
        dsts = [dst_ref[sl, :] for sl in heads]
        dlb_acc = jnp.zeros((1, D), F32)
        for cc in reversed(range(nb)):
            d = pre[cc]
            rows = slice(cc * CHUNK, (cc + 1) * CHUNK)
            dstb = [x.astype(BF16) for x in dsts]
            dk_state = cat([_dot(d["vb"][:, sl], x, NN) for sl, x in zip(heads, dstb)])
            dv = cat([dvi + _dot(d["kdb"][:, sl], x, NT) for dvi, sl, x in zip(d["dv_intra"], heads, dstb)])
            term = cat([jnp.sum(x * s, axis=0, keepdims=True) for x, s in zip(dsts, d["s1"])])
            dsts = [x * d["egl"][:, sl] + u for x, sl, u in zip(dsts, heads, d["upd"])]
            dk = d["dk_intra"] + dk_state * d["ekd"]
            dq = d["dq"]
            dg = d["qs"] * dq - d["key"] * dk
            dlogf = _select_sums(revv, dg) + term
            sgf = d["sig"]
            dforget = dlogf / d["forget"] - dk
            dlb_acc = dlb_acc + jnp.sum(dforget * (1.0 - sgf), axis=0, keepdims=True)
            sqv = d["sq"]
            dp_ref[rows, 0:D] = (dq * (sqv * (1.0 + d["qpre"] * (1.0 - sqv)))).astype(dp_ref.dtype)
            dp_ref[rows, D:2 * D] = (dforget * (1.0 - lbv) * (sgf * (1.0 - sgf))).astype(dp_ref.dtype)
            dp_ref[rows, 2 * D:3 * D] = dv.astype(dp_ref.dtype)
            dp_ref[rows, 3 * D:4 * D] = d["dgpre"].astype(dp_ref.dtype)
        for sl, x in zip(heads, dsts):
            dst_ref[sl, :] = x
        dlb_ref[...] += dlb_acc
        dgn_ref[...] += dgn_acc

    rb = nb * CHUNK
    rc = lambda c: nsteps - 1 - c
    return _call(
        body, name=name, grid=(nsteps,),
        in_specs=[pl.BlockSpec((rb, D4), lambda c: (rc(c), 0)), pl.BlockSpec((1, D), lambda c: (0, 0)),
                  pl.BlockSpec((1, HEAD), lambda c: (0, 0)), pl.BlockSpec((nsel, CHUNK), lambda c: (0, 0)),
                  pl.BlockSpec((CHUNK, CHUNK), lambda c: (0, 0)),
                  pl.BlockSpec((nb, D, HEAD), lambda c: (rc(c), 0, 0)),
                  pl.BlockSpec((None, D, HEAD), lambda c: (jnp.minimum((rc(c) + 1) * nb, nc - 1), 0, 0)),
                  pl.BlockSpec((rb, D), lambda c: (rc(c), 0)), pl.BlockSpec((rb, D), lambda c: (rc(c), 0))],
        out_specs=[pl.BlockSpec((rb, D4), lambda c: (rc(c), 0)), pl.BlockSpec((1, D), lambda c: (0, 0)),
                   pl.BlockSpec((1, HEAD), lambda c: (0, 0))],
        out_shape=[jax.ShapeDtypeStruct((T, D4), BF16), jax.ShapeDtypeStruct((1, D), F32),
                   jax.ShapeDtypeStruct((1, HEAD), F32)],
        scratch_shapes=[pltpu.VMEM((D, HEAD), F32)], sem=("arbitrary",), ride=ride,
        args=(p, lb.reshape(1, D), gn.reshape(1, HEAD), sel, rev, s0, s0, o_saved, dog))


def _rope_tables(positions):
    inv_freq = ROPE_THETA ** (-jnp.arange(0, ROPE, 2, dtype=F32) / ROPE)
    ang = positions.astype(F32)[:, None] * inv_freq
    cos, sin = jnp.cos(ang), jnp.sin(ang)
    z = jnp.zeros_like(cos)
    ctab = jnp.concatenate([cos, cos, z, z], axis=-1)
    s1 = jnp.concatenate([-sin, z, z, z], axis=-1)
    s2 = jnp.concatenate([z, sin, z, z], axis=-1)
    return ctab, s1, s2


def _rope(z, ct, s1, s2):
    return z * ct + pltpu.roll(z, 96, 1) * s1 + pltpu.roll(z, 32, 1) * s2


def _rope_t(d, ct, s1, s2):
    return d * ct + pltpu.roll(d * s1, 32, 1) + pltpu.roll(d * s2, 96, 1)


def _mla_prep_fwd(c, wq, wkv, ga_q, ga_kv, gq, gk, ct, s1, s2, *, name, tm=512):
    T, CW = c.shape
    R = (CW - LANES) // 2
    H = wq.shape[1] // QK_PAD
    tm = min(tm, T)

    def body(c_ref, wq_ref, wkv_ref, gaq_ref, gakv_ref, gq_ref, gk_ref, ct_ref, s1_ref, s2_ref,
             q_ref, k_ref, v_ref):
        cv = c_ref[...]
        cq, ckv, kr = cv[:, 0:R], cv[:, R:2 * R], cv[:, 2 * R:2 * R + LANES]
        rq = lax.rsqrt(jnp.mean(cq * cq, axis=-1, keepdims=True) + RMS_EPS)
        cqn = ((cq * rq) * gaq_ref[...]).astype(BF16)
        rk = lax.rsqrt(jnp.mean(ckv * ckv, axis=-1, keepdims=True) + RMS_EPS)
        ckvn = ((ckv * rk) * gakv_ref[...]).astype(BF16)
        qp = _dot(cqn, wq_ref[...], NN)
        kvp = _dot(ckvn, wkv_ref[...], NN)
        ctv, s1v, s2v = ct_ref[...], s1_ref[...], s2_ref[...]
        gqv, gkv = gq_ref[...], gk_ref[...]
        krs = jnp.sum(kr * kr, axis=-1, keepdims=True)
        for h in range(H):
            b = h * QK_PAD
            qn, qr = qp[:, b:b + HEAD], qp[:, b + HEAD:b + QK_PAD]
            ss = jnp.sum(qn * qn + qr * qr, axis=-1, keepdims=True)
            rr = lax.rsqrt(ss * (1.0 / QK_HEAD) + RMS_EPS)
            q_ref[:, b:b + HEAD] = ((qn * rr) * gqv[:, 0:HEAD]).astype(q_ref.dtype)
            q_ref[:, b + HEAD:b + QK_PAD] = _rope((qr * rr) * gqv[:, HEAD:QK_PAD], ctv, s1v, s2v).astype(q_ref.dtype)
            kn, vv = kvp[:, b:b + HEAD], kvp[:, b + HEAD:b + QK_PAD]
            ssk = jnp.sum(kn * kn, axis=-1, keepdims=True) + krs
            rrk = lax.rsqrt(ssk * (1.0 / QK_HEAD) + RMS_EPS)
            k_ref[:, b:b + HEAD] = ((kn * rrk) * gkv[:, 0:HEAD]).astype(k_ref.dtype)
            k_ref[:, b + HEAD:b + QK_PAD] = _rope((kr * rrk) * gkv[:, HEAD:QK_PAD], ctv, s1v, s2v).astype(k_ref.dtype)
            v_ref[:, h * HEAD:(h + 1) * HEAD] = vv.astype(v_ref.dtype)

    full = lambda shape: pl.BlockSpec(shape, lambda i: (0, 0))
    tok = lambda w: pl.BlockSpec((tm, w), lambda i: (i, 0))
    return pl.pallas_call(
        body, name=name, grid=(T // tm,),
        in_specs=[tok(CW), full(wq.shape), full(wkv.shape), full((1, R)), full((1, R)), full((1, QK_PAD)),
                  full((1, QK_PAD)), tok(LANES), tok(LANES), tok(LANES)],
        out_specs=[tok(H * QK_PAD), tok(H * QK_PAD), tok(H * HEAD)],
        out_shape=[jax.ShapeDtypeStruct((T, H * QK_PAD), BF16), jax.ShapeDtypeStruct((T, H * QK_PAD), BF16),
                   jax.ShapeDtypeStruct((T, H * HEAD), BF16)],
        compiler_params=_params(("parallel",)),
    )(c, wq, wkv, ga_q.reshape(1, R), ga_kv.reshape(1, R), gq, gk, ct, s1, s2)


def _mla_prep_bwd(c, wq, wkv, ga_q, ga_kv, gq, gk, ct, s1, s2, dq, dk, dv, *, name, tm=512):
    T, CW = c.shape
    R = (CW - LANES) // 2
    H = wq.shape[1] // QK_PAD
    tm = min(tm, T)

    def body(c_ref, wq_ref, wkv_ref, gaq_ref, gakv_ref, gq_ref, gk_ref, ct_ref, s1_ref, s2_ref,
             dq_ref, dk_ref, dv_ref,
             dc_ref, dwq_ref, dwkv_ref, dgaq_ref, dgakv_ref, dgq_ref, dgk_ref, dqp_ref, dkvp_ref):
        i = pl.program_id(0)

        @pl.when(i == 0)
        def _():
            for ref in (dwq_ref, dwkv_ref, dgaq_ref, dgakv_ref, dgq_ref, dgk_ref):
                ref[...] = jnp.zeros_like(ref)

        cv = c_ref[...]
        cq, ckv, kr = cv[:, 0:R], cv[:, R:2 * R], cv[:, 2 * R:2 * R + LANES]
        rq = lax.rsqrt(jnp.mean(cq * cq, axis=-1, keepdims=True) + RMS_EPS)
        nq = cq * rq
        cqn = (nq * gaq_ref[...]).astype(BF16)
        rk = lax.rsqrt(jnp.mean(ckv * ckv, axis=-1, keepdims=True) + RMS_EPS)
        nkv = ckv * rk
        ckvn = (nkv * gakv_ref[...]).astype(BF16)
        qp = _dot(cqn, wq_ref[...], NN)
        kvp = _dot(ckvn, wkv_ref[...], NN)
        ctv, s1v, s2v = ct_ref[...], s1_ref[...], s2_ref[...]
        gqv, gkv = gq_ref[...], gk_ref[...]
        krs = jnp.sum(kr * kr, axis=-1, keepdims=True)
        dkr = jnp.zeros((tm, LANES), F32)
        dgq_n = jnp.zeros((1, HEAD), F32)
        dgq_r = jnp.zeros((1, HEAD), F32)
        dgk_n = jnp.zeros((1, HEAD), F32)
        dgk_r = jnp.zeros((1, HEAD), F32)
        for h in range(H):
            b = h * QK_PAD
            qn, qr = qp[:, b:b + HEAD], qp[:, b + HEAD:b + QK_PAD]
            ss = jnp.sum(qn * qn + qr * qr, axis=-1, keepdims=True)
            rr = lax.rsqrt(ss * (1.0 / QK_HEAD) + RMS_EPS)
            un, ur = qn * rr, qr * rr
            dzn = dq_ref[:, b:b + HEAD]
            dzr = _rope_t(dq_ref[:, b + HEAD:b + QK_PAD], ctv, s1v, s2v)
            dgq_n = dgq_n + jnp.sum(dzn * un, axis=0, keepdims=True)
            dgq_r = dgq_r + jnp.sum(dzr * ur, axis=0, keepdims=True)
            dun, dur = dzn * gqv[:, 0:HEAD], dzr * gqv[:, HEAD:QK_PAD]
            m = jnp.sum(dun * un + dur * ur, axis=-1, keepdims=True) * (1.0 / QK_HEAD)
            dqp_ref[:, b:b + HEAD] = (rr * (dun - un * m)).astype(BF16)
            dqp_ref[:, b + HEAD:b + QK_PAD] = (rr * (dur - ur * m)).astype(BF16)
            kn = kvp[:, b:b + HEAD]
            ssk = jnp.sum(kn * kn, axis=-1, keepdims=True) + krs
            rrk = lax.rsqrt(ssk * (1.0 / QK_HEAD) + RMS_EPS)
            vn, vr = kn * rrk, kr * rrk
            dyn = dk_ref[:, b:b + HEAD]
            dyr = _rope_t(dk_ref[:, b + HEAD:b + QK_PAD], ctv, s1v, s2v)
            dgk_n = dgk_n + jnp.sum(dyn * vn, axis=0, keepdims=True)
            dgk_r = dgk_r + jnp.sum(dyr * vr, axis=0, keepdims=True)
            dvn, dvr = dyn * gkv[:, 0:HEAD], dyr * gkv[:, HEAD:QK_PAD]
            mk = jnp.sum(dvn * vn + dvr * vr, axis=-1, keepdims=True) * (1.0 / QK_HEAD)
            dkvp_ref[:, b:b + HEAD] = (rrk * (dvn - vn * mk)).astype(BF16)
            dkr = dkr + rrk * (dvr - vr * mk)
            dkvp_ref[:, b + HEAD:b + QK_PAD] = dv_ref[:, h * HEAD:(h + 1) * HEAD].astype(BF16)
        dgq_ref[:, 0:HEAD] += dgq_n
        dgq_ref[:, HEAD:QK_PAD] += dgq_r
        dgk_ref[:, 0:HEAD] += dgk_n
        dgk_ref[:, HEAD:QK_PAD] += dgk_r
        dqp = dqp_ref[...]
        dkvp = dkvp_ref[...]
        dwq_ref[...] += _dot(cqn, dqp, TN)
        dwkv_ref[...] += _dot(ckvn, dkvp, TN)
        dcqn = _dot(dqp, wq_ref[...], NT)
        dckvn = _dot(dkvp, wkv_ref[...], NT)
        dgaq_ref[...] += jnp.sum(dcqn * nq, axis=0, keepdims=True)
        dgakv_ref[...] += jnp.sum(dckvn * nkv, axis=0, keepdims=True)
        dnq = dcqn * gaq_ref[...]
        dnkv = dckvn * gakv_ref[...]
        dc_ref[:, 0:R] = (rq * (dnq - nq * jnp.mean(dnq * nq, axis=-1, keepdims=True))).astype(dc_ref.dtype)
        dc_ref[:, R:2 * R] = (rk * (dnkv - nkv * jnp.mean(dnkv * nkv, axis=-1, keepdims=True))).astype(dc_ref.dtype)
        dc_ref[:, 2 * R:2 * R + LANES] = dkr.astype(dc_ref.dtype)

    full = lambda shape: pl.BlockSpec(shape, lambda i: (0, 0))
    tok = lambda w: pl.BlockSpec((tm, w), lambda i: (i, 0))
    return pl.pallas_call(
        body, name=name, grid=(T // tm,),
        in_specs=[tok(CW), full(wq.shape), full(wkv.shape), full((1, R)), full((1, R)), full((1, QK_PAD)),
                  full((1, QK_PAD)), tok(LANES), tok(LANES), tok(LANES),
                  tok(H * QK_PAD), tok(H * QK_PAD), tok(H * HEAD)],
        out_specs=[tok(CW), full(wq.shape), full(wkv.shape), full((1, R)), full((1, R)), full((1, QK_PAD)),
                   full((1, QK_PAD))],
        out_shape=[jax.ShapeDtypeStruct((T, CW), BF16), jax.ShapeDtypeStruct(wq.shape, F32),
                   jax.ShapeDtypeStruct(wkv.shape, F32), jax.ShapeDtypeStruct((1, R), F32),
                   jax.ShapeDtypeStruct((1, R), F32), jax.ShapeDtypeStruct((1, QK_PAD), F32),
                   jax.ShapeDtypeStruct((1, QK_PAD), F32)],
        scratch_shapes=[pltpu.VMEM((tm, H * QK_PAD), BF16), pltpu.VMEM((tm, H * QK_PAD), BF16)],
        compiler_params=_params(("arbitrary",)),
    )(c, wq, wkv, ga_q.reshape(1, R), ga_kv.reshape(1, R), gq, gk, ct, s1, s2, dq, dk, dv)


NEG = -1e30
LOG2E = 1.4426950408889634


def _attn_fwd(q, k, v, *, name, tb=512, hp=2, ride=None):
    T = q.shape[0]
    H = q.shape[1] // QK_PAD
    tb = min(tb, T)
    nq = T // tb
    scale = QK_HEAD ** -0.5
    c2 = scale * LOG2E
    assert H % hp == 0

    def body(q_ref, k_ref, v_ref, ot_ref, lse_ref, m_ref, l_ref, acc_ref):
        i = pl.program_id(1)
        m_ref[...] = jnp.full_like(m_ref, NEG)
        l_ref[...] = jnp.zeros_like(l_ref)
        acc_ref[...] = jnp.zeros_like(acc_ref)

        def step(js, masked):
            offs = [pl.multiple_of(j * tb, tb) for j in js]
            sts = [[_dot(k_ref[pl.ds(off, tb), hh * QK_PAD:(hh + 1) * QK_PAD],
                         q_ref[:, hh * QK_PAD:(hh + 1) * QK_PAD], NT) for hh in range(hp)] for off in offs]
            for b, hh in [(b, hh) for b in range(len(js)) for hh in range(hp)]:
                off = offs[b]
                vs = slice(hh * HEAD, (hh + 1) * HEAD)
                vb = v_ref[pl.ds(off, tb), vs]
                st = sts[b][hh]
                if masked:
                    kpos = lax.broadcasted_iota(jnp.int32, (tb, tb), 0)
                    qpos = lax.broadcasted_iota(jnp.int32, (tb, tb), 1)
                    st = jnp.where(kpos <= qpos, st, NEG)
                m_old = m_ref[hh]
                m_new = jnp.maximum(m_old, jnp.max(st, axis=0, keepdims=True))
                alpha = jnp.exp2((m_old - m_new) * c2)
                pt = jnp.exp2((st - m_new) * c2)
                l_ref[hh] = l_ref[hh] * alpha + jnp.sum(pt, axis=0, keepdims=True)
                acc_ref[vs, :] = acc_ref[vs, :] * alpha + _dot(vb, pt.astype(BF16), TN)
                m_ref[hh] = m_new

        def pair_body(t, carry):
            step([2 * t, 2 * t + 1], False)
            return carry

        lax.fori_loop(0, i // 2, pair_body, 0)

        @pl.when(i % 2 == 1)
        def _():
            step([i - 1], False)

        step([i], True)
        for hh in range(hp):
            vs = slice(hh * HEAD, (hh + 1) * HEAD)
            l = l_ref[hh]
            ot_ref[vs, :] = (acc_ref[vs, :] / l).astype(ot_ref.dtype)
            lse_ref[hh] = m_ref[hh] * scale + jnp.log(l)

    return _call(
        body, name=name, grid=(H // hp, nq),
        in_specs=[pl.BlockSpec((tb, hp * QK_PAD), lambda g, i: (i, g)),
                  pl.BlockSpec((T, hp * QK_PAD), lambda g, i: (0, g)),
                  pl.BlockSpec((T, hp * HEAD), lambda g, i: (0, g))],
        out_specs=[pl.BlockSpec((None, hp * HEAD, tb), lambda g, i: (i, g, 0)),
                   pl.BlockSpec((hp, None, 1, tb), lambda g, i: (g, i, 0, 0))],
        out_shape=[jax.ShapeDtypeStruct((nq, H * HEAD, tb), BF16), jax.ShapeDtypeStruct((H, nq, 1, tb), F32)],
        scratch_shapes=[pltpu.VMEM((hp, 1, tb), F32), pltpu.VMEM((hp, 1, tb), F32),
                        pltpu.VMEM((hp * HEAD, tb), F32)],
        sem=("parallel", "arbitrary"), ride=ride, args=(q, k, v))


def _attn_bwd(q, k, v, ot, lse, dot_, *, name, ride=None):
    T = q.shape[0]
    H = q.shape[1] // QK_PAD
    nq, _, tb = ot.shape
    scale = QK_HEAD ** -0.5
    c2 = scale * LOG2E

    def body(q_ref, k_ref, v_ref, ot_ref, lse_ref, dot_ref, dq_ref, dk_ref, dv_ref, dka_ref, dva_ref):
        j = pl.program_id(1)

        @pl.when(j == 0)
        def _():
            dq_ref[...] = jnp.zeros_like(dq_ref)

        kb = k_ref[...]
        vb = v_ref[...]

        def products(i, masked):
            off = i * tb if isinstance(i, int) else pl.multiple_of(i * tb, tb)
            qb = q_ref[pl.ds(off, tb), :]
            dob = dot_ref[i]
            st = _dot(kb, qb, NT)
            if masked:
                kpos = lax.broadcasted_iota(jnp.int32, (tb, tb), 0)
                qpos = lax.broadcasted_iota(jnp.int32, (tb, tb), 1)
                st = jnp.where(kpos <= qpos, st, NEG)
            return i, off, qb, dob, st, _dot(vb, dob, NN)

        def finish(i, off, qb, dob, st, dpt):
            pt = jnp.exp2(st * c2 - lse_ref[i] * LOG2E)
            delta = jnp.sum(dob.astype(F32) * ot_ref[i].astype(F32), axis=0, keepdims=True)
            dst = (pt * (dpt - delta)).astype(BF16)
            dq_ref[pl.ds(off, tb), :] += _dot(dst, kb, TN)
            return _dot(dst, qb, NN), _dot(pt.astype(BF16), dob, NT)

        def step(i, masked):
            return finish(*products(i, masked))

        dk0, dv0 = step(j, True)
        dka_ref[...] = dk0
        dva_ref[...] = dv0

        rest = nq - 1 - j

        def pair_body(t, carry):
            i0 = j + 1 + 2 * t
            first = products(i0, False)
            second = products(i0 + 1, False)
            dk1, dv1 = finish(*first)
            dk2, dv2 = finish(*second)
            dka_ref[...] += dk1 + dk2
            dva_ref[...] += dv1 + dv2
            return carry

        lax.fori_loop(0, rest // 2, pair_body, 0)

        @pl.when(rest % 2 == 1)
        def _():
            dk1, dv1 = step(nq - 1, False)
            dka_ref[...] += dk1
            dva_ref[...] += dv1

        dk_ref[...] = dka_ref[...] * scale
        dv_ref[...] = dva_ref[...]

        @pl.when(j == nq - 1)
        def _():
            dq_ref[...] = dq_ref[...] * scale

    return _call(
        body, name=name, grid=(H, nq),
        in_specs=[pl.BlockSpec((T, QK_PAD), lambda h, j: (0, h)), pl.BlockSpec((tb, QK_PAD), lambda h, j: (j, h)),
                  pl.BlockSpec((tb, HEAD), lambda h, j: (j, h)),
                  pl.BlockSpec((nq, HEAD, tb), lambda h, j: (0, h, 0)),
                  pl.BlockSpec((None, nq, 1, tb), lambda h, j: (h, 0, 0, 0)),
                  pl.BlockSpec((nq, HEAD, tb), lambda h, j: (0, h, 0))],
        out_specs=[pl.BlockSpec((T, QK_PAD), lambda h, j: (0, h)), pl.BlockSpec((tb, QK_PAD), lambda h, j: (j, h)),
                   pl.BlockSpec((tb, HEAD), lambda h, j: (j, h))],
        out_shape=[jax.ShapeDtypeStruct((T, H * QK_PAD), F32), jax.ShapeDtypeStruct((T, H * QK_PAD), F32),
                   jax.ShapeDtypeStruct((T, H * HEAD), F32)],
        scratch_shapes=[pltpu.VMEM((tb, QK_PAD), F32), pltpu.VMEM((tb, HEAD), F32)],
        sem=("parallel", "arbitrary"), ride=ride, args=(q, k, v, ot, lse, dot_))


def _conv_taps(u, prev6, prev7):
    rows = lax.broadcasted_iota(jnp.int32, (u.shape[0], 1), 0)
    u1 = jnp.where(rows >= 1, pltpu.roll(u, 1, 0), prev7)
    u2 = jnp.where(rows >= 2, pltpu.roll(u, 2, 0), jnp.where(rows == 0, prev6, prev7))
    return u2, u1


def _conv_taps_ahead(d, next0, next1):
    tm = d.shape[0]
    rows = lax.broadcasted_iota(jnp.int32, (tm, 1), 0)
    d1 = jnp.where(rows < tm - 1, pltpu.roll(d, tm - 1, 0), next0)
    d2 = jnp.where(rows < tm - 2, pltpu.roll(d, tm - 2, 0), jnp.where(rows == tm - 2, next0, next1))
    return d1, d2


def _ffn_up_fwd(h, w_up, conv_w, conv_b, *, name, tm=512, ride=None):
    T, D = h.shape
    ns, _, fs = w_up.shape
    nh = ns // 2
    tm = min(tm, T)

    def body(h_ref, wg_ref, wu_ref, cwg_ref, cwu_ref, cbg_ref, cbu_ref, a_ref, u_ref, y_ref, cg_ref, cu_ref):
        i = pl.program_id(1)

        @pl.when(i == 0)
        def _():
            cg_ref[...] = jnp.zeros_like(cg_ref)
            cu_ref[...] = jnp.zeros_like(cu_ref)

        hv = h_ref[...]
        ys = []
        for idx, (w_ref, cw_ref, cb_ref, carry) in enumerate(
                ((wg_ref, cwg_ref, cbg_ref, cg_ref), (wu_ref, cwu_ref, cbu_ref, cu_ref))):
            u = _dot(hv, w_ref[...], NN)
            u_ref[idx] = u.astype(u_ref.dtype)
            u2, u1 = _conv_taps(u, carry[6:7, :], carry[7:8, :])
            y = cb_ref[...] + u2 * cw_ref[0:1, :]
            y = y + u1 * cw_ref[1:2, :]
            y = y + u * cw_ref[2:3, :]
            y_ref[idx] = y.astype(y_ref.dtype)
            ys.append(y)
            carry[...] = u[tm - 8:tm, :]
        yg, yu = ys
        a_ref[...] = ((yg * _sigmoid(yg)) * yu).astype(a_ref.dtype)

    shard = lambda r, off: pl.BlockSpec((None, r, fs), lambda j, i: (j + off, 0, 0))
    return _call(
        body, name=name, grid=(nh, T // tm),
        in_specs=[pl.BlockSpec((tm, D), lambda j, i: (i, 0)), shard(D, 0), shard(D, nh),
                  shard(3, 0), shard(3, nh), shard(1, 0), shard(1, nh)],
        out_specs=[pl.BlockSpec((None, tm, fs), lambda j, i: (j, i, 0)),
                   pl.BlockSpec((2, None, tm, fs), lambda j, i: (0, j, i, 0)),
                   pl.BlockSpec((2, None, tm, fs), lambda j, i: (0, j, i, 0))],
        out_shape=[jax.ShapeDtypeStruct((nh, T, fs), BF16), jax.ShapeDtypeStruct((2, nh, T, fs), BF16),
                   jax.ShapeDtypeStruct((2, nh, T, fs), BF16)],
        scratch_shapes=[pltpu.VMEM((8, fs), F32), pltpu.VMEM((8, fs), F32)],
        sem=("parallel", "arbitrary"), ride=ride, args=(h, w_up, w_up, conv_w, conv_w, conv_b, conv_b))


def _ffn_act_bwd(dxo, w_down, u, y, conv_w, *, name, tm=512):
    T, D = dxo.shape
    _, nh, _, fs = u.shape
    tm = min(tm, T)
    nt = T // tm

    def body(dx_ref, wd_ref, u_ref, y_ref, cwg_ref, cwu_ref, du_ref, dcw_ref, dcb_ref, cg_ref, cu_ref):
        i = pl.program_id(1)

        @pl.when(i == 0)
        def _():
            cg_ref[...] = jnp.zeros_like(cg_ref)
            cu_ref[...] = jnp.zeros_like(cu_ref)
            dcw_ref[...] = jnp.zeros_like(dcw_ref)
            dcb_ref[...] = jnp.zeros_like(dcb_ref)

        da = _dot(dx_ref[...].astype(BF16), wd_ref[...], NT)
        yg, yu = y_ref[0].astype(F32), y_ref[1].astype(F32)
        sg = _sigmoid(yg)
        dys = (da * yu * (sg * (1.0 + yg * (1.0 - sg))), da * (yg * sg))
        for idx, (cw_ref, carry) in enumerate(((cwg_ref, cg_ref), (cwu_ref, cu_ref))):
            dy = dys[idx]
            uv = u_ref[idx].astype(F32)
            d1, d2 = _conv_taps_ahead(dy, carry[0:1, :], carry[1:2, :])
            dcb_ref[idx] += jnp.sum(dy, axis=0, keepdims=True)
            dcw_ref[idx, 0:1, :] += jnp.sum(d2 * uv, axis=0, keepdims=True)
            dcw_ref[idx, 1:2, :] += jnp.sum(d1 * uv, axis=0, keepdims=True)
            dcw_ref[idx, 2:3, :] += jnp.sum(dy * uv, axis=0, keepdims=True)
            du = dy * cw_ref[2:3, :] + d1 * cw_ref[1:2, :] + d2 * cw_ref[0:1, :]
            du_ref[idx] = du.astype(du_ref.dtype)
            carry[...] = dy[0:8, :]

    rt = lambda i: nt - 1 - i
    shard = lambda r, off: pl.BlockSpec((None, r, fs), lambda j, i: (j + off, 0, 0))
    tile = pl.BlockSpec((2, None, tm, fs), lambda j, i: (0, j, rt(i), 0))
    return pl.pallas_call(
        body, name=name, grid=(nh, nt),
        in_specs=[pl.BlockSpec((tm, D), lambda j, i: (rt(i), 0)), pl.BlockSpec((fs, D), lambda j, i: (j, 0)),
                  tile, tile, shard(3, 0), shard(3, nh)],
        out_specs=[tile, pl.BlockSpec((2, None, 3, fs), lambda j, i: (0, j, 0, 0)),
                   pl.BlockSpec((2, None, 1, fs), lambda j, i: (0, j, 0, 0))],
        out_shape=[jax.ShapeDtypeStruct((2, nh, T, fs), BF16), jax.ShapeDtypeStruct((2, nh, 3, fs), F32),
                   jax.ShapeDtypeStruct((2, nh, 1, fs), F32)],
        scratch_shapes=[pltpu.VMEM((8, fs), F32), pltpu.VMEM((8, fs), F32)],
        compiler_params=_params(("parallel", "arbitrary")),
    )(dxo, w_down, u, y, conv_w, conv_w)


def _pad_cols(w, n):
    return jnp.pad(w, [(0, 0)] * (w.ndim - 1) + [(0, n - w.shape[-1])])


def _q_up_padded(w):
    R = w.shape[0]
    H = w.shape[1] // QK_HEAD
    return _pad_cols(w.reshape(R, H, QK_HEAD), QK_PAD).reshape(R, H * QK_PAD)


def _q_up_unpadded(w):
    R = w.shape[0]
    H = w.shape[1] // QK_PAD
    return w.reshape(R, H, QK_PAD)[:, :, :QK_HEAD].reshape(R, H * QK_HEAD)


def _xchg_copies(src_refs, out_refs, kinds, send_sems, recv_sems, local_sems):
    x, y, c = lax.axis_index("x"), lax.axis_index("y"), lax.axis_index("c")
    me = 4 * x + 2 * y + c
    copies = []
    for b, kind in enumerate(kinds):
        gather = kind == "gather"
        own = src_refs[b] if gather else src_refs[b].at[me]
        copies.append(pltpu.make_async_copy(own, out_refs[b].at[me], local_sems.at[b]))
        for kk in range(1, N_DEV):
            px = 1 - x if kk & 4 else x
            py = 1 - y if kk & 2 else y
            pc = 1 - c if kk & 1 else c
            peer = 4 * px + 2 * py + pc
            src = src_refs[b] if gather else src_refs[b].at[peer]
            copies.append(pltpu.make_async_remote_copy(
                src_ref=src, dst_ref=out_refs[b].at[me],
                send_sem=send_sems.at[b * (N_DEV - 1) + kk - 1],
                recv_sem=recv_sems.at[b * (N_DEV - 1) + kk - 1],
                device_id=(px, py, pc), device_id_type=pl.DeviceIdType.MESH))
    return copies


def _xchg_out_shapes(srcs, kinds):
    return [jax.ShapeDtypeStruct((N_DEV,) + s.shape if kind == "gather" else s.shape, s.dtype)
            for s, kind in zip(srcs, kinds)]


def _xchg_scratch(n):
    return [pltpu.SemaphoreType.DMA((n * (N_DEV - 1),)), pltpu.SemaphoreType.DMA((n * (N_DEV - 1),)),
            pltpu.SemaphoreType.DMA((n,))]


def _exchange(srcs, kinds, *, name):
    n = len(srcs)

    def body(*refs):
        copies = _xchg_copies(refs[:n], refs[n:2 * n], kinds, *refs[2 * n:])
        for cp in copies:
            cp.start()
        for cp in copies:
            cp.wait()

    hbm = pl.BlockSpec(memory_space=pl.ANY)
    return pl.pallas_call(
        body, name=name, in_specs=[hbm] * n, out_specs=[hbm] * n, out_shape=_xchg_out_shapes(srcs, kinds),
        scratch_shapes=_xchg_scratch(n),
    )(*srcs)


def _call(body, *, name, grid, in_specs, out_specs, out_shape, scratch_shapes, args, sem, ride=None):
    if ride is None:
        outs = pl.pallas_call(body, name=name, grid=grid, in_specs=in_specs, out_specs=out_specs,
                              out_shape=out_shape, scratch_shapes=scratch_shapes,
                              compiler_params=_params(sem))(*args)
        return list(outs), []
    srcs, kinds = ride
    n_in, n_out, n_sc, nx = len(in_specs), len(out_specs), len(scratch_shapes), len(srcs)

    def wrapped(*refs):
        ins, xs = refs[:n_in], refs[n_in:n_in + nx]
        o0 = n_in + nx
        outs, xo = refs[o0:o0 + n_out], refs[o0 + n_out:o0 + n_out + nx]
        s0 = o0 + n_out + nx
        sc, sems = refs[s0:s0 + n_sc], refs[s0 + n_sc:]
        first = functools.reduce(jnp.logical_and, [pl.program_id(d) == 0 for d in range(len(grid))])
        last = functools.reduce(jnp.logical_and, [pl.program_id(d) == grid[d] - 1 for d in range(len(grid))])

        @pl.when(first)
        def _():
            for cp in _xchg_copies(xs, xo, kinds, *sems):
                cp.start()

        body(*ins, *outs, *sc)

        @pl.when(last)
        def _():
            for cp in _xchg_copies(xs, xo, kinds, *sems):
                cp.wait()

    hbm = pl.BlockSpec(memory_space=pl.ANY)
    outs = pl.pallas_call(
        wrapped, name=name, grid=grid, in_specs=list(in_specs) + [hbm] * nx,
        out_specs=list(out_specs) + [hbm] * nx, out_shape=list(out_shape) + _xchg_out_shapes(srcs, kinds),
        scratch_shapes=list(scratch_shapes) + _xchg_scratch(nx),
        compiler_params=_params(("arbitrary",) * len(grid)),
    )(*args, *srcs)
    return list(outs[:n_out]), list(outs[n_out:])


def _sum_slots(parts, *, name):
    _, Rr, C = parts.shape

    def body(p_ref, o_ref):
        acc = p_ref[0].astype(F32)
        for d in range(1, N_DEV):
            acc = acc + p_ref[d].astype(F32)
        o_ref[...] = acc

    return pl.pallas_call(
        body, name=name, grid=(1,),
        in_specs=[pl.BlockSpec((N_DEV, Rr, C), lambda i: (0, 0, 0))],
        out_specs=pl.BlockSpec((Rr, C), lambda i: (0, 0)),
        out_shape=jax.ShapeDtypeStruct((Rr, C), F32),
        compiler_params=_params(("arbitrary",)),
    )(parts)


def _row_tile(rows, cap=512):
    if rows <= cap:
        return rows
    d = (cap // 8) * 8
    while d >= 8:
        if rows % d == 0:
            return d
        d -= 8
    raise ValueError(f"no row tile for {rows}")


def _adamw(parts, w, m, v, *, name):
    S, Rr, C = parts.shape
    tr = _row_tile(Rr)
    c1 = 1.0 - ADAM_B1 ** ADAM_STEP
    c2 = 1.0 - ADAM_B2 ** ADAM_STEP

    def body(p_ref, w_ref, m_ref, v_ref, g_ref, d_ref, nm_ref, nv_ref):
        g = p_ref[0].astype(F32)
        for d in range(1, S):
            g = g + p_ref[d].astype(F32)
        mm = ADAM_B1 * m_ref[...] + (1.0 - ADAM_B1) * g
        vv = ADAM_B2 * v_ref[...] + (1.0 - ADAM_B2) * (g * g)
        m_hat = mm / c1
        v_hat = vv / c2
        g_ref[...] = g
        d_ref[...] = -ADAM_LR * (m_hat / (jnp.sqrt(v_hat) + ADAM_EPS) + ADAM_WD * w_ref[...])
        nm_ref[...] = mm
        nv_ref[...] = vv

    spec = pl.BlockSpec((tr, C), lambda i: (i, 0))
    shape = jax.ShapeDtypeStruct((Rr, C), F32)
    return pl.pallas_call(
        body, name=name, grid=(Rr // tr,),
        in_specs=[pl.BlockSpec((S, tr, C), lambda i: (0, i, 0)), spec, spec, spec],
        out_specs=[spec] * 4, out_shape=[shape] * 4,
        compiler_params=_params(("parallel",)),
    )(parts, w, m, v)


def _pack(arrs, dtype, row_mult):
    flat = jnp.concatenate([a.reshape(-1).astype(dtype) for a in arrs])
    per = row_mult * PACK_COLS
    total = -(-flat.shape[0] // per) * per
    return jnp.pad(flat, (0, total - flat.shape[0])).reshape(total // PACK_COLS, PACK_COLS)


def _unpack(packed, shapes, lead=()):
    flat = packed.reshape(lead + (-1,))
    out, off = [], 0
    for shp in shapes:
        n = 1
        for d in shp:
            n *= d
        out.append(flat[..., off:off + n].reshape(lead + tuple(shp)))
        off += n
    return out


HGRN_W = ("hgrn_w_in", "hgrn_w_out")
MLA_W = ("mla_w_in", "mla_w_q_up", "mla_w_kv_up", "mla_w_out")
FFN_W = ("ffn_w_up", "ffn_w_down")
BIG = HGRN_W + MLA_W + FFN_W
SMALL_SHARDED = {"ffn_conv_w": 2, "mla_q_a_norm": 1, "mla_kv_a_norm": 1}
REPLICATED = ["norm_mix", "norm_ffn", "hgrn_lower_bounds", "hgrn_out_norm", "mla_q_norm", "mla_k_norm",
              "ffn_conv_b"]
WEIGHTS = ["norm_mix", "norm_ffn", "hgrn_w_in", "hgrn_lower_bounds", "hgrn_out_norm", "hgrn_w_out", "mla_w_in",
           "mla_q_a_norm", "mla_w_q_up", "mla_kv_a_norm", "mla_w_kv_up", "mla_q_norm", "mla_k_norm", "mla_w_out",
           "ffn_w_up", "ffn_conv_w", "ffn_conv_b", "ffn_w_down"]


def _shards_to_cols(g):
    return g.transpose(1, 0, 2).reshape(g.shape[1], N_DEV * g.shape[2])


def _cols_to_shards(w):
    R = w.shape[0]
    return w.reshape(R, N_DEV, w.shape[1] // N_DEV).transpose(1, 0, 2)


def kernel(x, positions, norm_mix, norm_ffn, hgrn_w_in, hgrn_lower_bounds, hgrn_out_norm, hgrn_w_out, mla_w_in, mla_q_a_norm, mla_w_q_up, mla_kv_a_norm, mla_w_kv_up, mla_q_norm, mla_k_norm, mla_w_out, ffn_w_up, ffn_conv_w, ffn_conv_b, ffn_w_down, loss_target, m_norm_mix, m_norm_ffn, m_hgrn_w_in, m_hgrn_lower_bounds, m_hgrn_out_norm, m_hgrn_w_out, m_mla_w_in, m_mla_q_a_norm, m_mla_w_q_up, m_mla_kv_a_norm, m_mla_w_kv_up, m_mla_q_norm, m_mla_k_norm, m_mla_w_out, m_ffn_w_up, m_ffn_conv_w, m_ffn_conv_b, m_ffn_w_down, v_norm_mix, v_norm_ffn, v_hgrn_w_in, v_hgrn_lower_bounds, v_hgrn_out_norm, v_hgrn_w_out, v_mla_w_in, v_mla_q_a_norm, v_mla_w_q_up, v_mla_kv_a_norm, v_mla_w_kv_up, v_mla_q_norm, v_mla_k_norm, v_mla_w_out, v_ffn_w_up, v_ffn_conv_w, v_ffn_conv_b, v_ffn_w_down):
    local = dict(norm_mix=norm_mix, norm_ffn=norm_ffn, hgrn_w_in=hgrn_w_in, hgrn_lower_bounds=hgrn_lower_bounds,
                 hgrn_out_norm=hgrn_out_norm, hgrn_w_out=hgrn_w_out, mla_w_in=mla_w_in, mla_q_a_norm=mla_q_a_norm,
                 mla_w_q_up=mla_w_q_up, mla_kv_a_norm=mla_kv_a_norm, mla_w_kv_up=mla_w_kv_up, mla_q_norm=mla_q_norm,
                 mla_k_norm=mla_k_norm, mla_w_out=mla_w_out, ffn_w_up=ffn_w_up, ffn_conv_w=ffn_conv_w,
                 ffn_conv_b=ffn_conv_b, ffn_w_down=ffn_w_down)
    mom_m = dict(norm_mix=m_norm_mix, norm_ffn=m_norm_ffn, hgrn_w_in=m_hgrn_w_in,
                 hgrn_lower_bounds=m_hgrn_lower_bounds, hgrn_out_norm=m_hgrn_out_norm, hgrn_w_out=m_hgrn_w_out,
                 mla_w_in=m_mla_w_in, mla_q_a_norm=m_mla_q_a_norm, mla_w_q_up=m_mla_w_q_up,
                 mla_kv_a_norm=m_mla_kv_a_norm, mla_w_kv_up=m_mla_w_kv_up, mla_q_norm=m_mla_q_norm,
                 mla_k_norm=m_mla_k_norm, mla_w_out=m_mla_w_out, ffn_w_up=m_ffn_w_up, ffn_conv_w=m_ffn_conv_w,
                 ffn_conv_b=m_ffn_conv_b, ffn_w_down=m_ffn_w_down)
    mom_v = dict(norm_mix=v_norm_mix, norm_ffn=v_norm_ffn, hgrn_w_in=v_hgrn_w_in,
                 hgrn_lower_bounds=v_hgrn_lower_bounds, hgrn_out_norm=v_hgrn_out_norm, hgrn_w_out=v_hgrn_w_out,
                 mla_w_in=v_mla_w_in, mla_q_a_norm=v_mla_q_a_norm, mla_w_q_up=v_mla_w_q_up,
                 mla_kv_a_norm=v_mla_kv_a_norm, mla_w_kv_up=v_mla_w_kv_up, mla_q_norm=v_mla_q_norm,
                 mla_k_norm=v_mla_k_norm, mla_w_out=v_mla_w_out, ffn_w_up=v_ffn_w_up, ffn_conv_w=v_ffn_conv_w,
                 ffn_conv_b=v_ffn_conv_b, ffn_w_down=v_ffn_w_down)
    me = 4 * lax.axis_index("x") + 2 * lax.axis_index("y") + lax.axis_index("c")
    x, positions, target = x[0], positions[0], loss_target[0]
    T, D = x.shape
    depth = norm_mix.shape[0]
    R = mla_w_q_up.shape[1]
    cw = 2 * R + LANES
    small_names = list(SMALL_SHARDED)

    def block_of(kind, l):
        names = {"hgrn": HGRN_W, "mla": MLA_W, "ffn": FFN_W}[kind]
        idx = l if kind == "ffn" else l // 2
        return [(n, idx) for n in names]

    def riders(l):
        nxt = l + 1 < depth
        if l % 2 == 1:
            return block_of("ffn", l) + (block_of("hgrn", l + 1) + block_of("ffn", l + 1) if nxt else [])
        return (block_of("mla", l + 1) if nxt else []) + (block_of("ffn", 0) if l == 0 else [])

    gathered = {}

    def gather_ride(host, l):
        if host == "mixer" and l % 2 == 0:
            keys = [("ffn_w_up", l)]
        elif host == "mixer":
            keys = block_of("ffn", l)
            if l + 1 < depth:
                keys += block_of("hgrn", l + 1) + [("ffn_w_down", l + 1)]
            if l + 2 < depth:
                keys += block_of("mla", l + 2)
        elif l == 0:
            keys = [("ffn_w_down", 0)] + (block_of("mla", 1) if depth > 1 else [])
        else:
            keys = []
        if not keys:
            return keys, None
        return keys, ([local[n][i].astype(BF16) for n, i in keys], ["gather"] * len(keys))

    def take_gathered(keys, arrs):
        for key, a in zip(keys, arrs):
            gathered[key] = a

    keys0 = block_of("hgrn", 0)
    small_local = _pack([local[n] for n in small_names], F32, 8)
    got = _exchange([local[n][i].astype(BF16) for n, i in keys0] + [small_local],
                    ["gather"] * (len(keys0) + 1), name="gather_first")
    take_gathered(keys0, got[:-1])
    small_all = _unpack(got[-1], [local[n].shape for n in small_names], lead=(N_DEV,))
    conv_w_all = small_all[0].transpose(1, 0, 2, 3)
    qa_all = small_all[1].transpose(1, 0, 2).reshape(-1, R)
    kva_all = small_all[2].transpose(1, 0, 2).reshape(-1, R)
    fs = conv_w_all.shape[-1]
    conv_b_s = ffn_conv_b.reshape(depth, N_DEV, 1, fs)

    ct, s1, s2 = _rope_tables(positions)
    lb_soft = jax.nn.softmax(hgrn_lower_bounds.astype(F32), axis=0)
    lower_bounds = jnp.cumsum(lb_soft, axis=0) - lb_soft[0:1]

    def mla_views(j):
        w_in = _pad_cols(gathered["mla_w_in", j].reshape(D, -1), cw)
        wq = _q_up_padded(_shards_to_cols(gathered["mla_w_q_up", j]))
        wkv = _shards_to_cols(gathered["mla_w_kv_up", j])
        gq = _pad_cols(mla_q_norm[j].reshape(1, QK_HEAD), QK_PAD)
        gk = _pad_cols(mla_k_norm[j].reshape(1, QK_HEAD), QK_PAD)
        return w_in, wq, wkv, gq, gk

    saved = []
    h = _rmsnorm_fwd(x, norm_mix[0], name="norm_mix_fwd_0")
    for layer in range(depth):
        j = layer // 2
        s = {"x_in": x}
        s["h_mix"] = h
        keys, ride = gather_ride("mixer", layer)
        if layer % 2 == 0:
            p = _mm(h, gathered["hgrn_w_in", j], b_fmt="knb", bm=1024, name=f"hgrn_in_{layer}")
            (og, o, s0), got = _hgrn_fwd(p, lower_bounds[j], hgrn_out_norm[j], name=f"hgrn_fwd_{layer}", ride=ride)
            s.update(p=p, og=og, o=o, s0=s0)
            take_gathered(keys, got)
            x, h = _mm(og, gathered["hgrn_w_out", j].reshape(D, D), res=x, norm_out=norm_ffn[layer],
                       name=f"hgrn_out_{layer}")
        else:
            w_in, wq, wkv, gq, gk = mla_views(j)
            c = _mm(h, w_in, bm=1024, name=f"mla_in_{layer}")
            q, k, v = _mla_prep_fwd(c, wq, wkv, qa_all[j], kva_all[j], gq, gk, ct, s1, s2,
                                    name=f"mla_prep_fwd_{layer}")
            (ot, lse), got = _attn_fwd(q, k, v, name=f"attn_fwd_{layer}", ride=ride)
            s.update(c=c, q=q, k=k, v=v, ot=ot, lse=lse)
            take_gathered(keys, got)
            x, h = _mm(ot, gathered["mla_w_out", j].reshape(D, D), a_fmt="kmb", res=x, norm_out=norm_ffn[layer],
                       name=f"mla_out_{layer}")
        s["x_mid"] = x
        s["h_ffn"] = h
        keys, ride = gather_ride("ffn_up", layer)
        (a, u, y), got = _ffn_up_fwd(h, gathered["ffn_w_up", layer], conv_w_all[layer], conv_b_s[layer],
                                     name=f"ffn_up_{layer}", ride=ride)
        take_gathered(keys, got)
        s.update(a=a, u=u, y=y)
        w_down = gathered["ffn_w_down", layer].reshape(-1, D)
        if layer + 1 < depth:
            x, h = _mm(a, w_down, a_fmt="mkb", res=x, norm_out=norm_mix[layer + 1], bm=1024, bk=fs,
                       name=f"ffn_down_{layer}")
        else:
            x = _mm(a, w_down, a_fmt="mkb", res=x, bm=1024, bk=fs, name=f"ffn_down_{layer}")
        saved.append(s)

    dx, loss_part = _loss_head(x, target, name="loss_head")

    parts = {}
    received = {}
    g_small = {n: [None] * local[n].shape[0] for n in REPLICATED + small_names}

    def scatter_ride(l):
        keys = riders(l)
        return keys, (([parts[key] for key in keys], ["scatter"] * len(keys)) if keys else None)

    def take_received(keys, arrs):
        for key, a in zip(keys, arrs):
            received[key] = a

    for layer in reversed(range(depth)):
        j = layer // 2
        s = saved[layer]
        parts["ffn_w_down", layer] = _mm(s["a"], dx, a_fmt="kmb", out_dtype=BF16, name=f"ffn_down_dw_{layer}"
                                         ).reshape(N_DEV, -1, D)
        du, dcw, dcb = _ffn_act_bwd(dx, gathered["ffn_w_down", layer].reshape(-1, D), s["u"], s["y"],
                                    conv_w_all[layer], name=f"ffn_act_bwd_{layer}")
        g_small["ffn_conv_w"][layer] = dcw.reshape(N_DEV, 3, fs)
        g_small["ffn_conv_b"][layer] = dcb.reshape(N_DEV * fs)
        du8 = du.reshape(N_DEV, T, fs)
        parts["ffn_w_up", layer] = _mm(s["h_ffn"], du8, a_fmt="km", b_fmt="knb", out_fmt="mnb", out_dtype=BF16,
                                       bm=1024, name=f"ffn_up_dw_{layer}")
        dx, dgain = _mm(du8, gathered["ffn_w_up", layer], a_fmt="mkb", b_fmt="nkb", bm=1024, bk=fs,
                        norm_bwd=(s["x_mid"], norm_ffn[layer], dx), name=f"ffn_up_dh_{layer}")
        g_small["norm_ffn"][layer] = dgain.reshape(D)
        keys, ride = scatter_ride(layer)
        if layer % 2 == 0:
            w_out = gathered["hgrn_w_out", j].reshape(D, D)
            parts["hgrn_w_out", j] = _mm(s["og"], dx, a_fmt="km", out_dtype=BF16, bm=1024,
                                         name=f"hgrn_out_dw_{layer}").reshape(N_DEV, -1, D)
            dog = _mm(dx, w_out, b_fmt="nk", name=f"hgrn_out_dx_{layer}")
            (dp, dlb, dgn), got = _hgrn_bwd(s["p"], lower_bounds[j], hgrn_out_norm[j], s["s0"], s["o"], dog,
                                            name=f"hgrn_bwd_{layer}", ride=ride)
            take_received(keys, got)
            g_small["hgrn_lower_bounds"][j] = dlb.reshape(D)
            g_small["hgrn_out_norm"][j] = dgn.reshape(HEAD)
            w_in_s = gathered["hgrn_w_in", j]
            parts["hgrn_w_in", j] = _mm(s["h_mix"], dp, a_fmt="km", out_fmt="mnb", out_dtype=BF16, bm=1024,
                                        bn=w_in_s.shape[2], name=f"hgrn_in_dw_{layer}")
            dx, dgain = _mm(dp, w_in_s, b_fmt="nkb", bm=1024, norm_bwd=(s["x_in"], norm_mix[layer], dx),
                            name=f"hgrn_in_dx_{layer}")
        else:
            w_in, wq, wkv, gq, gk = mla_views(j)
            w_out = gathered["mla_w_out", j].reshape(D, D)
            tb = s["ot"].shape[2]
            parts["mla_w_out", j] = _mm(s["ot"], dx, a_fmt="mkb", bk=tb, out_dtype=BF16, bm=1024,
                                        name=f"mla_out_dw_{layer}").reshape(N_DEV, -1, D)
            dot_ = _mm(w_out, dx, b_fmt="nk", out_fmt="mnb", out_dtype=BF16, bm=D, bn=tb,
                       name=f"mla_out_dx_{layer}")
            (dq, dk, dv), got = _attn_bwd(s["q"], s["k"], s["v"], s["ot"], s["lse"], dot_,
                                          name=f"attn_bwd_{layer}", ride=ride)
            take_received(keys, got)
            dc, dwq, dwkv, dgaq, dgakv, dgq, dgk = _mla_prep_bwd(
                s["c"], wq, wkv, qa_all[j], kva_all[j], gq, gk, ct, s1, s2, dq, dk, dv,
                name=f"mla_prep_bwd_{layer}")
            parts["mla_w_q_up", j] = _cols_to_shards(_q_up_unpadded(dwq)).astype(BF16)
            parts["mla_w_kv_up", j] = _cols_to_shards(dwkv).astype(BF16)
            g_small["mla_q_a_norm"][j] = dgaq.reshape(R)
            g_small["mla_kv_a_norm"][j] = dgakv.reshape(R)
            g_small["mla_q_norm"][j] = dgq[0, :QK_HEAD]
            g_small["mla_k_norm"][j] = dgk[0, :QK_HEAD]
            win_cols = mla_w_in.shape[2]
            dw_in = _mm(s["h_mix"], dc, a_fmt="km", bm=1024, name=f"mla_in_dw_{layer}")
            parts["mla_w_in", j] = dw_in[:, :win_cols].astype(BF16).reshape(N_DEV, -1, win_cols)
            dx, dgain = _mm(dc, w_in, b_fmt="nk", norm_bwd=(s["x_in"], norm_mix[layer], dx),
                            name=f"mla_in_dx_{layer}")
        g_small["norm_mix"][layer] = dgain.reshape(D)
    grad_x = dx

    dlb_eff = jnp.stack(g_small["hgrn_lower_bounds"])
    dsoft = jnp.cumsum(dlb_eff[::-1], axis=0)[::-1]
    dsoft = dsoft.at[0].add(-jnp.sum(dlb_eff, axis=0))
    g_lb = lb_soft * (dsoft - jnp.sum(dsoft * lb_soft, axis=0, keepdims=True))
    small_grads = {n: (g_lb if n == "hgrn_lower_bounds" else jnp.stack(g_small[n])) for n in g_small}

    small_grad_names = REPLICATED + small_names
    small_part = _pack([small_grads[n] for n in small_grad_names] + [loss_part], F32, 8)
    got = _exchange([parts[key] for key in keys0] + [small_part], ["scatter"] * len(keys0) + ["gather"],
                    name="exchange_last")
    take_received(keys0, got[:-1])
    small_recv = got[-1]

    out = {}
    for n in BIG:
        layers = local[n].shape[0]
        shard = local[n].shape[1:]
        p2 = jnp.concatenate([received[n, i].reshape(N_DEV, -1, shard[-1]) for i in range(layers)], axis=1)
        flat = lambda a: a.reshape(-1, shard[-1])
        res = _adamw(p2, flat(local[n]), flat(mom_m[n]), flat(mom_v[n]), name=f"adamw_{n}")
        for kind, a in zip(("grad", "delta", "new_m", "new_v"), res):
            out[kind, n] = a.reshape(local[n].shape)
    small_sum = _sum_slots(small_recv, name="sum_small")
    small_full = _unpack(small_sum, [small_grads[n].shape for n in small_grad_names] + [(1, LANES)])
    loss = small_full[-1][0, 0]
    g_mine = {}
    for n, a in zip(small_grad_names, small_full[:-1]):
        if n == "ffn_conv_w":
            a = lax.dynamic_index_in_dim(a, me, axis=1, keepdims=False)
        elif n in SMALL_SHARDED:
            size = local[n].shape[1]
            a = lax.dynamic_slice_in_dim(a, me * size, size, axis=1)
        g_mine[n] = a
    small_shapes = [local[n].shape for n in small_grad_names]
    res = _adamw(_pack([g_mine[n] for n in small_grad_names], F32, 8)[None],
                 _pack([local[n] for n in small_grad_names], F32, 8),
                 _pack([mom_m[n] for n in small_grad_names], F32, 8),
                 _pack([mom_v[n] for n in small_grad_names], F32, 8), name="adamw_small")
    for kind, packed in zip(("grad", "delta", "new_m", "new_v"), res):
        for n, a in zip(small_grad_names, _unpack(packed, small_shapes)):
            out[kind, n] = a

    outs = [loss, grad_x[None]]
    for kind in ("grad", "delta", "new_m", "new_v"):
        outs += [out[kind, n] for n in WEIGHTS]
    return tuple(outs)
```

```python
import functools

import jax
import jax.numpy as jnp
from jax import lax
from jax.experimental import pallas as pl
from jax.experimental.pallas import tpu as pltpu

F32 = jnp.float32
BF16 = jnp.bfloat16

RMS_EPS = 1e-6
ROPE_THETA = 10000.0
HEAD = 128
ROPE = 64
QK_HEAD = HEAD + ROPE
QK_PAD = 256
CHUNK = 64
SUB = 16
EXP_CLAMP = 60.0
HGRN_CHUNKS_PER_STEP = 8
HGRN_BWD_CHUNKS_PER_STEP = 4

ADAM_LR = 0.001
ADAM_B1 = 0.9
ADAM_B2 = 0.999
ADAM_EPS = 1e-08
ADAM_WD = 0.01
ADAM_STEP = 10

N_DEV = 8
LANES = 128
PACK_COLS = 1024
V7X_VMEM_LIMIT = 56 * 1024 * 1024

HI = lax.Precision.HIGHEST


def _params(sem):
    return pltpu.CompilerParams(dimension_semantics=sem, vmem_limit_bytes=V7X_VMEM_LIMIT)


def _blk(n, cap):
    if n <= cap:
        return n
    d = (cap // LANES) * LANES
    while d >= LANES:
        if n % d == 0:
            return d
        d -= LANES
    raise ValueError(f"no lane-aligned block for {n} under {cap}")


def _sigmoid(x):
    return jax.nn.sigmoid(x)


def _dot(a, b, dims, precision=None):
    return lax.dot_general(a, b, (dims, ((), ())), preferred_element_type=F32, precision=precision)


NN = ((1,), (0,))
NT = ((1,), (1,))
TN = ((0,), (0,))


def _mm(a, b, *, a_fmt="mk", b_fmt="kn", out_fmt="mn", res=None, norm_out=None, norm_bwd=None, out_dtype=F32,
        bm=512, bn=1024, bk=1024, name, ride=None):
    if a_fmt == "mk":
        M, K = a.shape
    elif a_fmt == "km":
        K, M = a.shape
    elif a_fmt == "kmb":
        nb, K, B = a.shape
        M = nb * B
    else:
        nb, M, B = a.shape
        K = nb * B
    if b_fmt == "kn":
        Kb, N = b.shape
    elif b_fmt == "nk":
        N, Kb = b.shape
    elif b_fmt == "knb":
        nbb, Kb, Bb = b.shape
        N = nbb * Bb
    else:
        nbb, N, Bb = b.shape
        Kb = nbb * Bb
    assert K == Kb, (a.shape, b.shape, a_fmt, b_fmt)
    bm = B if a_fmt == "kmb" else _blk(M, bm)
    bn = Bb if b_fmt == "knb" else _blk(N, bn)
    if a_fmt == "mkb" and b_fmt == "nkb":
        assert B == Bb
    bk = _blk(B, bk) if a_fmt == "mkb" else (_blk(Bb, bk) if b_fmt == "nkb" else _blk(K, bk))
    nm, nn, nk = M // bm, N // bn, K // bk

    if a_fmt == "mk":
        a_spec = pl.BlockSpec((bm, bk), lambda i, j, k: (i, k))
        a_dim = 1
    elif a_fmt == "km":
        a_spec = pl.BlockSpec((bk, bm), lambda i, j, k: (k, i))
        a_dim = 0
    elif a_fmt == "kmb":
        a_spec = pl.BlockSpec((None, bk, bm), lambda i, j, k: (i, k, 0))
        a_dim = 0
    else:
        per = B // bk
        a_spec = pl.BlockSpec((None, bm, bk), lambda i, j, k: (k // per, i, k % per))
        a_dim = 1
    if b_fmt == "kn":
        b_spec = pl.BlockSpec((bk, bn), lambda i, j, k: (k, j))
        b_dim = 0
    elif b_fmt == "nk":
        b_spec = pl.BlockSpec((bn, bk), lambda i, j, k: (j, k))
        b_dim = 1
    elif b_fmt == "knb":
        b_spec = pl.BlockSpec((None, bk, bn), lambda i, j, k: (j, k, 0))
        b_dim = 0
    else:
        perb = Bb // bk
        b_spec = pl.BlockSpec((None, bn, bk), lambda i, j, k: (k // perb, j, k % perb))
        b_dim = 1
    if out_fmt == "mn":
        o_spec = pl.BlockSpec((bm, bn), lambda i, j, k: (i, j))
        o_shape = jax.ShapeDtypeStruct((M, N), out_dtype)
    else:
        o_spec = pl.BlockSpec((None, bm, bn), lambda i, j, k: (j, i, 0))
        o_shape = jax.ShapeDtypeStruct((nn, M, bn), out_dtype)
    in_specs = [a_spec, b_spec]
    args = [a, b]
    row_tile = pl.BlockSpec((bm, bn), lambda i, j, k: (i, j))
    row_vec = pl.BlockSpec((1, bn), lambda i, j, k: (0, j))
    if res is not None:
        assert out_fmt == "mn"
        in_specs.append(row_tile)
        args.append(res)
    out_specs, out_shapes = [o_spec], [o_shape]
    if norm_out is not None:
        assert nn == 1 and out_fmt == "mn"
        in_specs.append(row_vec)
        args.append(norm_out.reshape(1, N))
        out_specs.append(row_tile)
        out_shapes.append(jax.ShapeDtypeStruct((M, N), BF16))
    if norm_bwd is not None:
        assert nn == 1 and out_fmt == "mn" and res is None and norm_out is None
        xin, gain, dres = norm_bwd
        in_specs += [row_tile, row_vec, row_tile]
        args += [xin, gain.reshape(1, N), dres]
        out_specs.append(row_vec)
        out_shapes.append(jax.ShapeDtypeStruct((1, N), F32))
    dims = ((a_dim,), (b_dim,))
    has_res = res is not None
    n_in = len(in_specs)

    def body(*refs):
        a_ref, b_ref = refs[0], refs[1]
        extra_in = list(refs[2:n_in])
        o_ref = refs[n_in]
        part = _dot(a_ref[...].astype(BF16), b_ref[...].astype(BF16), dims)

        def finish(out):
            if has_res:
                out = out + extra_in[0][...]
            if norm_out is not None:
                g_ref, h_ref = extra_in[-1], refs[n_in + 1]
                r = lax.rsqrt(jnp.mean(out * out, axis=-1, keepdims=True) + RMS_EPS)
                h_ref[...] = ((out * r) * g_ref[...]).astype(h_ref.dtype)
            if norm_bwd is not None:
                x_ref, g_ref, dr_ref = extra_in
                dg_ref = refs[n_in + 1]
                xv = x_ref[...]
                r = lax.rsqrt(jnp.mean(xv * xv, axis=-1, keepdims=True) + RMS_EPS)
                n = xv * r
                dn = out * g_ref[...]
                gpart = jnp.sum(out * n, axis=0, keepdims=True)
                i = pl.program_id(0)

                @pl.when(i == 0)
                def _():
                    dg_ref[...] = gpart

                @pl.when(i > 0)
                def _():
                    dg_ref[...] += gpart

                out = dr_ref[...] + r * (dn - n * jnp.mean(dn * n, axis=-1, keepdims=True))
            o_ref[...] = out.astype(o_ref.dtype)

        if nk == 1:
            finish(part)
            return
        acc_ref = refs[-1]
        k = pl.program_id(2)

        @pl.when(k == 0)
        def _():
            acc_ref[...] = part

        @pl.when(jnp.logical_and(k > 0, k < nk - 1))
        def _():
            acc_ref[...] += part

        @pl.when(k == nk - 1)
        def _():
            finish(acc_ref[...] + part)

    outs, got = _call(
        body, name=name, grid=(nm, nn, nk), in_specs=in_specs, out_specs=out_specs, out_shape=out_shapes,
        scratch_shapes=[] if nk == 1 else [pltpu.VMEM((bm, bn), F32)],
        sem=("arbitrary",) * 3 if norm_bwd is not None else ("parallel", "parallel", "arbitrary"),
        ride=ride, args=args)
    res = outs[0] if len(outs) == 1 else tuple(outs)
    return res if ride is None else (res, got)


def _rmsnorm_fwd(x, gain, *, name, tm=512):
    T, D = x.shape
    tm = min(tm, T)

    def body(x_ref, g_ref, o_ref):
        xv = x_ref[...]
        r = lax.rsqrt(jnp.mean(xv * xv, axis=-1, keepdims=True) + RMS_EPS)
        o_ref[...] = ((xv * r) * g_ref[...]).astype(o_ref.dtype)

    return pl.pallas_call(
        body, name=name, grid=(T // tm,),
        in_specs=[pl.BlockSpec((tm, D), lambda i: (i, 0)), pl.BlockSpec((1, D), lambda i: (0, 0))],
        out_specs=pl.BlockSpec((tm, D), lambda i: (i, 0)),
        out_shape=jax.ShapeDtypeStruct((T, D), BF16),
        compiler_params=_params(("parallel",)),
    )(x, gain.reshape(1, D))


def _loss_head(y, target, *, name, tm=512):
    T, D = y.shape
    tm = min(tm, T)

    def body(y_ref, t_ref, dy_ref, l_ref):
        i = pl.program_id(0)
        e = y_ref[...] - t_ref[...]
        dy_ref[...] = e * (1.0 / D)
        s = 0.5 * jnp.sum(jnp.mean(e * e, axis=-1, keepdims=True), axis=0, keepdims=True)
        part = jnp.broadcast_to(s, (1, LANES))

        @pl.when(i == 0)
        def _():
            l_ref[...] = part

        @pl.when(i > 0)
        def _():
            l_ref[...] += part

    return pl.pallas_call(
        body, name=name, grid=(T // tm,),
        in_specs=[pl.BlockSpec((tm, D), lambda i: (i, 0)), pl.BlockSpec((tm, D), lambda i: (i, 0))],
        out_specs=[pl.BlockSpec((tm, D), lambda i: (i, 0)), pl.BlockSpec((1, LANES), lambda i: (0, 0))],
        out_shape=[jax.ShapeDtypeStruct((T, D), F32), jax.ShapeDtypeStruct((1, LANES), F32)],
        compiler_params=_params(("arbitrary",)),
    )(y, target)


def _hgrn_selectors():
    t = jnp.arange(CHUNK)[:, None]
    s = jnp.arange(CHUNK)[None, :]
    mats = [s <= t, s < (t // SUB) * SUB]
    for i in range(1, CHUNK // SUB):
        mats.append(jnp.broadcast_to(s < i * SUB, (8, CHUNK)))
    mats.append(jnp.ones((8, CHUNK), bool))
    sel = jnp.concatenate([m.astype(BF16) for m in mats], axis=0)
    rev = (s >= t).astype(BF16)
    return sel, rev


def _select_sums(sel, x):
    hi = x.astype(BF16)
    r1 = x - hi.astype(F32)
    mid = r1.astype(BF16)
    lo = (r1 - mid.astype(F32)).astype(BF16)
    return _dot(sel, hi, NN) + (_dot(sel, mid, NN) + _dot(sel, lo, NN))


def _hgrn_cums(sel, logf):
    nsub = CHUNK // SUB
    cums = _select_sums(sel, logf)
    g = cums[0:CHUNK]
    rrow = cums[CHUNK:2 * CHUNK]
    base = 2 * CHUNK
    rsel = [None] + [jnp.max(cums[base + 8 * (i - 1):base + 8 * i], axis=0, keepdims=True) for i in range(1, nsub)]
    gl = jnp.max(cums[base + 8 * (nsub - 1):base + 8 * nsub], axis=0, keepdims=True)
    return g, rrow, rsel, gl


def _hgrn_gates(p, lb, D):
    qpre, fpre, iv, gpre = p[:, 0:D], p[:, D:2 * D], p[:, 2 * D:3 * D], p[:, 3 * D:4 * D]
    sig = _sigmoid(fpre)
    forget = lb + (1.0 - lb) * sig
    key = 1.0 - forget
    logf = jnp.log(forget)
    sq = _sigmoid(qpre)
    qs = qpre * sq
    return qpre, sq, qs, sig, forget, key, logf, iv, gpre


def _hgrn_fwd(p, lb, gn, *, name, ride=None):
    T, D4 = p.shape
    D = D4 // 4
    H = D // HEAD
    nc = T // CHUNK
    nb = min(HGRN_CHUNKS_PER_STEP, nc)
    assert nc % nb == 0
    sel, _ = _hgrn_selectors()
    nsel = sel.shape[0]
    nsub = CHUNK // SUB
    heads = [slice(h * HEAD, (h + 1) * HEAD) for h in range(H)]

    def body(p_ref, lb_ref, gn_ref, sel_ref, og_ref, o_ref, s0_ref, st_ref):
        @pl.when(pl.program_id(0) == 0)
        def _():
            st_ref[...] = jnp.zeros_like(st_ref)

        row = lax.broadcasted_iota(jnp.int32, (CHUNK, CHUNK), 0)
        col = lax.broadcasted_iota(jnp.int32, (CHUNK, CHUNK), 1)
        gnv = gn_ref[...]
        lbv = lb_ref[...]
        selv = sel_ref[...]
        pre = []
        for cc in range(nb):
            pv = p_ref[cc * CHUNK:(cc + 1) * CHUNK, :]
            _, _, qs, _, _, key, logf, iv, gpre = _hgrn_gates(pv, lbv, D)
            g, rrow, rsel_all, gl = _hgrn_cums(selv, logf)
            qgb = (qs * jnp.exp(g)).astype(BF16)
            qtb = (qs * jnp.exp(g - rrow)).astype(BF16)
            ktb = [(key * jnp.exp(jnp.minimum((0.0 if r is None else r) - g, EXP_CLAMP))).astype(BF16)
                   for r in rsel_all]
            kdb = (key * jnp.exp(gl - g)).astype(BF16)
            vb = iv.astype(BF16)
            blocks = [[_dot(qtb[i * SUB:(i + 1) * SUB, sl], ktb[i][:, sl], NT) for i in range(nsub)]
                      for sl in heads]
            amats = [jnp.where(col <= row, jnp.concatenate(bl, axis=0), 0.0).astype(BF16) for bl in blocks]
            pre.append(dict(qgb=qgb, egl=jnp.exp(gl), gate=gpre * _sigmoid(gpre),
                            intra=[_dot(a, vb[:, sl], NN) for a, sl in zip(amats, heads)],
                            upd=[_dot(vb[:, sl], kdb[:, sl], TN) for sl in heads]))
        sts = [st_ref[sl, :] for sl in heads]
        for cc, d in enumerate(pre):
            rows = slice(cc * CHUNK, (cc + 1) * CHUNK)
            inter = [_dot(d["qgb"][:, sl], st.astype(BF16), NT) for sl, st in zip(heads, sts)]
            for h, sl in enumerate(heads):
                s0_ref[cc, sl, :] = sts[h]
                o = inter[h] + d["intra"][h]
                o_ref[rows, sl] = o
                r = lax.rsqrt(jnp.mean(o * o, axis=-1, keepdims=True) + RMS_EPS)
                og_ref[rows, sl] = (((o * r) * gnv) * d["gate"][:, sl]).astype(og_ref.dtype)
            sts = [st * d["egl"][:, sl] + u for st, sl, u in zip(sts, heads, d["upd"])]
        for sl, st in zip(heads, sts):
            st_ref[sl, :] = st

    rb = nb * CHUNK
    return _call(
        body, name=name, grid=(nc // nb,),
        in_specs=[pl.BlockSpec((rb, D4), lambda c: (c, 0)), pl.BlockSpec((1, D), lambda c: (0, 0)),
                  pl.BlockSpec((1, HEAD), lambda c: (0, 0)), pl.BlockSpec((nsel, CHUNK), lambda c: (0, 0))],
        out_specs=[pl.BlockSpec((rb, D), lambda c: (c, 0)), pl.BlockSpec((rb, D), lambda c: (c, 0)),
                   pl.BlockSpec((nb, D, HEAD), lambda c: (c, 0, 0))],
        out_shape=[jax.ShapeDtypeStruct((T, D), BF16), jax.ShapeDtypeStruct((T, D), F32),
                   jax.ShapeDtypeStruct((nc, D, HEAD), F32)],
        scratch_shapes=[pltpu.VMEM((D, HEAD), F32)], sem=("arbitrary",), ride=ride,
        args=(p, lb.reshape(1, D), gn.reshape(1, HEAD), sel))


def _hgrn_bwd(p, lb, gn, s0, o_saved, dog, *, name, ride=None):
    T, D4 = p.shape
    D = D4 // 4
    H = D // HEAD
    nc = T // CHUNK
    nb = min(HGRN_BWD_CHUNKS_PER_STEP, nc)
    assert nc % nb == 0
    nsteps = nc // nb
    sel, rev = _hgrn_selectors()
    nsel = sel.shape[0]
    nsub = CHUNK // SUB
    heads = [slice(h * HEAD, (h + 1) * HEAD) for h in range(H)]
    cat = lambda xs: jnp.concatenate(xs, axis=1)

    def body(p_ref, lb_ref, gn_ref, sel_ref, rev_ref, s0_ref, s1_ref, o_ref, dog_ref,
             dp_ref, dlb_ref, dgn_ref, dst_ref):
        @pl.when(pl.program_id(0) == 0)
        def _():
            dst_ref[...] = jnp.zeros_like(dst_ref)
            dlb_ref[...] = jnp.zeros_like(dlb_ref)
            dgn_ref[...] = jnp.zeros_like(dgn_ref)

        row = lax.broadcasted_iota(jnp.int32, (CHUNK, CHUNK), 0)
        col = lax.broadcasted_iota(jnp.int32, (CHUNK, CHUNK), 1)
        causal = col <= row
        gnv, lbv, selv, revv = gn_ref[...], lb_ref[...], sel_ref[...], rev_ref[...]
        dgn_acc = jnp.zeros((1, HEAD), F32)
        pre = []
        for cc in range(nb):
            rows = slice(cc * CHUNK, (cc + 1) * CHUNK)
            qpre, sq, qs, sig, forget, key, logf, iv, gpre = _hgrn_gates(p_ref[rows, :], lbv, D)
            g, rrow, rsel_all, gl = _hgrn_cums(selv, logf)
            eg = jnp.exp(g)
            eqr = jnp.exp(g - rrow)
            eis = [jnp.exp(jnp.minimum((0.0 if r is None else r) - g, EXP_CLAMP)) for r in rsel_all]
            ekd = jnp.exp(gl - g)
            qgb = (qs * eg).astype(BF16)
            qtb = (qs * eqr).astype(BF16)
            ktb = [(key * e).astype(BF16) for e in eis]
            kdb = (key * ekd).astype(BF16)
            vb = iv.astype(BF16)
            sg = _sigmoid(gpre)
            gate = gpre * sg
            dgate = sg * (1.0 + gpre * (1.0 - sg))
            dobs, dgpres = [], []
            for sl in heads:
                o = o_ref[rows, sl]
                r = lax.rsqrt(jnp.mean(o * o, axis=-1, keepdims=True) + RMS_EPS)
                n = o * r
                dog_h = dog_ref[rows, sl]
                d_on = dog_h * gate[:, sl]
                dgpres.append(dog_h * (n * gnv) * dgate[:, sl])
                dgn_acc = dgn_acc + jnp.sum(d_on * n, axis=0, keepdims=True)
                dn = d_on * gnv
                dobs.append((r * (dn - n * jnp.mean(dn * n, axis=-1, keepdims=True))).astype(BF16))
            blocks = [[_dot(qtb[i * SUB:(i + 1) * SUB, sl], ktb[i][:, sl], NT) for i in range(nsub)]
                      for sl in heads]
            amats = [jnp.where(causal, jnp.concatenate(bl, axis=0), 0.0).astype(BF16) for bl in blocks]
            dabs = [jnp.where(causal, _dot(dob, vb[:, sl], NT), 0.0).astype(BF16) for dob, sl in zip(dobs, heads)]
            dq_inter = [_dot(dob, s0_ref[cc, sl, :].astype(BF16), NN) for dob, sl in zip(dobs, heads)]
            dqt = [jnp.concatenate([_dot(dab[i * SUB:(i + 1) * SUB], ktb[i][:, sl], NN) for i in range(nsub)],
                                   axis=0) for dab, sl in zip(dabs, heads)]
            dkt = [[_dot(dab[i * SUB:(i + 1) * SUB], qtb[i * SUB:(i + 1) * SUB, sl], TN) for i in range(nsub)]
                   for dab, sl in zip(dabs, heads)]
            dv_intra = [_dot(a, dob, TN) for a, dob in zip(amats, dobs)]
            upd = [_dot(dob, qgb[:, sl], TN) for dob, sl in zip(dobs, heads)]
            dq = cat(dq_inter) * eg + cat(dqt) * eqr
            dk_intra = cat([dkt[h][0] for h in range(H)]) * eis[0]
            for i in range(1, nsub):
                dk_intra = dk_intra + cat([dkt[h][i] for h in range(H)]) * eis[i]
            s1 = [s0_ref[cc + 1, sl, :] if cc + 1 < nb else s1_ref[sl, :] for sl in heads]
            pre.append(dict(qpre=qpre, sq=sq, qs=qs, sig=sig, forget=forget, key=key, ekd=ekd, egl=jnp.exp(gl),
                            kdb=kdb, vb=vb, dq=dq, dk_intra=dk_intra, dv_intra=dv_intra, upd=upd, s1=s1,
                            dgpre=cat(dgpres)))
        dsts = [dst_ref[sl, :] for sl in heads]
        dlb_acc = jnp.zeros((1, D), F32)
        for cc in reversed(range(nb)):
            d = pre[cc]
            rows = slice(cc * CHUNK, (cc + 1) * CHUNK)
            dstb = [x.astype(BF16) for x in dsts]
            dk_state = cat([_dot(d["vb"][:, sl], x, NN) for sl, x in zip(heads, dstb)])
            dv = cat([dvi + _dot(d["kdb"][:, sl], x, NT) for dvi, sl, x in zip(d["dv_intra"], heads, dstb)])
            term = cat([jnp.sum(x * s, axis=0, keepdims=True) for x, s in zip(dsts, d["s1"])])
            dsts = [x * d["egl"][:, sl] + u for x, sl, u in zip(dsts, heads, d["upd"])]
            dk = d["dk_intra"] + dk_state * d["ekd"]
            dq = d["dq"]
            dg = d["qs"] * dq - d["key"] * dk
            dlogf = _select_sums(revv, dg) + term
            sgf = d["sig"]
            dforget = dlogf / d["forget"] - dk
            dlb_acc = dlb_acc + jnp.sum(dforget * (1.0 - sgf), axis=0, keepdims=True)
            sqv = d["sq"]
            dp_ref[rows, 0:D] = (dq * (sqv * (1.0 + d["qpre"] * (1.0 - sqv)))).astype(dp_ref.dtype)
            dp_ref[rows, D:2 * D] = (dforget * (1.0 - lbv) * (sgf * (1.0 - sgf))).astype(dp_ref.dtype)
            dp_ref[rows, 2 * D:3 * D] = dv.astype(dp_ref.dtype)
            dp_ref[rows, 3 * D:4 * D] = d["dgpre"].astype(dp_ref.dtype)
        for sl, x in zip(heads, dsts):
            dst_ref[sl, :] = x
        dlb_ref[...] += dlb_acc
        dgn_ref[...] += dgn_acc

    rb = nb * CHUNK
    rc = lambda c: nsteps - 1 - c
    return _call(
        body, name=name, grid=(nsteps,),
        in_specs=[pl.BlockSpec((rb, D4), lambda c: (rc(c), 0)), pl.BlockSpec((1, D), lambda c: (0, 0)),
                  pl.BlockSpec((1, HEAD), lambda c: (0, 0)), pl.BlockSpec((nsel, CHUNK), lambda c: (0, 0)),
                  pl.BlockSpec((CHUNK, CHUNK), lambda c: (0, 0)),
                  pl.BlockSpec((nb, D, HEAD), lambda c: (rc(c), 0, 0)),
                  pl.BlockSpec((None, D, HEAD), lambda c: (jnp.minimum((rc(c) + 1) * nb, nc - 1), 0, 0)),
                  pl.BlockSpec((rb, D), lambda c: (rc(c), 0)), pl.BlockSpec((rb, D), lambda c: (rc(c), 0))],
        out_specs=[pl.BlockSpec((rb, D4), lambda c: (rc(c), 0)), pl.BlockSpec((1, D), lambda c: (0, 0)),
                   pl.BlockSpec((1, HEAD), lambda c: (0, 0))],
        out_shape=[jax.ShapeDtypeStruct((T, D4), BF16), jax.ShapeDtypeStruct((1, D), F32),
                   jax.ShapeDtypeStruct((1, HEAD), F32)],
        scratch_shapes=[pltpu.VMEM((D, HEAD), F32)], sem=("arbitrary",), ride=ride,
        args=(p, lb.reshape(1, D), gn.reshape(1, HEAD), sel, rev, s0, s0, o_saved, dog))


def _rope_tables(positions):
    inv_freq = ROPE_THETA ** (-jnp.arange(0, ROPE, 2, dtype=F32) / ROPE)
    ang = positions.astype(F32)[:, None] * inv_freq
    cos, sin = jnp.cos(ang), jnp.sin(ang)
    z = jnp.zeros_like(cos)
    ctab = jnp.concatenate([cos, cos, z, z], axis=-1)
    s1 = jnp.concatenate([-sin, z, z, z], axis=-1)
    s2 = jnp.concatenate([z, sin, z, z], axis=-1)
    return ctab, s1, s2


def _rope(z, ct, s1, s2):
    return z * ct + pltpu.roll(z, 96, 1) * s1 + pltpu.roll(z, 32, 1) * s2


def _rope_t(d, ct, s1, s2):
    return d * ct + pltpu.roll(d * s1, 32, 1) + pltpu.roll(d * s2, 96, 1)


def _mla_prep_fwd(c, wq, wkv, ga_q, ga_kv, gq, gk, ct, s1, s2, *, name, tm=512):
    T, CW = c.shape
    R = (CW - LANES) // 2
    H = wq.shape[1] // QK_PAD
    tm = min(tm, T)

    def body(c_ref, wq_ref, wkv_ref, gaq_ref, gakv_ref, gq_ref, gk_ref, ct_ref, s1_ref, s2_ref,
             q_ref, k_ref, v_ref):
        cv = c_ref[...]
        cq, ckv, kr = cv[:, 0:R], cv[:, R:2 * R], cv[:, 2 * R:2 * R + LANES]
        rq = lax.rsqrt(jnp.mean(cq * cq, axis=-1, keepdims=True) + RMS_EPS)
        cqn = ((cq * rq) * gaq_ref[...]).astype(BF16)
        rk = lax.rsqrt(jnp.mean(ckv * ckv, axis=-1, keepdims=True) + RMS_EPS)
        ckvn = ((ckv * rk) * gakv_ref[...]).astype(BF16)
        qp = _dot(cqn, wq_ref[...], NN)
        kvp = _dot(ckvn, wkv_ref[...], NN)
        ctv, s1v, s2v = ct_ref[...], s1_ref[...], s2_ref[...]
        gqv, gkv = gq_ref[...], gk_ref[...]
        krs = jnp.sum(kr * kr, axis=-1, keepdims=True)
        for h in range(H):
            b = h * QK_PAD
            qn, qr = qp[:, b:b + HEAD], qp[:, b + HEAD:b + QK_PAD]
            ss = jnp.sum(qn * qn + qr * qr, axis=-1, keepdims=True)
            rr = lax.rsqrt(ss * (1.0 / QK_HEAD) + RMS_EPS)
            q_ref[:, b:b + HEAD] = ((qn * rr) * gqv[:, 0:HEAD]).astype(q_ref.dtype)
            q_ref[:, b + HEAD:b + QK_PAD] = _rope((qr * rr) * gqv[:, HEAD:QK_PAD], ctv, s1v, s2v).astype(q_ref.dtype)
            kn, vv = kvp[:, b:b + HEAD], kvp[:, b + HEAD:b + QK_PAD]
            ssk = jnp.sum(kn * kn, axis=-1, keepdims=True) + krs
            rrk = lax.rsqrt(ssk * (1.0 / QK_HEAD) + RMS_EPS)
            k_ref[:, b:b + HEAD] = ((kn * rrk) * gkv[:, 0:HEAD]).astype(k_ref.dtype)
            k_ref[:, b + HEAD:b + QK_PAD] = _rope((kr * rrk) * gkv[:, HEAD:QK_PAD], ctv, s1v, s2v).astype(k_ref.dtype)
            v_ref[:, h * HEAD:(h + 1) * HEAD] = vv.astype(v_ref.dtype)

    full = lambda shape: pl.BlockSpec(shape, lambda i: (0, 0))
    tok = lambda w: pl.BlockSpec((tm, w), lambda i: (i, 0))
    return pl.pallas_call(
        body, name=name, grid=(T // tm,),
        in_specs=[tok(CW), full(wq.shape), full(wkv.shape), full((1, R)), full((1, R)), full((1, QK_PAD)),
                  full((1, QK_PAD)), tok(LANES), tok(LANES), tok(LANES)],
        out_specs=[tok(H * QK_PAD), tok(H * QK_PAD), tok(H * HEAD)],
        out_shape=[jax.ShapeDtypeStruct((T, H * QK_PAD), BF16), jax.ShapeDtypeStruct((T, H * QK_PAD), BF16),
                   jax.ShapeDtypeStruct((T, H * HEAD), BF16)],
        compiler_params=_params(("parallel",)),
    )(c, wq, wkv, ga_q.reshape(1, R), ga_kv.reshape(1, R), gq, gk, ct, s1, s2)


def _mla_prep_bwd(c, wq, wkv, ga_q, ga_kv, gq, gk, ct, s1, s2, dq, dk, dv, *, name, tm=512):
    T, CW = c.shape
    R = (CW - LANES) // 2
    H = wq.shape[1] // QK_PAD
    tm = min(tm, T)

    def body(c_ref, wq_ref, wkv_ref, gaq_ref, gakv_ref, gq_ref, gk_ref, ct_ref, s1_ref, s2_ref,
             dq_ref, dk_ref, dv_ref,
             dc_ref, dwq_ref, dwkv_ref, dgaq_ref, dgakv_ref, dgq_ref, dgk_ref, dqp_ref, dkvp_ref):
        i = pl.program_id(0)

        @pl.when(i == 0)
        def _():
            for ref in (dwq_ref, dwkv_ref, dgaq_ref, dgakv_ref, dgq_ref, dgk_ref):
                ref[...] = jnp.zeros_like(ref)

        cv = c_ref[...]
        cq, ckv, kr = cv[:, 0:R], cv[:, R:2 * R], cv[:, 2 * R:2 * R + LANES]
        rq = lax.rsqrt(jnp.mean(cq * cq, axis=-1, keepdims=True) + RMS_EPS)
        nq = cq * rq
        cqn = (nq * gaq_ref[...]).astype(BF16)
        rk = lax.rsqrt(jnp.mean(ckv * ckv, axis=-1, keepdims=True) + RMS_EPS)
        nkv = ckv * rk
        ckvn = (nkv * gakv_ref[...]).astype(BF16)
        qp = _dot(cqn, wq_ref[...], NN)
        kvp = _dot(ckvn, wkv_ref[...], NN)
        ctv, s1v, s2v = ct_ref[...], s1_ref[...], s2_ref[...]
        gqv, gkv = gq_ref[...], gk_ref[...]
        krs = jnp.sum(kr * kr, axis=-1, keepdims=True)
        dkr = jnp.zeros((tm, LANES), F32)
        dgq_n = jnp.zeros((1, HEAD), F32)
        dgq_r = jnp.zeros((1, HEAD), F32)
        dgk_n = jnp.zeros((1, HEAD), F32)
        dgk_r = jnp.zeros((1, HEAD), F32)
        for h in range(H):
            b = h * QK_PAD
            qn, qr = qp[:, b:b + HEAD], qp[:, b + HEAD:b + QK_PAD]
            ss = jnp.sum(qn * qn + qr * qr, axis=-1, keepdims=True)
            rr = lax.rsqrt(ss * (1.0 / QK_HEAD) + RMS_EPS)
            un, ur = qn * rr, qr * rr
            dzn = dq_ref[:, b:b + HEAD]
            dzr = _rope_t(dq_ref[:, b + HEAD:b + QK_PAD], ctv, s1v, s2v)
            dgq_n = dgq_n + jnp.sum(dzn * un, axis=0, keepdims=True)
            dgq_r = dgq_r + jnp.sum(dzr * ur, axis=0, keepdims=True)
            dun, dur = dzn * gqv[:, 0:HEAD], dzr * gqv[:, HEAD:QK_PAD]
            m = jnp.sum(dun * un + dur * ur, axis=-1, keepdims=True) * (1.0 / QK_HEAD)
            dqp_ref[:, b:b + HEAD] = (rr * (dun - un * m)).astype(BF16)
            dqp_ref[:, b + HEAD:b + QK_PAD] = (rr * (dur - ur * m)).astype(BF16)
            kn = kvp[:, b:b + HEAD]
            ssk = jnp.sum(kn * kn, axis=-1, keepdims=True) + krs
            rrk = lax.rsqrt(ssk * (1.0 / QK_HEAD) + RMS_EPS)
            vn, vr = kn * rrk, kr * rrk
            dyn = dk_ref[:, b:b + HEAD]
            dyr = _rope_t(dk_ref[:, b + HEAD:b + QK_PAD], ctv, s1v, s2v)
            dgk_n = dgk_n + jnp.sum(dyn * vn, axis=0, keepdims=True)
            dgk_r = dgk_r + jnp.sum(dyr * vr, axis=0, keepdims=True)
            dvn, dvr = dyn * gkv[:, 0:HEAD], dyr * gkv[:, HEAD:QK_PAD]
            mk = jnp.sum(dvn * vn + dvr * vr, axis=-1, keepdims=True) * (1.0 / QK_HEAD)
            dkvp_ref[:, b:b + HEAD] = (rrk * (dvn - vn * mk)).astype(BF16)
            dkr = dkr + rrk * (dvr - vr * mk)
            dkvp_ref[:, b + HEAD:b + QK_PAD] = dv_ref[:, h * HEAD:(h + 1) * HEAD].astype(BF16)
        dgq_ref[:, 0:HEAD] += dgq_n
        dgq_ref[:, HEAD:QK_PAD] += dgq_r
        dgk_ref[:, 0:HEAD] += dgk_n
        dgk_ref[:, HEAD:QK_PAD] += dgk_r
        dqp = dqp_ref[...]
        dkvp = dkvp_ref[...]
        dwq_ref[...] += _dot(cqn, dqp, TN)
        dwkv_ref[...] += _dot(ckvn, dkvp, TN)
        dcqn = _dot(dqp, wq_ref[...], NT)
        dckvn = _dot(dkvp, wkv_ref[...], NT)
        dgaq_ref[...] += jnp.sum(dcqn * nq, axis=0, keepdims=True)
        dgakv_ref[...] += jnp.sum(dckvn * nkv, axis=0, keepdims=True)
        dnq = dcqn * gaq_ref[...]
        dnkv = dckvn * gakv_ref[...]
        dc_ref[:, 0:R] = (rq * (dnq - nq * jnp.mean(dnq * nq, axis=-1, keepdims=True))).astype(dc_ref.dtype)
        dc_ref[:, R:2 * R] = (rk * (dnkv - nkv * jnp.mean(dnkv * nkv, axis=-1, keepdims=True))).astype(dc_ref.dtype)
        dc_ref[:, 2 * R:2 * R + LANES] = dkr.astype(dc_ref.dtype)

    full = lambda shape: pl.BlockSpec(shape, lambda i: (0, 0))
    tok = lambda w: pl.BlockSpec((tm, w), lambda i: (i, 0))
    return pl.pallas_call(
        body, name=name, grid=(T // tm,),
        in_specs=[tok(CW), full(wq.shape), full(wkv.shape), full((1, R)), full((1, R)), full((1, QK_PAD)),
                  full((1, QK_PAD)), tok(LANES), tok(LANES), tok(LANES),
                  tok(H * QK_PAD), tok(H * QK_PAD), tok(H * HEAD)],
        out_specs=[tok(CW), full(wq.shape), full(wkv.shape), full((1, R)), full((1, R)), full((1, QK_PAD)),
                   full((1, QK_PAD))],
        out_shape=[jax.ShapeDtypeStruct((T, CW), BF16), jax.ShapeDtypeStruct(wq.shape, F32),
                   jax.ShapeDtypeStruct(wkv.shape, F32), jax.ShapeDtypeStruct((1, R), F32),
                   jax.ShapeDtypeStruct((1, R), F32), jax.ShapeDtypeStruct((1, QK_PAD), F32),
                   jax.ShapeDtypeStruct((1, QK_PAD), F32)],
        scratch_shapes=[pltpu.VMEM((tm, H * QK_PAD), BF16), pltpu.VMEM((tm, H * QK_PAD), BF16)],
        compiler_params=_params(("arbitrary",)),
    )(c, wq, wkv, ga_q.reshape(1, R), ga_kv.reshape(1, R), gq, gk, ct, s1, s2, dq, dk, dv)


NEG = -1e30
LOG2E = 1.4426950408889634


def _attn_fwd(q, k, v, *, name, tb=512, hp=2, ride=None):
    T = q.shape[0]
    H = q.shape[1] // QK_PAD
    tb = min(tb, T)
    nq = T // tb
    scale = QK_HEAD ** -0.5
    c2 = scale * LOG2E
    assert H % hp == 0

    def body(q_ref, k_ref, v_ref, ot_ref, lse_ref, m_ref, l_ref, acc_ref):
        i = pl.program_id(1)
        m_ref[...] = jnp.full_like(m_ref, NEG)
        l_ref[...] = jnp.zeros_like(l_ref)
        acc_ref[...] = jnp.zeros_like(acc_ref)

        def step(js, masked):
            offs = [pl.multiple_of(j * tb, tb) for j in js]
            sts = [[_dot(k_ref[pl.ds(off, tb), hh * QK_PAD:(hh + 1) * QK_PAD],
                         q_ref[:, hh * QK_PAD:(hh + 1) * QK_PAD], NT) for hh in range(hp)] for off in offs]
            for b, hh in [(b, hh) for b in range(len(js)) for hh in range(hp)]:
                off = offs[b]
                vs = slice(hh * HEAD, (hh + 1) * HEAD)
                vb = v_ref[pl.ds(off, tb), vs]
                st = sts[b][hh]
                if masked:
                    kpos = lax.broadcasted_iota(jnp.int32, (tb, tb), 0)
                    qpos = lax.broadcasted_iota(jnp.int32, (tb, tb), 1)
                    st = jnp.where(kpos <= qpos, st, NEG)
                m_old = m_ref[hh]
                m_new = jnp.maximum(m_old, jnp.max(st, axis=0, keepdims=True))
                alpha = jnp.exp2((m_old - m_new) * c2)
                pt = jnp.exp2((st - m_new) * c2)
                l_ref[hh] = l_ref[hh] * alpha + jnp.sum(pt, axis=0, keepdims=True)
                acc_ref[vs, :] = acc_ref[vs, :] * alpha + _dot(vb, pt.astype(BF16), TN)
                m_ref[hh] = m_new

        def pair_body(t, carry):
            step([2 * t, 2 * t + 1], False)
            return carry

        lax.fori_loop(0, i // 2, pair_body, 0)

        @pl.when(i % 2 == 1)
        def _():
            step([i - 1], False)

        step([i], True)
        for hh in range(hp):
            vs = slice(hh * HEAD, (hh + 1) * HEAD)
            l = l_ref[hh]
            ot_ref[vs, :] = (acc_ref[vs, :] / l).astype(ot_ref.dtype)
            lse_ref[hh] = m_ref[hh] * scale + jnp.log(l)

    return _call(
        body, name=name, grid=(H // hp, nq),
        in_specs=[pl.BlockSpec((tb, hp * QK_PAD), lambda g, i: (i, g)),
                  pl.BlockSpec((T, hp * QK_PAD), lambda g, i: (0, g)),
                  pl.BlockSpec((T, hp * HEAD), lambda g, i: (0, g))],
        out_specs=[pl.BlockSpec((None, hp * HEAD, tb), lambda g, i: (i, g, 0)),
                   pl.BlockSpec((hp, None, 1, tb), lambda g, i: (g, i, 0, 0))],
        out_shape=[jax.ShapeDtypeStruct((nq, H * HEAD, tb), BF16), jax.ShapeDtypeStruct((H, nq, 1, tb), F32)],
        scratch_shapes=[pltpu.VMEM((hp, 1, tb), F32), pltpu.VMEM((hp, 1, tb), F32),
                        pltpu.VMEM((hp * HEAD, tb), F32)],
        sem=("parallel", "arbitrary"), ride=ride, args=(q, k, v))


def _attn_bwd(q, k, v, ot, lse, dot_, *, name, ride=None):
    T = q.shape[0]
    H = q.shape[1] // QK_PAD
    nq, _, tb = ot.shape
    scale = QK_HEAD ** -0.5
    c2 = scale * LOG2E

    def body(q_ref, k_ref, v_ref, ot_ref, lse_ref, dot_ref, dq_ref, dk_ref, dv_ref, dka_ref, dva_ref):
        j = pl.program_id(1)

        @pl.when(j == 0)
        def _():
            dq_ref[...] = jnp.zeros_like(dq_ref)

        kb = k_ref[...]
        vb = v_ref[...]

        def products(i, masked):
            off = i * tb if isinstance(i, int) else pl.multiple_of(i * tb, tb)
            qb = q_ref[pl.ds(off, tb), :]
            dob = dot_ref[i]
            st = _dot(kb, qb, NT)
            if masked:
                kpos = lax.broadcasted_iota(jnp.int32, (tb, tb), 0)
                qpos = lax.broadcasted_iota(jnp.int32, (tb, tb), 1)
                st = jnp.where(kpos <= qpos, st, NEG)
            return i, off, qb, dob, st, _dot(vb, dob, NN)

        def finish(i, off, qb, dob, st, dpt):
            pt = jnp.exp2(st * c2 - lse_ref[i] * LOG2E)
            delta = jnp.sum(dob.astype(F32) * ot_ref[i].astype(F32), axis=0, keepdims=True)
            dst = (pt * (dpt - delta)).astype(BF16)
            dq_ref[pl.ds(off, tb), :] += _dot(dst, kb, TN)
            return _dot(dst, qb, NN), _dot(pt.astype(BF16), dob, NT)

        def step(i, masked):
            return finish(*products(i, masked))

        dk0, dv0 = step(j, True)
        dka_ref[...] = dk0
        dva_ref[...] = dv0

        rest = nq - 1 - j

        def pair_body(t, carry):
            i0 = j + 1 + 2 * t
            first = products(i0, False)
            second = products(i0 + 1, False)
            dk1, dv1 = finish(*first)
            dk2, dv2 = finish(*second)
            dka_ref[...] += dk1 + dk2
            dva_ref[...] += dv1 + dv2
            return carry

        lax.fori_loop(0, rest // 2, pair_body, 0)

        @pl.when(rest % 2 == 1)
        def _():
            dk1, dv1 = step(nq - 1, False)
            dka_ref[...] += dk1
            dva_ref[...] += dv1

        dk_ref[...] = dka_ref[...] * scale
        dv_ref[...] = dva_ref[...]

        @pl.when(j == nq - 1)
        def _():
            dq_ref[...] = dq_ref[...] * scale

    return _call(
        body, name=name, grid=(H, nq),
        in_specs=[pl.BlockSpec((T, QK_PAD), lambda h, j: (0, h)), pl.BlockSpec((tb, QK_PAD), lambda h, j: (j, h)),
                  pl.BlockSpec((tb, HEAD), lambda h, j: (j, h)),
                  pl.BlockSpec((nq, HEAD, tb), lambda h, j: (0, h, 0)),
                  pl.BlockSpec((None, nq, 1, tb), lambda h, j: (h, 0, 0, 0)),
                  pl.BlockSpec((nq, HEAD, tb), lambda h, j: (0, h, 0))],
        out_specs=[pl.BlockSpec((T, QK_PAD), lambda h, j: (0, h)), pl.BlockSpec((tb, QK_PAD), lambda h, j: (j, h)),
                   pl.BlockSpec((tb, HEAD), lambda h, j: (j, h))],
        out_shape=[jax.ShapeDtypeStruct((T, H * QK_PAD), F32), jax.ShapeDtypeStruct((T, H * QK_PAD), F32),
                   jax.ShapeDtypeStruct((T, H * HEAD), F32)],
        scratch_shapes=[pltpu.VMEM((tb, QK_PAD), F32), pltpu.VMEM((tb, HEAD), F32)],
        sem=("parallel", "arbitrary"), ride=ride, args=(q, k, v, ot, lse, dot_))


def _conv_taps(u, prev6, prev7):
    rows = lax.broadcasted_iota(jnp.int32, (u.shape[0], 1), 0)
    u1 = jnp.where(rows >= 1, pltpu.roll(u, 1, 0), prev7)
    u2 = jnp.where(rows >= 2, pltpu.roll(u, 2, 0), jnp.where(rows == 0, prev6, prev7))
    return u2, u1


def _conv_taps_ahead(d, next0, next1):
    tm = d.shape[0]
    rows = lax.broadcasted_iota(jnp.int32, (tm, 1), 0)
    d1 = jnp.where(rows < tm - 1, pltpu.roll(d, tm - 1, 0), next0)
    d2 = jnp.where(rows < tm - 2, pltpu.roll(d, tm - 2, 0), jnp.where(rows == tm - 2, next0, next1))
    return d1, d2


def _ffn_up_fwd(h, w_up, conv_w, conv_b, *, name, tm=512, ride=None):
    T, D = h.shape
    ns, _, fs = w_up.shape
    nh = ns // 2
    tm = min(tm, T)

    def body(h_ref, wg_ref, wu_ref, cwg_ref, cwu_ref, cbg_ref, cbu_ref, a_ref, u_ref, y_ref, cg_ref, cu_ref):
        i = pl.program_id(1)

        @pl.when(i == 0)
        def _():
            cg_ref[...] = jnp.zeros_like(cg_ref)
            cu_ref[...] = jnp.zeros_like(cu_ref)

        hv = h_ref[...]
        ys = []
        for idx, (w_ref, cw_ref, cb_ref, carry) in enumerate(
                ((wg_ref, cwg_ref, cbg_ref, cg_ref), (wu_ref, cwu_ref, cbu_ref, cu_ref))):
            u = _dot(hv, w_ref[...], NN)
            u_ref[idx] = u.astype(u_ref.dtype)
            u2, u1 = _conv_taps(u, carry[6:7, :], carry[7:8, :])
            y = cb_ref[...] + u2 * cw_ref[0:1, :]
            y = y + u1 * cw_ref[1:2, :]
            y = y + u * cw_ref[2:3, :]
            y_ref[idx] = y.astype(y_ref.dtype)
            ys.append(y)
            carry[...] = u[tm - 8:tm, :]
        yg, yu = ys
        a_ref[...] = ((yg * _sigmoid(yg)) * yu).astype(a_ref.dtype)

    shard = lambda r, off: pl.BlockSpec((None, r, fs), lambda j, i: (j + off, 0, 0))
    return _call(
        body, name=name, grid=(nh, T // tm),
        in_specs=[pl.BlockSpec((tm, D), lambda j, i: (i, 0)), shard(D, 0), shard(D, nh),
                  shard(3, 0), shard(3, nh), shard(1, 0), shard(1, nh)],
        out_specs=[pl.BlockSpec((None, tm, fs), lambda j, i: (j, i, 0)),
                   pl.BlockSpec((2, None, tm, fs), lambda j, i: (0, j, i, 0)),
                   pl.BlockSpec((2, None, tm, fs), lambda j, i: (0, j, i, 0))],
        out_shape=[jax.ShapeDtypeStruct((nh, T, fs), BF16), jax.ShapeDtypeStruct((2, nh, T, fs), BF16),
                   jax.ShapeDtypeStruct((2, nh, T, fs), BF16)],
        scratch_shapes=[pltpu.VMEM((8, fs), F32), pltpu.VMEM((8, fs), F32)],
        sem=("parallel", "arbitrary"), ride=ride, args=(h, w_up, w_up, conv_w, conv_w, conv_b, conv_b))


def _ffn_act_bwd(dxo, w_down, u, y, conv_w, *, name, tm=512):
    T, D = dxo.shape
    _, nh, _, fs = u.shape
    tm = min(tm, T)
    nt = T // tm

    def body(dx_ref, wd_ref, u_ref, y_ref, cwg_ref, cwu_ref, du_ref, dcw_ref, dcb_ref, cg_ref, cu_ref):
        i = pl.program_id(1)

        @pl.when(i == 0)
        def _():
            cg_ref[...] = jnp.zeros_like(cg_ref)
            cu_ref[...] = jnp.zeros_like(cu_ref)
            dcw_ref[...] = jnp.zeros_like(dcw_ref)
            dcb_ref[...] = jnp.zeros_like(dcb_ref)

        da = _dot(dx_ref[...].astype(BF16), wd_ref[...], NT)
        yg, yu = y_ref[0].astype(F32), y_ref[1].astype(F32)
        sg = _sigmoid(yg)
        dys = (da * yu * (sg * (1.0 + yg * (1.0 - sg))), da * (yg * sg))
        for idx, (cw_ref, carry) in enumerate(((cwg_ref, cg_ref), (cwu_ref, cu_ref))):
            dy = dys[idx]
            uv = u_ref[idx].astype(F32)
            d1, d2 = _conv_taps_ahead(dy, carry[0:1, :], carry[1:2, :])
            dcb_ref[idx] += jnp.sum(dy, axis=0, keepdims=True)
            dcw_ref[idx, 0:1, :] += jnp.sum(d2 * uv, axis=0, keepdims=True)
            dcw_ref[idx, 1:2, :] += jnp.sum(d1 * uv, axis=0, keepdims=True)
            dcw_ref[idx, 2:3, :] += jnp.sum(dy * uv, axis=0, keepdims=True)
            du = dy * cw_ref[2:3, :] + d1 * cw_ref[1:2, :] + d2 * cw_ref[0:1, :]
            du_ref[idx] = du.astype(du_ref.dtype)
            carry[...] = dy[0:8, :]

    rt = lambda i: nt - 1 - i
    shard = lambda r, off: pl.BlockSpec((None, r, fs), lambda j, i: (j + off, 0, 0))
    tile = pl.BlockSpec((2, None, tm, fs), lambda j, i: (0, j, rt(i), 0))
    return pl.pallas_call(
        body, name=name, grid=(nh, nt),
        in_specs=[pl.BlockSpec((tm, D), lambda j, i: (rt(i), 0)), pl.BlockSpec((fs, D), lambda j, i: (j, 0)),
                  tile, tile, shard(3, 0), shard(3, nh)],
        out_specs=[tile, pl.BlockSpec((2, None, 3, fs), lambda j, i: (0, j, 0, 0)),
                   pl.BlockSpec((2, None, 1, fs), lambda j, i: (0, j, 0, 0))],
        out_shape=[jax.ShapeDtypeStruct((2, nh, T, fs), BF16), jax.ShapeDtypeStruct((2, nh, 3, fs), F32),
                   jax.ShapeDtypeStruct((2, nh, 1, fs), F32)],
        scratch_shapes=[pltpu.VMEM((8, fs), F32), pltpu.VMEM((8, fs), F32)],
        compiler_params=_params(("parallel", "arbitrary")),
    )(dxo, w_down, u, y, conv_w, conv_w)


def _pad_cols(w, n):
    return jnp.pad(w, [(0, 0)] * (w.ndim - 1) + [(0, n - w.shape[-1])])


def _q_up_padded(w):
    R = w.shape[0]
    H = w.shape[1] // QK_HEAD
    return _pad_cols(w.reshape(R, H, QK_HEAD), QK_PAD).reshape(R, H * QK_PAD)


def _q_up_unpadded(w):
    R = w.shape[0]
    H = w.shape[1] // QK_PAD
    return w.reshape(R, H, QK_PAD)[:, :, :QK_HEAD].reshape(R, H * QK_HEAD)


def _xchg_copies(src_refs, out_refs, kinds, send_sems, recv_sems, local_sems):
    x, y, c = lax.axis_index("x"), lax.axis_index("y"), lax.axis_index("c")
    me = 4 * x + 2 * y + c
    copies = []
    for b, kind in enumerate(kinds):
        gather = kind == "gather"
        own = src_refs[b] if gather else src_refs[b].at[me]
        copies.append(pltpu.make_async_copy(own, out_refs[b].at[me], local_sems.at[b]))
        for kk in range(1, N_DEV):
            px = 1 - x if kk & 4 else x
            py = 1 - y if kk & 2 else y
            pc = 1 - c if kk & 1 else c
            peer = 4 * px + 2 * py + pc
            src = src_refs[b] if gather else src_refs[b].at[peer]
            copies.append(pltpu.make_async_remote_copy(
                src_ref=src, dst_ref=out_refs[b].at[me],
                send_sem=send_sems.at[b * (N_DEV - 1) + kk - 1],
                recv_sem=recv_sems.at[b * (N_DEV - 1) + kk - 1],
                device_id=(px, py, pc), device_id_type=pl.DeviceIdType.MESH))
    return copies


def _xchg_out_shapes(srcs, kinds):
    return [jax.ShapeDtypeStruct((N_DEV,) + s.shape if kind == "gather" else s.shape, s.dtype)
            for s, kind in zip(srcs, kinds)]


def _xchg_scratch(n):
    return [pltpu.SemaphoreType.DMA((n * (N_DEV - 1),)), pltpu.SemaphoreType.DMA((n * (N_DEV - 1),)),
            pltpu.SemaphoreType.DMA((n,))]


def _exchange(srcs, kinds, *, name):
    n = len(srcs)

    def body(*refs):
        copies = _xchg_copies(refs[:n], refs[n:2 * n], kinds, *refs[2 * n:])
        for cp in copies:
            cp.start()
        for cp in copies:
            cp.wait()

    hbm = pl.BlockSpec(memory_space=pl.ANY)
    return pl.pallas_call(
        body, name=name, in_specs=[hbm] * n, out_specs=[hbm] * n, out_shape=_xchg_out_shapes(srcs, kinds),
        scratch_shapes=_xchg_scratch(n),
    )(*srcs)


def _call(body, *, name, grid, in_specs, out_specs, out_shape, scratch_shapes, args, sem, ride=None):
    if ride is None:
        outs = pl.pallas_call(body, name=name, grid=grid, in_specs=in_specs, out_specs=out_specs,
                              out_shape=out_shape, scratch_shapes=scratch_shapes,
                              compiler_params=_params(sem))(*args)
        return list(outs), []
    srcs, kinds = ride
    n_in, n_out, n_sc, nx = len(in_specs), len(out_specs), len(scratch_shapes), len(srcs)

    def wrapped(*refs):
        ins, xs = refs[:n_in], refs[n_in:n_in + nx]
        o0 = n_in + nx
        outs, xo = refs[o0:o0 + n_out], refs[o0 + n_out:o0 + n_out + nx]
        s0 = o0 + n_out + nx
        sc, sems = refs[s0:s0 + n_sc], refs[s0 + n_sc:]
        first = functools.reduce(jnp.logical_and, [pl.program_id(d) == 0 for d in range(len(grid))])
        last = functools.reduce(jnp.logical_and, [pl.program_id(d) == grid[d] - 1 for d in range(len(grid))])

        @pl.when(first)
        def _():
            for cp in _xchg_copies(xs, xo, kinds, *sems):
                cp.start()

        body(*ins, *outs, *sc)

        @pl.when(last)
        def _():
            for cp in _xchg_copies(xs, xo, kinds, *sems):
                cp.wait()

    hbm = pl.BlockSpec(memory_space=pl.ANY)
    outs = pl.pallas_call(
        wrapped, name=name, grid=grid, in_specs=list(in_specs) + [hbm] * nx,
        out_specs=list(out_specs) + [hbm] * nx, out_shape=list(out_shape) + _xchg_out_shapes(srcs, kinds),
        scratch_shapes=list(scratch_shapes) + _xchg_scratch(nx),
        compiler_params=_params(("arbitrary",) * len(grid)),
    )(*args, *srcs)
    return list(outs[:n_out]), list(outs[n_out:])


def _sum_slots(parts, *, name):
    _, Rr, C = parts.shape

    def body(p_ref, o_ref):
        acc = p_ref[0].astype(F32)
        for d in range(1, N_DEV):
            acc = acc + p_ref[d].astype(F32)
        o_ref[...] = acc

    return pl.pallas_call(
        body, name=name, grid=(1,),
        in_specs=[pl.BlockSpec((N_DEV, Rr, C), lambda i: (0, 0, 0))],
        out_specs=pl.BlockSpec((Rr, C), lambda i: (0, 0)),
        out_shape=jax.ShapeDtypeStruct((Rr, C), F32),
        compiler_params=_params(("arbitrary",)),
    )(parts)


def _row_tile(rows, cap=512):
    if rows <= cap:
        return rows
    d = (cap // 8) * 8
    while d >= 8:
        if rows % d == 0:
            return d
        d -= 8
    raise ValueError(f"no row tile for {rows}")


def _adamw(parts, w, m, v, *, name):
    S, Rr, C = parts.shape
    tr = _row_tile(Rr)
    c1 = 1.0 - ADAM_B1 ** ADAM_STEP
    c2 = 1.0 - ADAM_B2 ** ADAM_STEP

    def body(p_ref, w_ref, m_ref, v_ref, g_ref, d_ref, nm_ref, nv_ref):
        g = p_ref[0].astype(F32)
        for d in range(1, S):
            g = g + p_ref[d].astype(F32)
        mm = ADAM_B1 * m_ref[...] + (1.0 - ADAM_B1) * g
        vv = ADAM_B2 * v_ref[...] + (1.0 - ADAM_B2) * (g * g)
        m_hat = mm / c1
        v_hat = vv / c2
        g_ref[...] = g
        d_ref[...] = -ADAM_LR * (m_hat / (jnp.sqrt(v_hat) + ADAM_EPS) + ADAM_WD * w_ref[...])
        nm_ref[...] = mm
        nv_ref[...] = vv

    spec = pl.BlockSpec((tr, C), lambda i: (i, 0))
    shape = jax.ShapeDtypeStruct((Rr, C), F32)
    return pl.pallas_call(
        body, name=name, grid=(Rr // tr,),
        in_specs=[pl.BlockSpec((S, tr, C), lambda i: (0, i, 0)), spec, spec, spec],
        out_specs=[spec] * 4, out_shape=[shape] * 4,
        compiler_params=_params(("parallel",)),
    )(parts, w, m, v)


def _pack(arrs, dtype, row_mult):
    flat = jnp.concatenate([a.reshape(-1).astype(dtype) for a in arrs])
    per = row_mult * PACK_COLS
    total = -(-flat.shape[0] // per) * per
    return jnp.pad(flat, (0, total - flat.shape[0])).reshape(total // PACK_COLS, PACK_COLS)


def _unpack(packed, shapes, lead=()):
    flat = packed.reshape(lead + (-1,))
    out, off = [], 0
    for shp in shapes:
        n = 1
        for d in shp:
            n *= d
        out.append(flat[..., off:off + n].reshape(lead + tuple(shp)))
        off += n
    return out


HGRN_W = ("hgrn_w_in", "hgrn_w_out")
MLA_W = ("mla_w_in", "mla_w_q_up", "mla_w_kv_up", "mla_w_out")
FFN_W = ("ffn_w_up", "ffn_w_down")
BIG = HGRN_W + MLA_W + FFN_W
SMALL_SHARDED = {"ffn_conv_w": 2, "mla_q_a_norm": 1, "mla_kv_a_norm": 1}
REPLICATED = ["norm_mix", "norm_ffn", "hgrn_lower_bounds", "hgrn_out_norm", "mla_q_norm", "mla_k_norm",
              "ffn_conv_b"]
WEIGHTS = ["norm_mix", "norm_ffn", "hgrn_w_in", "hgrn_lower_bounds", "hgrn_out_norm", "hgrn_w_out", "mla_w_in",
           "mla_q_a_norm", "mla_w_q_up", "mla_kv_a_norm", "mla_w_kv_up", "mla_q_norm", "mla_k_norm", "mla_w_out",
           "ffn_w_up", "ffn_conv_w", "ffn_conv_b", "ffn_w_down"]


def _shards_to_cols(g):
    return g.transpose(1, 0, 2).reshape(g.shape[1], N_DEV * g.shape[2])


def _cols_to_shards(w):
    R = w.shape[0]
    return w.reshape(R, N_DEV, w.shape[1] // N_DEV).transpose(1, 0, 2)


def kernel(x, positions, norm_mix, norm_ffn, hgrn_w_in, hgrn_lower_bounds, hgrn_out_norm, hgrn_w_out, mla_w_in, mla_q_a_norm, mla_w_q_up, mla_kv_a_norm, mla_w_kv_up, mla_q_norm, mla_k_norm, mla_w_out, ffn_w_up, ffn_conv_w, ffn_conv_b, ffn_w_down, loss_target, m_norm_mix, m_norm_ffn, m_hgrn_w_in, m_hgrn_lower_bounds, m_hgrn_out_norm, m_hgrn_w_out, m_mla_w_in, m_mla_q_a_norm, m_mla_w_q_up, m_mla_kv_a_norm, m_mla_w_kv_up, m_mla_q_norm, m_mla_k_norm, m_mla_w_out, m_ffn_w_up, m_ffn_conv_w, m_ffn_conv_b, m_ffn_w_down, v_norm_mix, v_norm_ffn, v_hgrn_w_in, v_hgrn_lower_bounds, v_hgrn_out_norm, v_hgrn_w_out, v_mla_w_in, v_mla_q_a_norm, v_mla_w_q_up, v_mla_kv_a_norm, v_mla_w_kv_up, v_mla_q_norm, v_mla_k_norm, v_mla_w_out, v_ffn_w_up, v_ffn_conv_w, v_ffn_conv_b, v_ffn_w_down):
    local = dict(norm_mix=norm_mix, norm_ffn=norm_ffn, hgrn_w_in=hgrn_w_in, hgrn_lower_bounds=hgrn_lower_bounds,
                 hgrn_out_norm=hgrn_out_norm, hgrn_w_out=hgrn_w_out, mla_w_in=mla_w_in, mla_q_a_norm=mla_q_a_norm,
                 mla_w_q_up=mla_w_q_up, mla_kv_a_norm=mla_kv_a_norm, mla_w_kv_up=mla_w_kv_up, mla_q_norm=mla_q_norm,
                 mla_k_norm=mla_k_norm, mla_w_out=mla_w_out, ffn_w_up=ffn_w_up, ffn_conv_w=ffn_conv_w,
                 ffn_conv_b=ffn_conv_b, ffn_w_down=ffn_w_down)
    mom_m = dict(norm_mix=m_norm_mix, norm_ffn=m_norm_ffn, hgrn_w_in=m_hgrn_w_in,
                 hgrn_lower_bounds=m_hgrn_lower_bounds, hgrn_out_norm=m_hgrn_out_norm, hgrn_w_out=m_hgrn_w_out,
                 mla_w_in=m_mla_w_in, mla_q_a_norm=m_mla_q_a_norm, mla_w_q_up=m_mla_w_q_up,
                 mla_kv_a_norm=m_mla_kv_a_norm, mla_w_kv_up=m_mla_w_kv_up, mla_q_norm=m_mla_q_norm,
                 mla_k_norm=m_mla_k_norm, mla_w_out=m_mla_w_out, ffn_w_up=m_ffn_w_up, ffn_conv_w=m_ffn_conv_w,
                 ffn_conv_b=m_ffn_conv_b, ffn_w_down=m_ffn_w_down)
    mom_v = dict(norm_mix=v_norm_mix, norm_ffn=v_norm_ffn, hgrn_w_in=v_hgrn_w_in,
                 hgrn_lower_bounds=v_hgrn_lower_bounds, hgrn_out_norm=v_hgrn_out_norm, hgrn_w_out=v_hgrn_w_out,
                 mla_w_in=v_mla_w_in, mla_q_a_norm=v_mla_q_a_norm, mla_w_q_up=v_mla_w_q_up,
                 mla_kv_a_norm=v_mla_kv_a_norm, mla_w_kv_up=v_mla_w_kv_up, mla_q_norm=v_mla_q_norm,
                 mla_k_norm=v_mla_k_norm, mla_w_out=v_mla_w_out, ffn_w_up=v_ffn_w_up, ffn_conv_w=v_ffn_conv_w,
                 ffn_conv_b=v_ffn_conv_b, ffn_w_down=v_ffn_w_down)
    me = 4 * lax.axis_index("x") + 2 * lax.axis_index("y") + lax.axis_index("c")
    x, positions, target = x[0], positions[0], loss_target[0]
    T, D = x.shape
    depth = norm_mix.shape[0]
    R = mla_w_q_up.shape[1]
    cw = 2 * R + LANES
    small_names = list(SMALL_SHARDED)

    def block_of(kind, l):
        names = {"hgrn": HGRN_W, "mla": MLA_W, "ffn": FFN_W}[kind]
        idx = l if kind == "ffn" else l // 2
        return [(n, idx) for n in names]

    def riders(l):
        nxt = l + 1 < depth
        if l % 2 == 1:
            return block_of("ffn", l) + (block_of("hgrn", l + 1) + block_of("ffn", l + 1) if nxt else [])
        return (block_of("mla", l + 1) if nxt else []) + (block_of("ffn", 0) if l == 0 else [])

    gathered = {}

    def gather_ride(host, l):
        if host == "mixer" and l % 2 == 0:
            keys = [("ffn_w_up", l)]
        elif host == "mixer":
            keys = block_of("ffn", l)
            if l + 1 < depth:
                keys += block_of("hgrn", l + 1) + [("ffn_w_down", l + 1)]
            if l + 2 < depth:
                keys += block_of("mla", l + 2)
        elif l == 0:
            keys = [("ffn_w_down", 0)] + (block_of("mla", 1) if depth > 1 else [])
        else:
            keys = []
        if not keys:
            return keys, None
        return keys, ([local[n][i].astype(BF16) for n, i in keys], ["gather"] * len(keys))

    def take_gathered(keys, arrs):
        for key, a in zip(keys, arrs):
            gathered[key] = a

    keys0 = block_of("hgrn", 0)
    small_local = _pack([local[n] for n in small_names], F32, 8)
    got = _exchange([local[n][i].astype(BF16) for n, i in keys0] + [small_local],
                    ["gather"] * (len(keys0) + 1), name="gather_first")
    take_gathered(keys0, got[:-1])
    small_all = _unpack(got[-1], [local[n].shape for n in small_names], lead=(N_DEV,))
    conv_w_all = small_all[0].transpose(1, 0, 2, 3)
    qa_all = small_all[1].transpose(1, 0, 2).reshape(-1, R)
    kva_all = small_all[2].transpose(1, 0, 2).reshape(-1, R)
    fs = conv_w_all.shape[-1]
    conv_b_s = ffn_conv_b.reshape(depth, N_DEV, 1, fs)

    ct, s1, s2 = _rope_tables(positions)
    lb_soft = jax.nn.softmax(hgrn_lower_bounds.astype(F32), axis=0)
    lower_bounds = jnp.cumsum(lb_soft, axis=0) - lb_soft[0:1]

    def mla_views(j):
        w_in = _pad_cols(gathered["mla_w_in", j].reshape(D, -1), cw)
        wq = _q_up_padded(_shards_to_cols(gathered["mla_w_q_up", j]))
        wkv = _shards_to_cols(gathered["mla_w_kv_up", j])
        gq = _pad_cols(mla_q_norm[j].reshape(1, QK_HEAD), QK_PAD)
        gk = _pad_cols(mla_k_norm[j].reshape(1, QK_HEAD), QK_PAD)
        return w_in, wq, wkv, gq, gk

    saved = []
    h = _rmsnorm_fwd(x, norm_mix[0], name="norm_mix_fwd_0")
    for layer in range(depth):
        j = layer // 2
        s = {"x_in": x}
        s["h_mix"] = h
        keys, ride = gather_ride("mixer", layer)
        if layer % 2 == 0:
            p = _mm(h, gathered["hgrn_w_in", j], b_fmt="knb", bm=1024, name=f"hgrn_in_{layer}")
            (og, o, s0), got = _hgrn_fwd(p, lower_bounds[j], hgrn_out_norm[j], name=f"hgrn_fwd_{layer}", ride=ride)
            s.update(p=p, og=og, o=o, s0=s0)
            take_gathered(keys, got)
            x, h = _mm(og, gathered["hgrn_w_out", j].reshape(D, D), res=x, norm_out=norm_ffn[layer],
                       name=f"hgrn_out_{layer}")
        else:
            w_in, wq, wkv, gq, gk = mla_views(j)
            c = _mm(h, w_in, bm=1024, name=f"mla_in_{layer}")
            q, k, v = _mla_prep_fwd(c, wq, wkv, qa_all[j], kva_all[j], gq, gk, ct, s1, s2,
                                    name=f"mla_prep_fwd_{layer}")
            (ot, lse), got = _attn_fwd(q, k, v, name=f"attn_fwd_{layer}", ride=ride)
            s.update(c=c, q=q, k=k, v=v, ot=ot, lse=lse)
            take_gathered(keys, got)
            x, h = _mm(ot, gathered["mla_w_out", j].reshape(D, D), a_fmt="kmb", res=x, norm_out=norm_ffn[layer],
                       name=f"mla_out_{layer}")
        s["x_mid"] = x
        s["h_ffn"] = h
        keys, ride = gather_ride("ffn_up", layer)
        (a, u, y), got = _ffn_up_fwd(h, gathered["ffn_w_up", layer], conv_w_all[layer], conv_b_s[layer],
                                     name=f"ffn_up_{layer}", ride=ride)
        take_gathered(keys, got)
        s.update(a=a, u=u, y=y)
        w_down = gathered["ffn_w_down", layer].reshape(-1, D)
        if layer + 1 < depth:
            x, h = _mm(a, w_down, a_fmt="mkb", res=x, norm_out=norm_mix[layer + 1], bm=1024, bk=fs,
                       name=f"ffn_down_{layer}")
        else:
            x = _mm(a, w_down, a_fmt="mkb", res=x, bm=1024, bk=fs, name=f"ffn_down_{layer}")
        saved.append(s)

    dx, loss_part = _loss_head(x, target, name="loss_head")

    parts = {}
    received = {}
    g_small = {n: [None] * local[n].shape[0] for n in REPLICATED + small_names}

    def scatter_ride(l):
        keys = riders(l)
        return keys, (([parts[key] for key in keys], ["scatter"] * len(keys)) if keys else None)

    def take_received(keys, arrs):
        for key, a in zip(keys, arrs):
            received[key] = a

    for layer in reversed(range(depth)):
        j = layer // 2
        s = saved[layer]
        parts["ffn_w_down", layer] = _mm(s["a"], dx, a_fmt="kmb", out_dtype=BF16, name=f"ffn_down_dw_{layer}"
                                         ).reshape(N_DEV, -1, D)
        du, dcw, dcb = _ffn_act_bwd(dx, gathered["ffn_w_down", layer].reshape(-1, D), s["u"], s["y"],
                                    conv_w_all[layer], name=f"ffn_act_bwd_{layer}")
        g_small["ffn_conv_w"][layer] = dcw.reshape(N_DEV, 3, fs)
        g_small["ffn_conv_b"][layer] = dcb.reshape(N_DEV * fs)
        du8 = du.reshape(N_DEV, T, fs)
        parts["ffn_w_up", layer] = _mm(s["h_ffn"], du8, a_fmt="km", b_fmt="knb", out_fmt="mnb", out_dtype=BF16,
                                       bm=1024, name=f"ffn_up_dw_{layer}")
        dx, dgain = _mm(du8, gathered["ffn_w_up", layer], a_fmt="mkb", b_fmt="nkb", bm=1024, bk=fs,
                        norm_bwd=(s["x_mid"], norm_ffn[layer], dx), name=f"ffn_up_dh_{layer}")
        g_small["norm_ffn"][layer] = dgain.reshape(D)
        keys, ride = scatter_ride(layer)
        if layer % 2 == 0:
            w_out = gathered["hgrn_w_out", j].reshape(D, D)
            parts["hgrn_w_out", j] = _mm(s["og"], dx, a_fmt="km", out_dtype=BF16, bm=1024,
                                         name=f"hgrn_out_dw_{layer}").reshape(N_DEV, -1, D)
            dog = _mm(dx, w_out, b_fmt="nk", name=f"hgrn_out_dx_{layer}")
            (dp, dlb, dgn), got = _hgrn_bwd(s["p"], lower_bounds[j], hgrn_out_norm[j], s["s0"], s["o"], dog,
                                            name=f"hgrn_bwd_{layer}", ride=ride)
            take_received(keys, got)
            g_small["hgrn_lower_bounds"][j] = dlb.reshape(D)
            g_small["hgrn_out_norm"][j] = dgn.reshape(HEAD)
            w_in_s = gathered["hgrn_w_in", j]
            parts["hgrn_w_in", j] = _mm(s["h_mix"], dp, a_fmt="km", out_fmt="mnb", out_dtype=BF16, bm=1024,
                                        bn=w_in_s.shape[2], name=f"hgrn_in_dw_{layer}")
            if layer == 0:
                (dx, dgain), got = _mm(dp, w_in_s, b_fmt="nkb", bm=1024, norm_bwd=(s["x_in"], norm_mix[layer], dx),
                                       name=f"hgrn_in_dx_{layer}",
                                       ride=([parts[key] for key in keys0], ["scatter"] * len(keys0)))
                take_received(keys0, got)
            else:
                dx, dgain = _mm(dp, w_in_s, b_fmt="nkb", bm=1024, norm_bwd=(s["x_in"], norm_mix[layer], dx),
                                name=f"hgrn_in_dx_{layer}")
        else:
            w_in, wq, wkv, gq, gk = mla_views(j)
            w_out = gathered["mla_w_out", j].reshape(D, D)
            tb = s["ot"].shape[2]
            parts["mla_w_out", j] = _mm(s["ot"], dx, a_fmt="mkb", bk=tb, out_dtype=BF16, bm=1024,
                                        name=f"mla_out_dw_{layer}").reshape(N_DEV, -1, D)
            dot_ = _mm(w_out, dx, b_fmt="nk", out_fmt="mnb", out_dtype=BF16, bm=D, bn=tb,
                       name=f"mla_out_dx_{layer}")
            (dq, dk, dv), got = _attn_bwd(s["q"], s["k"], s["v"], s["ot"], s["lse"], dot_,
                                          name=f"attn_bwd_{layer}", ride=ride)
            take_received(keys, got)
            dc, dwq, dwkv, dgaq, dgakv, dgq, dgk = _mla_prep_bwd(
                s["c"], wq, wkv, qa_all[j], kva_all[j], gq, gk, ct, s1, s2, dq, dk, dv,
                name=f"mla_prep_bwd_{layer}")
            parts["mla_w_q_up", j] = _cols_to_shards(_q_up_unpadded(dwq)).astype(BF16)
            parts["mla_w_kv_up", j] = _cols_to_shards(dwkv).astype(BF16)
            g_small["mla_q_a_norm"][j] = dgaq.reshape(R)
            g_small["mla_kv_a_norm"][j] = dgakv.reshape(R)
            g_small["mla_q_norm"][j] = dgq[0, :QK_HEAD]
            g_small["mla_k_norm"][j] = dgk[0, :QK_HEAD]
            win_cols = mla_w_in.shape[2]
            dw_in = _mm(s["h_mix"], dc, a_fmt="km", bm=1024, name=f"mla_in_dw_{layer}")
            parts["mla_w_in", j] = dw_in[:, :win_cols].astype(BF16).reshape(N_DEV, -1, win_cols)
            dx, dgain = _mm(dc, w_in, b_fmt="nk", norm_bwd=(s["x_in"], norm_mix[layer], dx),
                            name=f"mla_in_dx_{layer}")
        g_small["norm_mix"][layer] = dgain.reshape(D)
    grad_x = dx

    dlb_eff = jnp.stack(g_small["hgrn_lower_bounds"])
    dsoft = jnp.cumsum(dlb_eff[::-1], axis=0)[::-1]
    dsoft = dsoft.at[0].add(-jnp.sum(dlb_eff, axis=0))
    g_lb = lb_soft * (dsoft - jnp.sum(dsoft * lb_soft, axis=0, keepdims=True))
    small_grads = {n: (g_lb if n == "hgrn_lower_bounds" else jnp.stack(g_small[n])) for n in g_small}

    small_grad_names = REPLICATED + small_names
    small_part = _pack([small_grads[n] for n in small_grad_names] + [loss_part], F32, 8)
    small_recv = _exchange([small_part], ["gather"], name="exchange_last")[0]

    out = {}
    for n in BIG:
        layers = local[n].shape[0]
        shard = local[n].shape[1:]
        p2 = jnp.concatenate([received[n, i].reshape(N_DEV, -1, shard[-1]) for i in range(layers)], axis=1)
        flat = lambda a: a.reshape(-1, shard[-1])
        res = _adamw(p2, flat(local[n]), flat(mom_m[n]), flat(mom_v[n]), name=f"adamw_{n}")
        for kind, a in zip(("grad", "delta", "new_m", "new_v"), res):
            out[kind, n] = a.reshape(local[n].shape)
    small_sum = _sum_slots(small_recv, name="sum_small")
    small_full = _unpack(small_sum, [small_grads[n].shape for n in small_grad_names] + [(1, LANES)])
    loss = small_full[-1][0, 0]
    g_mine = {}
    for n, a in zip(small_grad_names, small_full[:-1]):
        if n == "ffn_conv_w":
            a = lax.dynamic_index_in_dim(a, me, axis=1, keepdims=False)
        elif n in SMALL_SHARDED:
            size = local[n].shape[1]
            a = lax.dynamic_slice_in_dim(a, me * size, size, axis=1)
        g_mine[n] = a
    small_shapes = [local[n].shape for n in small_grad_names]
    res = _adamw(_pack([g_mine[n] for n in small_grad_names], F32, 8)[None],
                 _pack([local[n] for n in small_grad_names], F32, 8),
                 _pack([mom_m[n] for n in small_grad_names], F32, 8),
                 _pack([mom_v[n] for n in small_grad_names], F32, 8), name="adamw_small")
    for kind, packed in zip(("grad", "delta", "new_m", "new_v"), res):
        for n, a in zip(small_grad_names, _unpack(packed, small_shapes)):
            out[kind, n] = a

    outs = [loss, grad_x[None]]
    for kind in ("grad", "delta", "new_m", "new_v"):
        outs += [out[kind, n] for n in WEIGHTS]
    return tuple(outs)
```
